```python
import jax, jax.numpy as jnp
from jax import lax
import numpy as np

D_MODEL = 1024
BATCH = 2
SEQ = 8192
DEPTH = 2

HEAD_DIM = 64
H_HGRN = 4
H_GLA = 6
H_RWKV = 6
W_HGRN = H_HGRN * HEAD_DIM
W_GLA = H_GLA * HEAD_DIM
W_RWKV = H_RWKV * HEAD_DIM
MIX_WIDTH = W_HGRN + W_GLA + W_RWKV

GLA_GATE_RANK = 16
GLA_GATE_NORMALIZER = 16.0
RWKV_DECAY_RANK = 64
RWKV_ICLR_RANK = 64
RWKV_GATE_RANK = 128
RWKV_GN_EPS = 64e-5

N_HGRN_COLS = 4 * W_HGRN
N_GLA_COLS = 4 * W_GLA + GLA_GATE_RANK
N_RWKV_COLS = 3 * W_RWKV + RWKV_DECAY_RANK + RWKV_ICLR_RANK + RWKV_GATE_RANK
N_IN = N_HGRN_COLS + N_GLA_COLS + N_RWKV_COLS
RWKV_SPLITS = [W_RWKV, 2 * W_RWKV, 3 * W_RWKV, 3 * W_RWKV + RWKV_DECAY_RANK,
               3 * W_RWKV + RWKV_DECAY_RANK + RWKV_ICLR_RANK]

D_FF = 4 * D_MODEL
CHUNK = 64
LN_EPS = 1e-5
RMS_EPS = 1e-5
F_MIN = 1e-30
DEEPNORM_ALPHA = (2.0 * DEPTH) ** 0.25
DEEPNORM_BETA = (8.0 * DEPTH) ** -0.25

kernel_name = "hybrid_hgrn2_gla_rwkv7_deepnorm_adaln"


def split_heads(t, n_heads):
    return t.reshape(t.shape[:-1] + (n_heads, -1))


def layer_norm(x, g, b):
    xf = x.astype(jnp.float32)
    mu = jnp.mean(xf, axis=-1, keepdims=True)
    var = jnp.mean(jnp.square(xf - mu), axis=-1, keepdims=True)
    return ((xf - mu) * lax.rsqrt(var + LN_EPS)).astype(x.dtype) * g + b


def head_rms_norm(x, g, n_heads):
    xh = split_heads(x, n_heads).astype(jnp.float32)
    xh = xh * lax.rsqrt(jnp.mean(xh * xh, axis=-1, keepdims=True) + RMS_EPS)
    return xh.reshape(x.shape).astype(x.dtype) * g


def head_group_norm(x, g, b, n_heads, eps):
    xh = split_heads(x, n_heads).astype(jnp.float32)
    mu = jnp.mean(xh, axis=-1, keepdims=True)
    var = jnp.mean(jnp.square(xh - mu), axis=-1, keepdims=True)
    return ((xh - mu) * lax.rsqrt(var + eps)).reshape(x.shape).astype(x.dtype) * g + b


def chunked_gated_linear_attention(q, k, v, log_g):
    B, T, H, K = q.shape
    V = v.shape[-1]
    n = T // CHUNK

    def to_chunks(a):
        return jnp.moveaxis(a.astype(jnp.float32).reshape(B, n, CHUNK, H, a.shape[-1]), 1, 0)

    causal = jnp.tril(jnp.ones((CHUNK, CHUNK), dtype=bool))[None, :, :, None, None]

    def step(S, inp):
        qc, kc, vc, gc = inp
        b = jnp.cumsum(gc, axis=1)
        diff = b[:, :, None] - b[:, None, :]
        decay = jnp.where(causal, jnp.exp(jnp.minimum(diff, 0.0)), 0.0)
        scores = jnp.einsum('bthk,bshk,btshk->bths', qc, kc, decay)
        o = (jnp.einsum('bths,bshv->bthv', scores, vc)
             + jnp.einsum('bthk,bhkv->bthv', qc * jnp.exp(b), S))
        b_end = b[:, -1]
        S = (S * jnp.exp(b_end)[..., None]
             + jnp.einsum('bshk,bshv->bhkv', kc * jnp.exp(b_end[:, None] - b), vc))
        return S, o

    S0 = jnp.zeros((B, H, K, V), jnp.float32)
    _, o = lax.scan(step, S0, (to_chunks(q), to_chunks(k), to_chunks(v), to_chunks(log_g)))
    return jnp.moveaxis(o, 0, 1).reshape(B, T, H, V)


def rwkv7_recurrence(r, w, k, v, a, b):
    B, T, H, D = r.shape

    def step(S, inp):
        r_t, w_t, k_t, v_t, a_t, b_t = inp
        sa = jnp.einsum('bhvk,bhk->bhv', S, a_t)
        S = S * w_t[:, :, None, :] + sa[..., None] * b_t[:, :, None, :] + v_t[..., None] * k_t[:, :, None, :]
        return S, jnp.einsum('bhvk,bhk->bhv', S, r_t)

    xs = tuple(jnp.moveaxis(t.astype(jnp.float32), 1, 0) for t in (r, w, k, v, a, b))
    S0 = jnp.zeros((B, H, D, D), jnp.float32)
    _, y = lax.scan(step, S0, xs)
    return jnp.moveaxis(y, 0, 1)


def hgrn2_mixer(z, lower_bound, norm_g):
    B, T, _ = z.shape
    q, f_logit, i, gate = jnp.split(z, 4, axis=-1)
    zf = f_logit.astype(jnp.float32)
    f = lower_bound + (1.0 - lower_bound) * jax.nn.sigmoid(zf)
    log_f = jnp.log(jnp.maximum(f, F_MIN))
    k = (1.0 - lower_bound) * jax.nn.sigmoid(-zf)
    q = jax.nn.silu(q.astype(jnp.float32)) * HEAD_DIM ** -0.5
    o = chunked_gated_linear_attention(split_heads(q, H_HGRN), split_heads(k, H_HGRN),
                                       split_heads(i, H_HGRN), split_heads(log_f, H_HGRN))
    o = o.reshape(B, T, W_HGRN).astype(z.dtype)
    return head_rms_norm(o, norm_g, H_HGRN) * jax.nn.silu(gate)


def gla_mixer(z, alpha_up, alpha_b, norm_g):
    B, T, _ = z.shape
    q, k, v, r, h_alpha = jnp.split(z, [W_GLA, 2 * W_GLA, 3 * W_GLA, 4 * W_GLA], axis=-1)
    log_alpha = jax.nn.log_sigmoid((h_alpha @ alpha_up + alpha_b).astype(jnp.float32)) / GLA_GATE_NORMALIZER
    o = chunked_gated_linear_attention(split_heads(q * HEAD_DIM ** -0.5, H_GLA), split_heads(k, H_GLA),
                                       split_heads(v, H_GLA), split_heads(log_alpha, H_GLA))
    o = o.reshape(B, T, W_GLA).astype(z.dtype)
    return head_rms_norm(o, norm_g, H_GLA) * jax.nn.silu(r)


def rwkv7_mixer(z, mu, w0, w_up, a0, a_up, g_up, k_k, k_a, r_k, gn_g, gn_b):
    B, T, _ = z.shape
    z_prev = jnp.pad(z, ((0, 0), (1, 0), (0, 0)))[:, :-1]
    z = z + (z_prev - z) * mu
    r, k, v, h_w, h_a, h_g = jnp.split(z, RWKV_SPLITS, axis=-1)
    w_log = -jax.nn.softplus(-(w0 + jnp.tanh(h_w) @ w_up)) - 0.5
    decay = jnp.exp(-jnp.exp(w_log.astype(jnp.float32)))
    a = jax.nn.sigmoid(a0 + h_a @ a_up)
    g = jax.nn.sigmoid(h_g) @ g_up
    kk = split_heads(k * k_k, H_RWKV).astype(jnp.float32)
    kk = kk / jnp.maximum(jnp.sqrt(jnp.sum(kk * kk, axis=-1, keepdims=True)), 1e-12)
    k = k * (1.0 + (a - 1.0) * k_a)
    a_h = split_heads(a, H_RWKV).astype(jnp.float32)
    r_h, k_h, v_h = split_heads(r, H_RWKV), split_heads(k, H_RWKV), split_heads(v, H_RWKV)
    y = rwkv7_recurrence(r_h, split_heads(decay, H_RWKV), k_h, v_h, -kk, kk * a_h)
    y = head_group_norm(y.reshape(B, T, W_RWKV).astype(z.dtype), gn_g, gn_b, H_RWKV, RWKV_GN_EPS)
    bonus = jnp.sum(r_h * k_h * r_k, axis=-1, keepdims=True) * v_h
    return (y + bonus.reshape(B, T, W_RWKV)) * g


def setup_inputs(seed: int = 0) -> dict:
    key = jax.random.key(seed)
    ks = iter(jax.random.split(key, 32))
    nrm = lambda shape, s: s * jax.random.normal(next(ks), shape, jnp.float32)
    L = DEPTH
    return {
        "x": nrm((BATCH, SEQ, D_MODEL), 1.0),
        "c": nrm((BATCH, D_MODEL), 1.0),
        "hgrn_lb_logits": nrm((L, W_HGRN), 1.0),
        "ada_w": nrm((L, D_MODEL, 6 * D_MODEL), 0.1 * D_MODEL ** -0.5),
        "ada_b": nrm((L, 6 * D_MODEL), 0.01),
        "w_in": nrm((L, D_MODEL, N_IN), D_MODEL ** -0.5),
        "hgrn_norm_g": 1.0 + nrm((L, W_HGRN), 0.02),
        "gla_alpha_up": nrm((L, GLA_GATE_RANK, W_GLA), GLA_GATE_RANK ** -0.5),
        "gla_alpha_b": nrm((L, W_GLA), 0.1),
        "gla_norm_g": 1.0 + nrm((L, W_GLA), 0.02),
        "rwkv_mu": jax.random.uniform(next(ks), (L, N_RWKV_COLS), jnp.float32, 0.0, 1.0),
        "rwkv_w0": jax.random.uniform(next(ks), (L, W_RWKV), jnp.float32, -6.0, 1.0),
        "rwkv_w_up": nrm((L, RWKV_DECAY_RANK, W_RWKV), 0.5 * RWKV_DECAY_RANK ** -0.5),
        "rwkv_a0": nrm((L, W_RWKV), 0.1),
        "rwkv_a_up": nrm((L, RWKV_ICLR_RANK, W_RWKV), RWKV_ICLR_RANK ** -0.5),
        "rwkv_g_up": nrm((L, RWKV_GATE_RANK, W_RWKV), RWKV_GATE_RANK ** -0.5),
        "rwkv_k_k": 0.85 + nrm((L, W_RWKV), 0.02),
        "rwkv_k_a": 1.0 + nrm((L, W_RWKV), 0.02),
        "rwkv_r_k": nrm((L, H_RWKV, HEAD_DIM), 0.1),
        "rwkv_gn_g": 1.0 + nrm((L, W_RWKV), 0.02),
        "rwkv_gn_b": nrm((L, W_RWKV), 0.02),
        "w_out": nrm((L, MIX_WIDTH, D_MODEL), DEEPNORM_BETA * MIX_WIDTH ** -0.5),
        "ln1_g": 1.0 + nrm((L, D_MODEL), 0.02),
        "ln1_b": nrm((L, D_MODEL), 0.02),
        "mlp_w_up": nrm((L, D_MODEL, D_FF), D_MODEL ** -0.5),
        "mlp_w_down": nrm((L, D_FF, D_MODEL), DEEPNORM_BETA * D_FF ** -0.5),
        "ln2_g": 1.0 + nrm((L, D_MODEL), 0.02),
        "ln2_b": nrm((L, D_MODEL), 0.02),
    }


def reference(x, c, hgrn_lb_logits, ada_w, ada_b, w_in, hgrn_norm_g, gla_alpha_up, gla_alpha_b,
              gla_norm_g, rwkv_mu, rwkv_w0, rwkv_w_up, rwkv_a0, rwkv_a_up, rwkv_g_up, rwkv_k_k,
              rwkv_k_a, rwkv_r_k, rwkv_gn_g, rwkv_gn_b, w_out, ln1_g, ln1_b, mlp_w_up, mlp_w_down,
              ln2_g, ln2_b):
    p = jax.nn.softmax(hgrn_lb_logits.astype(jnp.float32), axis=0)
    lower_bounds = jnp.cumsum(p, axis=0) - p[0:1]
    c_act = jax.nn.silu(c)
    for l in range(DEPTH):
        mod = c_act @ ada_w[l] + ada_b[l]
        shift1, scale1, gate1, shift2, scale2, gate2 = [m[:, None, :] for m in jnp.split(mod, 6, axis=-1)]

        h = x * (1.0 + scale1) + shift1
        z = h @ w_in[l]
        z_h, z_g, z_r = jnp.split(z, [N_HGRN_COLS, N_HGRN_COLS + N_GLA_COLS], axis=-1)
        o_h = hgrn2_mixer(z_h, lower_bounds[l], hgrn_norm_g[l])
        o_g = gla_mixer(z_g, gla_alpha_up[l], gla_alpha_b[l], gla_norm_g[l])
        o_r = rwkv7_mixer(z_r, rwkv_mu[l], rwkv_w0[l], rwkv_w_up[l], rwkv_a0[l], rwkv_a_up[l],
                          rwkv_g_up[l], rwkv_k_k[l], rwkv_k_a[l], rwkv_r_k[l], rwkv_gn_g[l], rwkv_gn_b[l])
        o = jnp.concatenate([o_h, o_g, o_r], axis=-1) @ w_out[l]
        x = layer_norm(DEEPNORM_ALPHA * x + (1.0 + gate1) * o, ln1_g[l], ln1_b[l])

        h = x * (1.0 + scale2) + shift2
        m = jnp.square(jax.nn.relu(h @ mlp_w_up[l])) @ mlp_w_down[l]
        x = layer_norm(DEEPNORM_ALPHA * x + (1.0 + gate2) * m, ln2_g[l], ln2_b[l])
    return x
```

```python
import functools

import numpy as np
import jax
import jax.numpy as jnp
from jax import lax
from jax.experimental import pallas as pl
from jax.experimental.pallas import tpu as pltpu

F32 = jnp.float32
BF16 = jnp.bfloat16

HEAD_DIM = 64
H_HGRN, H_GLA, H_RWKV = 4, 6, 6
W_HGRN, W_GLA, W_RWKV = H_HGRN * HEAD_DIM, H_GLA * HEAD_DIM, H_RWKV * HEAD_DIM
GLA_GATE_RANK = 16
GLA_GATE_NORMALIZER = 16.0
RWKV_DECAY_RANK, RWKV_ICLR_RANK, RWKV_GATE_RANK = 64, 64, 128
RWKV_GN_EPS = 64e-5
N_HGRN_COLS = 4 * W_HGRN
N_GLA_COLS = 4 * W_GLA + GLA_GATE_RANK
N_RWKV_COLS = 3 * W_RWKV + RWKV_DECAY_RANK + RWKV_ICLR_RANK + RWKV_GATE_RANK
CHUNK = 64
LN_EPS = 1e-5
RMS_EPS = 1e-5
F_MIN = 1e-30

LANES = 128
PAIR = 2 * HEAD_DIM
SUB = 16
N_SUB = CHUNK // SUB
VMEM_LIMIT = 56 * 1024 * 1024

W_GL = W_HGRN + W_GLA
N_GL_PAIRS = W_GL // PAIR
N_RW_PAIRS = W_RWKV // PAIR
GLA_Z = 4 * W_GLA + LANES
Z_COLS = N_HGRN_COLS + GLA_Z + N_RWKV_COLS

NN = (((1,), (0,)), ((), ()))
NT = (((1,), (1,)), ((), ()))
TN = (((0,), (0,)), ((), ()))


def _mm(a, b, dims=NN):
    return lax.dot_general(a, b, dims, preferred_element_type=F32)


def _split2(x):
    hi = x.astype(BF16)
    lo = (x - hi.astype(F32)).astype(BF16)
    return hi, lo


def _split3(x):
    hi = x.astype(BF16)
    r = x - hi.astype(F32)
    mid = r.astype(BF16)
    lo = (r - mid.astype(F32)).astype(BF16)
    return hi, mid, lo


def _dot1(a, b, dims=NN):
    return _mm(a.astype(BF16), b.astype(BF16), dims)


def _dot3(a, b, dims=NN):
    ah, al = _split2(a)
    bh, bl = _split2(b)
    return _mm(ah, bh, dims) + (_mm(ah, bl, dims) + _mm(al, bh, dims))


def _dot_exact_lhs(a_bf16, b, dims=NN):
    bh, bm, bl = _split3(b)
    return _mm(a_bf16, bh, dims) + (_mm(a_bf16, bm, dims) + _mm(a_bf16, bl, dims))


def _dot_exact_rhs(a, b_bf16, dims=NN):
    ah, am, al = _split3(a)
    return _mm(ah, b_bf16, dims) + (_mm(am, b_bf16, dims) + _mm(al, b_bf16, dims))


def _sigmoid(x):
    return 1.0 / (1.0 + jnp.exp(-x))


def _silu(x):
    return x * _sigmoid(x)


def _softplus(x):
    return jnp.maximum(x, 0.0) + jnp.log1p(jnp.exp(-jnp.abs(x)))


def _log_sigmoid(x):
    return -_softplus(-x)


def _layer_norm(y, g, b):
    mu = jnp.mean(y, axis=-1, keepdims=True)
    d = y - mu
    var = jnp.mean(d * d, axis=-1, keepdims=True)
    return d * lax.rsqrt(var + LN_EPS) * g + b


def _iota(shape, axis):
    return lax.broadcasted_iota(jnp.int32, shape, axis)


def _stack_heads(x):
    lane = _iota(x.shape, 1)
    return jnp.concatenate([jnp.where(lane < HEAD_DIM, x, 0.0), jnp.where(lane >= HEAD_DIM, x, 0.0)], axis=0)


def _pair_masks():
    t = _iota((CHUNK, PAIR), 0)
    s = _iota((CHUNK, PAIR), 1) % HEAD_DIM
    r2 = _iota((PAIR, PAIR), 0) // HEAD_DIM
    c2 = _iota((PAIR, PAIR), 1) // HEAD_DIM
    return dict(
        strict=s < t,
        incl=s <= t,
        eye=s == t,
        diag=(s // SUB == t // SUB) & (s <= t),
        off=(s // SUB) < (t // SUB),
        bd=r2 == c2,
    )


def _tril_bf16():
    return jnp.where(_iota((CHUNK, CHUNK), 1) <= _iota((CHUNK, CHUNK), 0), 1.0, 0.0).astype(BF16)


def _seg_mean_matrix(width):
    r = _iota((width, width), 0) // HEAD_DIM
    c = _iota((width, width), 1) // HEAD_DIM
    return jnp.where(r == c, 1.0, 0.0).astype(BF16)


def _mod_kernel(c_ref, w_ref, b_ref, o_ref):
    c = _silu(c_ref[...])
    o_ref[0] = _dot3(c, w_ref[0]) + b_ref[0]


def _modulation(c, ada_w, ada_b):
    depth, d, n = ada_w.shape
    bsz = c.shape[0]
    tn = 1536
    return pl.pallas_call(
        _mod_kernel,
        grid=(depth, n // tn),
        in_specs=[
            pl.BlockSpec((bsz, d), lambda l, j: (0, 0)),
            pl.BlockSpec((1, d, tn), lambda l, j: (l, 0, j)),
            pl.BlockSpec((1, 1, tn), lambda l, j: (l, 0, j)),
        ],
        out_specs=pl.BlockSpec((1, bsz, tn), lambda l, j: (l, 0, j)),
        out_shape=jax.ShapeDtypeStruct((depth, bsz, n), F32),
        compiler_params=pltpu.CompilerParams(
            dimension_semantics=("arbitrary", "arbitrary"), vmem_limit_bytes=VMEM_LIMIT),
        name="adaln_modulation",
    )(c, ada_w, ada_b.reshape(depth, 1, n))


def _lower_bound_kernel(x_ref, o_ref):
    depth = x_ref.shape[0]
    rows = [x_ref[l:l + 1, :] for l in range(depth)]
    m = functools.reduce(jnp.maximum, rows)
    e = [jnp.exp(r - m) for r in rows]
    tot = functools.reduce(lambda a, b: a + b, e)
    p = [ei / tot for ei in e]
    acc = jnp.zeros_like(p[0])
    for l in range(depth):
        acc = acc + p[l]
        o_ref[l:l + 1, :] = acc - p[0]


def _lower_bounds(logits):
    return pl.pallas_call(
        _lower_bound_kernel,
        out_shape=jax.ShapeDtypeStruct(logits.shape, F32),
        name="hgrn_lower_bounds",
    )(logits.astype(F32))


def _inproj_kernel(x_ref, mod_ref, w_ref, zh_ref, zg_ref, zr_ref):
    m = mod_ref[0]
    h = (x_ref[...] * (1.0 + m[1:2, :]) + m[0:1, :]).astype(BF16)
    zh_ref[...] = _mm(h, w_ref[:, 0:N_HGRN_COLS])
    zg_ref[...] = _mm(h, w_ref[:, N_HGRN_COLS:N_HGRN_COLS + GLA_Z])
    zr_ref[...] = _mm(h, w_ref[:, N_HGRN_COLS + GLA_Z:Z_COLS])


def _in_projection(x2, mod_l, w_in_p, seq, tm):
    m_rows, d = x2.shape
    steps_per_batch = seq // tm
    row = lambda i: (i, 0)
    const = lambda i: (0, 0)
    return pl.pallas_call(
        _inproj_kernel,
        grid=(m_rows // tm,),
        in_specs=[
            pl.BlockSpec((tm, d), row),
            pl.BlockSpec((1, 6, d), lambda i: (i // steps_per_batch, 0, 0)),
            pl.BlockSpec((d, Z_COLS), const, pipeline_mode=pl.Buffered(1)),
        ],
        out_specs=[
            pl.BlockSpec((tm, N_HGRN_COLS), row),
            pl.BlockSpec((tm, GLA_Z), row),
            pl.BlockSpec((tm, N_RWKV_COLS), row),
        ],
        out_shape=[
            jax.ShapeDtypeStruct((m_rows, N_HGRN_COLS), F32),
            jax.ShapeDtypeStruct((m_rows, GLA_Z), F32),
            jax.ShapeDtypeStruct((m_rows, N_RWKV_COLS), F32),
        ],
        compiler_params=pltpu.CompilerParams(
            dimension_semantics=("arbitrary",), vmem_limit_bytes=VMEM_LIMIT),
        name="in_projection",
    )(x2, mod_l, w_in_p)


def _gl_kernel(lb_ref, zh_ref, zg_ref, au_ref, ab_ref, ng_ref, ind_ref, out_ref,
               st_ref, q_s, k_s, v_s, b_s, ad_s, e_s, o_s):
    tb = zh_ref.shape[0]
    n_chunks = tb // CHUNK

    @pl.when(pl.program_id(1) == 0)
    def _():
        st_ref[...] = jnp.zeros_like(st_ref)

    lb = lb_ref[...]
    zf = zh_ref[:, W_HGRN:2 * W_HGRN]
    f = lb + (1.0 - lb) * _sigmoid(zf)
    b_s[:, 0:W_HGRN] = jnp.log(jnp.maximum(f, F_MIN))
    k_s[:, 0:W_HGRN] = (1.0 - lb) * _sigmoid(-zf)
    q_s[:, 0:W_HGRN] = _silu(zh_ref[:, 0:W_HGRN]) * HEAD_DIM ** -0.5
    v_s[:, 0:W_HGRN] = zh_ref[:, 2 * W_HGRN:3 * W_HGRN]

    logit = _dot3(zg_ref[:, 4 * W_GLA:GLA_Z], au_ref[...]) + ab_ref[...]
    b_s[:, W_HGRN:W_GL] = _log_sigmoid(logit) / GLA_GATE_NORMALIZER
    q_s[:, W_HGRN:W_GL] = zg_ref[:, 0:W_GLA] * HEAD_DIM ** -0.5
    k_s[:, W_HGRN:W_GL] = zg_ref[:, W_GLA:2 * W_GLA]
    v_s[:, W_HGRN:W_GL] = zg_ref[:, 2 * W_GLA:3 * W_GLA]

    tril = _tril_bf16()

    def cum_body(c, carry):
        rows = pl.ds(pl.multiple_of(c * CHUNK, CHUNK), CHUNK)
        b_s[rows, :] = _dot_exact_lhs(tril, b_s[rows, :])
        return carry

    lax.fori_loop(0, n_chunks, cum_body, 0)

    for p in range(N_GL_PAIRS):
        lanes = slice(p * PAIR, (p + 1) * PAIR)
        q3 = q_s[:, lanes].reshape(tb // SUB, SUB, PAIR)
        k3 = k_s[:, lanes].reshape(tb // SUB, SUB, PAIR)
        b3 = b_s[:, lanes].reshape(tb // SUB, SUB, PAIR)
        for sg in range(SUB):
            dec = jnp.exp(jnp.minimum(b3 - b3[:, sg:sg + 1, :], 0.0))
            e = q3 * k3[:, sg:sg + 1, :] * dec
            e_s[:, sg * PAIR:(sg + 1) * PAIR] = e.reshape(tb, PAIR).astype(BF16)
        ad_s[:, lanes] = _mm(e_s[...], ind_ref[...])

    masks = _pair_masks()
    row_blk = _iota((CHUNK, PAIR), 0) // SUB

    def chunk_body(c, carry):
        rows = pl.ds(pl.multiple_of(c * CHUNK, CHUNK), CHUNK)
        for p in range(N_GL_PAIRS):
            lanes = slice(p * PAIR, (p + 1) * PAIR)
            q = q_s[rows, lanes]
            k = k_s[rows, lanes]
            v = v_s[rows, lanes]
            b = b_s[rows, lanes]
            b_end = b[CHUNK - 1:CHUNK, :]
            e_blk = jnp.concatenate(
                [jnp.broadcast_to(b[SUB * j + SUB - 1:SUB * j + SUB, :], (SUB, PAIR)) for j in range(N_SUB)], axis=0)
            k_hat = k * jnp.exp(e_blk - b)
            q_parts, k_parts = [], []
            for j in range(N_SUB - 1):
                e_j = b[SUB * j + SUB - 1:SUB * j + SUB, :]
                q_parts.append(q * jnp.exp(jnp.minimum(b - e_j, 0.0)))
                k_parts.append(_stack_heads(jnp.where(row_blk == j, k_hat, 0.0)))
            scores = _dot1(jnp.concatenate(q_parts, axis=1), jnp.concatenate(k_parts, axis=1), NT)
            a = jnp.where(masks["diag"], ad_s[rows, lanes], 0.0) + jnp.where(masks["off"], scores, 0.0)
            st = st_ref[p]
            o = _dot1(a, _stack_heads(v)) + _dot1(q * jnp.exp(b), st, NT)
            upd = _dot1(v, k * jnp.exp(b_end - b), TN)
            st_ref[p] = st * jnp.exp(b_end) + jnp.where(masks["bd"], upd, 0.0)
            o_s[rows, lanes] = o
        return carry

    lax.fori_loop(0, n_chunks, chunk_body, 0)

    o = o_s[...]
    ms = _dot_exact_rhs(o * o, _seg_mean_matrix(W_GL)) * (1.0 / HEAD_DIM)
    gate = jnp.concatenate([_silu(zh_ref[:, 3 * W_HGRN:4 * W_HGRN]), _silu(zg_ref[:, 3 * W_GLA:4 * W_GLA])], axis=1)
    out_ref[...] = (o * lax.rsqrt(ms + RMS_EPS) * ng_ref[...] * gate).astype(out_ref.dtype)


def _gl_indicator():
    r = np.arange(SUB * PAIR)
    c = np.arange(PAIR)
    sg = r // PAIR
    h = (r % PAIR) // HEAD_DIM
    ind = (h[:, None] == (c // HEAD_DIM)[None, :]) & (sg[:, None] == (c % SUB)[None, :])
    return jnp.asarray(ind, dtype=BF16)


def _gl_mixer(zh, zg, lb_l, alpha_up_p, alpha_b, norm_g, bsz, seq, tb):
    steps = seq // tb
    row = lambda b, i: (b * steps + i, 0)
    const = lambda b, i: (0, 0)
    scratch_f32 = pltpu.VMEM((tb, W_GL), F32)
    return pl.pallas_call(
        _gl_kernel,
        grid=(bsz, steps),
        in_specs=[
            pl.BlockSpec((1, W_HGRN), const),
            pl.BlockSpec((tb, N_HGRN_COLS), row),
            pl.BlockSpec((tb, GLA_Z), row),
            pl.BlockSpec((LANES, W_GLA), const),
            pl.BlockSpec((1, W_GLA), const),
            pl.BlockSpec((1, W_GL), const),
            pl.BlockSpec((SUB * PAIR, PAIR), const),
        ],
        out_specs=pl.BlockSpec((tb, W_GL), row),
        out_shape=jax.ShapeDtypeStruct((bsz * seq, W_GL), BF16),
        scratch_shapes=[
            pltpu.VMEM((N_GL_PAIRS, PAIR, PAIR), F32),
            scratch_f32, scratch_f32, scratch_f32, scratch_f32, scratch_f32,
            pltpu.VMEM((tb, SUB * PAIR), BF16),
            scratch_f32,
        ],
        compiler_params=pltpu.CompilerParams(
            dimension_semantics=("arbitrary", "arbitrary"), vmem_limit_bytes=VMEM_LIMIT),
        name="hgrn_gla_mixer",
    )(lb_l, zh, zg, alpha_up_p, alpha_b, norm_g, _gl_indicator())


def _rwkv_kernel(zr_ref, mu_ref, w0_ref, a0_ref, wa_ref, gup_ref, kk_ref, ka_ref, rk_ref, gng_ref, gnb_ref,
                 out_ref, st_ref, carry_s, r_s, lw_s, k_s, v_s, kk_s, a_s, y_s, hg_s):
    tb = zr_ref.shape[0]
    n_chunks = tb // CHUNK
    w3 = W_RWKV

    @pl.when(pl.program_id(1) == 0)
    def _():
        st_ref[...] = jnp.zeros_like(st_ref)
        carry_s[...] = jnp.zeros_like(carry_s)

    z = zr_ref[...]
    z_prev = pltpu.roll(z, 1, axis=0)
    z_prev = jnp.where(_iota(z.shape, 0) == 0, carry_s[...], z_prev)
    carry_s[...] = z[tb - 1:tb, :]
    z = z + (z_prev - z) * mu_ref[...]

    r = z[:, 0:w3]
    k = z[:, w3:2 * w3]
    v = z[:, 2 * w3:3 * w3]
    h_wa = z[:, 3 * w3:3 * w3 + LANES]
    h_wa = jnp.where(_iota(h_wa.shape, 1) < RWKV_DECAY_RANK, jnp.tanh(h_wa), h_wa)
    wa = _dot3(h_wa, wa_ref[...])
    w_log = -_softplus(-(w0_ref[...] + wa[:, 0:w3])) - 0.5
    a = _sigmoid(a0_ref[...] + wa[:, w3:2 * w3])
    seg = _seg_mean_matrix(w3)
    kk = k * kk_ref[...]
    kk = kk / jnp.maximum(jnp.sqrt(_dot_exact_rhs(kk * kk, seg)), 1e-12)
    k2 = k * (1.0 + (a - 1.0) * ka_ref[...])
    r_s[...] = r
    lw_s[...] = -jnp.exp(w_log)
    k_s[...] = k2
    v_s[...] = v
    kk_s[...] = kk
    a_s[...] = a
    hg_s[...] = _sigmoid(z[:, 3 * w3 + LANES:3 * w3 + LANES + RWKV_GATE_RANK])

    masks = _pair_masks()
    tril = _tril_bf16()
    eye = jnp.where(masks["eye"], 1.0, 0.0)

    def chunk_body(c, carry):
        rows = pl.ds(pl.multiple_of(c * CHUNK, CHUNK), CHUNK)
        g_all = _dot_exact_lhs(tril, lw_s[rows, :])
        for p in range(N_RW_PAIRS):
            lanes = slice(p * PAIR, (p + 1) * PAIR)
            g = g_all[:, lanes]
            lw = lw_s[rows, lanes]
            rr = r_s[rows, lanes]
            kc = k_s[rows, lanes]
            vc = v_s[rows, lanes]
            kkc = kk_s[rows, lanes]
            ka = kkc * a_s[rows, lanes]
            g_end = g[CHUNK - 1:CHUNK, :]
            e_neg = jnp.exp(-g)
            e_end = jnp.exp(g_end - g)
            at = -kkc * jnp.exp(g - lw)
            rt = rr * jnp.exp(g)
            bt = ka * e_neg
            kt = kc * e_neg
            gram = _dot3(jnp.concatenate([at, rt], axis=0),
                         jnp.concatenate([_stack_heads(bt), _stack_heads(kt)], axis=0), NT)
            l_mat = jnp.where(masks["strict"], gram[0:CHUNK, 0:PAIR], 0.0)
            m_k = jnp.where(masks["strict"], gram[0:CHUNK, PAIR:2 * PAIR], 0.0)
            n_b = jnp.where(masks["incl"], gram[CHUNK:2 * CHUNK, 0:PAIR], 0.0)
            n_k = jnp.where(masks["incl"], gram[CHUNK:2 * CHUNK, PAIR:2 * PAIR], 0.0)
            t_inv = eye + l_mat
            l_pow = l_mat
            for _ in range(5):
                l_pow = _dot3(l_pow, _stack_heads(l_pow))
                t_inv = t_inv + _dot3(t_inv, _stack_heads(l_pow))
            mv = _dot3(m_k, _stack_heads(vc))
            wu = _dot3(t_inv, jnp.concatenate([_stack_heads(at), _stack_heads(mv)], axis=1))
            st = st_ref[p]
            u = _dot3(wu[:, 0:PAIR], st, NT) + wu[:, PAIR:2 * PAIR]
            y = _dot3(rt, st, NT) + _dot3(n_b, _stack_heads(u)) + _dot3(n_k, _stack_heads(vc))
            upd = _dot3(jnp.concatenate([u, vc], axis=0),
                        jnp.concatenate([ka * e_end, kc * e_end], axis=0), TN)
            st_ref[p] = st * jnp.exp(g_end) + jnp.where(masks["bd"], upd, 0.0)
            y_s[rows, lanes] = y
        return carry

    lax.fori_loop(0, n_chunks, chunk_body, 0)

    y = y_s[...]
    mean = _dot_exact_rhs(y, seg) * (1.0 / HEAD_DIM)
    d = y - mean
    var = _dot_exact_rhs(d * d, seg) * (1.0 / HEAD_DIM)
    yn = d * lax.rsqrt(var + RWKV_GN_EPS) * gng_ref[...] + gnb_ref[...]
    bonus = _dot_exact_rhs(r_s[...] * k_s[...] * rk_ref[...], seg) * v_s[...]
    gate = _dot3(hg_s[...], gup_ref[...])
    out_ref[...] = ((yn + bonus) * gate).astype(out_ref.dtype)


def _rwkv_mixer(zr, mu, w0, a0, wa_blk, g_up, k_k, k_a, r_k, gn_g, gn_b, bsz, seq, tb):
    steps = seq // tb
    row = lambda b, i: (b * steps + i, 0)
    const = lambda b, i: (0, 0)
    vec = pl.BlockSpec((1, W_RWKV), const)
    slab = pltpu.VMEM((tb, W_RWKV), F32)
    return pl.pallas_call(
        _rwkv_kernel,
        grid=(bsz, steps),
        in_specs=[
            pl.BlockSpec((tb, N_RWKV_COLS), row),
            pl.BlockSpec((1, N_RWKV_COLS), const),
            vec, vec,
            pl.BlockSpec((LANES, 2 * W_RWKV), const),
            pl.BlockSpec((RWKV_GATE_RANK, W_RWKV), const),
            vec, vec, vec, vec, vec,
        ],
        out_specs=pl.BlockSpec((tb, W_RWKV), row),
        out_shape=jax.ShapeDtypeStruct((bsz * seq, W_RWKV), BF16),
        scratch_shapes=[
            pltpu.VMEM((N_RW_PAIRS, PAIR, PAIR), F32),
            pltpu.VMEM((1, N_RWKV_COLS), F32),
            slab, slab, slab, slab, slab, slab, slab,
            pltpu.VMEM((tb, RWKV_GATE_RANK), F32),
        ],
        compiler_params=pltpu.CompilerParams(
            dimension_semantics=("arbitrary", "arbitrary"), vmem_limit_bytes=VMEM_LIMIT),
        name="rwkv7_mixer",
    )(zr, mu, w0, a0, wa_blk, g_up, k_k, k_a, r_k, gn_g, gn_b)


def _post_kernel(alpha, ff_chunk, x_ref, ogl_ref, orw_ref, mod_ref, wout_ref, ln1g_ref, ln1b_ref,
                 wup_ref, wdn_ref, ln2g_ref, ln2b_ref, out_ref):
    m = mod_ref[0]
    gate1, shift2, scale2, gate2 = m[2:3, :], m[3:4, :], m[4:5, :], m[5:6, :]
    o = _mm(ogl_ref[...], wout_ref[0:W_GL, :]) + _mm(orw_ref[...], wout_ref[W_GL:W_GL + W_RWKV, :])
    x1 = _layer_norm(alpha * x_ref[...] + (1.0 + gate1) * o, ln1g_ref[...], ln1b_ref[...])
    h = (x1 * (1.0 + scale2) + shift2).astype(BF16)
    d_ff = wup_ref.shape[1]
    acc = jnp.zeros(x1.shape, F32)
    for j in range(d_ff // ff_chunk):
        cols = slice(j * ff_chunk, (j + 1) * ff_chunk)
        u = jnp.maximum(_mm(h, wup_ref[:, cols]), 0.0)
        acc = acc + _mm((u * u).astype(BF16), wdn_ref[cols, :])
    out_ref[...] = _layer_norm(alpha * x1 + (1.0 + gate2) * acc, ln2g_ref[...], ln2b_ref[...])


def _post_mixer(x2, ogl, orw, mod_l, w_out, ln1_g, ln1_b, w_up, w_dn, ln2_g, ln2_b, alpha, seq, tm):
    m_rows, d = x2.shape
    d_ff = w_up.shape[1]
    steps_per_batch = seq // tm
    row = lambda i: (i, 0)
    const = lambda i: (0, 0)
    vec = pl.BlockSpec((1, d), const)
    resident = lambda shape: pl.BlockSpec(shape, const, pipeline_mode=pl.Buffered(1))
    return pl.pallas_call(
        functools.partial(_post_kernel, alpha, 512),
        grid=(m_rows // tm,),
        in_specs=[
            pl.BlockSpec((tm, d), row),
            pl.BlockSpec((tm, W_GL), row),
            pl.BlockSpec((tm, W_RWKV), row),
            pl.BlockSpec((1, 6, d), lambda i: (i // steps_per_batch, 0, 0)),
            resident((W_GL + W_RWKV, d)),
            vec, vec,
            resident((d, d_ff)),
            resident((d_ff, d)),
            vec, vec,
        ],
        out_specs=pl.BlockSpec((tm, d), row),
        out_shape=jax.ShapeDtypeStruct((m_rows, d), F32),
        compiler_params=pltpu.CompilerParams(
            dimension_semantics=("arbitrary",), vmem_limit_bytes=VMEM_LIMIT),
        name="outproj_mlp",
    )(x2, ogl, orw, mod_l, w_out, ln1_g, ln1_b, w_up, w_dn, ln2_g, ln2_b)


def kernel(x, c, hgrn_lb_logits, ada_w, ada_b, w_in, hgrn_norm_g, gla_alpha_up, gla_alpha_b, gla_norm_g,
           rwkv_mu, rwkv_w0, rwkv_w_up, rwkv_a0, rwkv_a_up, rwkv_g_up, rwkv_k_k, rwkv_k_a, rwkv_r_k,
           rwkv_gn_g, rwkv_gn_b, w_out, ln1_g, ln1_b, mlp_w_up, mlp_w_down, ln2_g, ln2_b):
    bsz, seq, d = x.shape
    depth = w_in.shape[0]
    alpha = (2.0 * depth) ** 0.25
    tm = min(512, seq)
    tb = min(512, seq)
    assert seq % tm == 0 and seq % tb == 0 and tb % CHUNK == 0
    assert w_in.shape[2] == N_HGRN_COLS + N_GLA_COLS + N_RWKV_COLS

    mod = _modulation(c, ada_w, ada_b).reshape(depth, bsz, 6, d)
    lbs = _lower_bounds(hgrn_lb_logits)

    gla_end = N_HGRN_COLS + N_GLA_COLS
    w_in_p = jnp.concatenate(
        [w_in[:, :, :gla_end], jnp.zeros((depth, d, LANES - GLA_GATE_RANK), w_in.dtype), w_in[:, :, gla_end:]],
        axis=2).astype(BF16)
    alpha_up_p = jnp.concatenate(
        [gla_alpha_up, jnp.zeros((depth, LANES - GLA_GATE_RANK, W_GLA), gla_alpha_up.dtype)], axis=1)
    zeros_r = jnp.zeros((depth, RWKV_DECAY_RANK, W_RWKV), rwkv_w_up.dtype)
    wa_blk = jnp.concatenate(
        [jnp.concatenate([rwkv_w_up, zeros_r], axis=2), jnp.concatenate([zeros_r, rwkv_a_up], axis=2)], axis=1)
    norm_g = jnp.concatenate([hgrn_norm_g, gla_norm_g], axis=1)
    w_out_b = w_out.astype(BF16)
    w_up_b = mlp_w_up.astype(BF16)
    w_dn_b = mlp_w_down.astype(BF16)
    vec = lambda a, l: a[l].reshape(1, -1)

    x2 = x.reshape(bsz * seq, d)
    for l in range(depth):
        zh, zg, zr = _in_projection(x2, mod[l], w_in_p[l], seq, tm)
        ogl = _gl_mixer(zh, zg, vec(lbs, l), alpha_up_p[l], vec(gla_alpha_b, l), vec(norm_g, l), bsz, seq, tb)
        orw = _rwkv_mixer(zr, vec(rwkv_mu, l), vec(rwkv_w0, l), vec(rwkv_a0, l), wa_blk[l], rwkv_g_up[l],
                          vec(rwkv_k_k, l), vec(rwkv_k_a, l), vec(rwkv_r_k, l), vec(rwkv_gn_g, l),
                          vec(rwkv_gn_b, l), bsz, seq, tb)
        x2 = _post_mixer(x2, ogl, orw, mod[l], w_out_b[l], vec(ln1_g, l), vec(ln1_b, l), w_up_b[l], w_dn_b[l],
                         vec(ln2_g, l), vec(ln2_b, l), alpha, seq, tm)
    return x2.reshape(bsz, seq, d)
```

```python
import functools

import numpy as np
import jax
import jax.numpy as jnp
from jax import lax
from jax.experimental import pallas as pl
from jax.experimental.pallas import tpu as pltpu

F32 = jnp.float32
BF16 = jnp.bfloat16

HEAD_DIM = 64
H_HGRN, H_GLA, H_RWKV = 4, 6, 6
W_HGRN, W_GLA, W_RWKV = H_HGRN * HEAD_DIM, H_GLA * HEAD_DIM, H_RWKV * HEAD_DIM
GLA_GATE_RANK = 16
GLA_GATE_NORMALIZER = 16.0
RWKV_DECAY_RANK, RWKV_ICLR_RANK, RWKV_GATE_RANK = 64, 64, 128
RWKV_GN_EPS = 64e-5
N_HGRN_COLS = 4 * W_HGRN
N_GLA_COLS = 4 * W_GLA + GLA_GATE_RANK
N_RWKV_COLS = 3 * W_RWKV + RWKV_DECAY_RANK + RWKV_ICLR_RANK + RWKV_GATE_RANK
CHUNK = 64
LN_EPS = 1e-5
RMS_EPS = 1e-5
F_MIN = 1e-30

LANES = 128
PAIR = 2 * HEAD_DIM
SUB = 16
N_SUB = CHUNK // SUB
A_GROUP = 8
VMEM_LIMIT = 56 * 1024 * 1024

W_GL = W_HGRN + W_GLA
N_GL_PAIRS = W_GL // PAIR
N_RW_PAIRS = W_RWKV // PAIR
GLA_Z = 4 * W_GLA + LANES
Z_COLS = N_HGRN_COLS + GLA_Z + N_RWKV_COLS

NN = (((1,), (0,)), ((), ()))
NT = (((1,), (1,)), ((), ()))
TN = (((0,), (0,)), ((), ()))


def _mm(a, b, dims=NN):
    return lax.dot_general(a, b, dims, preferred_element_type=F32)


def _split2(x):
    hi = x.astype(BF16)
    lo = (x - hi.astype(F32)).astype(BF16)
    return hi, lo


def _split3(x):
    hi = x.astype(BF16)
    r = x - hi.astype(F32)
    mid = r.astype(BF16)
    lo = (r - mid.astype(F32)).astype(BF16)
    return hi, mid, lo


def _dot1(a, b, dims=NN):
    return _mm(a.astype(BF16), b.astype(BF16), dims)


def _dot3(a, b, dims=NN):
    ah, al = _split2(a)
    bh, bl = _split2(b)
    return _mm(ah, bh, dims) + (_mm(ah, bl, dims) + _mm(al, bh, dims))


def _dot_exact_lhs(a_bf16, b, dims=NN):
    bh, bm, bl = _split3(b)
    return _mm(a_bf16, bh, dims) + (_mm(a_bf16, bm, dims) + _mm(a_bf16, bl, dims))


def _dot_exact_rhs(a, b_bf16, dims=NN):
    ah, am, al = _split3(a)
    return _mm(ah, b_bf16, dims) + (_mm(am, b_bf16, dims) + _mm(al, b_bf16, dims))


def _sigmoid(x):
    return 1.0 / (1.0 + jnp.exp(-x))


def _silu(x):
    return x * _sigmoid(x)


def _softplus(x):
    return jnp.maximum(x, 0.0) + jnp.log1p(jnp.exp(-jnp.abs(x)))


def _log_sigmoid(x):
    return -_softplus(-x)


def _layer_norm(y, g, b):
    mu = jnp.mean(y, axis=-1, keepdims=True)
    d = y - mu
    var = jnp.mean(d * d, axis=-1, keepdims=True)
    return d * lax.rsqrt(var + LN_EPS) * g + b


def _iota(shape, axis):
    return lax.broadcasted_iota(jnp.int32, shape, axis)


def _stack_heads(x):
    lane = _iota(x.shape, 1)
    return jnp.concatenate([jnp.where(lane < HEAD_DIM, x, 0.0), jnp.where(lane >= HEAD_DIM, x, 0.0)], axis=0)


def _pair_masks():
    t = _iota((CHUNK, PAIR), 0)
    s = _iota((CHUNK, PAIR), 1) % HEAD_DIM
    r2 = _iota((PAIR, PAIR), 0) // HEAD_DIM
    c2 = _iota((PAIR, PAIR), 1) // HEAD_DIM
    return dict(
        strict=s < t,
        incl=s <= t,
        eye=s == t,
        diag=(s // SUB == t // SUB) & (s <= t),
        off=(s // SUB) < (t // SUB),
        bd=r2 == c2,
    )


def _tril_bf16():
    return jnp.where(_iota((CHUNK, CHUNK), 1) <= _iota((CHUNK, CHUNK), 0), 1.0, 0.0).astype(BF16)


def _seg_mean_matrix(width):
    r = _iota((width, width), 0) // HEAD_DIM
    c = _iota((width, width), 1) // HEAD_DIM
    return jnp.where(r == c, 1.0, 0.0).astype(BF16)


def _mod_kernel(c_ref, w_ref, b_ref, o_ref):
    c = _silu(c_ref[...])
    o_ref[0] = _dot3(c, w_ref[0]) + b_ref[0]


def _modulation(c, ada_w, ada_b):
    depth, d, n = ada_w.shape
    bsz = c.shape[0]
    tn = 1536
    return pl.pallas_call(
        _mod_kernel,
        grid=(depth, n // tn),
        in_specs=[
            pl.BlockSpec((bsz, d), lambda l, j: (0, 0)),
            pl.BlockSpec((1, d, tn), lambda l, j: (l, 0, j)),
            pl.BlockSpec((1, 1, tn), lambda l, j: (l, 0, j)),
        ],
        out_specs=pl.BlockSpec((1, bsz, tn), lambda l, j: (l, 0, j)),
        out_shape=jax.ShapeDtypeStruct((depth, bsz, n), F32),
        compiler_params=pltpu.CompilerParams(
            dimension_semantics=("arbitrary", "arbitrary"), vmem_limit_bytes=VMEM_LIMIT),
        name="adaln_modulation",
    )(c, ada_w, ada_b.reshape(depth, 1, n))


def _lower_bound_kernel(x_ref, o_ref):
    depth = x_ref.shape[0]
    rows = [x_ref[l:l + 1, :] for l in range(depth)]
    m = functools.reduce(jnp.maximum, rows)
    e = [jnp.exp(r - m) for r in rows]
    tot = functools.reduce(lambda a, b: a + b, e)
    p = [ei / tot for ei in e]
    acc = jnp.zeros_like(p[0])
    for l in range(depth):
        acc = acc + p[l]
        o_ref[l:l + 1, :] = acc - p[0]


def _lower_bounds(logits):
    return pl.pallas_call(
        _lower_bound_kernel,
        out_shape=jax.ShapeDtypeStruct(logits.shape, F32),
        name="hgrn_lower_bounds",
    )(logits.astype(F32))


def _inproj_kernel(x_ref, mod_ref, w_ref, zh_ref, zg_ref, zr_ref):
    m = mod_ref[0]
    h = (x_ref[...] * (1.0 + m[1:2, :]) + m[0:1, :]).astype(BF16)
    zh_ref[...] = _mm(h, w_ref[:, 0:N_HGRN_COLS])
    zg_ref[...] = _mm(h, w_ref[:, N_HGRN_COLS:N_HGRN_COLS + GLA_Z])
    zr_ref[...] = _mm(h, w_ref[:, N_HGRN_COLS + GLA_Z:Z_COLS])


def _in_projection(x2, mod_l, w_in_p, seq, tm):
    m_rows, d = x2.shape
    steps_per_batch = seq // tm
    row = lambda i: (i, 0)
    const = lambda i: (0, 0)
    return pl.pallas_call(
        _inproj_kernel,
        grid=(m_rows // tm,),
        in_specs=[
            pl.BlockSpec((tm, d), row),
            pl.BlockSpec((1, 6, d), lambda i: (i // steps_per_batch, 0, 0)),
            pl.BlockSpec((d, Z_COLS), const, pipeline_mode=pl.Buffered(1)),
        ],
        out_specs=[
            pl.BlockSpec((tm, N_HGRN_COLS), row),
            pl.BlockSpec((tm, GLA_Z), row),
            pl.BlockSpec((tm, N_RWKV_COLS), row),
        ],
        out_shape=[
            jax.ShapeDtypeStruct((m_rows, N_HGRN_COLS), F32),
            jax.ShapeDtypeStruct((m_rows, GLA_Z), F32),
            jax.ShapeDtypeStruct((m_rows, N_RWKV_COLS), F32),
        ],
        compiler_params=pltpu.CompilerParams(
            dimension_semantics=("arbitrary",), vmem_limit_bytes=VMEM_LIMIT),
        name="in_projection",
    )(x2, mod_l, w_in_p)


def _gl_kernel(lb_ref, zh_ref, zg_ref, au_ref, ab_ref, ng_ref, ind_ref, out_ref,
               st_ref, q_s, k_s, v_s, b_s, ad_s, e_s, o_s):
    tb = zh_ref.shape[0]
    n_chunks = tb // CHUNK

    @pl.when(pl.program_id(1) == 0)
    def _():
        st_ref[...] = jnp.zeros_like(st_ref)

    lb = lb_ref[...]
    zf = zh_ref[:, W_HGRN:2 * W_HGRN]
    f = lb + (1.0 - lb) * _sigmoid(zf)
    b_s[:, 0:W_HGRN] = jnp.log(jnp.maximum(f, F_MIN))
    k_s[:, 0:W_HGRN] = (1.0 - lb) * _sigmoid(-zf)
    q_s[:, 0:W_HGRN] = _silu(zh_ref[:, 0:W_HGRN]) * HEAD_DIM ** -0.5
    v_s[:, 0:W_HGRN] = zh_ref[:, 2 * W_HGRN:3 * W_HGRN]

    logit = _dot3(zg_ref[:, 4 * W_GLA:GLA_Z], au_ref[...]) + ab_ref[...]
    b_s[:, W_HGRN:W_GL] = _log_sigmoid(logit) / GLA_GATE_NORMALIZER
    q_s[:, W_HGRN:W_GL] = zg_ref[:, 0:W_GLA] * HEAD_DIM ** -0.5
    k_s[:, W_HGRN:W_GL] = zg_ref[:, W_GLA:2 * W_GLA]
    v_s[:, W_HGRN:W_GL] = zg_ref[:, 2 * W_GLA:3 * W_GLA]

    tril = _tril_bf16()

    def cum_body(c, carry):
        rows = pl.ds(pl.multiple_of(c * CHUNK, CHUNK), CHUNK)
        b_s[rows, :] = _dot_exact_lhs(tril, b_s[rows, :])
        return carry

    lax.fori_loop(0, n_chunks, cum_body, 0)

    for p in range(N_GL_PAIRS):
        lanes = slice(p * PAIR, (p + 1) * PAIR)
        q3 = q_s[:, lanes].reshape(tb // SUB, SUB, PAIR)
        k3 = k_s[:, lanes].reshape(tb // SUB, SUB, PAIR)
        b3 = b_s[:, lanes].reshape(tb // SUB, SUB, PAIR)
        for sg in range(SUB):
            dec = jnp.exp(jnp.minimum(b3 - b3[:, sg:sg + 1, :], 0.0))
            e = q3 * k3[:, sg:sg + 1, :] * dec
            e_s[:, sg * PAIR:(sg + 1) * PAIR] = e.reshape(tb, PAIR).astype(BF16)
        ad_s[:, lanes] = _mm(e_s[...], ind_ref[...])

    masks = _pair_masks()
    row_blk = _iota((CHUNK, PAIR), 0) // SUB

    def chunk_body(c, carry):
        rows = pl.ds(pl.multiple_of(c * CHUNK, CHUNK), CHUNK)
        for p in range(N_GL_PAIRS):
            lanes = slice(p * PAIR, (p + 1) * PAIR)
            q = q_s[rows, lanes]
            k = k_s[rows, lanes]
            v = v_s[rows, lanes]
            b = b_s[rows, lanes]
            b_end = b[CHUNK - 1:CHUNK, :]
            e_blk = jnp.concatenate(
                [jnp.broadcast_to(b[SUB * j + SUB - 1:SUB * j + SUB, :], (SUB, PAIR)) for j in range(N_SUB)], axis=0)
            k_hat = k * jnp.exp(e_blk - b)
            q_parts, k_parts = [], []
            for j in range(N_SUB - 1):
                e_j = b[SUB * j + SUB - 1:SUB * j + SUB, :]
                q_parts.append(q * jnp.exp(jnp.minimum(b - e_j, 0.0)))
                k_parts.append(_stack_heads(jnp.where(row_blk == j, k_hat, 0.0)))
            scores = _dot1(jnp.concatenate(q_parts, axis=1), jnp.concatenate(k_parts, axis=1), NT)
            a = jnp.where(masks["diag"], ad_s[rows, lanes], 0.0) + jnp.where(masks["off"], scores, 0.0)
            st = st_ref[p]
            o = _dot1(a, _stack_heads(v)) + _dot1(q * jnp.exp(b), st, NT)
            upd = _dot1(v, k * jnp.exp(b_end - b), TN)
            st_ref[p] = st * jnp.exp(b_end) + jnp.where(masks["bd"], upd, 0.0)
            o_s[rows, lanes] = o
        return carry

    lax.fori_loop(0, n_chunks, chunk_body, 0)

    o = o_s[...]
    ms = _dot_exact_rhs(o * o, _seg_mean_matrix(W_GL)) * (1.0 / HEAD_DIM)
    gate = jnp.concatenate([_silu(zh_ref[:, 3 * W_HGRN:4 * W_HGRN]), _silu(zg_ref[:, 3 * W_GLA:4 * W_GLA])], axis=1)
    out_ref[...] = (o * lax.rsqrt(ms + RMS_EPS) * ng_ref[...] * gate).astype(out_ref.dtype)


def _gl_indicator():
    r = np.arange(SUB * PAIR)
    c = np.arange(PAIR)
    sg = r // PAIR
    h = (r % PAIR) // HEAD_DIM
    ind = (h[:, None] == (c // HEAD_DIM)[None, :]) & (sg[:, None] == (c % SUB)[None, :])
    return jnp.asarray(ind, dtype=BF16)


def _gl_mixer(zh, zg, lb_l, alpha_up_p, alpha_b, norm_g, bsz, seq, tb):
    steps = seq // tb
    row = lambda b, i: (b * steps + i, 0)
    const = lambda b, i: (0, 0)
    scratch_f32 = pltpu.VMEM((tb, W_GL), F32)
    return pl.pallas_call(
        _gl_kernel,
        grid=(bsz, steps),
        in_specs=[
            pl.BlockSpec((1, W_HGRN), const),
            pl.BlockSpec((tb, N_HGRN_COLS), row),
            pl.BlockSpec((tb, GLA_Z), row),
            pl.BlockSpec((LANES, W_GLA), const),
            pl.BlockSpec((1, W_GLA), const),
            pl.BlockSpec((1, W_GL), const),
            pl.BlockSpec((SUB * PAIR, PAIR), const),
        ],
        out_specs=pl.BlockSpec((tb, W_GL), row),
        out_shape=jax.ShapeDtypeStruct((bsz * seq, W_GL), BF16),
        scratch_shapes=[
            pltpu.VMEM((N_GL_PAIRS, PAIR, PAIR), F32),
            scratch_f32, scratch_f32, scratch_f32, scratch_f32, scratch_f32,
            pltpu.VMEM((tb, SUB * PAIR), BF16),
            scratch_f32,
        ],
        compiler_params=pltpu.CompilerParams(
            dimension_semantics=("arbitrary", "arbitrary"), vmem_limit_bytes=VMEM_LIMIT),
        name="hgrn_gla_mixer",
    )(lb_l, zh, zg, alpha_up_p, alpha_b, norm_g, _gl_indicator())


def _rwkv_kernel(zr_ref, mu_ref, w0_ref, a0_ref, wa_ref, gup_ref, kk_ref, ka_ref, rk_ref, gng_ref, gnb_ref,
                 out_ref, st_ref, carry_s, r_s, lw_s, k_s, v_s, kk_s, a_s, y_s,
                 rt_s, bh_s, kh_s, nb_s, nk_s, w_s, u0_s, eend_s, hg_s):
    tb = zr_ref.shape[0]
    n_chunks = tb // CHUNK
    w3 = W_RWKV

    @pl.when(pl.program_id(1) == 0)
    def _():
        st_ref[...] = jnp.zeros_like(st_ref)
        carry_s[...] = jnp.zeros_like(carry_s)

    z = zr_ref[...]
    z_prev = pltpu.roll(z, 1, axis=0)
    z_prev = jnp.where(_iota(z.shape, 0) == 0, carry_s[...], z_prev)
    carry_s[...] = z[tb - 1:tb, :]
    z = z + (z_prev - z) * mu_ref[...]

    r = z[:, 0:w3]
    k = z[:, w3:2 * w3]
    v = z[:, 2 * w3:3 * w3]
    h_wa = z[:, 3 * w3:3 * w3 + LANES]
    h_wa = jnp.where(_iota(h_wa.shape, 1) < RWKV_DECAY_RANK, jnp.tanh(h_wa), h_wa)
    wa = _dot3(h_wa, wa_ref[...])
    w_log = -_softplus(-(w0_ref[...] + wa[:, 0:w3])) - 0.5
    a = _sigmoid(a0_ref[...] + wa[:, w3:2 * w3])
    seg = _seg_mean_matrix(w3)
    kk = k * kk_ref[...]
    kk = kk / jnp.maximum(jnp.sqrt(_dot_exact_rhs(kk * kk, seg)), 1e-12)
    k2 = k * (1.0 + (a - 1.0) * ka_ref[...])
    r_s[...] = r
    lw_s[...] = -jnp.exp(w_log)
    k_s[...] = k2
    v_s[...] = v
    kk_s[...] = kk
    a_s[...] = a
    hg_s[...] = _sigmoid(z[:, 3 * w3 + LANES:3 * w3 + LANES + RWKV_GATE_RANK])

    masks = _pair_masks()
    tril = _tril_bf16()
    eye = jnp.where(masks["eye"], 1.0, 0.0)

    def phase_a(i, carry):
        units = []
        for cc in range(A_GROUP):
            c = i * A_GROUP + cc
            rows = pl.ds(pl.multiple_of(c * CHUNK, CHUNK), CHUNK)
            g_all = _dot_exact_lhs(tril, lw_s[rows, :])
            for p in range(N_RW_PAIRS):
                lanes = slice(p * PAIR, (p + 1) * PAIR)
                units.append(dict(c=c, rows=rows, lanes=lanes, g=g_all[:, lanes]))
        for un in units:
            rows, lanes, g = un["rows"], un["lanes"], un["g"]
            kc = k_s[rows, lanes]
            kkc = kk_s[rows, lanes]
            ka = kkc * a_s[rows, lanes]
            g_end = g[CHUNK - 1:CHUNK, :]
            e_neg = jnp.exp(-g)
            e_end = jnp.exp(g_end - g)
            un["at"] = -kkc * jnp.exp(g - lw_s[rows, lanes])
            rt = r_s[rows, lanes] * jnp.exp(g)
            un["rt"] = rt
            un["bt"] = ka * e_neg
            un["kt"] = kc * e_neg
            rt_s[rows, lanes] = rt
            bh_s[rows, lanes] = ka * e_end
            kh_s[rows, lanes] = kc * e_end
            eend_s[pl.ds(pl.multiple_of(un["c"] * 8, 8), 8), lanes] = jnp.broadcast_to(jnp.exp(g_end), (8, PAIR))
        for un in units:
            gram = _dot1(jnp.concatenate([un["at"], un["rt"]], axis=0),
                         jnp.concatenate([_stack_heads(un["bt"]), _stack_heads(un["kt"])], axis=0), NT)
            un["l"] = jnp.where(masks["strict"], gram[0:CHUNK, 0:PAIR], 0.0)
            un["mk"] = jnp.where(masks["strict"], gram[0:CHUNK, PAIR:2 * PAIR], 0.0)
            nb_s[un["rows"], un["lanes"]] = jnp.where(masks["incl"], gram[CHUNK:2 * CHUNK, 0:PAIR], 0.0)
            nk_s[un["rows"], un["lanes"]] = jnp.where(masks["incl"], gram[CHUNK:2 * CHUNK, PAIR:2 * PAIR], 0.0)
        for un in units:
            un["t"] = eye + un["l"]
            un["lp"] = un["l"]
        for _ in range(5):
            for un in units:
                un["lp"] = _dot1(un["lp"], _stack_heads(un["lp"]))
            for un in units:
                un["t"] = un["t"] + _dot1(un["t"], _stack_heads(un["lp"]))
        for un in units:
            un["mv"] = _dot1(un["mk"], _stack_heads(v_s[un["rows"], un["lanes"]]))
        for un in units:
            wu = _dot1(un["t"], jnp.concatenate([_stack_heads(un["at"]), _stack_heads(un["mv"])], axis=1))
            w_s[un["rows"], un["lanes"]] = wu[:, 0:PAIR]
            u0_s[un["rows"], un["lanes"]] = wu[:, PAIR:2 * PAIR]
        return carry

    lax.fori_loop(0, n_chunks // A_GROUP, phase_a, 0)

    def phase_b(c, carry):
        rows = pl.ds(pl.multiple_of(c * CHUNK, CHUNK), CHUNK)
        pairs = [slice(p * PAIR, (p + 1) * PAIR) for p in range(N_RW_PAIRS)]
        sts = [st_ref[p] for p in range(N_RW_PAIRS)]
        vs = [v_s[rows, lanes] for lanes in pairs]
        us = [_dot1(w_s[rows, lanes], st, NT) + u0_s[rows, lanes] for lanes, st in zip(pairs, sts)]
        upds = [_dot1(jnp.concatenate([u, vc], axis=0),
                      jnp.concatenate([bh_s[rows, lanes], kh_s[rows, lanes]], axis=0), TN)
                for lanes, u, vc in zip(pairs, us, vs)]
        for p, (lanes, st, upd) in enumerate(zip(pairs, sts, upds)):
            e_end = eend_s[pl.ds(pl.multiple_of(c * 8, 8), 8), lanes][0:1, :]
            st_ref[p] = st * e_end + jnp.where(masks["bd"], upd, 0.0)
        for lanes, st, u, vc in zip(pairs, sts, us, vs):
            y_s[rows, lanes] = (
                _dot1(rt_s[rows, lanes], st, NT)
                + _dot1(jnp.concatenate([nb_s[rows, lanes], nk_s[rows, lanes]], axis=1),
                        jnp.concatenate([_stack_heads(u), _stack_heads(vc)], axis=0)))
        return carry

    lax.fori_loop(0, n_chunks, phase_b, 0)

    y = y_s[...]
    mean = _dot_exact_rhs(y, seg) * (1.0 / HEAD_DIM)
    d = y - mean
    var = _dot_exact_rhs(d * d, seg) * (1.0 / HEAD_DIM)
    yn = d * lax.rsqrt(var + RWKV_GN_EPS) * gng_ref[...] + gnb_ref[...]
    bonus = _dot_exact_rhs(r_s[...] * k_s[...] * rk_ref[...], seg) * v_s[...]
    gate = _dot1(hg_s[...], gup_ref[...])
    out_ref[...] = ((yn + bonus) * gate).astype(out_ref.dtype)


def _rwkv_mixer(zr, mu, w0, a0, wa_blk, g_up, k_k, k_a, r_k, gn_g, gn_b, bsz, seq, tb):
    steps = seq // tb
    row = lambda b, i: (b * steps + i, 0)
    const = lambda b, i: (0, 0)
    vec = pl.BlockSpec((1, W_RWKV), const)
    slab = pltpu.VMEM((tb, W_RWKV), F32)
    return pl.pallas_call(
        _rwkv_kernel,
        grid=(bsz, steps),
        in_specs=[
            pl.BlockSpec((tb, N_RWKV_COLS), row),
            pl.BlockSpec((1, N_RWKV_COLS), const),
            vec, vec,
            pl.BlockSpec((LANES, 2 * W_RWKV), const),
            pl.BlockSpec((RWKV_GATE_RANK, W_RWKV), const),
            vec, vec, vec, vec, vec,
        ],
        out_specs=pl.BlockSpec((tb, W_RWKV), row),
        out_shape=jax.ShapeDtypeStruct((bsz * seq, W_RWKV), BF16),
        scratch_shapes=[
            pltpu.VMEM((N_RW_PAIRS, PAIR, PAIR), F32),
            pltpu.VMEM((1, N_RWKV_COLS), F32),
            slab, slab, slab, slab, slab, slab, slab,
            slab, slab, slab, slab, slab, slab, slab,
            pltpu.VMEM((tb // CHUNK * 8, W_RWKV), F32),
            pltpu.VMEM((tb, RWKV_GATE_RANK), F32),
        ],
        compiler_params=pltpu.CompilerParams(
            dimension_semantics=("arbitrary", "arbitrary"), vmem_limit_bytes=VMEM_LIMIT),
        name="rwkv7_mixer",
    )(zr, mu, w0, a0, wa_blk, g_up, k_k, k_a, r_k, gn_g, gn_b)


def _post_kernel(alpha, ff_chunk, x_ref, ogl_ref, orw_ref, mod_ref, wout_ref, ln1g_ref, ln1b_ref,
                 wup_ref, wdn_ref, ln2g_ref, ln2b_ref, out_ref):
    m = mod_ref[0]
    gate1, shift2, scale2, gate2 = m[2:3, :], m[3:4, :], m[4:5, :], m[5:6, :]
    o = _mm(ogl_ref[...], wout_ref[0:W_GL, :]) + _mm(orw_ref[...], wout_ref[W_GL:W_GL + W_RWKV, :])
    x1 = _layer_norm(alpha * x_ref[...] + (1.0 + gate1) * o, ln1g_ref[...], ln1b_ref[...])
    h = (x1 * (1.0 + scale2) + shift2).astype(BF16)
    d_ff = wup_ref.shape[1]
    acc = jnp.zeros(x1.shape, F32)
    for j in range(d_ff // ff_chunk):
        cols = slice(j * ff_chunk, (j + 1) * ff_chunk)
        u = jnp.maximum(_mm(h, wup_ref[:, cols]), 0.0)
        acc = acc + _mm((u * u).astype(BF16), wdn_ref[cols, :])
    out_ref[...] = _layer_norm(alpha * x1 + (1.0 + gate2) * acc, ln2g_ref[...], ln2b_ref[...])


def _post_mixer(x2, ogl, orw, mod_l, w_out, ln1_g, ln1_b, w_up, w_dn, ln2_g, ln2_b, alpha, seq, tm):
    m_rows, d = x2.shape
    d_ff = w_up.shape[1]
    steps_per_batch = seq // tm
    row = lambda i: (i, 0)
    const = lambda i: (0, 0)
    vec = pl.BlockSpec((1, d), const)
    resident = lambda shape: pl.BlockSpec(shape, const, pipeline_mode=pl.Buffered(1))
    return pl.pallas_call(
        functools.partial(_post_kernel, alpha, 512),
        grid=(m_rows // tm,),
        in_specs=[
            pl.BlockSpec((tm, d), row),
            pl.BlockSpec((tm, W_GL), row),
            pl.BlockSpec((tm, W_RWKV), row),
            pl.BlockSpec((1, 6, d), lambda i: (i // steps_per_batch, 0, 0)),
            resident((W_GL + W_RWKV, d)),
            vec, vec,
            resident((d, d_ff)),
            resident((d_ff, d)),
            vec, vec,
        ],
        out_specs=pl.BlockSpec((tm, d), row),
        out_shape=jax.ShapeDtypeStruct((m_rows, d), F32),
        compiler_params=pltpu.CompilerParams(
            dimension_semantics=("arbitrary",), vmem_limit_bytes=VMEM_LIMIT),
        name="outproj_mlp",
    )(x2, ogl, orw, mod_l, w_out, ln1_g, ln1_b, w_up, w_dn, ln2_g, ln2_b)


def kernel(x, c, hgrn_lb_logits, ada_w, ada_b, w_in, hgrn_norm_g, gla_alpha_up, gla_alpha_b, gla_norm_g,
           rwkv_mu, rwkv_w0, rwkv_w_up, rwkv_a0, rwkv_a_up, rwkv_g_up, rwkv_k_k, rwkv_k_a, rwkv_r_k,
           rwkv_gn_g, rwkv_gn_b, w_out, ln1_g, ln1_b, mlp_w_up, mlp_w_down, ln2_g, ln2_b):
    bsz, seq, d = x.shape
    depth = w_in.shape[0]
    alpha = (2.0 * depth) ** 0.25
    tm = min(512, seq)
    tb = min(512, seq)
    assert seq % tm == 0 and seq % tb == 0 and tb % CHUNK == 0
    assert w_in.shape[2] == N_HGRN_COLS + N_GLA_COLS + N_RWKV_COLS

    mod = _modulation(c, ada_w, ada_b).reshape(depth, bsz, 6, d)
    lbs = _lower_bounds(hgrn_lb_logits)

    gla_end = N_HGRN_COLS + N_GLA_COLS
    w_in_p = jnp.concatenate(
        [w_in[:, :, :gla_end], jnp.zeros((depth, d, LANES - GLA_GATE_RANK), w_in.dtype), w_in[:, :, gla_end:]],
        axis=2).astype(BF16)
    alpha_up_p = jnp.concatenate(
        [gla_alpha_up, jnp.zeros((depth, LANES - GLA_GATE_RANK, W_GLA), gla_alpha_up.dtype)], axis=1)
    zeros_r = jnp.zeros((depth, RWKV_DECAY_RANK, W_RWKV), rwkv_w_up.dtype)
    wa_blk = jnp.concatenate(
        [jnp.concatenate([rwkv_w_up, zeros_r], axis=2), jnp.concatenate([zeros_r, rwkv_a_up], axis=2)], axis=1)
    norm_g = jnp.concatenate([hgrn_norm_g, gla_norm_g], axis=1)
    w_out_b = w_out.astype(BF16)
    w_up_b = mlp_w_up.astype(BF16)
    w_dn_b = mlp_w_down.astype(BF16)
    vec = lambda a, l: a[l].reshape(1, -1)

    x2 = x.reshape(bsz * seq, d)
    for l in range(depth):
        zh, zg, zr = _in_projection(x2, mod[l], w_in_p[l], seq, tm)
        ogl = _gl_mixer(zh, zg, vec(lbs, l), alpha_up_p[l], vec(gla_alpha_b, l), vec(norm_g, l), bsz, seq, tb)
        orw = _rwkv_mixer(zr, vec(rwkv_mu, l), vec(rwkv_w0, l), vec(rwkv_a0, l), wa_blk[l], rwkv_g_up[l],
                          vec(rwkv_k_k, l), vec(rwkv_k_a, l), vec(rwkv_r_k, l), vec(rwkv_gn_g, l),
                          vec(rwkv_gn_b, l), bsz, seq, tb)
        x2 = _post_mixer(x2, ogl, orw, mod[l], w_out_b[l], vec(ln1_g, l), vec(ln1_b, l), w_up_b[l], w_dn_b[l],
                         vec(ln2_g, l), vec(ln2_b, l), alpha, seq, tm)
    return x2.reshape(bsz, seq, d)
```

```python
import functools

import numpy as np
import jax
import jax.numpy as jnp
from jax import lax
from jax.experimental import pallas as pl
from jax.experimental.pallas import tpu as pltpu

F32 = jnp.float32
BF16 = jnp.bfloat16

HEAD_DIM = 64
H_HGRN, H_GLA, H_RWKV = 4, 6, 6
W_HGRN, W_GLA, W_RWKV = H_HGRN * HEAD_DIM, H_GLA * HEAD_DIM, H_RWKV * HEAD_DIM
GLA_GATE_RANK = 16
GLA_GATE_NORMALIZER = 16.0
RWKV_DECAY_RANK, RWKV_ICLR_RANK, RWKV_GATE_RANK = 64, 64, 128
RWKV_GN_EPS = 64e-5
N_HGRN_COLS = 4 * W_HGRN
N_GLA_COLS = 4 * W_GLA + GLA_GATE_RANK
N_RWKV_COLS = 3 * W_RWKV + RWKV_DECAY_RANK + RWKV_ICLR_RANK + RWKV_GATE_RANK
CHUNK = 64
LN_EPS = 1e-5
RMS_EPS = 1e-5
F_MIN = 1e-30

LANES = 128
PAIR = 2 * HEAD_DIM
SUB = 8
LEVEL_HALVES = (SUB, 2 * SUB, 4 * SUB)
assert 2 * LEVEL_HALVES[-1] == CHUNK
A_GROUP = 8
A_GROUP_GL = 4
LOG2E = 1.4426950408889634
VMEM_LIMIT = 56 * 1024 * 1024

W_GL = W_HGRN + W_GLA
N_GL_PAIRS = W_GL // PAIR
N_RW_PAIRS = W_RWKV // PAIR
GLA_Z = 4 * W_GLA + LANES
Z_COLS = N_HGRN_COLS + GLA_Z + N_RWKV_COLS

NN = (((1,), (0,)), ((), ()))
NT = (((1,), (1,)), ((), ()))
TN = (((0,), (0,)), ((), ()))


def _mm(a, b, dims=NN):
    return lax.dot_general(a, b, dims, preferred_element_type=F32)


def _split2(x):
    hi = x.astype(BF16)
    lo = (x - hi.astype(F32)).astype(BF16)
    return hi, lo


def _bf16_parts(x, n):
    parts = []
    for i in range(n):
        p = x.astype(BF16)
        parts.append(p)
        if i + 1 < n:
            x = x - p.astype(F32)
    return parts


def _dot1(a, b, dims=NN):
    return _mm(a.astype(BF16), b.astype(BF16), dims)


def _dot3(a, b, dims=NN):
    ah, al = _split2(a)
    bh, bl = _split2(b)
    return _mm(ah, bh, dims) + (_mm(ah, bl, dims) + _mm(al, bh, dims))


def _dot_exact_lhs(a_bf16, b, n, dims=NN):
    return functools.reduce(lambda x, y: x + y, [_mm(a_bf16, bp, dims) for bp in _bf16_parts(b, n)])


def _dot_exact_rhs(a, b_bf16, n, dims=NN):
    return functools.reduce(lambda x, y: x + y, [_mm(ap, b_bf16, dims) for ap in _bf16_parts(a, n)])


def _sigmoid(x):
    return 1.0 / (1.0 + jnp.exp(-x))


def _silu(x):
    return x * _sigmoid(x)


def _softplus(x):
    return jnp.maximum(x, 0.0) + jnp.log1p(jnp.exp(-jnp.abs(x)))


def _log_sigmoid(x):
    return -_softplus(-x)


def _layer_norm(y, g, b):
    mu = jnp.mean(y, axis=-1, keepdims=True)
    d = y - mu
    var = jnp.mean(d * d, axis=-1, keepdims=True)
    return d * lax.rsqrt(var + LN_EPS) * g + b


def _iota(shape, axis):
    return lax.broadcasted_iota(jnp.int32, shape, axis)


def _stack_heads(x):
    lane = _iota(x.shape, 1)
    return jnp.concatenate([jnp.where(lane < HEAD_DIM, x, 0.0), jnp.where(lane >= HEAD_DIM, x, 0.0)], axis=0)


def _pair_masks():
    t = _iota((CHUNK, PAIR), 0)
    s = _iota((CHUNK, PAIR), 1) % HEAD_DIM
    r2 = _iota((PAIR, PAIR), 0) // HEAD_DIM
    c2 = _iota((PAIR, PAIR), 1) // HEAD_DIM
    return dict(
        strict=s < t,
        incl=s <= t,
        eye=s == t,
        diag=(s // SUB == t // SUB) & (s <= t),
        level={hs: (s // (2 * hs) == t // (2 * hs)) & ((t // hs) % 2 == 1) & ((s // hs) % 2 == 0)
               for hs in LEVEL_HALVES},
        bd=r2 == c2,
    )


def _tril_bf16():
    return jnp.where(_iota((CHUNK, CHUNK), 1) <= _iota((CHUNK, CHUNK), 0), 1.0, 0.0).astype(BF16)


def _seg_mean_matrix(width):
    r = _iota((width, width), 0) // HEAD_DIM
    c = _iota((width, width), 1) // HEAD_DIM
    return jnp.where(r == c, 1.0, 0.0).astype(BF16)


def _mod_kernel(c_ref, w_ref, b_ref, o_ref):
    c = _silu(c_ref[...])
    o_ref[0] = _dot3(c, w_ref[0]) + b_ref[0]


def _modulation(c, ada_w, ada_b):
    depth, d, n = ada_w.shape
    bsz = c.shape[0]
    tn = 1536
    return pl.pallas_call(
        _mod_kernel,
        grid=(depth, n // tn),
        in_specs=[
            pl.BlockSpec((bsz, d), lambda l, j: (0, 0)),
            pl.BlockSpec((1, d, tn), lambda l, j: (l, 0, j)),
            pl.BlockSpec((1, 1, tn), lambda l, j: (l, 0, j)),
        ],
        out_specs=pl.BlockSpec((1, bsz, tn), lambda l, j: (l, 0, j)),
        out_shape=jax.ShapeDtypeStruct((depth, bsz, n), F32),
        compiler_params=pltpu.CompilerParams(
            dimension_semantics=("arbitrary", "arbitrary"), vmem_limit_bytes=VMEM_LIMIT),
        name="adaln_modulation",
    )(c, ada_w, ada_b.reshape(depth, 1, n))


def _lower_bound_kernel(x_ref, o_ref):
    depth = x_ref.shape[0]
    rows = [x_ref[l:l + 1, :] for l in range(depth)]
    m = functools.reduce(jnp.maximum, rows)
    e = [jnp.exp(r - m) for r in rows]
    tot = functools.reduce(lambda a, b: a + b, e)
    p = [ei / tot for ei in e]
    acc = jnp.zeros_like(p[0])
    for l in range(depth):
        acc = acc + p[l]
        o_ref[l:l + 1, :] = acc - p[0]


def _lower_bounds(logits):
    return pl.pallas_call(
        _lower_bound_kernel,
        out_shape=jax.ShapeDtypeStruct(logits.shape, F32),
        name="hgrn_lower_bounds",
    )(logits.astype(F32))


def _inproj_kernel(x_ref, mod_ref, w_ref, zh_ref, zg_ref, zr_ref):
    m = mod_ref[0]
    h = (x_ref[...] * (1.0 + m[1:2, :]) + m[0:1, :]).astype(BF16)
    zh_ref[...] = _mm(h, w_ref[:, 0:N_HGRN_COLS])
    zg_ref[...] = _mm(h, w_ref[:, N_HGRN_COLS:N_HGRN_COLS + GLA_Z])
    zr_ref[...] = _mm(h, w_ref[:, N_HGRN_COLS + GLA_Z:Z_COLS])


def _in_projection(x2, mod_l, w_in_p, seq, tm):
    m_rows, d = x2.shape
    steps_per_batch = seq // tm
    row = lambda i: (i, 0)
    const = lambda i: (0, 0)
    return pl.pallas_call(
        _inproj_kernel,
        grid=(m_rows // tm,),
        in_specs=[
            pl.BlockSpec((tm, d), row),
            pl.BlockSpec((1, 6, d), lambda i: (i // steps_per_batch, 0, 0)),
            pl.BlockSpec((d, Z_COLS), const, pipeline_mode=pl.Buffered(1)),
        ],
        out_specs=[
            pl.BlockSpec((tm, N_HGRN_COLS), row),
            pl.BlockSpec((tm, GLA_Z), row),
            pl.BlockSpec((tm, N_RWKV_COLS), row),
        ],
        out_shape=[
            jax.ShapeDtypeStruct((m_rows, N_HGRN_COLS), F32),
            jax.ShapeDtypeStruct((m_rows, GLA_Z), F32),
            jax.ShapeDtypeStruct((m_rows, N_RWKV_COLS), F32),
        ],
        compiler_params=pltpu.CompilerParams(
            dimension_semantics=("arbitrary",), vmem_limit_bytes=VMEM_LIMIT),
        name="in_projection",
    )(x2, mod_l, w_in_p)


def _gl_kernel(lb_ref, zh_ref, zg_ref, au_ref, ab_ref, ng_ref, ind_ref, out_ref,
               st_ref, q_s, k_s, v_s, b_s, ad_s, e_s, o_s, sc_s, eend_s):
    tb = zh_ref.shape[0]
    n_chunks = tb // CHUNK

    @pl.when(pl.program_id(1) == 0)
    def _():
        st_ref[...] = jnp.zeros_like(st_ref)

    lb = lb_ref[...]
    zf = zh_ref[:, W_HGRN:2 * W_HGRN]
    f = lb + (1.0 - lb) * _sigmoid(zf)
    b_s[:, 0:W_HGRN] = jnp.log(jnp.maximum(f, F_MIN))
    k_s[:, 0:W_HGRN] = (1.0 - lb) * _sigmoid(-zf)
    q_s[:, 0:W_HGRN] = _silu(zh_ref[:, 0:W_HGRN]) * HEAD_DIM ** -0.5
    v_s[:, 0:W_HGRN] = zh_ref[:, 2 * W_HGRN:3 * W_HGRN]

    logit = _dot3(zg_ref[:, 4 * W_GLA:GLA_Z], au_ref[...]) + ab_ref[...]
    b_s[:, W_HGRN:W_GL] = _log_sigmoid(logit) / GLA_GATE_NORMALIZER
    q_s[:, W_HGRN:W_GL] = zg_ref[:, 0:W_GLA] * HEAD_DIM ** -0.5
    k_s[:, W_HGRN:W_GL] = zg_ref[:, W_GLA:2 * W_GLA]
    v_s[:, W_HGRN:W_GL] = zg_ref[:, 2 * W_GLA:3 * W_GLA]

    tril = _tril_bf16()
    for c in range(n_chunks):
        rows = slice(c * CHUNK, (c + 1) * CHUNK)
        b_s[rows, :] = _dot_exact_lhs(tril, b_s[rows, :], 2)

    for p in range(N_GL_PAIRS):
        lanes = slice(p * PAIR, (p + 1) * PAIR)
        q3 = q_s[:, lanes].reshape(tb // SUB, SUB, PAIR)
        k3 = k_s[:, lanes].reshape(tb // SUB, SUB, PAIR)
        b3 = (b_s[:, lanes] * LOG2E).reshape(tb // SUB, SUB, PAIR)
        for sg in range(SUB):
            dec = jnp.exp2(jnp.minimum(b3 - b3[:, sg:sg + 1, :], 0.0))
            e = q3 * k3[:, sg:sg + 1, :] * dec
            e_s[:, sg * PAIR:(sg + 1) * PAIR] = e.reshape(tb, PAIR).astype(BF16)
        ad_s[:, lanes] = _mm(e_s[...], ind_ref[...])

    masks = _pair_masks()
    t_idx = _iota((CHUNK, PAIR), 0)
    second_half = {hs: (t_idx // hs) % 2 == 1 for hs in LEVEL_HALVES}
    level_sign = {hs: jnp.where(second_half[hs], LOG2E, -LOG2E) for hs in LEVEL_HALVES}

    def group_units(i):
        units = []
        for cc in range(A_GROUP_GL):
            c = i * A_GROUP_GL + cc
            rows = pl.ds(pl.multiple_of(c * CHUNK, CHUNK), CHUNK)
            for p in range(N_GL_PAIRS):
                units.append(dict(rows=rows, lanes=slice(p * PAIR, (p + 1) * PAIR),
                                  e_rows=pl.ds(pl.multiple_of(c * 8, 8), 8),
                                  st_rows=pl.ds(pl.multiple_of((c * N_GL_PAIRS + p) * PAIR, PAIR), PAIR)))
        return units

    def increments(i, carry):
        units = group_units(i)
        for un in units:
            b = b_s[un["rows"], un["lanes"]]
            b_end = b[CHUNK - 1:CHUNK, :]
            un["k_end"] = k_s[un["rows"], un["lanes"]] * jnp.exp(b_end - b)
            eend_s[un["e_rows"], un["lanes"]] = jnp.broadcast_to(jnp.exp(b_end), (8, PAIR))
        for un in units:
            upd = _dot1(v_s[un["rows"], un["lanes"]], un["k_end"], TN)
            sc_s[un["st_rows"], :] = jnp.where(masks["bd"], upd, 0.0)
        return carry

    lax.fori_loop(0, n_chunks // A_GROUP_GL, increments, 0)

    sts = [st_ref[p] for p in range(N_GL_PAIRS)]
    for c in range(n_chunks):
        for p in range(N_GL_PAIRS):
            st_rows = slice((c * N_GL_PAIRS + p) * PAIR, (c * N_GL_PAIRS + p + 1) * PAIR)
            upd = sc_s[st_rows, :]
            sc_s[st_rows, :] = sts[p]
            sts[p] = sts[p] * eend_s[c * 8:c * 8 + 1, p * PAIR:(p + 1) * PAIR] + upd
    for p in range(N_GL_PAIRS):
        st_ref[p] = sts[p]

    def outputs(i, carry):
        units = group_units(i)
        for un in units:
            q = q_s[un["rows"], un["lanes"]]
            k = k_s[un["rows"], un["lanes"]]
            b = b_s[un["rows"], un["lanes"]]
            un["lv"] = []
            for hs in LEVEL_HALVES:
                e_mid = jnp.concatenate(
                    [jnp.broadcast_to(b[m:m + 1, :], (2 * hs, PAIR)) for m in range(hs - 1, CHUNK, 2 * hs)], axis=0)
                dec = jnp.exp2(jnp.minimum((b - e_mid) * level_sign[hs], 0.0))
                un["lv"].append(jnp.where(second_half[hs], q, k) * dec)
            un["q_in"] = q * jnp.exp(b)
        for un in units:
            a = jnp.where(masks["diag"], ad_s[un["rows"], un["lanes"]], 0.0)
            for hs, lv in zip(LEVEL_HALVES, un["lv"]):
                a = a + jnp.where(masks["level"][hs], _dot1(lv, _stack_heads(lv), NT), 0.0)
            un["a"] = a
        for un in units:
            o_s[un["rows"], un["lanes"]] = (_dot1(un["a"], _stack_heads(v_s[un["rows"], un["lanes"]]))
                                            + _dot1(un["q_in"], sc_s[un["st_rows"], :], NT))
        return carry

    lax.fori_loop(0, n_chunks // A_GROUP_GL, outputs, 0)

    o = o_s[...]
    ms = _dot_exact_rhs(o * o, _seg_mean_matrix(W_GL), 1) * (1.0 / HEAD_DIM)
    gate = jnp.concatenate([_silu(zh_ref[:, 3 * W_HGRN:4 * W_HGRN]), _silu(zg_ref[:, 3 * W_GLA:4 * W_GLA])], axis=1)
    out_ref[...] = (o * lax.rsqrt(ms + RMS_EPS) * ng_ref[...] * gate).astype(out_ref.dtype)


def _gl_indicator():
    r = np.arange(SUB * PAIR)
    c = np.arange(PAIR)
    sg = r // PAIR
    h = (r % PAIR) // HEAD_DIM
    ind = (h[:, None] == (c // HEAD_DIM)[None, :]) & (sg[:, None] == (c % SUB)[None, :])
    return jnp.asarray(ind, dtype=BF16)


def _gl_mixer(zh, zg, lb_l, alpha_up_p, alpha_b, norm_g, bsz, seq, tb):
    steps = seq // tb
    row = lambda b, i: (b * steps + i, 0)
    const = lambda b, i: (0, 0)
    scratch_f32 = pltpu.VMEM((tb, W_GL), F32)
    return pl.pallas_call(
        _gl_kernel,
        grid=(bsz, steps),
        in_specs=[
            pl.BlockSpec((1, W_HGRN), const),
            pl.BlockSpec((tb, N_HGRN_COLS), row),
            pl.BlockSpec((tb, GLA_Z), row),
            pl.BlockSpec((LANES, W_GLA), const),
            pl.BlockSpec((1, W_GLA), const),
            pl.BlockSpec((1, W_GL), const),
            pl.BlockSpec((SUB * PAIR, PAIR), const),
        ],
        out_specs=pl.BlockSpec((tb, W_GL), row),
        out_shape=jax.ShapeDtypeStruct((bsz * seq, W_GL), BF16),
        scratch_shapes=[
            pltpu.VMEM((N_GL_PAIRS, PAIR, PAIR), F32),
            scratch_f32, scratch_f32, scratch_f32, scratch_f32, scratch_f32,
            pltpu.VMEM((tb, SUB * PAIR), BF16),
            scratch_f32,
            pltpu.VMEM((tb // CHUNK * N_GL_PAIRS * PAIR, PAIR), F32),
            pltpu.VMEM((tb // CHUNK * 8, W_GL), F32),
        ],
        compiler_params=pltpu.CompilerParams(
            dimension_semantics=("arbitrary", "arbitrary"), vmem_limit_bytes=VMEM_LIMIT),
        name="hgrn_gla_mixer",
    )(lb_l, zh, zg, alpha_up_p, alpha_b, norm_g, _gl_indicator())


def _rwkv_kernel(zr_ref, mu_ref, w0_ref, a0_ref, wa_ref, gup_ref, kk_ref, ka_ref, rk_ref, gng_ref, gnb_ref,
                 out_ref, st_ref, carry_s, r_s, lw_s, k_s, v_s, kk_s, a_s, y_s,
                 rt_s, bh_s, kh_s, nb_s, nk_s, w_s, u0_s, eend_s, hg_s):
    tb = zr_ref.shape[0]
    n_chunks = tb // CHUNK
    w3 = W_RWKV

    @pl.when(pl.program_id(1) == 0)
    def _():
        st_ref[...] = jnp.zeros_like(st_ref)
        carry_s[...] = jnp.zeros_like(carry_s)

    z = zr_ref[...]
    z_prev = pltpu.roll(z, 1, axis=0)
    z_prev = jnp.where(_iota(z.shape, 0) == 0, carry_s[...], z_prev)
    carry_s[...] = z[tb - 1:tb, :]
    z = z + (z_prev - z) * mu_ref[...]

    r = z[:, 0:w3]
    k = z[:, w3:2 * w3]
    v = z[:, 2 * w3:3 * w3]
    h_wa = z[:, 3 * w3:3 * w3 + LANES]
    h_wa = jnp.where(_iota(h_wa.shape, 1) < RWKV_DECAY_RANK, jnp.tanh(h_wa), h_wa)
    wa = _dot3(h_wa, wa_ref[...])
    w_log = -_softplus(-(w0_ref[...] + wa[:, 0:w3])) - 0.5
    a = _sigmoid(a0_ref[...] + wa[:, w3:2 * w3])
    seg = _seg_mean_matrix(w3)
    kk = k * kk_ref[...]
    kk = kk / jnp.maximum(jnp.sqrt(_dot_exact_rhs(kk * kk, seg, 1)), 1e-12)
    k2 = k * (1.0 + (a - 1.0) * ka_ref[...])
    r_s[...] = r
    lw_s[...] = -jnp.exp(w_log)
    k_s[...] = k2
    v_s[...] = v
    kk_s[...] = kk
    a_s[...] = a
    hg_s[...] = _sigmoid(z[:, 3 * w3 + LANES:3 * w3 + LANES + RWKV_GATE_RANK])

    masks = _pair_masks()
    tril = _tril_bf16()
    eye = jnp.where(masks["eye"], 1.0, 0.0)

    def phase_a(i, carry):
        units = []
        for cc in range(A_GROUP):
            c = i * A_GROUP + cc
            rows = pl.ds(pl.multiple_of(c * CHUNK, CHUNK), CHUNK)
            g_all = _dot_exact_lhs(tril, lw_s[rows, :], 2)
            for p in range(N_RW_PAIRS):
                lanes = slice(p * PAIR, (p + 1) * PAIR)
                units.append(dict(c=c, rows=rows, lanes=lanes, g=g_all[:, lanes]))
        for un in units:
            rows, lanes, g = un["rows"], un["lanes"], un["g"]
            kc = k_s[rows, lanes]
            kkc = kk_s[rows, lanes]
            ka = kkc * a_s[rows, lanes]
            g_end = g[CHUNK - 1:CHUNK, :]
            e_neg = jnp.exp(-g)
            e_end = jnp.exp(g_end - g)
            un["at"] = -kkc * jnp.exp(g - lw_s[rows, lanes])
            rt = r_s[rows, lanes] * jnp.exp(g)
            un["rt"] = rt
            un["bt"] = ka * e_neg
            un["kt"] = kc * e_neg
            rt_s[rows, lanes] = rt
            bh_s[rows, lanes] = ka * e_end
            kh_s[rows, lanes] = kc * e_end
            eend_s[pl.ds(pl.multiple_of(un["c"] * 8, 8), 8), lanes] = jnp.broadcast_to(jnp.exp(g_end), (8, PAIR))
        for un in units:
            gram = _dot1(jnp.concatenate([un["at"], un["rt"]], axis=0),
                         jnp.concatenate([_stack_heads(un["bt"]), _stack_heads(un["kt"])], axis=0), NT)
            un["l"] = jnp.where(masks["strict"], gram[0:CHUNK, 0:PAIR], 0.0)
            un["mk"] = jnp.where(masks["strict"], gram[0:CHUNK, PAIR:2 * PAIR], 0.0)
            nb_s[un["rows"], un["lanes"]] = jnp.where(masks["incl"], gram[CHUNK:2 * CHUNK, 0:PAIR], 0.0)
            nk_s[un["rows"], un["lanes"]] = jnp.where(masks["incl"], gram[CHUNK:2 * CHUNK, PAIR:2 * PAIR], 0.0)
        for un in units:
            un["t"] = eye + un["l"]
            un["lp"] = un["l"]
        for _ in range(5):
            for un in units:
                un["lp"] = _dot1(un["lp"], _stack_heads(un["lp"]))
            for un in units:
                un["t"] = un["t"] + _dot1(un["t"], _stack_heads(un["lp"]))
        for un in units:
            un["mv"] = _dot1(un["mk"], _stack_heads(v_s[un["rows"], un["lanes"]]))
        for un in units:
            wu = _dot1(un["t"], jnp.concatenate([_stack_heads(un["at"]), _stack_heads(un["mv"])], axis=1))
            w_s[un["rows"], un["lanes"]] = wu[:, 0:PAIR]
            u0_s[un["rows"], un["lanes"]] = wu[:, PAIR:2 * PAIR]
        return carry

    lax.fori_loop(0, n_chunks // A_GROUP, phase_a, 0)

    def phase_b(c, carry):
        rows = pl.ds(pl.multiple_of(c * CHUNK, CHUNK), CHUNK)
        pairs = [slice(p * PAIR, (p + 1) * PAIR) for p in range(N_RW_PAIRS)]
        sts = [st_ref[p] for p in range(N_RW_PAIRS)]
        vs = [v_s[rows, lanes] for lanes in pairs]
        us = [_dot1(w_s[rows, lanes], st, NT) + u0_s[rows, lanes] for lanes, st in zip(pairs, sts)]
        upds = [_dot1(jnp.concatenate([u, vc], axis=0),
                      jnp.concatenate([bh_s[rows, lanes], kh_s[rows, lanes]], axis=0), TN)
                for lanes, u, vc in zip(pairs, us, vs)]
        for p, (lanes, st, upd) in enumerate(zip(pairs, sts, upds)):
            e_end = eend_s[pl.ds(pl.multiple_of(c * 8, 8), 8), lanes][0:1, :]
            st_ref[p] = st * e_end + jnp.where(masks["bd"], upd, 0.0)
        for lanes, st, u, vc in zip(pairs, sts, us, vs):
            y_s[rows, lanes] = (
                _dot1(rt_s[rows, lanes], st, NT)
                + _dot1(jnp.concatenate([nb_s[rows, lanes], nk_s[rows, lanes]], axis=1),
                        jnp.concatenate([_stack_heads(u), _stack_heads(vc)], axis=0)))
        return carry

    lax.fori_loop(0, n_chunks, phase_b, 0)

    y = y_s[...]
    mean = _dot_exact_rhs(y, seg, 2) * (1.0 / HEAD_DIM)
    d = y - mean
    var = _dot_exact_rhs(d * d, seg, 1) * (1.0 / HEAD_DIM)
    yn = d * lax.rsqrt(var + RWKV_GN_EPS) * gng_ref[...] + gnb_ref[...]
    bonus = _dot_exact_rhs(r_s[...] * k_s[...] * rk_ref[...], seg, 2) * v_s[...]
    gate = _dot1(hg_s[...], gup_ref[...])
    out_ref[...] = ((yn + bonus) * gate).astype(out_ref.dtype)


def _rwkv_mixer(zr, mu, w0, a0, wa_blk, g_up, k_k, k_a, r_k, gn_g, gn_b, bsz, seq, tb):
    steps = seq // tb
    row = lambda b, i: (b * steps + i, 0)
    const = lambda b, i: (0, 0)
    vec = pl.BlockSpec((1, W_RWKV), const)
    slab = pltpu.VMEM((tb, W_RWKV), F32)
    return pl.pallas_call(
        _rwkv_kernel,
        grid=(bsz, steps),
        in_specs=[
            pl.BlockSpec((tb, N_RWKV_COLS), row),
            pl.BlockSpec((1, N_RWKV_COLS), const),
            vec, vec,
            pl.BlockSpec((LANES, 2 * W_RWKV), const),
            pl.BlockSpec((RWKV_GATE_RANK, W_RWKV), const),
            vec, vec, vec, vec, vec,
        ],
        out_specs=pl.BlockSpec((tb, W_RWKV), row),
        out_shape=jax.ShapeDtypeStruct((bsz * seq, W_RWKV), BF16),
        scratch_shapes=[
            pltpu.VMEM((N_RW_PAIRS, PAIR, PAIR), F32),
            pltpu.VMEM((1, N_RWKV_COLS), F32),
            slab, slab, slab, slab, slab, slab, slab,
            slab, slab, slab, slab, slab, slab, slab,
            pltpu.VMEM((tb // CHUNK * 8, W_RWKV), F32),
            pltpu.VMEM((tb, RWKV_GATE_RANK), F32),
        ],
        compiler_params=pltpu.CompilerParams(
            dimension_semantics=("arbitrary", "arbitrary"), vmem_limit_bytes=VMEM_LIMIT),
        name="rwkv7_mixer",
    )(zr, mu, w0, a0, wa_blk, g_up, k_k, k_a, r_k, gn_g, gn_b)


def _post_kernel(alpha, ff_chunk, x_ref, ogl_ref, orw_ref, mod_ref, wout_ref, ln1g_ref, ln1b_ref,
                 wup_ref, wdn_ref, ln2g_ref, ln2b_ref, out_ref):
    m = mod_ref[0]
    gate1, shift2, scale2, gate2 = m[2:3, :], m[3:4, :], m[4:5, :], m[5:6, :]
    o = _mm(ogl_ref[...], wout_ref[0:W_GL, :]) + _mm(orw_ref[...], wout_ref[W_GL:W_GL + W_RWKV, :])
    x1 = _layer_norm(alpha * x_ref[...] + (1.0 + gate1) * o, ln1g_ref[...], ln1b_ref[...])
    h = (x1 * (1.0 + scale2) + shift2).astype(BF16)
    d_ff = wup_ref.shape[1]
    acc = jnp.zeros(x1.shape, F32)
    for j in range(d_ff // ff_chunk):
        cols = slice(j * ff_chunk, (j + 1) * ff_chunk)
        u = jnp.maximum(_mm(h, wup_ref[:, cols]), 0.0)
        acc = acc + _mm((u * u).astype(BF16), wdn_ref[cols, :])
    out_ref[...] = _layer_norm(alpha * x1 + (1.0 + gate2) * acc, ln2g_ref[...], ln2b_ref[...])


def _post_mixer(x2, ogl, orw, mod_l, w_out, ln1_g, ln1_b, w_up, w_dn, ln2_g, ln2_b, alpha, seq, tm):
    m_rows, d = x2.shape
    d_ff = w_up.shape[1]
    steps_per_batch = seq // tm
    row = lambda i: (i, 0)
    const = lambda i: (0, 0)
    vec = pl.BlockSpec((1, d), const)
    resident = lambda shape: pl.BlockSpec(shape, const, pipeline_mode=pl.Buffered(1))
    return pl.pallas_call(
        functools.partial(_post_kernel, alpha, 512),
        grid=(m_rows // tm,),
        in_specs=[
            pl.BlockSpec((tm, d), row),
            pl.BlockSpec((tm, W_GL), row),
            pl.BlockSpec((tm, W_RWKV), row),
            pl.BlockSpec((1, 6, d), lambda i: (i // steps_per_batch, 0, 0)),
            resident((W_GL + W_RWKV, d)),
            vec, vec,
            resident((d, d_ff)),
            resident((d_ff, d)),
            vec, vec,
        ],
        out_specs=pl.BlockSpec((tm, d), row),
        out_shape=jax.ShapeDtypeStruct((m_rows, d), F32),
        compiler_params=pltpu.CompilerParams(
            dimension_semantics=("arbitrary",), vmem_limit_bytes=VMEM_LIMIT),
        name="outproj_mlp",
    )(x2, ogl, orw, mod_l, w_out, ln1_g, ln1_b, w_up, w_dn, ln2_g, ln2_b)


def kernel(x, c, hgrn_lb_logits, ada_w, ada_b, w_in, hgrn_norm_g, gla_alpha_up, gla_alpha_b, gla_norm_g,
           rwkv_mu, rwkv_w0, rwkv_w_up, rwkv_a0, rwkv_a_up, rwkv_g_up, rwkv_k_k, rwkv_k_a, rwkv_r_k,
           rwkv_gn_g, rwkv_gn_b, w_out, ln1_g, ln1_b, mlp_w_up, mlp_w_down, ln2_g, ln2_b):
    bsz, seq, d = x.shape
    depth = w_in.shape[0]
    alpha = (2.0 * depth) ** 0.25
    tm = min(512, seq)
    tb = min(512, seq)
    assert seq % tm == 0 and seq % tb == 0 and tb % CHUNK == 0
    assert w_in.shape[2] == N_HGRN_COLS + N_GLA_COLS + N_RWKV_COLS

    mod = _modulation(c, ada_w, ada_b).reshape(depth, bsz, 6, d)
    lbs = _lower_bounds(hgrn_lb_logits)

    gla_end = N_HGRN_COLS + N_GLA_COLS
    w_in_p = jnp.concatenate(
        [w_in[:, :, :gla_end], jnp.zeros((depth, d, LANES - GLA_GATE_RANK), w_in.dtype), w_in[:, :, gla_end:]],
        axis=2).astype(BF16)
    alpha_up_p = jnp.concatenate(
        [gla_alpha_up, jnp.zeros((depth, LANES - GLA_GATE_RANK, W_GLA), gla_alpha_up.dtype)], axis=1)
    zeros_r = jnp.zeros((depth, RWKV_DECAY_RANK, W_RWKV), rwkv_w_up.dtype)
    wa_blk = jnp.concatenate(
        [jnp.concatenate([rwkv_w_up, zeros_r], axis=2), jnp.concatenate([zeros_r, rwkv_a_up], axis=2)], axis=1)
    norm_g = jnp.concatenate([hgrn_norm_g, gla_norm_g], axis=1)
    w_out_b = w_out.astype(BF16)
    w_up_b = mlp_w_up.astype(BF16)
    w_dn_b = mlp_w_down.astype(BF16)
    vec = lambda a, l: a[l].reshape(1, -1)

    x2 = x.reshape(bsz * seq, d)
    for l in range(depth):
        zh, zg, zr = _in_projection(x2, mod[l], w_in_p[l], seq, tm)
        ogl = _gl_mixer(zh, zg, vec(lbs, l), alpha_up_p[l], vec(gla_alpha_b, l), vec(norm_g, l), bsz, seq, tb)
        orw = _rwkv_mixer(zr, vec(rwkv_mu, l), vec(rwkv_w0, l), vec(rwkv_a0, l), wa_blk[l], rwkv_g_up[l],
                          vec(rwkv_k_k, l), vec(rwkv_k_a, l), vec(rwkv_r_k, l), vec(rwkv_gn_g, l),
                          vec(rwkv_gn_b, l), bsz, seq, tb)
        x2 = _post_mixer(x2, ogl, orw, mod[l], w_out_b[l], vec(ln1_g, l), vec(ln1_b, l), w_up_b[l], w_dn_b[l],
                         vec(ln2_g, l), vec(ln2_b, l), alpha, seq, tm)
    return x2.reshape(bsz, seq, d)
```

```python
import functools

import numpy as np
import jax
import jax.numpy as jnp
from jax import lax
from jax.experimental import pallas as pl
from jax.experimental.pallas import tpu as pltpu

F32 = jnp.float32
BF16 = jnp.bfloat16

HEAD_DIM = 64
H_HGRN, H_GLA, H_RWKV = 4, 6, 6
W_HGRN, W_GLA, W_RWKV = H_HGRN * HEAD_DIM, H_GLA * HEAD_DIM, H_RWKV * HEAD_DIM
GLA_GATE_RANK = 16
GLA_GATE_NORMALIZER = 16.0
RWKV_DECAY_RANK, RWKV_ICLR_RANK, RWKV_GATE_RANK = 64, 64, 128
RWKV_GN_EPS = 64e-5
N_HGRN_COLS = 4 * W_HGRN
N_GLA_COLS = 4 * W_GLA + GLA_GATE_RANK
N_RWKV_COLS = 3 * W_RWKV + RWKV_DECAY_RANK + RWKV_ICLR_RANK + RWKV_GATE_RANK
CHUNK = 64
LN_EPS = 1e-5
RMS_EPS = 1e-5
F_MIN = 1e-30

LANES = 128
PAIR = 2 * HEAD_DIM
SUB = 8
LEVEL_HALVES = (SUB, 2 * SUB, 4 * SUB)
assert 2 * LEVEL_HALVES[-1] == CHUNK
A_GROUP = 8
A_GROUP_GL = 4
LOG2E = 1.4426950408889634
VMEM_LIMIT = 56 * 1024 * 1024

W_GL = W_HGRN + W_GLA
N_GL_PAIRS = W_GL // PAIR
N_RW_PAIRS = W_RWKV // PAIR
GLA_Z = 4 * W_GLA + LANES
Z_COLS = N_HGRN_COLS + GLA_Z + N_RWKV_COLS

NN = (((1,), (0,)), ((), ()))
NT = (((1,), (1,)), ((), ()))
TN = (((0,), (0,)), ((), ()))


def _mm(a, b, dims=NN):
    return lax.dot_general(a, b, dims, preferred_element_type=F32)


def _split2(x):
    hi = x.astype(BF16)
    lo = (x - hi.astype(F32)).astype(BF16)
    return hi, lo


def _bf16_parts(x, n):
    parts = []
    for i in range(n):
        p = x.astype(BF16)
        parts.append(p)
        if i + 1 < n:
            x = x - p.astype(F32)
    return parts


def _dot1(a, b, dims=NN):
    return _mm(a.astype(BF16), b.astype(BF16), dims)


def _dot3(a, b, dims=NN):
    ah, al = _split2(a)
    bh, bl = _split2(b)
    return _mm(ah, bh, dims) + (_mm(ah, bl, dims) + _mm(al, bh, dims))


def _dot_exact_lhs(a_bf16, b, n, dims=NN):
    return functools.reduce(lambda x, y: x + y, [_mm(a_bf16, bp, dims) for bp in _bf16_parts(b, n)])


def _dot_exact_rhs(a, b_bf16, n, dims=NN):
    return functools.reduce(lambda x, y: x + y, [_mm(ap, b_bf16, dims) for ap in _bf16_parts(a, n)])


def _sigmoid(x):
    return 1.0 / (1.0 + jnp.exp(-x))


def _silu(x):
    return x * _sigmoid(x)


def _softplus(x):
    return jnp.maximum(x, 0.0) + jnp.log1p(jnp.exp(-jnp.abs(x)))


def _log_sigmoid(x):
    return -_softplus(-x)


def _layer_norm(y, g, b):
    mu = jnp.mean(y, axis=-1, keepdims=True)
    d = y - mu
    var = jnp.mean(d * d, axis=-1, keepdims=True)
    return d * lax.rsqrt(var + LN_EPS) * g + b


def _iota(shape, axis):
    return lax.broadcasted_iota(jnp.int32, shape, axis)


def _stack_heads(x):
    lane = _iota(x.shape, 1)
    return jnp.concatenate([jnp.where(lane < HEAD_DIM, x, 0.0), jnp.where(lane >= HEAD_DIM, x, 0.0)], axis=0)


def _pair_masks():
    t = _iota((CHUNK, PAIR), 0)
    s = _iota((CHUNK, PAIR), 1) % HEAD_DIM
    r2 = _iota((PAIR, PAIR), 0) // HEAD_DIM
    c2 = _iota((PAIR, PAIR), 1) // HEAD_DIM
    return dict(
        strict=s < t,
        incl=s <= t,
        eye=s == t,
        diag=(s // SUB == t // SUB) & (s <= t),
        level={hs: (s // (2 * hs) == t // (2 * hs)) & ((t // hs) % 2 == 1) & ((s // hs) % 2 == 0)
               for hs in LEVEL_HALVES},
        bd=r2 == c2,
    )


def _tril_bf16():
    return jnp.where(_iota((CHUNK, CHUNK), 1) <= _iota((CHUNK, CHUNK), 0), 1.0, 0.0).astype(BF16)


def _seg_mean_matrix(width):
    r = _iota((width, width), 0) // HEAD_DIM
    c = _iota((width, width), 1) // HEAD_DIM
    return jnp.where(r == c, 1.0, 0.0).astype(BF16)


def _mod_kernel(c_ref, w_ref, b_ref, o_ref):
    c = _silu(c_ref[...])
    o_ref[0] = _dot3(c, w_ref[0]) + b_ref[0]


def _modulation(c, ada_w, ada_b):
    depth, d, n = ada_w.shape
    bsz = c.shape[0]
    tn = 1536
    return pl.pallas_call(
        _mod_kernel,
        grid=(depth, n // tn),
        in_specs=[
            pl.BlockSpec((bsz, d), lambda l, j: (0, 0)),
            pl.BlockSpec((1, d, tn), lambda l, j: (l, 0, j)),
            pl.BlockSpec((1, 1, tn), lambda l, j: (l, 0, j)),
        ],
        out_specs=pl.BlockSpec((1, bsz, tn), lambda l, j: (l, 0, j)),
        out_shape=jax.ShapeDtypeStruct((depth, bsz, n), F32),
        compiler_params=pltpu.CompilerParams(
            dimension_semantics=("arbitrary", "arbitrary"), vmem_limit_bytes=VMEM_LIMIT),
        name="adaln_modulation",
    )(c, ada_w, ada_b.reshape(depth, 1, n))


def _lower_bound_kernel(x_ref, o_ref):
    depth = x_ref.shape[0]
    rows = [x_ref[l:l + 1, :] for l in range(depth)]
    m = functools.reduce(jnp.maximum, rows)
    e = [jnp.exp(r - m) for r in rows]
    tot = functools.reduce(lambda a, b: a + b, e)
    p = [ei / tot for ei in e]
    acc = jnp.zeros_like(p[0])
    for l in range(depth):
        acc = acc + p[l]
        o_ref[l:l + 1, :] = acc - p[0]


def _lower_bounds(logits):
    return pl.pallas_call(
        _lower_bound_kernel,
        out_shape=jax.ShapeDtypeStruct(logits.shape, F32),
        name="hgrn_lower_bounds",
    )(logits.astype(F32))


def _inproj_kernel(steps_per_batch, x_ref, mod_ref, w_ref, lb_ref, au_ref, ab_ref, mu_ref, w0_ref, a0_ref,
                   wa_ref, kk_ref, ka_ref,
                   gq_ref, gk_ref, gv_ref, gb_ref, gg_ref,
                   rr_ref, rlw_ref, rk_ref, rv_ref, rkk_ref, rka_ref, rhg_ref, carry_s):
    tm = x_ref.shape[0]
    w3 = W_RWKV

    @pl.when(pl.program_id(0) % steps_per_batch == 0)
    def _():
        carry_s[...] = jnp.zeros_like(carry_s)

    m = mod_ref[0]
    h = (x_ref[...] * (1.0 + m[1:2, :]) + m[0:1, :]).astype(BF16)

    def proj(c0, width):
        return _mm(h, w_ref[:, c0:c0 + width])

    first_row = _iota((tm, LANES), 0) == 0

    def shifted(c0, width):
        z = proj(N_HGRN_COLS + GLA_Z + c0, width)
        z_prev = pltpu.roll(z, 1, axis=0)
        z_prev = jnp.where(jnp.concatenate([first_row] * (width // LANES), axis=1), carry_s[:, c0:c0 + width], z_prev)
        carry_s[:, c0:c0 + width] = z[tm - 1:tm, :]
        return z + (z_prev - z) * mu_ref[:, c0:c0 + width]

    z_v = shifted(2 * w3, w3 + LANES)
    h_wa = z_v[:, w3:w3 + LANES]
    h_wa = jnp.where(_iota(h_wa.shape, 1) < RWKV_DECAY_RANK, jnp.tanh(h_wa), h_wa)
    wa = _dot1(h_wa, wa_ref[...])
    w_log = -_softplus(-(w0_ref[...] + wa[:, 0:w3])) - 0.5
    rlw_ref[...] = -jnp.exp(w_log)
    a = _sigmoid(a0_ref[...] + wa[:, w3:2 * w3])
    rv_ref[...] = z_v[:, 0:w3]
    z_rk = shifted(0, 2 * w3)
    k = z_rk[:, w3:2 * w3]
    kk = k * kk_ref[...]
    kk = kk / jnp.maximum(jnp.sqrt(_dot_exact_rhs(kk * kk, _seg_mean_matrix(w3), 1)), 1e-12)
    rkk_ref[...] = kk
    rka_ref[...] = kk * a
    rk_ref[...] = k * (1.0 + (a - 1.0) * ka_ref[...])
    rr_ref[...] = z_rk[:, 0:w3]
    rhg_ref[...] = _sigmoid(shifted(3 * w3 + LANES, RWKV_GATE_RANK)).astype(BF16)

    g0 = N_HGRN_COLS
    logit = _dot1(proj(g0 + 4 * W_GLA, LANES), au_ref[...]) + ab_ref[...]
    gb_ref[:, W_HGRN:W_GL] = _log_sigmoid(logit) / GLA_GATE_NORMALIZER
    z_qk = proj(g0, 2 * W_GLA)
    gq_ref[:, W_HGRN:W_GL] = z_qk[:, 0:W_GLA] * HEAD_DIM ** -0.5
    gk_ref[:, W_HGRN:W_GL] = z_qk[:, W_GLA:2 * W_GLA]
    z_vr = proj(g0 + 2 * W_GLA, 2 * W_GLA)
    gv_ref[:, W_HGRN:W_GL] = z_vr[:, 0:W_GLA].astype(BF16)
    gg_ref[:, W_HGRN:W_GL] = _silu(z_vr[:, W_GLA:2 * W_GLA]).astype(BF16)

    lb = lb_ref[...]
    zf = proj(W_HGRN, W_HGRN)
    f = lb + (1.0 - lb) * _sigmoid(zf)
    gb_ref[:, 0:W_HGRN] = jnp.log(jnp.maximum(f, F_MIN))
    gk_ref[:, 0:W_HGRN] = (1.0 - lb) * _sigmoid(-zf)
    gq_ref[:, 0:W_HGRN] = _silu(proj(0, W_HGRN)) * HEAD_DIM ** -0.5
    gg_ref[:, 0:W_HGRN] = _silu(proj(3 * W_HGRN, W_HGRN)).astype(BF16)
    gv_ref[:, 0:W_HGRN] = proj(2 * W_HGRN, W_HGRN).astype(BF16)


def _in_projection(x2, mod_l, w_in_p, lb_l, alpha_up_p, alpha_b, mu, w0, a0, wa_blk, k_k, k_a, seq, tm):
    m_rows, d = x2.shape
    steps_per_batch = seq // tm
    row = lambda i: (i, 0)
    const = lambda i: (0, 0)
    vec = lambda n: pl.BlockSpec((1, n), const)
    out = lambda n, dt: (pl.BlockSpec((tm, n), row), jax.ShapeDtypeStruct((m_rows, n), dt))
    outs = [out(W_GL, F32), out(W_GL, F32), out(W_GL, BF16), out(W_GL, F32), out(W_GL, BF16),
            out(W_RWKV, F32), out(W_RWKV, F32), out(W_RWKV, F32), out(W_RWKV, F32), out(W_RWKV, F32),
            out(W_RWKV, F32), out(RWKV_GATE_RANK, BF16)]
    return pl.pallas_call(
        functools.partial(_inproj_kernel, steps_per_batch),
        grid=(m_rows // tm,),
        in_specs=[
            pl.BlockSpec((tm, d), row),
            pl.BlockSpec((1, 6, d), lambda i: (i // steps_per_batch, 0, 0)),
            pl.BlockSpec((d, Z_COLS), const, pipeline_mode=pl.Buffered(1)),
            vec(W_HGRN),
            pl.BlockSpec((LANES, W_GLA), const),
            vec(W_GLA),
            vec(N_RWKV_COLS), vec(W_RWKV), vec(W_RWKV),
            pl.BlockSpec((LANES, 2 * W_RWKV), const),
            vec(W_RWKV), vec(W_RWKV),
        ],
        out_specs=[o[0] for o in outs],
        out_shape=[o[1] for o in outs],
        scratch_shapes=[pltpu.VMEM((1, N_RWKV_COLS), F32)],
        compiler_params=pltpu.CompilerParams(
            dimension_semantics=("arbitrary",), vmem_limit_bytes=VMEM_LIMIT),
        name="in_projection",
    )(x2, mod_l, w_in_p, lb_l, alpha_up_p, alpha_b, mu, w0, a0, wa_blk, k_k, k_a)


def _gl_kernel(q_s, k_s, v_s, lg_ref, gate_ref, ng_ref, ind_ref, out_ref,
               st_ref, b_s, ad_s, e_s, o_s, sc_s, eend_s):
    tb = q_s.shape[0]
    n_chunks = tb // CHUNK

    @pl.when(pl.program_id(1) == 0)
    def _():
        st_ref[...] = jnp.zeros_like(st_ref)

    tril = _tril_bf16()
    for c in range(n_chunks):
        rows = slice(c * CHUNK, (c + 1) * CHUNK)
        b_s[rows, :] = _dot_exact_lhs(tril, lg_ref[rows, :], 2)

    for p in range(N_GL_PAIRS):
        lanes = slice(p * PAIR, (p + 1) * PAIR)
        q3 = q_s[:, lanes].reshape(tb // SUB, SUB, PAIR)
        k3 = k_s[:, lanes].reshape(tb // SUB, SUB, PAIR)
        b3 = (b_s[:, lanes] * LOG2E).reshape(tb // SUB, SUB, PAIR)
        for sg in range(SUB):
            dec = jnp.exp2(jnp.minimum(b3 - b3[:, sg:sg + 1, :], 0.0))
            e = q3 * k3[:, sg:sg + 1, :] * dec
            e_s[:, sg * PAIR:(sg + 1) * PAIR] = e.reshape(tb, PAIR).astype(BF16)
        ad_s[:, lanes] = _mm(e_s[...], ind_ref[...])

    masks = _pair_masks()
    t_idx = _iota((CHUNK, PAIR), 0)
    second_half = {hs: (t_idx // hs) % 2 == 1 for hs in LEVEL_HALVES}
    level_sign = {hs: jnp.where(second_half[hs], LOG2E, -LOG2E) for hs in LEVEL_HALVES}

    def group_units(i):
        units = []
        for cc in range(A_GROUP_GL):
            c = i * A_GROUP_GL + cc
            rows = pl.ds(pl.multiple_of(c * CHUNK, CHUNK), CHUNK)
            for p in range(N_GL_PAIRS):
                units.append(dict(rows=rows, lanes=slice(p * PAIR, (p + 1) * PAIR),
                                  e_rows=pl.ds(pl.multiple_of(c * 8, 8), 8),
                                  st_rows=pl.ds(pl.multiple_of((c * N_GL_PAIRS + p) * PAIR, PAIR), PAIR)))
        return units

    def increments(i, carry):
        units = group_units(i)
        for un in units:
            b = b_s[un["rows"], un["lanes"]]
            b_end = b[CHUNK - 1:CHUNK, :]
            un["k_end"] = k_s[un["rows"], un["lanes"]] * jnp.exp(b_end - b)
            eend_s[un["e_rows"], un["lanes"]] = jnp.broadcast_to(jnp.exp(b_end), (8, PAIR))
        for un in units:
            upd = _dot1(v_s[un["rows"], un["lanes"]], un["k_end"], TN)
            sc_s[un["st_rows"], :] = jnp.where(masks["bd"], upd, 0.0)
        return carry

    lax.fori_loop(0, n_chunks // A_GROUP_GL, increments, 0)

    sts = [st_ref[p] for p in range(N_GL_PAIRS)]
    for c in range(n_chunks):
        for p in range(N_GL_PAIRS):
            st_rows = slice((c * N_GL_PAIRS + p) * PAIR, (c * N_GL_PAIRS + p + 1) * PAIR)
            upd = sc_s[st_rows, :]
            sc_s[st_rows, :] = sts[p]
            sts[p] = sts[p] * eend_s[c * 8:c * 8 + 1, p * PAIR:(p + 1) * PAIR] + upd
    for p in range(N_GL_PAIRS):
        st_ref[p] = sts[p]

    def outputs(i, carry):
        units = group_units(i)
        for un in units:
            q = q_s[un["rows"], un["lanes"]]
            k = k_s[un["rows"], un["lanes"]]
            b = b_s[un["rows"], un["lanes"]]
            un["lv"] = []
            for hs in LEVEL_HALVES:
                e_mid = jnp.concatenate(
                    [jnp.broadcast_to(b[m:m + 1, :], (2 * hs, PAIR)) for m in range(hs - 1, CHUNK, 2 * hs)], axis=0)
                dec = jnp.exp2(jnp.minimum((b - e_mid) * level_sign[hs], 0.0))
                un["lv"].append(jnp.where(second_half[hs], q, k) * dec)
            un["q_in"] = q * jnp.exp(b)
        for un in units:
            a = jnp.where(masks["diag"], ad_s[un["rows"], un["lanes"]], 0.0)
            for hs, lv in zip(LEVEL_HALVES, un["lv"]):
                a = a + jnp.where(masks["level"][hs], _dot1(lv, _stack_heads(lv), NT), 0.0)
            un["a"] = a
        for un in units:
            o_s[un["rows"], un["lanes"]] = (_dot1(un["a"], _stack_heads(v_s[un["rows"], un["lanes"]]))
                                            + _dot1(un["q_in"], sc_s[un["st_rows"], :], NT))
        return carry

    lax.fori_loop(0, n_chunks // A_GROUP_GL, outputs, 0)

    o = o_s[...]
    ms = _dot_exact_rhs(o * o, _seg_mean_matrix(W_GL), 1) * (1.0 / HEAD_DIM)
    out_ref[...] = (o * lax.rsqrt(ms + RMS_EPS) * ng_ref[...] * gate_ref[...]).astype(out_ref.dtype)


def _gl_indicator():
    r = np.arange(SUB * PAIR)
    c = np.arange(PAIR)
    sg = r // PAIR
    h = (r % PAIR) // HEAD_DIM
    ind = (h[:, None] == (c // HEAD_DIM)[None, :]) & (sg[:, None] == (c % SUB)[None, :])
    return jnp.asarray(ind, dtype=BF16)


def _gl_mixer(q, k, v, lg, gate, norm_g, bsz, seq, tb):
    steps = seq // tb
    row = lambda b, i: (b * steps + i, 0)
    const = lambda b, i: (0, 0)
    slab = pl.BlockSpec((tb, W_GL), row)
    scratch_f32 = pltpu.VMEM((tb, W_GL), F32)
    return pl.pallas_call(
        _gl_kernel,
        grid=(bsz, steps),
        in_specs=[
            slab, slab, slab, slab, slab,
            pl.BlockSpec((1, W_GL), const),
            pl.BlockSpec((SUB * PAIR, PAIR), const),
        ],
        out_specs=pl.BlockSpec((tb, W_GL), row),
        out_shape=jax.ShapeDtypeStruct((bsz * seq, W_GL), BF16),
        scratch_shapes=[
            pltpu.VMEM((N_GL_PAIRS, PAIR, PAIR), F32),
            scratch_f32, scratch_f32,
            pltpu.VMEM((tb, SUB * PAIR), BF16),
            scratch_f32,
            pltpu.VMEM((tb // CHUNK * N_GL_PAIRS * PAIR, PAIR), F32),
            pltpu.VMEM((tb // CHUNK * 8, W_GL), F32),
        ],
        compiler_params=pltpu.CompilerParams(
            dimension_semantics=("arbitrary", "arbitrary"), vmem_limit_bytes=VMEM_LIMIT),
        name="hgrn_gla_mixer",
    )(q, k, v, lg, gate, norm_g, _gl_indicator())


def _rwkv_kernel(r_s, lw_s, k_s, v_s, kk_s, ka_s, hg_s, gup_ref, rk_ref, gng_ref, gnb_ref,
                 out_ref, st_ref, y_s, rt_s, bh_s, kh_s, nb_s, nk_s, w_s, u0_s, eend_s):
    tb = r_s.shape[0]
    n_chunks = tb // CHUNK

    @pl.when(pl.program_id(1) == 0)
    def _():
        st_ref[...] = jnp.zeros_like(st_ref)

    seg = _seg_mean_matrix(W_RWKV)
    masks = _pair_masks()
    tril = _tril_bf16()
    eye = jnp.where(masks["eye"], 1.0, 0.0)

    def phase_a(i, carry):
        units = []
        for cc in range(A_GROUP):
            c = i * A_GROUP + cc
            rows = pl.ds(pl.multiple_of(c * CHUNK, CHUNK), CHUNK)
            g_all = _dot_exact_lhs(tril, lw_s[rows, :], 2)
            for p in range(N_RW_PAIRS):
                lanes = slice(p * PAIR, (p + 1) * PAIR)
                units.append(dict(c=c, rows=rows, lanes=lanes, g=g_all[:, lanes]))
        for un in units:
            rows, lanes, g = un["rows"], un["lanes"], un["g"]
            kc = k_s[rows, lanes]
            kkc = kk_s[rows, lanes]
            ka = ka_s[rows, lanes]
            g_end = g[CHUNK - 1:CHUNK, :]
            e_neg = jnp.exp(-g)
            e_end = jnp.exp(g_end - g)
            un["at"] = -kkc * jnp.exp(g - lw_s[rows, lanes])
            rt = r_s[rows, lanes] * jnp.exp(g)
            un["rt"] = rt
            un["bt"] = ka * e_neg
            un["kt"] = kc * e_neg
            rt_s[rows, lanes] = rt
            bh_s[rows, lanes] = ka * e_end
            kh_s[rows, lanes] = kc * e_end
            eend_s[pl.ds(pl.multiple_of(un["c"] * 8, 8), 8), lanes] = jnp.broadcast_to(jnp.exp(g_end), (8, PAIR))
        for un in units:
            gram = _dot1(jnp.concatenate([un["at"], un["rt"]], axis=0),
                         jnp.concatenate([_stack_heads(un["bt"]), _stack_heads(un["kt"])], axis=0), NT)
            un["l"] = jnp.where(masks["strict"], gram[0:CHUNK, 0:PAIR], 0.0)
            un["mk"] = jnp.where(masks["strict"], gram[0:CHUNK, PAIR:2 * PAIR], 0.0)
            nb_s[un["rows"], un["lanes"]] = jnp.where(masks["incl"], gram[CHUNK:2 * CHUNK, 0:PAIR], 0.0)
            nk_s[un["rows"], un["lanes"]] = jnp.where(masks["incl"], gram[CHUNK:2 * CHUNK, PAIR:2 * PAIR], 0.0)
        for un in units:
            un["t"] = eye + un["l"]
            un["lp"] = un["l"]
        for _ in range(5):
            for un in units:
                un["lp"] = _dot1(un["lp"], _stack_heads(un["lp"]))
            for un in units:
                un["t"] = un["t"] + _dot1(un["t"], _stack_heads(un["lp"]))
        for un in units:
            un["mv"] = _dot1(un["mk"], _stack_heads(v_s[un["rows"], un["lanes"]]))
        for un in units:
            wu = _dot1(un["t"], jnp.concatenate([_stack_heads(un["at"]), _stack_heads(un["mv"])], axis=1))
            w_s[un["rows"], un["lanes"]] = wu[:, 0:PAIR]
            u0_s[un["rows"], un["lanes"]] = wu[:, PAIR:2 * PAIR]
        return carry

    lax.fori_loop(0, n_chunks // A_GROUP, phase_a, 0)

    def phase_b(c, carry):
        rows = pl.ds(pl.multiple_of(c * CHUNK, CHUNK), CHUNK)
        pairs = [slice(p * PAIR, (p + 1) * PAIR) for p in range(N_RW_PAIRS)]
        sts = [st_ref[p] for p in range(N_RW_PAIRS)]
        vs = [v_s[rows, lanes] for lanes in pairs]
        us = [_dot1(w_s[rows, lanes], st, NT) + u0_s[rows, lanes] for lanes, st in zip(pairs, sts)]
        upds = [_dot1(jnp.concatenate([u, vc], axis=0),
                      jnp.concatenate([bh_s[rows, lanes], kh_s[rows, lanes]], axis=0), TN)
                for lanes, u, vc in zip(pairs, us, vs)]
        for p, (lanes, st, upd) in enumerate(zip(pairs, sts, upds)):
            e_end = eend_s[pl.ds(pl.multiple_of(c * 8, 8), 8), lanes][0:1, :]
            st_ref[p] = st * e_end + jnp.where(masks["bd"], upd, 0.0)
        for lanes, st, u, vc in zip(pairs, sts, us, vs):
            y_s[rows, lanes] = (
                _dot1(rt_s[rows, lanes], st, NT)
                + _dot1(jnp.concatenate([nb_s[rows, lanes], nk_s[rows, lanes]], axis=1),
                        jnp.concatenate([_stack_heads(u), _stack_heads(vc)], axis=0)))
        return carry

    lax.fori_loop(0, n_chunks, phase_b, 0)

    y = y_s[...]
    mean = _dot_exact_rhs(y, seg, 2) * (1.0 / HEAD_DIM)
    d = y - mean
    var = _dot_exact_rhs(d * d, seg, 1) * (1.0 / HEAD_DIM)
    yn = d * lax.rsqrt(var + RWKV_GN_EPS) * gng_ref[...] + gnb_ref[...]
    bonus = _dot_exact_rhs(r_s[...] * k_s[...] * rk_ref[...], seg, 2) * v_s[...]
    gate = _dot1(hg_s[...], gup_ref[...])
    out_ref[...] = ((yn + bonus) * gate).astype(out_ref.dtype)


def _rwkv_mixer(r, lw, k, v, kk, ka, hg, g_up, r_k, gn_g, gn_b, bsz, seq, tb):
    steps = seq // tb
    row = lambda b, i: (b * steps + i, 0)
    const = lambda b, i: (0, 0)
    vec = pl.BlockSpec((1, W_RWKV), const)
    tok = pl.BlockSpec((tb, W_RWKV), row)
    slab = pltpu.VMEM((tb, W_RWKV), F32)
    return pl.pallas_call(
        _rwkv_kernel,
        grid=(bsz, steps),
        in_specs=[
            tok, tok, tok, tok, tok, tok,
            pl.BlockSpec((tb, RWKV_GATE_RANK), row),
            pl.BlockSpec((RWKV_GATE_RANK, W_RWKV), const),
            vec, vec, vec,
        ],
        out_specs=pl.BlockSpec((tb, W_RWKV), row),
        out_shape=jax.ShapeDtypeStruct((bsz * seq, W_RWKV), BF16),
        scratch_shapes=[
            pltpu.VMEM((N_RW_PAIRS, PAIR, PAIR), F32),
            slab,
            slab, slab, slab, slab, slab, slab, slab,
            pltpu.VMEM((tb // CHUNK * 8, W_RWKV), F32),
        ],
        compiler_params=pltpu.CompilerParams(
            dimension_semantics=("arbitrary", "arbitrary"), vmem_limit_bytes=VMEM_LIMIT),
        name="rwkv7_mixer",
    )(r, lw, k, v, kk, ka, hg, g_up, r_k, gn_g, gn_b)


def _post_kernel(alpha, ff_chunk, x_ref, ogl_ref, orw_ref, mod_ref, wout_ref, ln1g_ref, ln1b_ref,
                 wup_ref, wdn_ref, ln2g_ref, ln2b_ref, out_ref):
    m = mod_ref[0]
    gate1, shift2, scale2, gate2 = m[2:3, :], m[3:4, :], m[4:5, :], m[5:6, :]
    o = _mm(ogl_ref[...], wout_ref[0:W_GL, :]) + _mm(orw_ref[...], wout_ref[W_GL:W_GL + W_RWKV, :])
    x1 = _layer_norm(alpha * x_ref[...] + (1.0 + gate1) * o, ln1g_ref[...], ln1b_ref[...])
    h = (x1 * (1.0 + scale2) + shift2).astype(BF16)
    d_ff = wup_ref.shape[1]
    acc = jnp.zeros(x1.shape, F32)
    for j in range(d_ff // ff_chunk):
        cols = slice(j * ff_chunk, (j + 1) * ff_chunk)
        u = jnp.maximum(_mm(h, wup_ref[:, cols]), 0.0)
        acc = acc + _mm((u * u).astype(BF16), wdn_ref[cols, :])
    out_ref[...] = _layer_norm(alpha * x1 + (1.0 + gate2) * acc, ln2g_ref[...], ln2b_ref[...])


def _post_mixer(x2, ogl, orw, mod_l, w_out, ln1_g, ln1_b, w_up, w_dn, ln2_g, ln2_b, alpha, seq, tm):
    m_rows, d = x2.shape
    d_ff = w_up.shape[1]
    steps_per_batch = seq // tm
    row = lambda i: (i, 0)
    const = lambda i: (0, 0)
    vec = pl.BlockSpec((1, d), const)
    resident = lambda shape: pl.BlockSpec(shape, const, pipeline_mode=pl.Buffered(1))
    return pl.pallas_call(
        functools.partial(_post_kernel, alpha, 512),
        grid=(m_rows // tm,),
        in_specs=[
            pl.BlockSpec((tm, d), row),
            pl.BlockSpec((tm, W_GL), row),
            pl.BlockSpec((tm, W_RWKV), row),
            pl.BlockSpec((1, 6, d), lambda i: (i // steps_per_batch, 0, 0)),
            resident((W_GL + W_RWKV, d)),
            vec, vec,
            resident((d, d_ff)),
            resident((d_ff, d)),
            vec, vec,
        ],
        out_specs=pl.BlockSpec((tm, d), row),
        out_shape=jax.ShapeDtypeStruct((m_rows, d), F32),
        compiler_params=pltpu.CompilerParams(
            dimension_semantics=("arbitrary",), vmem_limit_bytes=VMEM_LIMIT),
        name="outproj_mlp",
    )(x2, ogl, orw, mod_l, w_out, ln1_g, ln1_b, w_up, w_dn, ln2_g, ln2_b)


def kernel(x, c, hgrn_lb_logits, ada_w, ada_b, w_in, hgrn_norm_g, gla_alpha_up, gla_alpha_b, gla_norm_g,
           rwkv_mu, rwkv_w0, rwkv_w_up, rwkv_a0, rwkv_a_up, rwkv_g_up, rwkv_k_k, rwkv_k_a, rwkv_r_k,
           rwkv_gn_g, rwkv_gn_b, w_out, ln1_g, ln1_b, mlp_w_up, mlp_w_down, ln2_g, ln2_b):
    bsz, seq, d = x.shape
    depth = w_in.shape[0]
    alpha = (2.0 * depth) ** 0.25
    tm = min(512, seq)
    tb = min(512, seq)
    assert seq % tm == 0 and seq % tb == 0 and tb % CHUNK == 0
    assert w_in.shape[2] == N_HGRN_COLS + N_GLA_COLS + N_RWKV_COLS

    mod = _modulation(c, ada_w, ada_b).reshape(depth, bsz, 6, d)
    lbs = _lower_bounds(hgrn_lb_logits)

    gla_end = N_HGRN_COLS + N_GLA_COLS
    w_in_p = jnp.concatenate(
        [w_in[:, :, :gla_end], jnp.zeros((depth, d, LANES - GLA_GATE_RANK), w_in.dtype), w_in[:, :, gla_end:]],
        axis=2).astype(BF16)
    alpha_up_p = jnp.concatenate(
        [gla_alpha_up, jnp.zeros((depth, LANES - GLA_GATE_RANK, W_GLA), gla_alpha_up.dtype)], axis=1)
    zeros_r = jnp.zeros((depth, RWKV_DECAY_RANK, W_RWKV), rwkv_w_up.dtype)
    wa_blk = jnp.concatenate(
        [jnp.concatenate([rwkv_w_up, zeros_r], axis=2), jnp.concatenate([zeros_r, rwkv_a_up], axis=2)], axis=1)
    norm_g = jnp.concatenate([hgrn_norm_g, gla_norm_g], axis=1)
    w_out_b = w_out.astype(BF16)
    w_up_b = mlp_w_up.astype(BF16)
    w_dn_b = mlp_w_down.astype(BF16)
    vec = lambda a, l: a[l].reshape(1, -1)

    x2 = x.reshape(bsz * seq, d)
    for l in range(depth):
        gq, gk, gv, glg, ggate, rr, rlw, rk, rv, rkk, rka, rhg = _in_projection(
            x2, mod[l], w_in_p[l], vec(lbs, l), alpha_up_p[l], vec(gla_alpha_b, l), vec(rwkv_mu, l),
            vec(rwkv_w0, l), vec(rwkv_a0, l), wa_blk[l], vec(rwkv_k_k, l), vec(rwkv_k_a, l), seq, tm)
        ogl = _gl_mixer(gq, gk, gv, glg, ggate, vec(norm_g, l), bsz, seq, tb)
        orw = _rwkv_mixer(rr, rlw, rk, rv, rkk, rka, rhg, rwkv_g_up[l], vec(rwkv_r_k, l), vec(rwkv_gn_g, l),
                          vec(rwkv_gn_b, l), bsz, seq, tb)
        x2 = _post_mixer(x2, ogl, orw, mod[l], w_out_b[l], vec(ln1_g, l), vec(ln1_b, l), w_up_b[l], w_dn_b[l],
                         vec(ln2_g, l), vec(ln2_b, l), alpha, seq, tm)
    return x2.reshape(bsz, seq, d)
```

```python
import functools

import numpy as np
import jax
import jax.numpy as jnp
from jax import lax
from jax.experimental import pallas as pl
from jax.experimental.pallas import tpu as pltpu

F32 = jnp.float32
BF16 = jnp.bfloat16

HEAD_DIM = 64
H_HGRN, H_GLA, H_RWKV = 4, 6, 6
W_HGRN, W_GLA, W_RWKV = H_HGRN * HEAD_DIM, H_GLA * HEAD_DIM, H_RWKV * HEAD_DIM
GLA_GATE_RANK = 16
GLA_GATE_NORMALIZER = 16.0
RWKV_DECAY_RANK, RWKV_ICLR_RANK, RWKV_GATE_RANK = 64, 64, 128
RWKV_GN_EPS = 64e-5
N_HGRN_COLS = 4 * W_HGRN
N_GLA_COLS = 4 * W_GLA + GLA_GATE_RANK
N_RWKV_COLS = 3 * W_RWKV + RWKV_DECAY_RANK + RWKV_ICLR_RANK + RWKV_GATE_RANK
CHUNK = 64
LN_EPS = 1e-5
RMS_EPS = 1e-5
F_MIN = 1e-30

LANES = 128
PAIR = 2 * HEAD_DIM
SUB = 8
LEVEL_HALVES = (SUB, 2 * SUB, 4 * SUB)
assert 2 * LEVEL_HALVES[-1] == CHUNK
A_GROUP_GL = 4
LOG2E = 1.4426950408889634
VMEM_LIMIT = 56 * 1024 * 1024

W_GL = W_HGRN + W_GLA
N_GL_PAIRS = W_GL // PAIR
N_RW_PAIRS = W_RWKV // PAIR
GLA_Z = 4 * W_GLA + LANES
Z_COLS = N_HGRN_COLS + GLA_Z + N_RWKV_COLS

NN = (((1,), (0,)), ((), ()))
NT = (((1,), (1,)), ((), ()))
TN = (((0,), (0,)), ((), ()))


def _mm(a, b, dims=NN):
    return lax.dot_general(a, b, dims, preferred_element_type=F32)


def _split2(x):
    hi = x.astype(BF16)
    lo = (x - hi.astype(F32)).astype(BF16)
    return hi, lo


def _bf16_parts(x, n):
    parts = []
    for i in range(n):
        p = x.astype(BF16)
        parts.append(p)
        if i + 1 < n:
            x = x - p.astype(F32)
    return parts


def _dot1(a, b, dims=NN):
    return _mm(a.astype(BF16), b.astype(BF16), dims)


def _dot3(a, b, dims=NN):
    ah, al = _split2(a)
    bh, bl = _split2(b)
    return _mm(ah, bh, dims) + (_mm(ah, bl, dims) + _mm(al, bh, dims))


def _dot_exact_lhs(a_bf16, b, n, dims=NN):
    return functools.reduce(lambda x, y: x + y, [_mm(a_bf16, bp, dims) for bp in _bf16_parts(b, n)])


def _sigmoid(x):
    return 1.0 / (1.0 + jnp.exp(-x))


def _silu(x):
    return x * _sigmoid(x)


def _softplus(x):
    return jnp.maximum(x, 0.0) + jnp.log1p(jnp.exp(-jnp.abs(x)))


def _log_sigmoid(x):
    return -_softplus(-x)


def _layer_norm(y, g, b):
    mu = jnp.mean(y, axis=-1, keepdims=True)
    d = y - mu
    var = jnp.mean(d * d, axis=-1, keepdims=True)
    return d * lax.rsqrt(var + LN_EPS) * g + b


def _iota(shape, axis):
    return lax.broadcasted_iota(jnp.int32, shape, axis)


def _stack_heads(x):
    lane = _iota(x.shape, 1)
    return jnp.concatenate([jnp.where(lane < HEAD_DIM, x, 0.0), jnp.where(lane >= HEAD_DIM, x, 0.0)], axis=0)


def _pair_masks():
    t = _iota((CHUNK, PAIR), 0)
    s = _iota((CHUNK, PAIR), 1) % HEAD_DIM
    r2 = _iota((PAIR, PAIR), 0) // HEAD_DIM
    c2 = _iota((PAIR, PAIR), 1) // HEAD_DIM
    return dict(
        strict=s < t,
        incl=s <= t,
        eye=s == t,
        diag=(s // SUB == t // SUB) & (s <= t),
        level={hs: (s // (2 * hs) == t // (2 * hs)) & ((t // hs) % 2 == 1) & ((s // hs) % 2 == 0)
               for hs in LEVEL_HALVES},
        bd=r2 == c2,
    )


def _tril_bf16():
    return jnp.where(_iota((CHUNK, CHUNK), 1) <= _iota((CHUNK, CHUNK), 0), 1.0, 0.0).astype(BF16)


def _seg_mean_matrix(width):
    r = _iota((width, width), 0) // HEAD_DIM
    c = _iota((width, width), 1) // HEAD_DIM
    return jnp.where(r == c, 1.0, 0.0).astype(BF16)


def _head_sums(xs):
    tiles = [x[:, j:j + LANES].astype(BF16) for x in xs for j in range(0, x.shape[1], LANES)]
    sums = []
    for i in range(0, len(tiles) - 1, 2):
        s = _mm(jnp.concatenate(tiles[i:i + 2], axis=1), _seg_mean_matrix(2 * LANES))
        sums += [s[:, 0:LANES], s[:, LANES:2 * LANES]]
    if len(tiles) % 2:
        sums.append(_mm(tiles[-1], _seg_mean_matrix(LANES)))
    out, i = [], 0
    for x in xs:
        n = x.shape[1] // LANES
        out.append(jnp.concatenate(sums[i:i + n], axis=1))
        i += n
    return out


def _mod_kernel(c_ref, w_ref, b_ref, o_ref):
    c = _silu(c_ref[...])
    o_ref[0] = _dot3(c, w_ref[0]) + b_ref[0]


def _modulation(c, ada_w, ada_b):
    depth, d, n = ada_w.shape
    bsz = c.shape[0]
    tn = 1536
    return pl.pallas_call(
        _mod_kernel,
        grid=(depth, n // tn),
        in_specs=[
            pl.BlockSpec((bsz, d), lambda l, j: (0, 0)),
            pl.BlockSpec((1, d, tn), lambda l, j: (l, 0, j)),
            pl.BlockSpec((1, 1, tn), lambda l, j: (l, 0, j)),
        ],
        out_specs=pl.BlockSpec((1, bsz, tn), lambda l, j: (l, 0, j)),
        out_shape=jax.ShapeDtypeStruct((depth, bsz, n), F32),
        compiler_params=pltpu.CompilerParams(
            dimension_semantics=("arbitrary", "arbitrary"), vmem_limit_bytes=VMEM_LIMIT),
        name="adaln_modulation",
    )(c, ada_w, ada_b.reshape(depth, 1, n))


def _lower_bound_kernel(x_ref, o_ref):
    depth = x_ref.shape[0]
    rows = [x_ref[l:l + 1, :] for l in range(depth)]
    m = functools.reduce(jnp.maximum, rows)
    e = [jnp.exp(r - m) for r in rows]
    tot = functools.reduce(lambda a, b: a + b, e)
    p = [ei / tot for ei in e]
    acc = jnp.zeros_like(p[0])
    for l in range(depth):
        acc = acc + p[l]
        o_ref[l:l + 1, :] = acc - p[0]


def _lower_bounds(logits):
    return pl.pallas_call(
        _lower_bound_kernel,
        out_shape=jax.ShapeDtypeStruct(logits.shape, F32),
        name="hgrn_lower_bounds",
    )(logits.astype(F32))


def _inproj_kernel(steps_per_batch, x_ref, mod_ref, w_ref, lb_ref, au_ref, ab_ref, mu_ref, w0_ref, a0_ref,
                   wa_ref, kk_ref, ka_ref,
                   gq_ref, gk_ref, gv_ref, gb_ref, gg_ref,
                   rr_ref, rlw_ref, rk_ref, rv_ref, rkk_ref, rka_ref, rhg_ref, carry_s):
    tm = x_ref.shape[0]
    w3 = W_RWKV

    @pl.when(pl.program_id(0) % steps_per_batch == 0)
    def _():
        carry_s[...] = jnp.zeros_like(carry_s)

    m = mod_ref[0]
    h = (x_ref[...] * (1.0 + m[1:2, :]) + m[0:1, :]).astype(BF16)

    def proj(c0, width):
        return _mm(h, w_ref[:, c0:c0 + width])

    first_row = _iota((tm, LANES), 0) == 0

    def shifted(c0, width):
        z = proj(N_HGRN_COLS + GLA_Z + c0, width)
        z_prev = pltpu.roll(z, 1, axis=0)
        z_prev = jnp.where(jnp.concatenate([first_row] * (width // LANES), axis=1), carry_s[:, c0:c0 + width], z_prev)
        carry_s[:, c0:c0 + width] = z[tm - 1:tm, :]
        return z + (z_prev - z) * mu_ref[:, c0:c0 + width]

    z_v = shifted(2 * w3, w3 + LANES)
    h_wa = z_v[:, w3:w3 + LANES]
    h_wa = jnp.where(_iota(h_wa.shape, 1) < RWKV_DECAY_RANK, jnp.tanh(h_wa), h_wa)
    wa = _dot1(h_wa, wa_ref[...])
    w_log = -_softplus(-(w0_ref[...] + wa[:, 0:w3])) - 0.5
    rlw_ref[...] = -jnp.exp(w_log)
    a = _sigmoid(a0_ref[...] + wa[:, w3:2 * w3])
    rv_ref[...] = z_v[:, 0:w3]
    z_rk = shifted(0, 2 * w3)
    k = z_rk[:, w3:2 * w3]
    kk = k * kk_ref[...]
    kk = kk / jnp.maximum(jnp.sqrt(_head_sums([kk * kk])[0]), 1e-12)
    rkk_ref[...] = kk
    rka_ref[...] = kk * a
    rk_ref[...] = k * (1.0 + (a - 1.0) * ka_ref[...])
    rr_ref[...] = z_rk[:, 0:w3]
    rhg_ref[...] = _sigmoid(shifted(3 * w3 + LANES, RWKV_GATE_RANK)).astype(BF16)

    g0 = N_HGRN_COLS
    logit = _dot1(proj(g0 + 4 * W_GLA, LANES), au_ref[...]) + ab_ref[...]
    gb_ref[:, W_HGRN:W_GL] = _log_sigmoid(logit) / GLA_GATE_NORMALIZER
    z_qk = proj(g0, 2 * W_GLA)
    gq_ref[:, W_HGRN:W_GL] = z_qk[:, 0:W_GLA] * HEAD_DIM ** -0.5
    gk_ref[:, W_HGRN:W_GL] = z_qk[:, W_GLA:2 * W_GLA]
    z_vr = proj(g0 + 2 * W_GLA, 2 * W_GLA)
    gv_ref[:, W_HGRN:W_GL] = z_vr[:, 0:W_GLA].astype(BF16)
    gg_ref[:, W_HGRN:W_GL] = _silu(z_vr[:, W_GLA:2 * W_GLA]).astype(BF16)

    lb = lb_ref[...]
    zf = proj(W_HGRN, W_HGRN)
    f = lb + (1.0 - lb) * _sigmoid(zf)
    gb_ref[:, 0:W_HGRN] = jnp.log(jnp.maximum(f, F_MIN))
    gk_ref[:, 0:W_HGRN] = (1.0 - lb) * _sigmoid(-zf)
    gq_ref[:, 0:W_HGRN] = _silu(proj(0, W_HGRN)) * HEAD_DIM ** -0.5
    gg_ref[:, 0:W_HGRN] = _silu(proj(3 * W_HGRN, W_HGRN)).astype(BF16)
    gv_ref[:, 0:W_HGRN] = proj(2 * W_HGRN, W_HGRN).astype(BF16)


def _in_projection(x2, mod_l, w_in_p, lb_l, alpha_up_p, alpha_b, mu, w0, a0, wa_blk, k_k, k_a, seq, tm, layer):
    m_rows, d = x2.shape
    steps_per_batch = seq // tm
    row = lambda i: (i, 0)
    const = lambda i: (0, 0)
    vec = lambda n: pl.BlockSpec((1, n), const)
    out = lambda n, dt: (pl.BlockSpec((tm, n), row), jax.ShapeDtypeStruct((m_rows, n), dt))
    outs = [out(W_GL, F32), out(W_GL, F32), out(W_GL, BF16), out(W_GL, F32), out(W_GL, BF16),
            out(W_RWKV, F32), out(W_RWKV, F32), out(W_RWKV, F32), out(W_RWKV, F32), out(W_RWKV, F32),
            out(W_RWKV, F32), out(RWKV_GATE_RANK, BF16)]
    return pl.pallas_call(
        functools.partial(_inproj_kernel, steps_per_batch),
        grid=(m_rows // tm,),
        in_specs=[
            pl.BlockSpec((tm, d), row),
            pl.BlockSpec((1, 6, d), lambda i: (i // steps_per_batch, 0, 0)),
            pl.BlockSpec((None, d, Z_COLS), lambda i: (layer, 0, 0), pipeline_mode=pl.Buffered(1)),
            vec(W_HGRN),
            pl.BlockSpec((LANES, W_GLA), const),
            vec(W_GLA),
            vec(N_RWKV_COLS), vec(W_RWKV), vec(W_RWKV),
            pl.BlockSpec((LANES, 2 * W_RWKV), const),
            vec(W_RWKV), vec(W_RWKV),
        ],
        out_specs=[o[0] for o in outs],
        out_shape=[o[1] for o in outs],
        scratch_shapes=[pltpu.VMEM((1, N_RWKV_COLS), F32)],
        compiler_params=pltpu.CompilerParams(
            dimension_semantics=("arbitrary",), vmem_limit_bytes=VMEM_LIMIT),
        name="in_projection",
    )(x2, mod_l, w_in_p, lb_l, alpha_up_p, alpha_b, mu, w0, a0, wa_blk, k_k, k_a)


def _gl_kernel(q_s, k_s, v_s, lg_ref, gate_ref, ng_ref, ind_ref, out_ref,
               st_ref, b_s, ad_s, e_s, o_s, sc_s, eend_s):
    tb = q_s.shape[0]
    n_chunks = tb // CHUNK

    @pl.when(pl.program_id(1) == 0)
    def _():
        st_ref[...] = jnp.zeros_like(st_ref)

    tril = _tril_bf16()
    for c in range(n_chunks):
        rows = slice(c * CHUNK, (c + 1) * CHUNK)
        b_s[rows, :] = _dot_exact_lhs(tril, lg_ref[rows, :], 2)

    for p in range(N_GL_PAIRS):
        lanes = slice(p * PAIR, (p + 1) * PAIR)
        q3 = q_s[:, lanes].reshape(tb // SUB, SUB, PAIR)
        k3 = k_s[:, lanes].reshape(tb // SUB, SUB, PAIR)
        b3 = (b_s[:, lanes] * LOG2E).reshape(tb // SUB, SUB, PAIR)
        for sg in range(SUB):
            dec = jnp.exp2(jnp.minimum(b3 - b3[:, sg:sg + 1, :], 0.0))
            e = q3 * k3[:, sg:sg + 1, :] * dec
            e_s[:, sg * PAIR:(sg + 1) * PAIR] = e.reshape(tb, PAIR).astype(BF16)
        ad_s[:, lanes] = _mm(e_s[...], ind_ref[...])

    masks = _pair_masks()
    t_idx = _iota((CHUNK, PAIR), 0)
    second_half = {hs: (t_idx // hs) % 2 == 1 for hs in LEVEL_HALVES}
    level_sign = {hs: jnp.where(second_half[hs], LOG2E, -LOG2E) for hs in LEVEL_HALVES}

    def group_units(i):
        units = []
        for cc in range(A_GROUP_GL):
            c = i * A_GROUP_GL + cc
            rows = pl.ds(pl.multiple_of(c * CHUNK, CHUNK), CHUNK)
            for p in range(N_GL_PAIRS):
                units.append(dict(rows=rows, lanes=slice(p * PAIR, (p + 1) * PAIR),
                                  e_rows=pl.ds(pl.multiple_of(c * 8, 8), 8),
                                  st_rows=pl.ds(pl.multiple_of((c * N_GL_PAIRS + p) * PAIR, PAIR), PAIR)))
        return units

    def increments(i, carry):
        units = group_units(i)
        for un in units:
            b = b_s[un["rows"], un["lanes"]]
            b_end = b[CHUNK - 1:CHUNK, :]
            un["k_end"] = k_s[un["rows"], un["lanes"]] * jnp.exp(b_end - b)
            eend_s[un["e_rows"], un["lanes"]] = jnp.broadcast_to(jnp.exp(b_end), (8, PAIR))
        for un in units:
            upd = _dot1(v_s[un["rows"], un["lanes"]], un["k_end"], TN)
            sc_s[un["st_rows"], :] = jnp.where(masks["bd"], upd, 0.0)
        return carry

    lax.fori_loop(0, n_chunks // A_GROUP_GL, increments, 0)

    sts = [st_ref[p] for p in range(N_GL_PAIRS)]
    for c in range(n_chunks):
        for p in range(N_GL_PAIRS):
            st_rows = slice((c * N_GL_PAIRS + p) * PAIR, (c * N_GL_PAIRS + p + 1) * PAIR)
            upd = sc_s[st_rows, :]
            sc_s[st_rows, :] = sts[p]
            sts[p] = sts[p] * eend_s[c * 8:c * 8 + 1, p * PAIR:(p + 1) * PAIR] + upd
    for p in range(N_GL_PAIRS):
        st_ref[p] = sts[p]

    def outputs(i, carry):
        units = group_units(i)
        for un in units:
            q = q_s[un["rows"], un["lanes"]]
            k = k_s[un["rows"], un["lanes"]]
            b = b_s[un["rows"], un["lanes"]]
            un["lv"] = []
            for hs in LEVEL_HALVES:
                e_mid = jnp.concatenate(
                    [jnp.broadcast_to(b[m:m + 1, :], (2 * hs, PAIR)) for m in range(hs - 1, CHUNK, 2 * hs)], axis=0)
                dec = jnp.exp2(jnp.minimum((b - e_mid) * level_sign[hs], 0.0))
                un["lv"].append(jnp.where(second_half[hs], q, k) * dec)
            un["q_in"] = q * jnp.exp(b)
        for un in units:
            a = jnp.where(masks["diag"], ad_s[un["rows"], un["lanes"]], 0.0)
            for hs, lv in zip(LEVEL_HALVES, un["lv"]):
                a = a + jnp.where(masks["level"][hs], _dot1(lv, _stack_heads(lv), NT), 0.0)
            un["a"] = a
        for un in units:
            o_s[un["rows"], un["lanes"]] = (_dot1(un["a"], _stack_heads(v_s[un["rows"], un["lanes"]]))
                                            + _dot1(un["q_in"], sc_s[un["st_rows"], :], NT))
        return carry

    lax.fori_loop(0, n_chunks // A_GROUP_GL, outputs, 0)

    o = o_s[...]
    ms = _head_sums([o * o])[0] * (1.0 / HEAD_DIM)
    out_ref[...] = (o * lax.rsqrt(ms + RMS_EPS) * ng_ref[...] * gate_ref[...]).astype(out_ref.dtype)


def _gl_indicator():
    r = np.arange(SUB * PAIR)
    c = np.arange(PAIR)
    sg = r // PAIR
    h = (r % PAIR) // HEAD_DIM
    ind = (h[:, None] == (c // HEAD_DIM)[None, :]) & (sg[:, None] == (c % SUB)[None, :])
    return jnp.asarray(ind, dtype=BF16)


def _gl_mixer(q, k, v, lg, gate, norm_g, bsz, seq, tb):
    steps = seq // tb
    row = lambda b, i: (b * steps + i, 0)
    const = lambda b, i: (0, 0)
    slab = pl.BlockSpec((tb, W_GL), row)
    scratch_f32 = pltpu.VMEM((tb, W_GL), F32)
    return pl.pallas_call(
        _gl_kernel,
        grid=(bsz, steps),
        in_specs=[
            slab, slab, slab, slab, slab,
            pl.BlockSpec((1, W_GL), const),
            pl.BlockSpec((SUB * PAIR, PAIR), const),
        ],
        out_specs=pl.BlockSpec((tb, W_GL), row),
        out_shape=jax.ShapeDtypeStruct((bsz * seq, W_GL), BF16),
        scratch_shapes=[
            pltpu.VMEM((N_GL_PAIRS, PAIR, PAIR), F32),
            scratch_f32, scratch_f32,
            pltpu.VMEM((tb, SUB * PAIR), BF16),
            scratch_f32,
            pltpu.VMEM((tb // CHUNK * N_GL_PAIRS * PAIR, PAIR), F32),
            pltpu.VMEM((tb // CHUNK * 8, W_GL), F32),
        ],
        compiler_params=pltpu.CompilerParams(
            dimension_semantics=("arbitrary", "arbitrary"), vmem_limit_bytes=VMEM_LIMIT),
        name="hgrn_gla_mixer",
    )(q, k, v, lg, gate, norm_g, _gl_indicator())


def _rwkv_kernel(r_s, lw_s, k_s, v_s, kk_s, ka_s, hg_s, gup_ref, rk_ref, gng_ref, gnb_ref,
                 out_ref, st_ref, y_s, rt_s, bh_s, kh_s, nb_s, nk_s, w_s, u0_s, eend_s):
    bsz, tb = r_s.shape[0], r_s.shape[1]
    n_chunks = tb // CHUNK

    @pl.when(pl.program_id(0) == 0)
    def _():
        st_ref[...] = jnp.zeros_like(st_ref)

    masks = _pair_masks()
    tril = _tril_bf16()
    eye = jnp.where(masks["eye"], 1.0, 0.0)

    def phase_a(b, carry):
        units = []
        for c in range(n_chunks):
            rows = slice(c * CHUNK, (c + 1) * CHUNK)
            g_all = _dot_exact_lhs(tril, lw_s[b, rows, :], 2)
            for p in range(N_RW_PAIRS):
                lanes = slice(p * PAIR, (p + 1) * PAIR)
                units.append(dict(c=c, rows=rows, lanes=lanes, g=g_all[:, lanes]))
        for un in units:
            rows, lanes, g = un["rows"], un["lanes"], un["g"]
            kc = k_s[b, rows, lanes]
            kkc = kk_s[b, rows, lanes]
            ka = ka_s[b, rows, lanes]
            g_end = g[CHUNK - 1:CHUNK, :]
            e_neg = jnp.exp(-g)
            e_end = jnp.exp(g_end - g)
            un["at"] = -kkc * jnp.exp(g - lw_s[b, rows, lanes])
            rt = r_s[b, rows, lanes] * jnp.exp(g)
            un["rt"] = rt
            un["bt"] = ka * e_neg
            un["kt"] = kc * e_neg
            rt_s[b, rows, lanes] = rt
            bh_s[b, rows, lanes] = ka * e_end
            kh_s[b, rows, lanes] = kc * e_end
            eend_s[b, un["c"] * 8:un["c"] * 8 + 8, lanes] = jnp.broadcast_to(jnp.exp(g_end), (8, PAIR))
        for un in units:
            gram = _dot1(jnp.concatenate([un["at"], un["rt"]], axis=0),
                         jnp.concatenate([_stack_heads(un["bt"]), _stack_heads(un["kt"])], axis=0), NT)
            un["l"] = jnp.where(masks["strict"], gram[0:CHUNK, 0:PAIR], 0.0)
            un["mk"] = jnp.where(masks["strict"], gram[0:CHUNK, PAIR:2 * PAIR], 0.0)
            nb_s[b, un["rows"], un["lanes"]] = jnp.where(masks["incl"], gram[CHUNK:2 * CHUNK, 0:PAIR], 0.0)
            nk_s[b, un["rows"], un["lanes"]] = jnp.where(masks["incl"], gram[CHUNK:2 * CHUNK, PAIR:2 * PAIR], 0.0)
        for un in units:
            un["t"] = eye + un["l"]
            un["lp"] = un["l"]
        for _ in range(5):
            for un in units:
                un["lp"] = _dot1(un["lp"], _stack_heads(un["lp"]))
            for un in units:
                un["t"] = un["t"] + _dot1(un["t"], _stack_heads(un["lp"]))
        for un in units:
            un["mv"] = _dot1(un["mk"], _stack_heads(v_s[b, un["rows"], un["lanes"]]))
        for un in units:
            wu = _dot1(un["t"], jnp.concatenate([_stack_heads(un["at"]), _stack_heads(un["mv"])], axis=1))
            w_s[b, un["rows"], un["lanes"]] = wu[:, 0:PAIR]
            u0_s[b, un["rows"], un["lanes"]] = wu[:, PAIR:2 * PAIR]
        return carry

    lax.fori_loop(0, bsz, phase_a, 0)

    def phase_b(c, carry):
        rows = pl.ds(pl.multiple_of(c * CHUNK, CHUNK), CHUNK)
        e_rows = pl.ds(pl.multiple_of(c * 8, 8), 8)
        seqs = [(b, p, slice(p * PAIR, (p + 1) * PAIR)) for b in range(bsz) for p in range(N_RW_PAIRS)]
        sts = [st_ref[b * N_RW_PAIRS + p] for b, p, _ in seqs]
        vs = [v_s[b, rows, lanes] for b, _, lanes in seqs]
        us = [_dot1(w_s[b, rows, lanes], st, NT) + u0_s[b, rows, lanes] for (b, _, lanes), st in zip(seqs, sts)]
        upds = [_dot1(jnp.concatenate([u, vc], axis=0),
                      jnp.concatenate([bh_s[b, rows, lanes], kh_s[b, rows, lanes]], axis=0), TN)
                for (b, _, lanes), u, vc in zip(seqs, us, vs)]
        for (b, p, lanes), st, upd in zip(seqs, sts, upds):
            e_end = eend_s[b, e_rows, lanes][0:1, :]
            st_ref[b * N_RW_PAIRS + p] = st * e_end + jnp.where(masks["bd"], upd, 0.0)
        for (b, _, lanes), st, u, vc in zip(seqs, sts, us, vs):
            y_s[b, rows, lanes] = (
                _dot1(rt_s[b, rows, lanes], st, NT)
                + _dot1(jnp.concatenate([nb_s[b, rows, lanes], nk_s[b, rows, lanes]], axis=1),
                        jnp.concatenate([_stack_heads(u), _stack_heads(vc)], axis=0)))
        return carry

    lax.fori_loop(0, n_chunks, phase_b, 0)

    for b in range(bsz):
        y = y_s[b]
        s_hi, s_lo = _head_sums(_bf16_parts(y, 2))
        d = y - (s_hi + s_lo) * (1.0 / HEAD_DIM)
        s_var, s_bonus = _head_sums([d * d, r_s[b] * k_s[b] * rk_ref[...]])
        yn = d * lax.rsqrt(s_var * (1.0 / HEAD_DIM) + RWKV_GN_EPS) * gng_ref[...] + gnb_ref[...]
        gate = _dot1(hg_s[b], gup_ref[...])
        out_ref[b] = ((yn + s_bonus * v_s[b]) * gate).astype(out_ref.dtype)


def _rwkv_mixer(r, lw, k, v, kk, ka, hg, g_up, r_k, gn_g, gn_b, bsz, seq, tb):
    blk = lambda i: (0, i, 0)
    const = lambda i: (0, 0)
    vec = pl.BlockSpec((1, W_RWKV), const)
    tok = pl.BlockSpec((bsz, tb, W_RWKV), blk)
    slab = pltpu.VMEM((bsz, tb, W_RWKV), F32)
    seq_major = lambda a: a.reshape(bsz, seq, a.shape[-1])
    out = pl.pallas_call(
        _rwkv_kernel,
        grid=(seq // tb,),
        in_specs=[
            tok, tok, tok, tok, tok, tok,
            pl.BlockSpec((bsz, tb, RWKV_GATE_RANK), blk),
            pl.BlockSpec((RWKV_GATE_RANK, W_RWKV), const),
            vec, vec, vec,
        ],
        out_specs=pl.BlockSpec((bsz, tb, W_RWKV), blk),
        out_shape=jax.ShapeDtypeStruct((bsz, seq, W_RWKV), BF16),
        scratch_shapes=[
            pltpu.VMEM((bsz * N_RW_PAIRS, PAIR, PAIR), F32),
            slab,
            slab, slab, slab, slab, slab, slab, slab,
            pltpu.VMEM((bsz, tb // CHUNK * 8, W_RWKV), F32),
        ],
        compiler_params=pltpu.CompilerParams(
            dimension_semantics=("arbitrary",), vmem_limit_bytes=VMEM_LIMIT),
        name="rwkv7_mixer",
    )(*[seq_major(a) for a in (r, lw, k, v, kk, ka, hg)], g_up, r_k, gn_g, gn_b)
    return out.reshape(bsz * seq, W_RWKV)


def _post_kernel(alpha, ff_chunk, x_ref, ogl_ref, orw_ref, mod_ref, wout_ref, ln1g_ref, ln1b_ref,
                 wup_ref, wdn_ref, ln2g_ref, ln2b_ref, out_ref):
    m = mod_ref[0]
    gate1, shift2, scale2, gate2 = m[2:3, :], m[3:4, :], m[4:5, :], m[5:6, :]
    o = _mm(ogl_ref[...], wout_ref[0:W_GL, :]) + _mm(orw_ref[...], wout_ref[W_GL:W_GL + W_RWKV, :])
    x1 = _layer_norm(alpha * x_ref[...] + (1.0 + gate1) * o, ln1g_ref[...], ln1b_ref[...])
    h = (x1 * (1.0 + scale2) + shift2).astype(BF16)
    d_ff = wup_ref.shape[1]
    acc = jnp.zeros(x1.shape, F32)
    for j in range(d_ff // ff_chunk):
        cols = slice(j * ff_chunk, (j + 1) * ff_chunk)
        u = jnp.maximum(_mm(h, wup_ref[:, cols]), 0.0)
        acc = acc + _mm((u * u).astype(BF16), wdn_ref[cols, :])
    out_ref[...] = _layer_norm(alpha * x1 + (1.0 + gate2) * acc, ln2g_ref[...], ln2b_ref[...])


def _post_mixer(x2, ogl, orw, mod_l, w_out, ln1_g, ln1_b, w_up, w_dn, ln2_g, ln2_b, alpha, seq, tm, layer):
    m_rows, d = x2.shape
    d_ff = w_up.shape[2]
    steps_per_batch = seq // tm
    row = lambda i: (i, 0)
    const = lambda i: (0, 0)
    vec = pl.BlockSpec((1, d), const)
    resident = lambda shape: pl.BlockSpec((None,) + shape, lambda i: (layer, 0, 0), pipeline_mode=pl.Buffered(1))
    return pl.pallas_call(
        functools.partial(_post_kernel, alpha, 512),
        grid=(m_rows // tm,),
        in_specs=[
            pl.BlockSpec((tm, d), row),
            pl.BlockSpec((tm, W_GL), row),
            pl.BlockSpec((tm, W_RWKV), row),
            pl.BlockSpec((1, 6, d), lambda i: (i // steps_per_batch, 0, 0)),
            resident((W_GL + W_RWKV, d)),
            vec, vec,
            resident((d, d_ff)),
            resident((d_ff, d)),
            vec, vec,
        ],
        out_specs=pl.BlockSpec((tm, d), row),
        out_shape=jax.ShapeDtypeStruct((m_rows, d), F32),
        compiler_params=pltpu.CompilerParams(
            dimension_semantics=("arbitrary",), vmem_limit_bytes=VMEM_LIMIT),
        name="outproj_mlp",
    )(x2, ogl, orw, mod_l, w_out, ln1_g, ln1_b, w_up, w_dn, ln2_g, ln2_b)


def kernel(x, c, hgrn_lb_logits, ada_w, ada_b, w_in, hgrn_norm_g, gla_alpha_up, gla_alpha_b, gla_norm_g,
           rwkv_mu, rwkv_w0, rwkv_w_up, rwkv_a0, rwkv_a_up, rwkv_g_up, rwkv_k_k, rwkv_k_a, rwkv_r_k,
           rwkv_gn_g, rwkv_gn_b, w_out, ln1_g, ln1_b, mlp_w_up, mlp_w_down, ln2_g, ln2_b):
    bsz, seq, d = x.shape
    depth = w_in.shape[0]
    alpha = (2.0 * depth) ** 0.25
    tm = min(512, seq)
    tb = min(512, seq)
    assert seq % tm == 0 and seq % tb == 0 and tb % CHUNK == 0
    assert w_in.shape[2] == N_HGRN_COLS + N_GLA_COLS + N_RWKV_COLS

    mod = _modulation(c, ada_w, ada_b).reshape(depth, bsz, 6, d)
    lbs = _lower_bounds(hgrn_lb_logits)

    gla_end = N_HGRN_COLS + N_GLA_COLS
    w_in_p = jnp.concatenate(
        [w_in[:, :, :gla_end], jnp.zeros((depth, d, LANES - GLA_GATE_RANK), w_in.dtype), w_in[:, :, gla_end:]],
        axis=2).astype(BF16)
    alpha_up_p = jnp.concatenate(
        [gla_alpha_up, jnp.zeros((depth, LANES - GLA_GATE_RANK, W_GLA), gla_alpha_up.dtype)], axis=1)
    zeros_r = jnp.zeros((depth, RWKV_DECAY_RANK, W_RWKV), rwkv_w_up.dtype)
    wa_blk = jnp.concatenate(
        [jnp.concatenate([rwkv_w_up, zeros_r], axis=2), jnp.concatenate([zeros_r, rwkv_a_up], axis=2)], axis=1)
    norm_g = jnp.concatenate([hgrn_norm_g, gla_norm_g], axis=1)
    w_out_b = w_out.astype(BF16)
    w_up_b = mlp_w_up.astype(BF16)
    w_dn_b = mlp_w_down.astype(BF16)
    vec = lambda a, l: a[l].reshape(1, -1)

    x2 = x.reshape(bsz * seq, d)
    for l in range(depth):
        gq, gk, gv, glg, ggate, rr, rlw, rk, rv, rkk, rka, rhg = _in_projection(
            x2, mod[l], w_in_p, vec(lbs, l), alpha_up_p[l], vec(gla_alpha_b, l), vec(rwkv_mu, l),
            vec(rwkv_w0, l), vec(rwkv_a0, l), wa_blk[l], vec(rwkv_k_k, l), vec(rwkv_k_a, l), seq, tm, l)
        ogl = _gl_mixer(gq, gk, gv, glg, ggate, vec(norm_g, l), bsz, seq, tb)
        orw = _rwkv_mixer(rr, rlw, rk, rv, rkk, rka, rhg, rwkv_g_up[l], vec(rwkv_r_k, l), vec(rwkv_gn_g, l),
                          vec(rwkv_gn_b, l), bsz, seq, tb)
        x2 = _post_mixer(x2, ogl, orw, mod[l], w_out_b, vec(ln1_g, l), vec(ln1_b, l), w_up_b, w_dn_b,
                         vec(ln2_g, l), vec(ln2_b, l), alpha, seq, tm, l)
    return x2.reshape(bsz, seq, d)
```

```python
import functools

import numpy as np
import jax
import jax.numpy as jnp
from jax import lax
from jax.experimental import pallas as pl
from jax.experimental.pallas import tpu as pltpu

F32 = jnp.float32
BF16 = jnp.bfloat16

HEAD_DIM = 64
H_HGRN, H_GLA, H_RWKV = 4, 6, 6
W_HGRN, W_GLA, W_RWKV = H_HGRN * HEAD_DIM, H_GLA * HEAD_DIM, H_RWKV * HEAD_DIM
GLA_GATE_RANK = 16
GLA_GATE_NORMALIZER = 16.0
RWKV_DECAY_RANK, RWKV_ICLR_RANK, RWKV_GATE_RANK = 64, 64, 128
RWKV_GN_EPS = 64e-5
N_HGRN_COLS = 4 * W_HGRN
N_GLA_COLS = 4 * W_GLA + GLA_GATE_RANK
N_RWKV_COLS = 3 * W_RWKV + RWKV_DECAY_RANK + RWKV_ICLR_RANK + RWKV_GATE_RANK
CHUNK = 64
LN_EPS = 1e-5
RMS_EPS = 1e-5
F_MIN = 1e-30

LANES = 128
PAIR = 2 * HEAD_DIM
SUB = 8
LEVEL_HALVES = (SUB, 2 * SUB, 4 * SUB)
assert 2 * LEVEL_HALVES[-1] == CHUNK
A_GROUP_GL = 4
LOG2E = 1.4426950408889634
VMEM_LIMIT = 56 * 1024 * 1024

W_GL = W_HGRN + W_GLA
N_GL_PAIRS = W_GL // PAIR
N_RW_PAIRS = W_RWKV // PAIR

NN = (((1,), (0,)), ((), ()))
NT = (((1,), (1,)), ((), ()))
TN = (((0,), (0,)), ((), ()))


def _mm(a, b, dims=NN):
    return lax.dot_general(a, b, dims, preferred_element_type=F32)


def _split2(x):
    hi = x.astype(BF16)
    lo = (x - hi.astype(F32)).astype(BF16)
    return hi, lo


def _bf16_parts(x, n):
    parts = []
    for i in range(n):
        p = x.astype(BF16)
        parts.append(p)
        if i + 1 < n:
            x = x - p.astype(F32)
    return parts


def _dot1(a, b, dims=NN):
    return _mm(a.astype(BF16), b.astype(BF16), dims)


def _dot3(a, b, dims=NN):
    ah, al = _split2(a)
    bh, bl = _split2(b)
    return _mm(ah, bh, dims) + (_mm(ah, bl, dims) + _mm(al, bh, dims))


def _dot_exact_lhs(a_bf16, b, n, dims=NN):
    return functools.reduce(lambda x, y: x + y, [_mm(a_bf16, bp, dims) for bp in _bf16_parts(b, n)])


def _sigmoid(x):
    return 1.0 / (1.0 + jnp.exp(-x))


def _silu(x):
    return x * _sigmoid(x)


def _softplus(x):
    return jnp.maximum(x, 0.0) + jnp.log1p(jnp.exp(-jnp.abs(x)))


def _log_sigmoid(x):
    return -_softplus(-x)


def _layer_norm(y, g, b):
    mu = jnp.mean(y, axis=-1, keepdims=True)
    d = y - mu
    var = jnp.mean(d * d, axis=-1, keepdims=True)
    return d * lax.rsqrt(var + LN_EPS) * g + b


def _iota(shape, axis):
    return lax.broadcasted_iota(jnp.int32, shape, axis)


def _stack_heads(x):
    lane = _iota(x.shape, 1)
    return jnp.concatenate([jnp.where(lane < HEAD_DIM, x, 0.0), jnp.where(lane >= HEAD_DIM, x, 0.0)], axis=0)


def _pair_masks():
    t = _iota((CHUNK, PAIR), 0)
    s = _iota((CHUNK, PAIR), 1) % HEAD_DIM
    r2 = _iota((PAIR, PAIR), 0) // HEAD_DIM
    c2 = _iota((PAIR, PAIR), 1) // HEAD_DIM
    return dict(
        strict=s < t,
        incl=s <= t,
        eye=s == t,
        diag=(s // SUB == t // SUB) & (s <= t),
        level={hs: (s // (2 * hs) == t // (2 * hs)) & ((t // hs) % 2 == 1) & ((s // hs) % 2 == 0)
               for hs in LEVEL_HALVES},
        bd=r2 == c2,
    )


def _tril_bf16():
    return jnp.where(_iota((CHUNK, CHUNK), 1) <= _iota((CHUNK, CHUNK), 0), 1.0, 0.0).astype(BF16)


def _seg_mean_matrix(width):
    r = _iota((width, width), 0) // HEAD_DIM
    c = _iota((width, width), 1) // HEAD_DIM
    return jnp.where(r == c, 1.0, 0.0).astype(BF16)


def _head_sums(xs):
    tiles = [x[:, j:j + LANES].astype(BF16) for x in xs for j in range(0, x.shape[1], LANES)]
    sums = []
    for i in range(0, len(tiles) - 1, 2):
        s = _mm(jnp.concatenate(tiles[i:i + 2], axis=1), _seg_mean_matrix(2 * LANES))
        sums += [s[:, 0:LANES], s[:, LANES:2 * LANES]]
    if len(tiles) % 2:
        sums.append(_mm(tiles[-1], _seg_mean_matrix(LANES)))
    out, i = [], 0
    for x in xs:
        n = x.shape[1] // LANES
        out.append(jnp.concatenate(sums[i:i + n], axis=1))
        i += n
    return out


def _mod_kernel(c_ref, w_ref, b_ref, o_ref):
    c = _silu(c_ref[...])
    o_ref[0] = _dot3(c, w_ref[0]) + b_ref[0]


def _modulation(c, ada_w, ada_b):
    depth, d, n = ada_w.shape
    bsz = c.shape[0]
    tn = 1536
    return pl.pallas_call(
        _mod_kernel,
        grid=(depth, n // tn),
        in_specs=[
            pl.BlockSpec((bsz, d), lambda l, j: (0, 0)),
            pl.BlockSpec((1, d, tn), lambda l, j: (l, 0, j)),
            pl.BlockSpec((1, 1, tn), lambda l, j: (l, 0, j)),
        ],
        out_specs=pl.BlockSpec((1, bsz, tn), lambda l, j: (l, 0, j)),
        out_shape=jax.ShapeDtypeStruct((depth, bsz, n), F32),
        compiler_params=pltpu.CompilerParams(
            dimension_semantics=("arbitrary", "arbitrary"), vmem_limit_bytes=VMEM_LIMIT),
        name="adaln_modulation",
    )(c, ada_w, ada_b.reshape(depth, 1, n))


def _lower_bound_kernel(x_ref, o_ref):
    depth = x_ref.shape[0]
    rows = [x_ref[l:l + 1, :] for l in range(depth)]
    m = functools.reduce(jnp.maximum, rows)
    e = [jnp.exp(r - m) for r in rows]
    tot = functools.reduce(lambda a, b: a + b, e)
    p = [ei / tot for ei in e]
    acc = jnp.zeros_like(p[0])
    for l in range(depth):
        acc = acc + p[l]
        o_ref[l:l + 1, :] = acc - p[0]


def _lower_bounds(logits):
    return pl.pallas_call(
        _lower_bound_kernel,
        out_shape=jax.ShapeDtypeStruct(logits.shape, F32),
        name="hgrn_lower_bounds",
    )(logits.astype(F32))


def _inproj_kernel(steps_per_batch, x_ref, mod_ref, w_ref, wal_ref, wr_ref, lb_ref, au_ref, ab_ref, mu_ref,
                   w0_ref, a0_ref, wa_ref, kk_ref, ka_ref,
                   gq_ref, gk_ref, gv_ref, gb_ref, gg_ref,
                   rr_ref, rlw_ref, rk_ref, rv_ref, rkk_ref, rka_ref, rhg_ref, carry_s):
    tm = x_ref.shape[0]
    w3 = W_RWKV

    @pl.when(pl.program_id(0) % steps_per_batch == 0)
    def _():
        carry_s[...] = jnp.zeros_like(carry_s)

    m = mod_ref[0]
    h = (x_ref[...] * (1.0 + m[1:2, :]) + m[0:1, :]).astype(BF16)

    def proj(c0, width):
        return _mm(h, w_ref[:, c0:c0 + width])

    first_row = _iota((tm, LANES), 0) == 0

    def shifted(c0, width):
        z = _mm(h, wr_ref[:, c0:c0 + width])
        z_prev = pltpu.roll(z, 1, axis=0)
        z_prev = jnp.where(jnp.concatenate([first_row] * (width // LANES), axis=1), carry_s[:, c0:c0 + width], z_prev)
        carry_s[:, c0:c0 + width] = z[tm - 1:tm, :]
        return z + (z_prev - z) * mu_ref[:, c0:c0 + width]

    z_v = shifted(2 * w3, w3 + LANES)
    h_wa = z_v[:, w3:w3 + LANES]
    h_wa = jnp.where(_iota(h_wa.shape, 1) < RWKV_DECAY_RANK, jnp.tanh(h_wa), h_wa)
    wa = _dot1(h_wa, wa_ref[...])
    w_log = -_softplus(-(w0_ref[...] + wa[:, 0:w3])) - 0.5
    rlw_ref[...] = -jnp.exp(w_log)
    a = _sigmoid(a0_ref[...] + wa[:, w3:2 * w3])
    rv_ref[...] = z_v[:, 0:w3]
    z_rk = shifted(0, 2 * w3)
    k = z_rk[:, w3:2 * w3]
    kk = k * kk_ref[...]
    kk = kk / jnp.maximum(jnp.sqrt(_head_sums([kk * kk])[0]), 1e-12)
    rkk_ref[...] = kk
    rka_ref[...] = kk * a
    rk_ref[...] = k * (1.0 + (a - 1.0) * ka_ref[...])
    rr_ref[...] = z_rk[:, 0:w3]
    rhg_ref[...] = _sigmoid(shifted(3 * w3 + LANES, RWKV_GATE_RANK)).astype(BF16)

    g0 = N_HGRN_COLS
    logit = _dot1(_mm(h, wal_ref[...]), au_ref[...]) + ab_ref[...]
    gb_ref[:, W_HGRN:W_GL] = _log_sigmoid(logit) / GLA_GATE_NORMALIZER
    z_qk = proj(g0, 2 * W_GLA)
    gq_ref[:, W_HGRN:W_GL] = z_qk[:, 0:W_GLA] * HEAD_DIM ** -0.5
    gk_ref[:, W_HGRN:W_GL] = z_qk[:, W_GLA:2 * W_GLA]
    z_vr = proj(g0 + 2 * W_GLA, 2 * W_GLA)
    gv_ref[:, W_HGRN:W_GL] = z_vr[:, 0:W_GLA].astype(BF16)
    gg_ref[:, W_HGRN:W_GL] = _silu(z_vr[:, W_GLA:2 * W_GLA]).astype(BF16)

    lb = lb_ref[...]
    zf = proj(W_HGRN, W_HGRN)
    f = lb + (1.0 - lb) * _sigmoid(zf)
    gb_ref[:, 0:W_HGRN] = jnp.log(jnp.maximum(f, F_MIN))
    gk_ref[:, 0:W_HGRN] = (1.0 - lb) * _sigmoid(-zf)
    gq_ref[:, 0:W_HGRN] = _silu(proj(0, W_HGRN)) * HEAD_DIM ** -0.5
    gg_ref[:, 0:W_HGRN] = _silu(proj(3 * W_HGRN, W_HGRN)).astype(BF16)
    gv_ref[:, 0:W_HGRN] = proj(2 * W_HGRN, W_HGRN).astype(BF16)


def _in_projection(x2, mod_l, w_main, w_alpha, w_rwkv, lb_l, alpha_up_p, alpha_b, mu, w0, a0, wa_blk, k_k, k_a,
                   seq, tm, layer):
    m_rows, d = x2.shape
    resident = lambda n: pl.BlockSpec((None, d, n), lambda i: (layer, 0, 0), pipeline_mode=pl.Buffered(1))
    steps_per_batch = seq // tm
    row = lambda i: (i, 0)
    const = lambda i: (0, 0)
    vec = lambda n: pl.BlockSpec((1, n), const)
    out = lambda n, dt: (pl.BlockSpec((tm, n), row), jax.ShapeDtypeStruct((m_rows, n), dt))
    outs = [out(W_GL, F32), out(W_GL, F32), out(W_GL, BF16), out(W_GL, F32), out(W_GL, BF16),
            out(W_RWKV, F32), out(W_RWKV, F32), out(W_RWKV, F32), out(W_RWKV, F32), out(W_RWKV, F32),
            out(W_RWKV, F32), out(RWKV_GATE_RANK, BF16)]
    return pl.pallas_call(
        functools.partial(_inproj_kernel, steps_per_batch),
        grid=(m_rows // tm,),
        in_specs=[
            pl.BlockSpec((tm, d), row),
            pl.BlockSpec((1, 6, d), lambda i: (i // steps_per_batch, 0, 0)),
            resident(N_HGRN_COLS + 4 * W_GLA), resident(LANES), resident(N_RWKV_COLS),
            vec(W_HGRN),
            pl.BlockSpec((LANES, W_GLA), const),
            vec(W_GLA),
            vec(N_RWKV_COLS), vec(W_RWKV), vec(W_RWKV),
            pl.BlockSpec((LANES, 2 * W_RWKV), const),
            vec(W_RWKV), vec(W_RWKV),
        ],
        out_specs=[o[0] for o in outs],
        out_shape=[o[1] for o in outs],
        scratch_shapes=[pltpu.VMEM((1, N_RWKV_COLS), F32)],
        compiler_params=pltpu.CompilerParams(
            dimension_semantics=("arbitrary",), vmem_limit_bytes=VMEM_LIMIT),
        name="in_projection",
    )(x2, mod_l, w_main, w_alpha, w_rwkv, lb_l, alpha_up_p, alpha_b, mu, w0, a0, wa_blk, k_k, k_a)


def _gl_kernel(q_s, k_s, v_s, lg_ref, gate_ref, ng_ref, ind_ref, out_ref,
               st_ref, b_s, ad_s, e_s, o_s, sc_s, eend_s):
    tb = q_s.shape[0]
    n_chunks = tb // CHUNK

    @pl.when(pl.program_id(1) == 0)
    def _():
        st_ref[...] = jnp.zeros_like(st_ref)

    tril = _tril_bf16()
    for c in range(n_chunks):
        rows = slice(c * CHUNK, (c + 1) * CHUNK)
        b_s[rows, :] = _dot_exact_lhs(tril, lg_ref[rows, :], 2)

    for p in range(N_GL_PAIRS):
        lanes = slice(p * PAIR, (p + 1) * PAIR)
        q3 = q_s[:, lanes].reshape(tb // SUB, SUB, PAIR)
        k3 = k_s[:, lanes].reshape(tb // SUB, SUB, PAIR)
        b3 = (b_s[:, lanes] * LOG2E).reshape(tb // SUB, SUB, PAIR)
        for sg in range(SUB):
            dec = jnp.exp2(jnp.minimum(b3 - b3[:, sg:sg + 1, :], 0.0))
            e = q3 * k3[:, sg:sg + 1, :] * dec
            e_s[:, sg * PAIR:(sg + 1) * PAIR] = e.reshape(tb, PAIR).astype(BF16)
        ad_s[:, lanes] = _mm(e_s[...], ind_ref[...])

    masks = _pair_masks()
    t_idx = _iota((CHUNK, PAIR), 0)
    second_half = {hs: (t_idx // hs) % 2 == 1 for hs in LEVEL_HALVES}
    level_sign = {hs: jnp.where(second_half[hs], LOG2E, -LOG2E) for hs in LEVEL_HALVES}

    def group_units(i):
        units = []
        for cc in range(A_GROUP_GL):
            c = i * A_GROUP_GL + cc
            rows = pl.ds(pl.multiple_of(c * CHUNK, CHUNK), CHUNK)
            for p in range(N_GL_PAIRS):
                units.append(dict(rows=rows, lanes=slice(p * PAIR, (p + 1) * PAIR),
                                  e_rows=pl.ds(pl.multiple_of(c * 8, 8), 8),
                                  st_rows=pl.ds(pl.multiple_of((c * N_GL_PAIRS + p) * PAIR, PAIR), PAIR)))
        return units

    def increments(i, carry):
        units = group_units(i)
        for un in units:
            b = b_s[un["rows"], un["lanes"]]
            b_end = b[CHUNK - 1:CHUNK, :]
            un["k_end"] = k_s[un["rows"], un["lanes"]] * jnp.exp(b_end - b)
            eend_s[un["e_rows"], un["lanes"]] = jnp.broadcast_to(jnp.exp(b_end), (8, PAIR))
        for un in units:
            upd = _dot1(v_s[un["rows"], un["lanes"]], un["k_end"], TN)
            sc_s[un["st_rows"], :] = jnp.where(masks["bd"], upd, 0.0)
        return carry

    lax.fori_loop(0, n_chunks // A_GROUP_GL, increments, 0)

    sts = [st_ref[p] for p in range(N_GL_PAIRS)]
    for c in range(n_chunks):
        for p in range(N_GL_PAIRS):
            st_rows = slice((c * N_GL_PAIRS + p) * PAIR, (c * N_GL_PAIRS + p + 1) * PAIR)
            upd = sc_s[st_rows, :]
            sc_s[st_rows, :] = sts[p]
            sts[p] = sts[p] * eend_s[c * 8:c * 8 + 1, p * PAIR:(p + 1) * PAIR] + upd
    for p in range(N_GL_PAIRS):
        st_ref[p] = sts[p]

    def outputs(i, carry):
        units = group_units(i)
        for un in units:
            q = q_s[un["rows"], un["lanes"]]
            k = k_s[un["rows"], un["lanes"]]
            b = b_s[un["rows"], un["lanes"]]
            un["lv"] = []
            for hs in LEVEL_HALVES:
                e_mid = jnp.concatenate(
                    [jnp.broadcast_to(b[m:m + 1, :], (2 * hs, PAIR)) for m in range(hs - 1, CHUNK, 2 * hs)], axis=0)
                dec = jnp.exp2(jnp.minimum((b - e_mid) * level_sign[hs], 0.0))
                un["lv"].append(jnp.where(second_half[hs], q, k) * dec)
            un["q_in"] = q * jnp.exp(b)
        for un in units:
            a = jnp.where(masks["diag"], ad_s[un["rows"], un["lanes"]], 0.0)
            for hs, lv in zip(LEVEL_HALVES, un["lv"]):
                a = a + jnp.where(masks["level"][hs], _dot1(lv, _stack_heads(lv), NT), 0.0)
            un["a"] = a
        for un in units:
            o_s[un["rows"], un["lanes"]] = (_dot1(un["a"], _stack_heads(v_s[un["rows"], un["lanes"]]))
                                            + _dot1(un["q_in"], sc_s[un["st_rows"], :], NT))
        return carry

    lax.fori_loop(0, n_chunks // A_GROUP_GL, outputs, 0)

    o = o_s[...]
    ms = _head_sums([o * o])[0] * (1.0 / HEAD_DIM)
    out_ref[...] = (o * lax.rsqrt(ms + RMS_EPS) * ng_ref[...] * gate_ref[...]).astype(out_ref.dtype)


def _gl_indicator():
    r = np.arange(SUB * PAIR)
    c = np.arange(PAIR)
    sg = r // PAIR
    h = (r % PAIR) // HEAD_DIM
    ind = (h[:, None] == (c // HEAD_DIM)[None, :]) & (sg[:, None] == (c % SUB)[None, :])
    return jnp.asarray(ind, dtype=BF16)


def _gl_mixer(q, k, v, lg, gate, norm_g, bsz, seq, tb):
    steps = seq // tb
    row = lambda b, i: (b * steps + i, 0)
    const = lambda b, i: (0, 0)
    slab = pl.BlockSpec((tb, W_GL), row)
    scratch_f32 = pltpu.VMEM((tb, W_GL), F32)
    return pl.pallas_call(
        _gl_kernel,
        grid=(bsz, steps),
        in_specs=[
            slab, slab, slab, slab, slab,
            pl.BlockSpec((1, W_GL), const),
            pl.BlockSpec((SUB * PAIR, PAIR), const),
        ],
        out_specs=pl.BlockSpec((tb, W_GL), row),
        out_shape=jax.ShapeDtypeStruct((bsz * seq, W_GL), BF16),
        scratch_shapes=[
            pltpu.VMEM((N_GL_PAIRS, PAIR, PAIR), F32),
            scratch_f32, scratch_f32,
            pltpu.VMEM((tb, SUB * PAIR), BF16),
            scratch_f32,
            pltpu.VMEM((tb // CHUNK * N_GL_PAIRS * PAIR, PAIR), F32),
            pltpu.VMEM((tb // CHUNK * 8, W_GL), F32),
        ],
        compiler_params=pltpu.CompilerParams(
            dimension_semantics=("arbitrary", "arbitrary"), vmem_limit_bytes=VMEM_LIMIT),
        name="hgrn_gla_mixer",
    )(q, k, v, lg, gate, norm_g, _gl_indicator())


def _rwkv_kernel(r_s, lw_s, k_s, v_s, kk_s, ka_s, hg_s, gup_ref, rk_ref, gng_ref, gnb_ref,
                 out_ref, st_ref, y_s, rt_s, bh_s, kh_s, nb_s, nk_s, w_s, u0_s, eend_s):
    bsz, tb = r_s.shape[0], r_s.shape[1]
    n_chunks = tb // CHUNK

    @pl.when(pl.program_id(0) == 0)
    def _():
        st_ref[...] = jnp.zeros_like(st_ref)

    masks = _pair_masks()
    tril = _tril_bf16()
    eye = jnp.where(masks["eye"], 1.0, 0.0)

    def phase_a(b, carry):
        units = []
        for c in range(n_chunks):
            rows = slice(c * CHUNK, (c + 1) * CHUNK)
            g_all = _dot_exact_lhs(tril, lw_s[b, rows, :], 2)
            for p in range(N_RW_PAIRS):
                lanes = slice(p * PAIR, (p + 1) * PAIR)
                units.append(dict(c=c, rows=rows, lanes=lanes, g=g_all[:, lanes]))
        for un in units:
            rows, lanes, g = un["rows"], un["lanes"], un["g"]
            kc = k_s[b, rows, lanes]
            kkc = kk_s[b, rows, lanes]
            ka = ka_s[b, rows, lanes]
            g_end = g[CHUNK - 1:CHUNK, :]
            e_neg = jnp.exp(-g)
            e_end = jnp.exp(g_end - g)
            un["at"] = -kkc * jnp.exp(g - lw_s[b, rows, lanes])
            rt = r_s[b, rows, lanes] * jnp.exp(g)
            un["rt"] = rt
            un["bt"] = ka * e_neg
            un["kt"] = kc * e_neg
            rt_s[b, rows, lanes] = rt
            bh_s[b, rows, lanes] = ka * e_end
            kh_s[b, rows, lanes] = kc * e_end
            eend_s[b, un["c"] * 8:un["c"] * 8 + 8, lanes] = jnp.broadcast_to(jnp.exp(g_end), (8, PAIR))
        for un in units:
            gram = _dot1(jnp.concatenate([un["at"], un["rt"]], axis=0),
                         jnp.concatenate([_stack_heads(un["bt"]), _stack_heads(un["kt"])], axis=0), NT)
            un["l"] = jnp.where(masks["strict"], gram[0:CHUNK, 0:PAIR], 0.0)
            un["mk"] = jnp.where(masks["strict"], gram[0:CHUNK, PAIR:2 * PAIR], 0.0)
            nb_s[b, un["rows"], un["lanes"]] = jnp.where(masks["incl"], gram[CHUNK:2 * CHUNK, 0:PAIR], 0.0)
            nk_s[b, un["rows"], un["lanes"]] = jnp.where(masks["incl"], gram[CHUNK:2 * CHUNK, PAIR:2 * PAIR], 0.0)
        for un in units:
            un["t"] = eye + un["l"]
            un["lp"] = _dot1(un["l"], _stack_heads(un["l"]))
        for _ in range(4):
            for un in units:
                both = _dot1(jnp.concatenate([un["lp"], un["t"]], axis=0), _stack_heads(un["lp"]))
                un["lp"] = both[0:CHUNK, :]
                un["t"] = un["t"] + both[CHUNK:2 * CHUNK, :]
        for un in units:
            un["t"] = un["t"] + _dot1(un["t"], _stack_heads(un["lp"]))
        for un in units:
            un["mv"] = _dot1(un["mk"], _stack_heads(v_s[b, un["rows"], un["lanes"]]))
        for un in units:
            wu = _dot1(un["t"], jnp.concatenate([_stack_heads(un["at"]), _stack_heads(un["mv"])], axis=1))
            w_s[b, un["rows"], un["lanes"]] = wu[:, 0:PAIR]
            u0_s[b, un["rows"], un["lanes"]] = wu[:, PAIR:2 * PAIR]
        return carry

    lax.fori_loop(0, bsz, phase_a, 0)

    def phase_b(c, carry):
        rows = pl.ds(pl.multiple_of(c * CHUNK, CHUNK), CHUNK)
        e_rows = pl.ds(pl.multiple_of(c * 8, 8), 8)
        seqs = [(b, p, slice(p * PAIR, (p + 1) * PAIR)) for b in range(bsz) for p in range(N_RW_PAIRS)]
        sts = [st_ref[b * N_RW_PAIRS + p] for b, p, _ in seqs]
        vs = [v_s[b, rows, lanes] for b, _, lanes in seqs]
        wrs = [_dot1(jnp.concatenate([w_s[b, rows, lanes], rt_s[b, rows, lanes]], axis=0), st, NT)
               for (b, _, lanes), st in zip(seqs, sts)]
        us = [wr[0:CHUNK, :] + u0_s[b, rows, lanes] for (b, _, lanes), wr in zip(seqs, wrs)]
        upds = [_dot1(jnp.concatenate([u, vc], axis=0),
                      jnp.concatenate([bh_s[b, rows, lanes], kh_s[b, rows, lanes]], axis=0), TN)
                for (b, _, lanes), u, vc in zip(seqs, us, vs)]
        for (b, p, lanes), st, upd in zip(seqs, sts, upds):
            e_end = eend_s[b, e_rows, lanes][0:1, :]
            st_ref[b * N_RW_PAIRS + p] = st * e_end + jnp.where(masks["bd"], upd, 0.0)
        for (b, _, lanes), wr, u, vc in zip(seqs, wrs, us, vs):
            y_s[b, rows, lanes] = (
                wr[CHUNK:2 * CHUNK, :]
                + _dot1(jnp.concatenate([nb_s[b, rows, lanes], nk_s[b, rows, lanes]], axis=1),
                        jnp.concatenate([_stack_heads(u), _stack_heads(vc)], axis=0)))
        return carry

    lax.fori_loop(0, n_chunks, phase_b, 0)

    for b in range(bsz):
        y = y_s[b]
        s_hi, s_lo = _head_sums(_bf16_parts(y, 2))
        d = y - (s_hi + s_lo) * (1.0 / HEAD_DIM)
        s_var, s_bonus = _head_sums([d * d, r_s[b] * k_s[b] * rk_ref[...]])
        yn = d * lax.rsqrt(s_var * (1.0 / HEAD_DIM) + RWKV_GN_EPS) * gng_ref[...] + gnb_ref[...]
        gate = _dot1(hg_s[b], gup_ref[...])
        out_ref[b] = ((yn + s_bonus * v_s[b]) * gate).astype(out_ref.dtype)


def _rwkv_mixer(r, lw, k, v, kk, ka, hg, g_up, r_k, gn_g, gn_b, bsz, seq, tb):
    blk = lambda i: (0, i, 0)
    const = lambda i: (0, 0)
    vec = pl.BlockSpec((1, W_RWKV), const)
    tok = pl.BlockSpec((bsz, tb, W_RWKV), blk)
    slab = pltpu.VMEM((bsz, tb, W_RWKV), F32)
    seq_major = lambda a: a.reshape(bsz, seq, a.shape[-1])
    out = pl.pallas_call(
        _rwkv_kernel,
        grid=(seq // tb,),
        in_specs=[
            tok, tok, tok, tok, tok, tok,
            pl.BlockSpec((bsz, tb, RWKV_GATE_RANK), blk),
            pl.BlockSpec((RWKV_GATE_RANK, W_RWKV), const),
            vec, vec, vec,
        ],
        out_specs=pl.BlockSpec((bsz, tb, W_RWKV), blk),
        out_shape=jax.ShapeDtypeStruct((bsz, seq, W_RWKV), BF16),
        scratch_shapes=[
            pltpu.VMEM((bsz * N_RW_PAIRS, PAIR, PAIR), F32),
            slab,
            slab, slab, slab, slab, slab, slab, slab,
            pltpu.VMEM((bsz, tb // CHUNK * 8, W_RWKV), F32),
        ],
        compiler_params=pltpu.CompilerParams(
            dimension_semantics=("arbitrary",), vmem_limit_bytes=VMEM_LIMIT),
        name="rwkv7_mixer",
    )(*[seq_major(a) for a in (r, lw, k, v, kk, ka, hg)], g_up, r_k, gn_g, gn_b)
    return out.reshape(bsz * seq, W_RWKV)


def _post_kernel(alpha, ff_chunk, x_ref, ogl_ref, orw_ref, mod_ref, wout_ref, ln1g_ref, ln1b_ref,
                 wup_ref, wdn_ref, ln2g_ref, ln2b_ref, out_ref):
    m = mod_ref[0]
    gate1, shift2, scale2, gate2 = m[2:3, :], m[3:4, :], m[4:5, :], m[5:6, :]
    o = _mm(ogl_ref[...], wout_ref[0:W_GL, :]) + _mm(orw_ref[...], wout_ref[W_GL:W_GL + W_RWKV, :])
    x1 = _layer_norm(alpha * x_ref[...] + (1.0 + gate1) * o, ln1g_ref[...], ln1b_ref[...])
    h = (x1 * (1.0 + scale2) + shift2).astype(BF16)
    d_ff = wup_ref.shape[1]
    acc = jnp.zeros(x1.shape, F32)
    for j in range(d_ff // ff_chunk):
        cols = slice(j * ff_chunk, (j + 1) * ff_chunk)
        u = jnp.maximum(_mm(h, wup_ref[:, cols]), 0.0)
        acc = acc + _mm((u * u).astype(BF16), wdn_ref[cols, :])
    out_ref[...] = _layer_norm(alpha * x1 + (1.0 + gate2) * acc, ln2g_ref[...], ln2b_ref[...])


def _post_mixer(x2, ogl, orw, mod_l, w_out, ln1_g, ln1_b, w_up, w_dn, ln2_g, ln2_b, alpha, seq, tm, layer):
    m_rows, d = x2.shape
    d_ff = w_up.shape[2]
    steps_per_batch = seq // tm
    row = lambda i: (i, 0)
    const = lambda i: (0, 0)
    vec = pl.BlockSpec((1, d), const)
    resident = lambda shape: pl.BlockSpec((None,) + shape, lambda i: (layer, 0, 0), pipeline_mode=pl.Buffered(1))
    return pl.pallas_call(
        functools.partial(_post_kernel, alpha, 512),
        grid=(m_rows // tm,),
        in_specs=[
            pl.BlockSpec((tm, d), row),
            pl.BlockSpec((tm, W_GL), row),
            pl.BlockSpec((tm, W_RWKV), row),
            pl.BlockSpec((1, 6, d), lambda i: (i // steps_per_batch, 0, 0)),
            resident((W_GL + W_RWKV, d)),
            vec, vec,
            resident((d, d_ff)),
            resident((d_ff, d)),
            vec, vec,
        ],
        out_specs=pl.BlockSpec((tm, d), row),
        out_shape=jax.ShapeDtypeStruct((m_rows, d), F32),
        compiler_params=pltpu.CompilerParams(
            dimension_semantics=("arbitrary",), vmem_limit_bytes=VMEM_LIMIT),
        name="outproj_mlp",
    )(x2, ogl, orw, mod_l, w_out, ln1_g, ln1_b, w_up, w_dn, ln2_g, ln2_b)


def kernel(x, c, hgrn_lb_logits, ada_w, ada_b, w_in, hgrn_norm_g, gla_alpha_up, gla_alpha_b, gla_norm_g,
           rwkv_mu, rwkv_w0, rwkv_w_up, rwkv_a0, rwkv_a_up, rwkv_g_up, rwkv_k_k, rwkv_k_a, rwkv_r_k,
           rwkv_gn_g, rwkv_gn_b, w_out, ln1_g, ln1_b, mlp_w_up, mlp_w_down, ln2_g, ln2_b):
    bsz, seq, d = x.shape
    depth = w_in.shape[0]
    alpha = (2.0 * depth) ** 0.25
    tm = min(512, seq)
    tb = min(512, seq)
    assert seq % tm == 0 and seq % tb == 0 and tb % CHUNK == 0
    assert w_in.shape[2] == N_HGRN_COLS + N_GLA_COLS + N_RWKV_COLS

    mod = _modulation(c, ada_w, ada_b).reshape(depth, bsz, 6, d)
    lbs = _lower_bounds(hgrn_lb_logits)

    gla_end = N_HGRN_COLS + N_GLA_COLS
    w_main = w_in[:, :, :gla_end - GLA_GATE_RANK].astype(BF16)
    w_alpha = jnp.pad(w_in[:, :, gla_end - GLA_GATE_RANK:gla_end],
                      ((0, 0), (0, 0), (0, LANES - GLA_GATE_RANK))).astype(BF16)
    w_rwkv = w_in[:, :, gla_end:].astype(BF16)
    alpha_up_p = jnp.concatenate(
        [gla_alpha_up, jnp.zeros((depth, LANES - GLA_GATE_RANK, W_GLA), gla_alpha_up.dtype)], axis=1)
    zeros_r = jnp.zeros((depth, RWKV_DECAY_RANK, W_RWKV), rwkv_w_up.dtype)
    wa_blk = jnp.concatenate(
        [jnp.concatenate([rwkv_w_up, zeros_r], axis=2), jnp.concatenate([zeros_r, rwkv_a_up], axis=2)], axis=1)
    norm_g = jnp.concatenate([hgrn_norm_g, gla_norm_g], axis=1)
    w_out_b = w_out.astype(BF16)
    w_up_b = mlp_w_up.astype(BF16)
    w_dn_b = mlp_w_down.astype(BF16)
    vec = lambda a, l: a[l].reshape(1, -1)

    x2 = x.reshape(bsz * seq, d)
    for l in range(depth):
        gq, gk, gv, glg, ggate, rr, rlw, rk, rv, rkk, rka, rhg = _in_projection(
            x2, mod[l], w_main, w_alpha, w_rwkv, vec(lbs, l), alpha_up_p[l], vec(gla_alpha_b, l), vec(rwkv_mu, l),
            vec(rwkv_w0, l), vec(rwkv_a0, l), wa_blk[l], vec(rwkv_k_k, l), vec(rwkv_k_a, l), seq, tm, l)
        ogl = _gl_mixer(gq, gk, gv, glg, ggate, vec(norm_g, l), bsz, seq, tb)
        orw = _rwkv_mixer(rr, rlw, rk, rv, rkk, rka, rhg, rwkv_g_up[l], vec(rwkv_r_k, l), vec(rwkv_gn_g, l),
                          vec(rwkv_gn_b, l), bsz, seq, tb)
        x2 = _post_mixer(x2, ogl, orw, mod[l], w_out_b, vec(ln1_g, l), vec(ln1_b, l), w_up_b, w_dn_b,
                         vec(ln2_g, l), vec(ln2_b, l), alpha, seq, tm, l)
    return x2.reshape(bsz, seq, d)
```

```python
import functools

import numpy as np
import jax
import jax.numpy as jnp
from jax import lax
from jax.experimental import pallas as pl
from jax.experimental.pallas import tpu as pltpu

F32 = jnp.float32
BF16 = jnp.bfloat16

HEAD_DIM = 64
H_HGRN, H_GLA, H_RWKV = 4, 6, 6
W_HGRN, W_GLA, W_RWKV = H_HGRN * HEAD_DIM, H_GLA * HEAD_DIM, H_RWKV * HEAD_DIM
GLA_GATE_RANK = 16
GLA_GATE_NORMALIZER = 16.0
RWKV_DECAY_RANK, RWKV_ICLR_RANK, RWKV_GATE_RANK = 64, 64, 128
RWKV_GN_EPS = 64e-5
N_HGRN_COLS = 4 * W_HGRN
N_GLA_COLS = 4 * W_GLA + GLA_GATE_RANK
N_RWKV_COLS = 3 * W_RWKV + RWKV_DECAY_RANK + RWKV_ICLR_RANK + RWKV_GATE_RANK
CHUNK = 64
LN_EPS = 1e-5
RMS_EPS = 1e-5
F_MIN = 1e-30

LANES = 128
PAIR = 2 * HEAD_DIM
SUB = 8
LEVEL_HALVES = (SUB, 2 * SUB, 4 * SUB)
assert 2 * LEVEL_HALVES[-1] == CHUNK
A_GROUP_GL = 4
MAP_ROWS = 64
LOG2E = 1.4426950408889634
VMEM_LIMIT = 56 * 1024 * 1024

W_GL = W_HGRN + W_GLA
N_GL_PAIRS = W_GL // PAIR
N_RW_PAIRS = W_RWKV // PAIR

NN = (((1,), (0,)), ((), ()))
NT = (((1,), (1,)), ((), ()))
TN = (((0,), (0,)), ((), ()))


def _mm(a, b, dims=NN):
    return lax.dot_general(a, b, dims, preferred_element_type=F32)


def _split2(x):
    hi = x.astype(BF16)
    lo = (x - hi.astype(F32)).astype(BF16)
    return hi, lo


def _bf16_parts(x, n):
    parts = []
    for i in range(n):
        p = x.astype(BF16)
        parts.append(p)
        if i + 1 < n:
            x = x - p.astype(F32)
    return parts


def _dot1(a, b, dims=NN):
    return _mm(a.astype(BF16), b.astype(BF16), dims)


def _dot3(a, b, dims=NN):
    ah, al = _split2(a)
    bh, bl = _split2(b)
    return _mm(ah, bh, dims) + (_mm(ah, bl, dims) + _mm(al, bh, dims))


def _dot_exact_lhs(a_bf16, b, n):
    return _mm(jnp.concatenate([a_bf16] * n, axis=1), jnp.concatenate(_bf16_parts(b, n), axis=0))


def _sigmoid(x):
    return 1.0 / (1.0 + jnp.exp(-x))


def _silu(x):
    return x * _sigmoid(x)


def _softplus(x):
    return jnp.maximum(x, 0.0) + jnp.log1p(jnp.exp(-jnp.abs(x)))


def _log_sigmoid(x):
    return -_softplus(-x)


def _layer_norm(y, g, b):
    mu = jnp.mean(y, axis=-1, keepdims=True)
    d = y - mu
    var = jnp.mean(d * d, axis=-1, keepdims=True)
    return d * lax.rsqrt(var + LN_EPS) * g + b


def _iota(shape, axis):
    return lax.broadcasted_iota(jnp.int32, shape, axis)


def _stack_heads(x):
    lane = _iota(x.shape, 1)
    return jnp.concatenate([jnp.where(lane < HEAD_DIM, x, 0.0), jnp.where(lane >= HEAD_DIM, x, 0.0)], axis=0)


def _pair_masks():
    t = _iota((CHUNK, PAIR), 0)
    s = _iota((CHUNK, PAIR), 1) % HEAD_DIM
    r2 = _iota((PAIR, PAIR), 0) // HEAD_DIM
    c2 = _iota((PAIR, PAIR), 1) // HEAD_DIM
    return dict(
        strict=s < t,
        incl=s <= t,
        eye=s == t,
        diag=(s // SUB == t // SUB) & (s <= t),
        level={hs: (s // (2 * hs) == t // (2 * hs)) & ((t // hs) % 2 == 1) & ((s // hs) % 2 == 0)
               for hs in LEVEL_HALVES},
        bd=r2 == c2,
    )


def _tril_bf16():
    return jnp.where(_iota((CHUNK, CHUNK), 1) <= _iota((CHUNK, CHUNK), 0), 1.0, 0.0).astype(BF16)


def _seg_mean_matrix(width):
    r = _iota((width, width), 0) // HEAD_DIM
    c = _iota((width, width), 1) // HEAD_DIM
    return jnp.where(r == c, 1.0, 0.0).astype(BF16)


def _head_sums(xs):
    tiles = [x[:, j:j + LANES].astype(BF16) for x in xs for j in range(0, x.shape[1], LANES)]
    sums = []
    for i in range(0, len(tiles) - 1, 2):
        s = _mm(jnp.concatenate(tiles[i:i + 2], axis=1), _seg_mean_matrix(2 * LANES))
        sums += [s[:, 0:LANES], s[:, LANES:2 * LANES]]
    if len(tiles) % 2:
        sums.append(_mm(tiles[-1], _seg_mean_matrix(LANES)))
    out, i = [], 0
    for x in xs:
        n = x.shape[1] // LANES
        out.append(jnp.concatenate(sums[i:i + n], axis=1))
        i += n
    return out


def _mod_kernel(c_ref, w_ref, b_ref, o_ref):
    c = _silu(c_ref[...])
    o_ref[0] = _dot3(c, w_ref[0]) + b_ref[0]


def _modulation(c, ada_w, ada_b):
    depth, d, n = ada_w.shape
    bsz = c.shape[0]
    tn = 1536
    return pl.pallas_call(
        _mod_kernel,
        grid=(depth, n // tn),
        in_specs=[
            pl.BlockSpec((bsz, d), lambda l, j: (0, 0)),
            pl.BlockSpec((1, d, tn), lambda l, j: (l, 0, j)),
            pl.BlockSpec((1, 1, tn), lambda l, j: (l, 0, j)),
        ],
        out_specs=pl.BlockSpec((1, bsz, tn), lambda l, j: (l, 0, j)),
        out_shape=jax.ShapeDtypeStruct((depth, bsz, n), F32),
        compiler_params=pltpu.CompilerParams(
            dimension_semantics=("arbitrary", "arbitrary"), vmem_limit_bytes=VMEM_LIMIT),
        name="adaln_modulation",
    )(c, ada_w, ada_b.reshape(depth, 1, n))


def _lower_bound_kernel(x_ref, o_ref):
    depth = x_ref.shape[0]
    rows = [x_ref[l:l + 1, :] for l in range(depth)]
    m = functools.reduce(jnp.maximum, rows)
    e = [jnp.exp(r - m) for r in rows]
    tot = functools.reduce(lambda a, b: a + b, e)
    p = [ei / tot for ei in e]
    acc = jnp.zeros_like(p[0])
    for l in range(depth):
        acc = acc + p[l]
        o_ref[l:l + 1, :] = acc - p[0]


def _lower_bounds(logits):
    return pl.pallas_call(
        _lower_bound_kernel,
        out_shape=jax.ShapeDtypeStruct(logits.shape, F32),
        name="hgrn_lower_bounds",
    )(logits.astype(F32))


def _inproj_kernel(steps_per_batch, x_ref, mod_ref, w_ref, wal_ref, wr_ref, lb_ref, au_ref, ab_ref, mu_ref,
                   w0_ref, a0_ref, wa_ref, kk_ref, ka_ref,
                   gq_ref, gk_ref, gv_ref, gb_ref, gg_ref,
                   rr_ref, rlw_ref, rk_ref, rv_ref, rkk_ref, rka_ref, rhg_ref,
                   carry_s, zm_a, zal_a, zr_a, zm_b, zal_b, zr_b):
    step = pl.program_id(0)

    @pl.when(step == 0)
    def _():
        for z in (zm_b, zal_b, zr_b, carry_s):
            z[...] = jnp.zeros_like(z)

    @pl.when((step - 1) % steps_per_batch == 0)
    def _():
        carry_s[...] = jnp.zeros_like(carry_s)

    def step_body(write, read):
        m = mod_ref[0]
        h = (x_ref[...] * (1.0 + m[1:2, :]) + m[0:1, :]).astype(BF16)
        zm_w, zal_w, zr_w = write
        tm, n_main = zm_w.shape
        pieces = [(zm_w, w_ref, c0, 2 * LANES) for c0 in range(0, n_main, 2 * LANES)]
        pieces += [(zr_w, wr_ref, c0, min(2 * LANES, N_RWKV_COLS - c0)) for c0 in range(0, N_RWKV_COLS, 2 * LANES)]
        pieces += [(zal_w, wal_ref, 0, LANES)]
        n_blocks = tm // MAP_ROWS
        for blk in range(n_blocks):
            for z_w, wgt, c0, width in pieces[blk * len(pieces) // n_blocks:(blk + 1) * len(pieces) // n_blocks]:
                z_w[:, c0:c0 + width] = _mm(h, wgt[:, c0:c0 + width])
            _token_maps(slice(blk * MAP_ROWS, (blk + 1) * MAP_ROWS), *read,
                        lb_ref, au_ref, ab_ref, mu_ref, w0_ref, a0_ref, wa_ref, kk_ref, ka_ref,
                        gq_ref, gk_ref, gv_ref, gb_ref, gg_ref,
                        rr_ref, rlw_ref, rk_ref, rv_ref, rkk_ref, rka_ref, rhg_ref, carry_s)

    @pl.when(step % 2 == 0)
    def _():
        step_body((zm_a, zal_a, zr_a), (zm_b, zal_b, zr_b))

    @pl.when(step % 2 == 1)
    def _():
        step_body((zm_b, zal_b, zr_b), (zm_a, zal_a, zr_a))


def _token_maps(rows, zm_ref, zal_ref, zr_ref, lb_ref, au_ref, ab_ref, mu_ref, w0_ref, a0_ref, wa_ref, kk_ref, ka_ref,
                gq_ref, gk_ref, gv_ref, gb_ref, gg_ref,
                rr_ref, rlw_ref, rk_ref, rv_ref, rkk_ref, rka_ref, rhg_ref, carry_s):
    n_rows = rows.stop - rows.start
    w3 = W_RWKV

    def proj(c0, width):
        return zm_ref[rows, c0:c0 + width]

    first_row = _iota((n_rows, LANES), 0) == 0

    def shifted(c0, width):
        z = zr_ref[rows, c0:c0 + width]
        z_prev = pltpu.roll(z, 1, axis=0)
        z_prev = jnp.where(jnp.concatenate([first_row] * (width // LANES), axis=1), carry_s[:, c0:c0 + width], z_prev)
        carry_s[:, c0:c0 + width] = z[n_rows - 1:n_rows, :]
        return z + (z_prev - z) * mu_ref[:, c0:c0 + width]

    z_v = shifted(2 * w3, w3 + LANES)
    z_rk = shifted(0, 2 * w3)
    z_hg = shifted(3 * w3 + LANES, RWKV_GATE_RANK)
    g0 = N_HGRN_COLS

    h_wa = z_v[:, w3:w3 + LANES]
    h_wa = jnp.where(_iota(h_wa.shape, 1) < RWKV_DECAY_RANK, jnp.tanh(h_wa), h_wa)
    wa = _dot1(h_wa, wa_ref[...])
    w_log = -_softplus(-(w0_ref[...] + wa[:, 0:w3])) - 0.5
    rlw_ref[rows, :] = -jnp.exp(w_log)
    a = _sigmoid(a0_ref[...] + wa[:, w3:2 * w3])
    rv_ref[rows, :] = z_v[:, 0:w3].astype(BF16)
    k = z_rk[:, w3:2 * w3]
    kk = k * kk_ref[...]
    kk = kk / jnp.maximum(jnp.sqrt(_head_sums([kk * kk])[0]), 1e-12)
    rkk_ref[rows, :] = kk.astype(BF16)
    rka_ref[rows, :] = (kk * a).astype(BF16)
    rk_ref[rows, :] = (k * (1.0 + (a - 1.0) * ka_ref[...])).astype(BF16)
    rr_ref[rows, :] = z_rk[:, 0:w3].astype(BF16)
    rhg_ref[rows, :] = _sigmoid(z_hg).astype(BF16)

    logit = _dot1(zal_ref[rows, :], au_ref[...]) + ab_ref[...]
    gb_ref[rows, W_HGRN:W_GL] = _log_sigmoid(logit) / GLA_GATE_NORMALIZER
    gq_ref[rows, W_HGRN:W_GL] = (proj(g0, W_GLA) * HEAD_DIM ** -0.5).astype(BF16)
    gk_ref[rows, W_HGRN:W_GL] = proj(g0 + W_GLA, W_GLA).astype(BF16)
    gv_ref[rows, W_HGRN:W_GL] = proj(g0 + 2 * W_GLA, W_GLA).astype(BF16)
    gg_ref[rows, W_HGRN:W_GL] = _silu(proj(g0 + 3 * W_GLA, W_GLA)).astype(BF16)

    lb = lb_ref[...]
    zf = proj(W_HGRN, W_HGRN)
    f = lb + (1.0 - lb) * _sigmoid(zf)
    gb_ref[rows, 0:W_HGRN] = jnp.log(jnp.maximum(f, F_MIN))
    gk_ref[rows, 0:W_HGRN] = ((1.0 - lb) * _sigmoid(-zf)).astype(BF16)
    gq_ref[rows, 0:W_HGRN] = (_silu(proj(0, W_HGRN)) * HEAD_DIM ** -0.5).astype(BF16)
    gg_ref[rows, 0:W_HGRN] = _silu(proj(3 * W_HGRN, W_HGRN)).astype(BF16)
    gv_ref[rows, 0:W_HGRN] = proj(2 * W_HGRN, W_HGRN).astype(BF16)


def _in_projection(x2, mod_l, w_main, w_alpha, w_rwkv, lb_l, alpha_up_p, alpha_b, mu, w0, a0, wa_blk, k_k, k_a,
                   seq, tm, layer):
    m_rows, d = x2.shape
    resident = lambda n: pl.BlockSpec((None, d, n), lambda i: (layer, 0, 0), pipeline_mode=pl.Buffered(1))
    steps_per_batch = seq // tm
    n_tiles = m_rows // tm
    in_tile = lambda i: jnp.minimum(i, n_tiles - 1)
    out_row = lambda i: (jnp.maximum(i - 1, 0), 0)
    const = lambda i: (0, 0)
    vec = lambda n: pl.BlockSpec((1, n), const)
    out = lambda n, dt: (pl.BlockSpec((tm, n), out_row), jax.ShapeDtypeStruct((m_rows, n), dt))
    outs = [out(W_GL, BF16), out(W_GL, BF16), out(W_GL, BF16), out(W_GL, F32), out(W_GL, BF16),
            out(W_RWKV, BF16), out(W_RWKV, F32), out(W_RWKV, BF16), out(W_RWKV, BF16), out(W_RWKV, BF16),
            out(W_RWKV, BF16), out(RWKV_GATE_RANK, BF16)]
    n_main = N_HGRN_COLS + 4 * W_GLA
    z_bufs = [pltpu.VMEM((tm, n_main), F32), pltpu.VMEM((tm, LANES), F32), pltpu.VMEM((tm, N_RWKV_COLS), F32)]
    return pl.pallas_call(
        functools.partial(_inproj_kernel, steps_per_batch),
        grid=(n_tiles + 1,),
        in_specs=[
            pl.BlockSpec((tm, d), lambda i: (in_tile(i), 0)),
            pl.BlockSpec((1, 6, d), lambda i: (in_tile(i) // steps_per_batch, 0, 0)),
            resident(n_main), resident(LANES), resident(N_RWKV_COLS),
            vec(W_HGRN),
            pl.BlockSpec((LANES, W_GLA), const),
            vec(W_GLA),
            vec(N_RWKV_COLS), vec(W_RWKV), vec(W_RWKV),
            pl.BlockSpec((LANES, 2 * W_RWKV), const),
            vec(W_RWKV), vec(W_RWKV),
        ],
        out_specs=[o[0] for o in outs],
        out_shape=[o[1] for o in outs],
        scratch_shapes=[pltpu.VMEM((1, N_RWKV_COLS), F32)] + z_bufs + z_bufs,
        compiler_params=pltpu.CompilerParams(
            dimension_semantics=("arbitrary",), vmem_limit_bytes=VMEM_LIMIT),
        name="in_projection",
    )(x2, mod_l, w_main, w_alpha, w_rwkv, lb_l, alpha_up_p, alpha_b, mu, w0, a0, wa_blk, k_k, k_a)


def _gl_kernel(q_s, k_s, v_s, lg_ref, gate_ref, ng_ref, ind_ref, out_ref,
               st_ref, b_s, ad_s, e_s, o_s, sc_s, eend_s):
    tb = q_s.shape[0]
    n_chunks = tb // CHUNK

    @pl.when(pl.program_id(1) == 0)
    def _():
        st_ref[...] = jnp.zeros_like(st_ref)

    tril = _tril_bf16()
    for c in range(n_chunks):
        rows = slice(c * CHUNK, (c + 1) * CHUNK)
        b_s[rows, :] = _dot_exact_lhs(tril, lg_ref[rows, :], 2)

    for p in range(N_GL_PAIRS):
        lanes = slice(p * PAIR, (p + 1) * PAIR)
        q3 = q_s[:, lanes].astype(F32).reshape(tb // SUB, SUB, PAIR)
        k3 = k_s[:, lanes].astype(F32).reshape(tb // SUB, SUB, PAIR)
        b3 = (b_s[:, lanes] * LOG2E).reshape(tb // SUB, SUB, PAIR)
        for sg in range(SUB):
            dec = jnp.exp2(jnp.minimum(b3 - b3[:, sg:sg + 1, :], 0.0))
            e = q3 * k3[:, sg:sg + 1, :] * dec
            e_s[:, sg * PAIR:(sg + 1) * PAIR] = e.reshape(tb, PAIR).astype(BF16)
        ad_s[:, lanes] = _mm(e_s[...], ind_ref[...])

    masks = _pair_masks()
    t_idx = _iota((CHUNK, PAIR), 0)
    second_half = {hs: (t_idx // hs) % 2 == 1 for hs in LEVEL_HALVES}
    level_sign = {hs: jnp.where(second_half[hs], LOG2E, -LOG2E) for hs in LEVEL_HALVES}

    def group_units(i):
        units = []
        for cc in range(A_GROUP_GL):
            c = i * A_GROUP_GL + cc
            rows = pl.ds(pl.multiple_of(c * CHUNK, CHUNK), CHUNK)
            for p in range(N_GL_PAIRS):
                units.append(dict(rows=rows, lanes=slice(p * PAIR, (p + 1) * PAIR),
                                  e_rows=pl.ds(pl.multiple_of(c * 8, 8), 8),
                                  st_rows=pl.ds(pl.multiple_of((c * N_GL_PAIRS + p) * PAIR, PAIR), PAIR)))
        return units

    def increments(i, carry):
        units = group_units(i)
        for un in units:
            b = b_s[un["rows"], un["lanes"]]
            b_end = b[CHUNK - 1:CHUNK, :]
            un["k_end"] = k_s[un["rows"], un["lanes"]].astype(F32) * jnp.exp(b_end - b)
            eend_s[un["e_rows"], un["lanes"]] = jnp.broadcast_to(jnp.exp(b_end), (8, PAIR))
        for un in units:
            upd = _dot1(v_s[un["rows"], un["lanes"]], un["k_end"], TN)
            sc_s[un["st_rows"], :] = jnp.where(masks["bd"], upd, 0.0)
        return carry

    lax.fori_loop(0, n_chunks // A_GROUP_GL, increments, 0)

    sts = [st_ref[p] for p in range(N_GL_PAIRS)]
    for c in range(n_chunks):
        for p in range(N_GL_PAIRS):
            st_rows = slice((c * N_GL_PAIRS + p) * PAIR, (c * N_GL_PAIRS + p + 1) * PAIR)
            upd = sc_s[st_rows, :]
            sc_s[st_rows, :] = sts[p]
            sts[p] = sts[p] * eend_s[c * 8:c * 8 + 1, p * PAIR:(p + 1) * PAIR] + upd
    for p in range(N_GL_PAIRS):
        st_ref[p] = sts[p]

    def outputs(i, carry):
        units = group_units(i)
        for un in units:
            q = q_s[un["rows"], un["lanes"]].astype(F32)
            k = k_s[un["rows"], un["lanes"]].astype(F32)
            b = b_s[un["rows"], un["lanes"]]
            un["lv"] = []
            for hs in LEVEL_HALVES:
                e_mid = jnp.concatenate(
                    [jnp.broadcast_to(b[m:m + 1, :], (2 * hs, PAIR)) for m in range(hs - 1, CHUNK, 2 * hs)], axis=0)
                dec = jnp.exp2(jnp.minimum((b - e_mid) * level_sign[hs], 0.0))
                un["lv"].append(jnp.where(second_half[hs], q, k) * dec)
            un["q_in"] = q * jnp.exp(b)
        for un in units:
            a = jnp.zeros((CHUNK, PAIR), F32)
            for hs, lv in zip(LEVEL_HALVES, un["lv"]):
                a = jnp.where(masks["level"][hs], _dot1(lv, jnp.transpose(_stack_heads(lv))), a)
            un["a"] = jnp.where(masks["diag"], ad_s[un["rows"], un["lanes"]], a)
        for un in units:
            o_s[un["rows"], un["lanes"]] = (_dot1(un["a"], _stack_heads(v_s[un["rows"], un["lanes"]]))
                                            + _dot1(un["q_in"], sc_s[un["st_rows"], :], NT))
        return carry

    lax.fori_loop(0, n_chunks // A_GROUP_GL, outputs, 0)

    o = o_s[...]
    ms = _head_sums([o * o])[0] * (1.0 / HEAD_DIM)
    out_ref[...] = (o * lax.rsqrt(ms + RMS_EPS) * ng_ref[...] * gate_ref[...]).astype(out_ref.dtype)


def _gl_indicator():
    r = np.arange(SUB * PAIR)
    c = np.arange(PAIR)
    sg = r // PAIR
    h = (r % PAIR) // HEAD_DIM
    ind = (h[:, None] == (c // HEAD_DIM)[None, :]) & (sg[:, None] == (c % SUB)[None, :])
    return jnp.asarray(ind, dtype=BF16)


def _gl_mixer(q, k, v, lg, gate, norm_g, bsz, seq, tb):
    steps = seq // tb
    row = lambda b, i: (b * steps + i, 0)
    const = lambda b, i: (0, 0)
    slab = pl.BlockSpec((tb, W_GL), row)
    scratch_f32 = pltpu.VMEM((tb, W_GL), F32)
    return pl.pallas_call(
        _gl_kernel,
        grid=(bsz, steps),
        in_specs=[
            slab, slab, slab, slab, slab,
            pl.BlockSpec((1, W_GL), const),
            pl.BlockSpec((SUB * PAIR, PAIR), const),
        ],
        out_specs=pl.BlockSpec((tb, W_GL), row),
        out_shape=jax.ShapeDtypeStruct((bsz * seq, W_GL), BF16),
        scratch_shapes=[
            pltpu.VMEM((N_GL_PAIRS, PAIR, PAIR), F32),
            scratch_f32, scratch_f32,
            pltpu.VMEM((tb, SUB * PAIR), BF16),
            scratch_f32,
            pltpu.VMEM((tb // CHUNK * N_GL_PAIRS * PAIR, PAIR), F32),
            pltpu.VMEM((tb // CHUNK * 8, W_GL), F32),
        ],
        compiler_params=pltpu.CompilerParams(
            dimension_semantics=("arbitrary", "arbitrary"), vmem_limit_bytes=VMEM_LIMIT),
        name="hgrn_gla_mixer",
    )(q, k, v, lg, gate, norm_g, _gl_indicator())


def _rwkv_kernel(r_s, lw_s, k_s, v_s, kk_s, ka_s, hg_s, gup_ref, rk_ref, gng_ref, gnb_ref,
                 out_ref, st_ref, y_s, rt_s, bh_s, kh_s, nb_s, nk_s, w_s, u0_s, eend_s):
    bsz, tb = r_s.shape[0], r_s.shape[1]
    n_chunks = tb // CHUNK

    @pl.when(pl.program_id(0) == 0)
    def _():
        st_ref[...] = jnp.zeros_like(st_ref)

    masks = _pair_masks()
    tril = _tril_bf16()
    eye = jnp.where(masks["eye"], 1.0, 0.0)

    def phase_a(b, carry):
        units = []
        for c in range(n_chunks):
            rows = slice(c * CHUNK, (c + 1) * CHUNK)
            g_all = _dot_exact_lhs(tril, lw_s[b, rows, :], 2)
            for p in range(N_RW_PAIRS):
                lanes = slice(p * PAIR, (p + 1) * PAIR)
                units.append(dict(c=c, rows=rows, lanes=lanes, g=g_all[:, lanes]))
        for un in units:
            rows, lanes, g = un["rows"], un["lanes"], un["g"]
            kc = k_s[b, rows, lanes].astype(F32)
            kkc = kk_s[b, rows, lanes].astype(F32)
            ka = ka_s[b, rows, lanes].astype(F32)
            g_end = g[CHUNK - 1:CHUNK, :]
            e_neg = jnp.exp(-g)
            e_end = jnp.exp(g_end - g)
            un["at"] = -kkc * jnp.exp(g - lw_s[b, rows, lanes])
            rt = r_s[b, rows, lanes].astype(F32) * jnp.exp(g)
            un["rt"] = rt
            un["bt"] = ka * e_neg
            un["kt"] = kc * e_neg
            rt_s[b, rows, lanes] = rt
            bh_s[b, rows, lanes] = ka * e_end
            kh_s[b, rows, lanes] = kc * e_end
            eend_s[b, un["c"] * 8:un["c"] * 8 + 8, lanes] = jnp.broadcast_to(jnp.exp(g_end), (8, PAIR))
        for un in units:
            gram = _dot1(jnp.concatenate([un["at"], un["rt"]], axis=0),
                         jnp.concatenate([_stack_heads(un["bt"]), _stack_heads(un["kt"])], axis=0), NT)
            un["l"] = jnp.where(masks["strict"], gram[0:CHUNK, 0:PAIR], 0.0)
            un["mk"] = jnp.where(masks["strict"], gram[0:CHUNK, PAIR:2 * PAIR], 0.0)
            nb_s[b, un["rows"], un["lanes"]] = jnp.where(masks["incl"], gram[CHUNK:2 * CHUNK, 0:PAIR], 0.0)
            nk_s[b, un["rows"], un["lanes"]] = jnp.where(masks["incl"], gram[CHUNK:2 * CHUNK, PAIR:2 * PAIR], 0.0)
        for un in units:
            un["t"] = eye + un["l"]
            un["lp"] = _dot1(un["l"], _stack_heads(un["l"]))
        for _ in range(4):
            for un in units:
                both = _dot1(jnp.concatenate([un["lp"], un["t"]], axis=0), _stack_heads(un["lp"]))
                un["lp"] = both[0:CHUNK, :]
                un["t"] = un["t"] + both[CHUNK:2 * CHUNK, :]
        for un in units:
            un["t"] = un["t"] + _dot1(un["t"], _stack_heads(un["lp"]))
        for un in units:
            un["mv"] = _dot1(un["mk"], _stack_heads(v_s[b, un["rows"], un["lanes"]]))
        for un in units:
            wu = _dot1(un["t"], jnp.concatenate([_stack_heads(un["at"]), _stack_heads(un["mv"])], axis=1))
            w_s[b, un["rows"], un["lanes"]] = wu[:, 0:PAIR]
            u0_s[b, un["rows"], un["lanes"]] = wu[:, PAIR:2 * PAIR]
        return carry

    lax.fori_loop(0, bsz, phase_a, 0)

    def phase_b(c, carry):
        rows = pl.ds(pl.multiple_of(c * CHUNK, CHUNK), CHUNK)
        e_rows = pl.ds(pl.multiple_of(c * 8, 8), 8)
        seqs = [(b, p, slice(p * PAIR, (p + 1) * PAIR)) for b in range(bsz) for p in range(N_RW_PAIRS)]
        sts = [st_ref[b * N_RW_PAIRS + p] for b, p, _ in seqs]
        vs = [v_s[b, rows, lanes].astype(F32) for b, _, lanes in seqs]
        wrs = [_dot1(jnp.concatenate([w_s[b, rows, lanes], rt_s[b, rows, lanes]], axis=0), st, NT)
               for (b, _, lanes), st in zip(seqs, sts)]
        us = [wr[0:CHUNK, :] + u0_s[b, rows, lanes] for (b, _, lanes), wr in zip(seqs, wrs)]
        upds = [_dot1(jnp.concatenate([u, vc], axis=0),
                      jnp.concatenate([bh_s[b, rows, lanes], kh_s[b, rows, lanes]], axis=0), TN)
                for (b, _, lanes), u, vc in zip(seqs, us, vs)]
        for (b, p, lanes), st, upd in zip(seqs, sts, upds):
            e_end = eend_s[b, e_rows, lanes][0:1, :]
            st_ref[b * N_RW_PAIRS + p] = st * e_end + jnp.where(masks["bd"], upd, 0.0)
        for (b, _, lanes), wr, u, vc in zip(seqs, wrs, us, vs):
            y_s[b, rows, lanes] = (
                wr[CHUNK:2 * CHUNK, :]
                + _dot1(jnp.concatenate([nb_s[b, rows, lanes], nk_s[b, rows, lanes]], axis=1),
                        jnp.concatenate([_stack_heads(u), _stack_heads(vc)], axis=0)))
        return carry

    lax.fori_loop(0, n_chunks, phase_b, 0)

    for b in range(bsz):
        y = y_s[b]
        s_hi, s_lo = _head_sums(_bf16_parts(y, 2))
        d = y - (s_hi + s_lo) * (1.0 / HEAD_DIM)
        s_var, s_bonus = _head_sums([d * d, r_s[b].astype(F32) * k_s[b].astype(F32) * rk_ref[...]])
        yn = d * lax.rsqrt(s_var * (1.0 / HEAD_DIM) + RWKV_GN_EPS) * gng_ref[...] + gnb_ref[...]
        gate = _dot1(hg_s[b], gup_ref[...])
        out_ref[b] = ((yn + s_bonus * v_s[b]) * gate).astype(out_ref.dtype)


def _rwkv_mixer(r, lw, k, v, kk, ka, hg, g_up, r_k, gn_g, gn_b, bsz, seq, tb):
    blk = lambda i: (0, i, 0)
    const = lambda i: (0, 0)
    vec = pl.BlockSpec((1, W_RWKV), const)
    tok = pl.BlockSpec((bsz, tb, W_RWKV), blk)
    slab = pltpu.VMEM((bsz, tb, W_RWKV), F32)
    seq_major = lambda a: a.reshape(bsz, seq, a.shape[-1])
    out = pl.pallas_call(
        _rwkv_kernel,
        grid=(seq // tb,),
        in_specs=[
            tok, tok, tok, tok, tok, tok,
            pl.BlockSpec((bsz, tb, RWKV_GATE_RANK), blk),
            pl.BlockSpec((RWKV_GATE_RANK, W_RWKV), const),
            vec, vec, vec,
        ],
        out_specs=pl.BlockSpec((bsz, tb, W_RWKV), blk),
        out_shape=jax.ShapeDtypeStruct((bsz, seq, W_RWKV), BF16),
        scratch_shapes=[
            pltpu.VMEM((bsz * N_RW_PAIRS, PAIR, PAIR), F32),
            slab,
            slab, slab, slab, slab, slab, slab, slab,
            pltpu.VMEM((bsz, tb // CHUNK * 8, W_RWKV), F32),
        ],
        compiler_params=pltpu.CompilerParams(
            dimension_semantics=("arbitrary",), vmem_limit_bytes=VMEM_LIMIT),
        name="rwkv7_mixer",
    )(*[seq_major(a) for a in (r, lw, k, v, kk, ka, hg)], g_up, r_k, gn_g, gn_b)
    return out.reshape(bsz * seq, W_RWKV)


def _post_kernel(alpha, ff_chunk, x_ref, ogl_ref, orw_ref, mod_ref, wout_ref, ln1g_ref, ln1b_ref,
                 wup_ref, wdn_ref, ln2g_ref, ln2b_ref, out_ref):
    m = mod_ref[0]
    gate1, shift2, scale2, gate2 = m[2:3, :], m[3:4, :], m[4:5, :], m[5:6, :]
    o = _mm(ogl_ref[...], wout_ref[0:W_GL, :]) + _mm(orw_ref[...], wout_ref[W_GL:W_GL + W_RWKV, :])
    x1 = _layer_norm(alpha * x_ref[...] + (1.0 + gate1) * o, ln1g_ref[...], ln1b_ref[...])
    h = (x1 * (1.0 + scale2) + shift2).astype(BF16)
    d_ff = wup_ref.shape[1]
    acc = jnp.zeros(x1.shape, F32)
    for j in range(d_ff // ff_chunk):
        cols = slice(j * ff_chunk, (j + 1) * ff_chunk)
        u = jnp.maximum(_mm(h, wup_ref[:, cols]), 0.0)
        acc = acc + _mm((u * u).astype(BF16), wdn_ref[cols, :])
    out_ref[...] = _layer_norm(alpha * x1 + (1.0 + gate2) * acc, ln2g_ref[...], ln2b_ref[...])


def _post_mixer(x2, ogl, orw, mod_l, w_out, ln1_g, ln1_b, w_up, w_dn, ln2_g, ln2_b, alpha, seq, tm, layer):
    m_rows, d = x2.shape
    d_ff = w_up.shape[2]
    steps_per_batch = seq // tm
    row = lambda i: (i, 0)
    const = lambda i: (0, 0)
    vec = pl.BlockSpec((1, d), const)
    resident = lambda shape: pl.BlockSpec((None,) + shape, lambda i: (layer, 0, 0), pipeline_mode=pl.Buffered(1))
    return pl.pallas_call(
        functools.partial(_post_kernel, alpha, 512),
        grid=(m_rows // tm,),
        in_specs=[
            pl.BlockSpec((tm, d), row),
            pl.BlockSpec((tm, W_GL), row),
            pl.BlockSpec((tm, W_RWKV), row),
            pl.BlockSpec((1, 6, d), lambda i: (i // steps_per_batch, 0, 0)),
            resident((W_GL + W_RWKV, d)),
            vec, vec,
            resident((d, d_ff)),
            resident((d_ff, d)),
            vec, vec,
        ],
        out_specs=pl.BlockSpec((tm, d), row),
        out_shape=jax.ShapeDtypeStruct((m_rows, d), F32),
        compiler_params=pltpu.CompilerParams(
            dimension_semantics=("arbitrary",), vmem_limit_bytes=VMEM_LIMIT),
        name="outproj_mlp",
    )(x2, ogl, orw, mod_l, w_out, ln1_g, ln1_b, w_up, w_dn, ln2_g, ln2_b)


def kernel(x, c, hgrn_lb_logits, ada_w, ada_b, w_in, hgrn_norm_g, gla_alpha_up, gla_alpha_b, gla_norm_g,
           rwkv_mu, rwkv_w0, rwkv_w_up, rwkv_a0, rwkv_a_up, rwkv_g_up, rwkv_k_k, rwkv_k_a, rwkv_r_k,
           rwkv_gn_g, rwkv_gn_b, w_out, ln1_g, ln1_b, mlp_w_up, mlp_w_down, ln2_g, ln2_b):
    bsz, seq, d = x.shape
    depth = w_in.shape[0]
    alpha = (2.0 * depth) ** 0.25
    tm = min(512, seq)
    tb = min(512, seq)
    assert seq % tm == 0 and seq % tb == 0 and tb % CHUNK == 0
    assert w_in.shape[2] == N_HGRN_COLS + N_GLA_COLS + N_RWKV_COLS

    mod = _modulation(c, ada_w, ada_b).reshape(depth, bsz, 6, d)
    lbs = _lower_bounds(hgrn_lb_logits)

    gla_end = N_HGRN_COLS + N_GLA_COLS
    w_main = w_in[:, :, :gla_end - GLA_GATE_RANK].astype(BF16)
    w_alpha = jnp.pad(w_in[:, :, gla_end - GLA_GATE_RANK:gla_end],
                      ((0, 0), (0, 0), (0, LANES - GLA_GATE_RANK))).astype(BF16)
    w_rwkv = w_in[:, :, gla_end:].astype(BF16)
    alpha_up_p = jnp.concatenate(
        [gla_alpha_up, jnp.zeros((depth, LANES - GLA_GATE_RANK, W_GLA), gla_alpha_up.dtype)], axis=1)
    zeros_r = jnp.zeros((depth, RWKV_DECAY_RANK, W_RWKV), rwkv_w_up.dtype)
    wa_blk = jnp.concatenate(
        [jnp.concatenate([rwkv_w_up, zeros_r], axis=2), jnp.concatenate([zeros_r, rwkv_a_up], axis=2)], axis=1)
    norm_g = jnp.concatenate([hgrn_norm_g, gla_norm_g], axis=1)
    w_out_b = w_out.astype(BF16)
    w_up_b = mlp_w_up.astype(BF16)
    w_dn_b = mlp_w_down.astype(BF16)
    vec = lambda a, l: a[l].reshape(1, -1)

    x2 = x.reshape(bsz * seq, d)
    for l in range(depth):
        gq, gk, gv, glg, ggate, rr, rlw, rk, rv, rkk, rka, rhg = _in_projection(
            x2, mod[l], w_main, w_alpha, w_rwkv, vec(lbs, l), alpha_up_p[l], vec(gla_alpha_b, l), vec(rwkv_mu, l),
            vec(rwkv_w0, l), vec(rwkv_a0, l), wa_blk[l], vec(rwkv_k_k, l), vec(rwkv_k_a, l), seq, tm, l)
        ogl = _gl_mixer(gq, gk, gv, glg, ggate, vec(norm_g, l), bsz, seq, tb)
        orw = _rwkv_mixer(rr, rlw, rk, rv, rkk, rka, rhg, rwkv_g_up[l], vec(rwkv_r_k, l), vec(rwkv_gn_g, l),
                          vec(rwkv_gn_b, l), bsz, seq, tb)
        x2 = _post_mixer(x2, ogl, orw, mod[l], w_out_b, vec(ln1_g, l), vec(ln1_b, l), w_up_b, w_dn_b,
                         vec(ln2_g, l), vec(ln2_b, l), alpha, seq, tm, l)
    return x2.reshape(bsz, seq, d)
```

```python
import functools

import numpy as np
import jax
import jax.numpy as jnp
from jax import lax
from jax.experimental import pallas as pl
from jax.experimental.pallas import tpu as pltpu

F32 = jnp.float32
BF16 = jnp.bfloat16

HEAD_DIM = 64
H_HGRN, H_GLA, H_RWKV = 4, 6, 6
W_HGRN, W_GLA, W_RWKV = H_HGRN * HEAD_DIM, H_GLA * HEAD_DIM, H_RWKV * HEAD_DIM
GLA_GATE_RANK = 16
GLA_GATE_NORMALIZER = 16.0
RWKV_DECAY_RANK, RWKV_ICLR_RANK, RWKV_GATE_RANK = 64, 64, 128
RWKV_GN_EPS = 64e-5
N_HGRN_COLS = 4 * W_HGRN
N_GLA_COLS = 4 * W_GLA + GLA_GATE_RANK
N_RWKV_COLS = 3 * W_RWKV + RWKV_DECAY_RANK + RWKV_ICLR_RANK + RWKV_GATE_RANK
CHUNK = 64
LN_EPS = 1e-5
RMS_EPS = 1e-5
F_MIN = 1e-30

LANES = 128
PAIR = 2 * HEAD_DIM
SUB = 8
LEVEL_HALVES = (SUB, 2 * SUB, 4 * SUB)
assert 2 * LEVEL_HALVES[-1] == CHUNK
A_GROUP_GL = 4
LOG2E = 1.4426950408889634
VMEM_LIMIT = 56 * 1024 * 1024

W_GL = W_HGRN + W_GLA
N_GL_PAIRS = W_GL // PAIR
N_RW_PAIRS = W_RWKV // PAIR

NN = (((1,), (0,)), ((), ()))
NT = (((1,), (1,)), ((), ()))
TN = (((0,), (0,)), ((), ()))


def _mm(a, b, dims=NN):
    return lax.dot_general(a, b, dims, preferred_element_type=F32)


def _split2(x):
    hi = x.astype(BF16)
    lo = (x - hi.astype(F32)).astype(BF16)
    return hi, lo


def _bf16_parts(x, n):
    parts = []
    for i in range(n):
        p = x.astype(BF16)
        parts.append(p)
        if i + 1 < n:
            x = x - p.astype(F32)
    return parts


def _dot1(a, b, dims=NN):
    return _mm(a.astype(BF16), b.astype(BF16), dims)


def _dot3(a, b, dims=NN):
    ah, al = _split2(a)
    bh, bl = _split2(b)
    return _mm(ah, bh, dims) + (_mm(ah, bl, dims) + _mm(al, bh, dims))


def _dot_exact_lhs(a_bf16, b, n):
    return _mm(jnp.concatenate([a_bf16] * n, axis=1), jnp.concatenate(_bf16_parts(b, n), axis=0))


def _sigmoid(x):
    return 1.0 / (1.0 + jnp.exp(-x))


def _silu(x):
    return x * _sigmoid(x)


def _softplus(x):
    return jnp.maximum(x, 0.0) + jnp.log1p(jnp.exp(-jnp.abs(x)))


def _log_sigmoid(x):
    return -_softplus(-x)


def _layer_norm(y, g, b):
    mu = jnp.mean(y, axis=-1, keepdims=True)
    d = y - mu
    var = jnp.mean(d * d, axis=-1, keepdims=True)
    return d * lax.rsqrt(var + LN_EPS) * g + b


def _iota(shape, axis):
    return lax.broadcasted_iota(jnp.int32, shape, axis)


def _stack_heads(x):
    lane = _iota(x.shape, 1)
    return jnp.concatenate([jnp.where(lane < HEAD_DIM, x, 0.0), jnp.where(lane >= HEAD_DIM, x, 0.0)], axis=0)


def _pair_masks():
    t = _iota((CHUNK, PAIR), 0)
    s = _iota((CHUNK, PAIR), 1) % HEAD_DIM
    r2 = _iota((PAIR, PAIR), 0) // HEAD_DIM
    c2 = _iota((PAIR, PAIR), 1) // HEAD_DIM
    return dict(
        strict=s < t,
        incl=s <= t,
        eye=s == t,
        diag=(s // SUB == t // SUB) & (s <= t),
        level={hs: (s // (2 * hs) == t // (2 * hs)) & ((t // hs) % 2 == 1) & ((s // hs) % 2 == 0)
               for hs in LEVEL_HALVES},
        bd=r2 == c2,
    )


def _tril_bf16():
    return jnp.where(_iota((CHUNK, CHUNK), 1) <= _iota((CHUNK, CHUNK), 0), 1.0, 0.0).astype(BF16)


def _seg_mean_matrix(width):
    r = _iota((width, width), 0) // HEAD_DIM
    c = _iota((width, width), 1) // HEAD_DIM
    return jnp.where(r == c, 1.0, 0.0).astype(BF16)


def _head_sums(xs):
    tiles = [x[:, j:j + LANES].astype(BF16) for x in xs for j in range(0, x.shape[1], LANES)]
    sums = []
    for i in range(0, len(tiles) - 1, 2):
        s = _mm(jnp.concatenate(tiles[i:i + 2], axis=1), _seg_mean_matrix(2 * LANES))
        sums += [s[:, 0:LANES], s[:, LANES:2 * LANES]]
    if len(tiles) % 2:
        sums.append(_mm(tiles[-1], _seg_mean_matrix(LANES)))
    out, i = [], 0
    for x in xs:
        n = x.shape[1] // LANES
        out.append(jnp.concatenate(sums[i:i + n], axis=1))
        i += n
    return out


def _mod_kernel(c_ref, w_ref, b_ref, o_ref):
    c = _silu(c_ref[...])
    o_ref[0] = _dot3(c, w_ref[0]) + b_ref[0]


def _modulation(c, ada_w, ada_b):
    depth, d, n = ada_w.shape
    bsz = c.shape[0]
    tn = 1536
    return pl.pallas_call(
        _mod_kernel,
        grid=(depth, n // tn),
        in_specs=[
            pl.BlockSpec((bsz, d), lambda l, j: (0, 0)),
            pl.BlockSpec((1, d, tn), lambda l, j: (l, 0, j)),
            pl.BlockSpec((1, 1, tn), lambda l, j: (l, 0, j)),
        ],
        out_specs=pl.BlockSpec((1, bsz, tn), lambda l, j: (l, 0, j)),
        out_shape=jax.ShapeDtypeStruct((depth, bsz, n), F32),
        compiler_params=pltpu.CompilerParams(
            dimension_semantics=("arbitrary", "arbitrary"), vmem_limit_bytes=VMEM_LIMIT),
        name="adaln_modulation",
    )(c, ada_w, ada_b.reshape(depth, 1, n))


def _lower_bound_kernel(x_ref, o_ref):
    depth = x_ref.shape[0]
    rows = [x_ref[l:l + 1, :] for l in range(depth)]
    m = functools.reduce(jnp.maximum, rows)
    e = [jnp.exp(r - m) for r in rows]
    tot = functools.reduce(lambda a, b: a + b, e)
    p = [ei / tot for ei in e]
    acc = jnp.zeros_like(p[0])
    for l in range(depth):
        acc = acc + p[l]
        o_ref[l:l + 1, :] = acc - p[0]


def _lower_bounds(logits):
    return pl.pallas_call(
        _lower_bound_kernel,
        out_shape=jax.ShapeDtypeStruct(logits.shape, F32),
        name="hgrn_lower_bounds",
    )(logits.astype(F32))


def _inproj_kernel(steps_per_batch, x_ref, mod_ref, w_ref, wal_ref, wr_ref, lb_ref, au_ref, ab_ref, mu_ref,
                   w0_ref, a0_ref, wa_ref, kk_ref, ka_ref,
                   gq_ref, gk_ref, gv_ref, gb_ref, gg_ref,
                   rr_ref, rlw_ref, rk_ref, rv_ref, rkk_ref, rka_ref, rhg_ref, carry_s):
    tm = x_ref.shape[0]
    w3 = W_RWKV

    @pl.when(pl.program_id(0) % steps_per_batch == 0)
    def _():
        carry_s[...] = jnp.zeros_like(carry_s)

    m = mod_ref[0]
    h = (x_ref[...] * (1.0 + m[1:2, :]) + m[0:1, :]).astype(BF16)

    def proj(c0, width):
        return _mm(h, w_ref[:, c0:c0 + width])

    first_row = _iota((tm, LANES), 0) == 0

    def shifted(c0, width):
        z = _mm(h, wr_ref[:, c0:c0 + width])
        z_prev = pltpu.roll(z, 1, axis=0)
        z_prev = jnp.where(jnp.concatenate([first_row] * (width // LANES), axis=1), carry_s[:, c0:c0 + width], z_prev)
        carry_s[:, c0:c0 + width] = z[tm - 1:tm, :]
        return z + (z_prev - z) * mu_ref[:, c0:c0 + width]

    z_v = shifted(2 * w3, w3 + LANES)
    z_rk = shifted(0, 2 * w3)
    z_hg = shifted(3 * w3 + LANES, RWKV_GATE_RANK)
    g0 = N_HGRN_COLS
    z_al = _mm(h, wal_ref[...])
    z_qk = proj(g0, 2 * W_GLA)
    z_vr = proj(g0 + 2 * W_GLA, 2 * W_GLA)
    zf = proj(W_HGRN, W_HGRN)
    zq = proj(0, W_HGRN)
    zg = proj(3 * W_HGRN, W_HGRN)
    zi = proj(2 * W_HGRN, W_HGRN)

    h_wa = z_v[:, w3:w3 + LANES]
    h_wa = jnp.where(_iota(h_wa.shape, 1) < RWKV_DECAY_RANK, jnp.tanh(h_wa), h_wa)
    wa = _dot1(h_wa, wa_ref[...])
    w_log = -_softplus(-(w0_ref[...] + wa[:, 0:w3])) - 0.5
    rlw_ref[...] = -jnp.exp(w_log)
    a = _sigmoid(a0_ref[...] + wa[:, w3:2 * w3])
    rv_ref[...] = z_v[:, 0:w3]
    k = z_rk[:, w3:2 * w3]
    kk = k * kk_ref[...]
    kk = kk / jnp.maximum(jnp.sqrt(_head_sums([kk * kk])[0]), 1e-12)
    rkk_ref[...] = kk
    rka_ref[...] = kk * a
    rk_ref[...] = k * (1.0 + (a - 1.0) * ka_ref[...])
    rr_ref[...] = z_rk[:, 0:w3]
    rhg_ref[...] = _sigmoid(z_hg).astype(BF16)

    logit = _dot1(z_al, au_ref[...]) + ab_ref[...]
    gb_ref[:, W_HGRN:W_GL] = _log_sigmoid(logit) / GLA_GATE_NORMALIZER
    gq_ref[:, W_HGRN:W_GL] = z_qk[:, 0:W_GLA] * HEAD_DIM ** -0.5
    gk_ref[:, W_HGRN:W_GL] = z_qk[:, W_GLA:2 * W_GLA]
    gv_ref[:, W_HGRN:W_GL] = z_vr[:, 0:W_GLA].astype(BF16)
    gg_ref[:, W_HGRN:W_GL] = _silu(z_vr[:, W_GLA:2 * W_GLA]).astype(BF16)

    lb = lb_ref[...]
    f = lb + (1.0 - lb) * _sigmoid(zf)
    gb_ref[:, 0:W_HGRN] = jnp.log(jnp.maximum(f, F_MIN))
    gk_ref[:, 0:W_HGRN] = (1.0 - lb) * _sigmoid(-zf)
    gq_ref[:, 0:W_HGRN] = _silu(zq) * HEAD_DIM ** -0.5
    gg_ref[:, 0:W_HGRN] = _silu(zg).astype(BF16)
    gv_ref[:, 0:W_HGRN] = zi.astype(BF16)


def _in_projection(x2, mod_l, w_main, w_alpha, w_rwkv, lb_l, alpha_up_p, alpha_b, mu, w0, a0, wa_blk, k_k, k_a,
                   seq, tm, layer):
    m_rows, d = x2.shape
    resident = lambda n: pl.BlockSpec((None, d, n), lambda i: (layer, 0, 0), pipeline_mode=pl.Buffered(1))
    steps_per_batch = seq // tm
    row = lambda i: (i, 0)
    const = lambda i: (0, 0)
    vec = lambda n: pl.BlockSpec((1, n), const)
    out = lambda n, dt: (pl.BlockSpec((tm, n), row), jax.ShapeDtypeStruct((m_rows, n), dt))
    outs = [out(W_GL, F32), out(W_GL, F32), out(W_GL, BF16), out(W_GL, F32), out(W_GL, BF16),
            out(W_RWKV, F32), out(W_RWKV, F32), out(W_RWKV, F32), out(W_RWKV, F32), out(W_RWKV, F32),
            out(W_RWKV, F32), out(RWKV_GATE_RANK, BF16)]
    return pl.pallas_call(
        functools.partial(_inproj_kernel, steps_per_batch),
        grid=(m_rows // tm,),
        in_specs=[
            pl.BlockSpec((tm, d), row),
            pl.BlockSpec((1, 6, d), lambda i: (i // steps_per_batch, 0, 0)),
            resident(N_HGRN_COLS + 4 * W_GLA), resident(LANES), resident(N_RWKV_COLS),
            vec(W_HGRN),
            pl.BlockSpec((LANES, W_GLA), const),
            vec(W_GLA),
            vec(N_RWKV_COLS), vec(W_RWKV), vec(W_RWKV),
            pl.BlockSpec((LANES, 2 * W_RWKV), const),
            vec(W_RWKV), vec(W_RWKV),
        ],
        out_specs=[o[0] for o in outs],
        out_shape=[o[1] for o in outs],
        scratch_shapes=[pltpu.VMEM((1, N_RWKV_COLS), F32)],
        compiler_params=pltpu.CompilerParams(
            dimension_semantics=("arbitrary",), vmem_limit_bytes=VMEM_LIMIT),
        name="in_projection",
    )(x2, mod_l, w_main, w_alpha, w_rwkv, lb_l, alpha_up_p, alpha_b, mu, w0, a0, wa_blk, k_k, k_a)


def _gl_kernel(q_s, k_s, v_s, lg_ref, gate_ref, ng_ref, ind_ref, out_ref,
               st_ref, b_s, ad_s, e_s, o_s, sc_s, eend_s):
    tb = q_s.shape[0]
    n_chunks = tb // CHUNK

    @pl.when(pl.program_id(1) == 0)
    def _():
        st_ref[...] = jnp.zeros_like(st_ref)

    tril = _tril_bf16()
    for c in range(n_chunks):
        rows = slice(c * CHUNK, (c + 1) * CHUNK)
        b_s[rows, :] = _dot_exact_lhs(tril, lg_ref[rows, :], 2)

    for p in range(N_GL_PAIRS):
        lanes = slice(p * PAIR, (p + 1) * PAIR)
        q3 = q_s[:, lanes].reshape(tb // SUB, SUB, PAIR)
        k3 = k_s[:, lanes].reshape(tb // SUB, SUB, PAIR)
        b3 = (b_s[:, lanes] * LOG2E).reshape(tb // SUB, SUB, PAIR)
        for sg in range(SUB):
            dec = jnp.exp2(jnp.minimum(b3 - b3[:, sg:sg + 1, :], 0.0))
            e = q3 * k3[:, sg:sg + 1, :] * dec
            e_s[:, sg * PAIR:(sg + 1) * PAIR] = e.reshape(tb, PAIR).astype(BF16)
        ad_s[:, lanes] = _mm(e_s[...], ind_ref[...])

    masks = _pair_masks()
    t_idx = _iota((CHUNK, PAIR), 0)
    second_half = {hs: (t_idx // hs) % 2 == 1 for hs in LEVEL_HALVES}
    level_sign = {hs: jnp.where(second_half[hs], LOG2E, -LOG2E) for hs in LEVEL_HALVES}

    def group_units(i):
        units = []
        for cc in range(A_GROUP_GL):
            c = i * A_GROUP_GL + cc
            rows = pl.ds(pl.multiple_of(c * CHUNK, CHUNK), CHUNK)
            for p in range(N_GL_PAIRS):
                units.append(dict(rows=rows, lanes=slice(p * PAIR, (p + 1) * PAIR),
                                  e_rows=pl.ds(pl.multiple_of(c * 8, 8), 8),
                                  st_rows=pl.ds(pl.multiple_of((c * N_GL_PAIRS + p) * PAIR, PAIR), PAIR)))
        return units

    def increments(i, carry):
        units = group_units(i)
        for un in units:
            b = b_s[un["rows"], un["lanes"]]
            b_end = b[CHUNK - 1:CHUNK, :]
            un["k_end"] = k_s[un["rows"], un["lanes"]] * jnp.exp(b_end - b)
            eend_s[un["e_rows"], un["lanes"]] = jnp.broadcast_to(jnp.exp(b_end), (8, PAIR))
        for un in units:
            upd = _dot1(v_s[un["rows"], un["lanes"]], un["k_end"], TN)
            sc_s[un["st_rows"], :] = jnp.where(masks["bd"], upd, 0.0)
        return carry

    lax.fori_loop(0, n_chunks // A_GROUP_GL, increments, 0)

    sts = [st_ref[p] for p in range(N_GL_PAIRS)]
    for c in range(n_chunks):
        for p in range(N_GL_PAIRS):
            st_rows = slice((c * N_GL_PAIRS + p) * PAIR, (c * N_GL_PAIRS + p + 1) * PAIR)
            upd = sc_s[st_rows, :]
            sc_s[st_rows, :] = sts[p]
            sts[p] = sts[p] * eend_s[c * 8:c * 8 + 1, p * PAIR:(p + 1) * PAIR] + upd
    for p in range(N_GL_PAIRS):
        st_ref[p] = sts[p]

    def outputs(i, carry):
        units = group_units(i)
        for un in units:
            q = q_s[un["rows"], un["lanes"]]
            k = k_s[un["rows"], un["lanes"]]
            b = b_s[un["rows"], un["lanes"]]
            un["lv"] = []
            for hs in LEVEL_HALVES:
                e_mid = jnp.concatenate(
                    [jnp.broadcast_to(b[m:m + 1, :], (2 * hs, PAIR)) for m in range(hs - 1, CHUNK, 2 * hs)], axis=0)
                dec = jnp.exp2(jnp.minimum((b - e_mid) * level_sign[hs], 0.0))
                un["lv"].append(jnp.where(second_half[hs], q, k) * dec)
            un["q_in"] = q * jnp.exp(b)
        for un in units:
            a = jnp.zeros((CHUNK, PAIR), F32)
            for hs, lv in zip(LEVEL_HALVES, un["lv"]):
                a = jnp.where(masks["level"][hs], _dot1(lv, jnp.transpose(_stack_heads(lv))), a)
            un["a"] = jnp.where(masks["diag"], ad_s[un["rows"], un["lanes"]], a)
        for un in units:
            o_s[un["rows"], un["lanes"]] = (_dot1(un["a"], _stack_heads(v_s[un["rows"], un["lanes"]]))
                                            + _dot1(un["q_in"], sc_s[un["st_rows"], :], NT))
        return carry

    lax.fori_loop(0, n_chunks // A_GROUP_GL, outputs, 0)

    o = o_s[...]
    ms = _head_sums([o * o])[0] * (1.0 / HEAD_DIM)
    out_ref[...] = (o * lax.rsqrt(ms + RMS_EPS) * ng_ref[...] * gate_ref[...]).astype(out_ref.dtype)


def _gl_indicator():
    r = np.arange(SUB * PAIR)
    c = np.arange(PAIR)
    sg = r // PAIR
    h = (r % PAIR) // HEAD_DIM
    ind = (h[:, None] == (c // HEAD_DIM)[None, :]) & (sg[:, None] == (c % SUB)[None, :])
    return jnp.asarray(ind, dtype=BF16)


def _gl_mixer(q, k, v, lg, gate, norm_g, bsz, seq, tb):
    steps = seq // tb
    row = lambda b, i: (b * steps + i, 0)
    const = lambda b, i: (0, 0)
    slab = pl.BlockSpec((tb, W_GL), row)
    scratch_f32 = pltpu.VMEM((tb, W_GL), F32)
    return pl.pallas_call(
        _gl_kernel,
        grid=(bsz, steps),
        in_specs=[
            slab, slab, slab, slab, slab,
            pl.BlockSpec((1, W_GL), const),
            pl.BlockSpec((SUB * PAIR, PAIR), const),
        ],
        out_specs=pl.BlockSpec((tb, W_GL), row),
        out_shape=jax.ShapeDtypeStruct((bsz * seq, W_GL), BF16),
        scratch_shapes=[
            pltpu.VMEM((N_GL_PAIRS, PAIR, PAIR), F32),
            scratch_f32, scratch_f32,
            pltpu.VMEM((tb, SUB * PAIR), BF16),
            scratch_f32,
            pltpu.VMEM((tb // CHUNK * N_GL_PAIRS * PAIR, PAIR), F32),
            pltpu.VMEM((tb // CHUNK * 8, W_GL), F32),
        ],
        compiler_params=pltpu.CompilerParams(
            dimension_semantics=("arbitrary", "arbitrary"), vmem_limit_bytes=VMEM_LIMIT),
        name="hgrn_gla_mixer",
    )(q, k, v, lg, gate, norm_g, _gl_indicator())


def _rwkv_kernel(r_s, lw_s, k_s, v_s, kk_s, ka_s, hg_s, gup_ref, rk_ref, gng_ref, gnb_ref,
                 out_ref, st_ref, y_s, rt_s, bh_s, kh_s, nb_s, nk_s, w_s, u0_s, eend_s):
    bsz, tb = r_s.shape[0], r_s.shape[1]
    n_chunks = tb // CHUNK

    @pl.when(pl.program_id(0) == 0)
    def _():
        st_ref[...] = jnp.zeros_like(st_ref)

    masks = _pair_masks()
    tril = _tril_bf16()
    eye = jnp.where(masks["eye"], 1.0, 0.0)

    def phase_a(b, carry):
        units = []
        for c in range(n_chunks):
            rows = slice(c * CHUNK, (c + 1) * CHUNK)
            g_all = _dot_exact_lhs(tril, lw_s[b, rows, :], 2)
            for p in range(N_RW_PAIRS):
                lanes = slice(p * PAIR, (p + 1) * PAIR)
                units.append(dict(c=c, rows=rows, lanes=lanes, g=g_all[:, lanes]))
        for un in units:
            rows, lanes, g = un["rows"], un["lanes"], un["g"]
            kc = k_s[b, rows, lanes]
            kkc = kk_s[b, rows, lanes]
            ka = ka_s[b, rows, lanes]
            g_end = g[CHUNK - 1:CHUNK, :]
            e_neg = jnp.exp(-g)
            e_end = jnp.exp(g_end - g)
            un["at"] = -kkc * jnp.exp(g - lw_s[b, rows, lanes])
            rt = r_s[b, rows, lanes] * jnp.exp(g)
            un["rt"] = rt
            un["bt"] = ka * e_neg
            un["kt"] = kc * e_neg
            rt_s[b, rows, lanes] = rt
            bh_s[b, rows, lanes] = ka * e_end
            kh_s[b, rows, lanes] = kc * e_end
            eend_s[b, un["c"] * 8:un["c"] * 8 + 8, lanes] = jnp.broadcast_to(jnp.exp(g_end), (8, PAIR))
        for un in units:
            gram = _dot1(jnp.concatenate([un["at"], un["rt"]], axis=0),
                         jnp.concatenate([_stack_heads(un["bt"]), _stack_heads(un["kt"])], axis=0), NT)
            un["l"] = jnp.where(masks["strict"], gram[0:CHUNK, 0:PAIR], 0.0)
            un["mk"] = jnp.where(masks["strict"], gram[0:CHUNK, PAIR:2 * PAIR], 0.0)
            nb_s[b, un["rows"], un["lanes"]] = jnp.where(masks["incl"], gram[CHUNK:2 * CHUNK, 0:PAIR], 0.0)
            nk_s[b, un["rows"], un["lanes"]] = jnp.where(masks["incl"], gram[CHUNK:2 * CHUNK, PAIR:2 * PAIR], 0.0)
        for un in units:
            un["t"] = eye + un["l"]
            un["lp"] = _dot1(un["l"], _stack_heads(un["l"]))
        for _ in range(4):
            for un in units:
                both = _dot1(jnp.concatenate([un["lp"], un["t"]], axis=0), _stack_heads(un["lp"]))
                un["lp"] = both[0:CHUNK, :]
                un["t"] = un["t"] + both[CHUNK:2 * CHUNK, :]
        for un in units:
            un["t"] = un["t"] + _dot1(un["t"], _stack_heads(un["lp"]))
        for un in units:
            un["mv"] = _dot1(un["mk"], _stack_heads(v_s[b, un["rows"], un["lanes"]]))
        for un in units:
            wu = _dot1(un["t"], jnp.concatenate([_stack_heads(un["at"]), _stack_heads(un["mv"])], axis=1))
            w_s[b, un["rows"], un["lanes"]] = wu[:, 0:PAIR]
            u0_s[b, un["rows"], un["lanes"]] = wu[:, PAIR:2 * PAIR]
        return carry

    lax.fori_loop(0, bsz, phase_a, 0)

    def phase_b(c, carry):
        rows = pl.ds(pl.multiple_of(c * CHUNK, CHUNK), CHUNK)
        e_rows = pl.ds(pl.multiple_of(c * 8, 8), 8)
        seqs = [(b, p, slice(p * PAIR, (p + 1) * PAIR)) for b in range(bsz) for p in range(N_RW_PAIRS)]
        sts = [st_ref[b * N_RW_PAIRS + p] for b, p, _ in seqs]
        vs = [v_s[b, rows, lanes] for b, _, lanes in seqs]
        wrs = [_dot1(jnp.concatenate([w_s[b, rows, lanes], rt_s[b, rows, lanes]], axis=0), st, NT)
               for (b, _, lanes), st in zip(seqs, sts)]
        us = [wr[0:CHUNK, :] + u0_s[b, rows, lanes] for (b, _, lanes), wr in zip(seqs, wrs)]
        upds = [_dot1(jnp.concatenate([u, vc], axis=0),
                      jnp.concatenate([bh_s[b, rows, lanes], kh_s[b, rows, lanes]], axis=0), TN)
                for (b, _, lanes), u, vc in zip(seqs, us, vs)]
        for (b, p, lanes), st, upd in zip(seqs, sts, upds):
            e_end = eend_s[b, e_rows, lanes][0:1, :]
            st_ref[b * N_RW_PAIRS + p] = st * e_end + jnp.where(masks["bd"], upd, 0.0)
        for (b, _, lanes), wr, u, vc in zip(seqs, wrs, us, vs):
            y_s[b, rows, lanes] = (
                wr[CHUNK:2 * CHUNK, :]
                + _dot1(jnp.concatenate([nb_s[b, rows, lanes], nk_s[b, rows, lanes]], axis=1),
                        jnp.concatenate([_stack_heads(u), _stack_heads(vc)], axis=0)))
        return carry

    lax.fori_loop(0, n_chunks, phase_b, 0)

    for b in range(bsz):
        y = y_s[b]
        s_hi, s_lo = _head_sums(_bf16_parts(y, 2))
        d = y - (s_hi + s_lo) * (1.0 / HEAD_DIM)
        s_var, s_bonus = _head_sums([d * d, r_s[b] * k_s[b] * rk_ref[...]])
        yn = d * lax.rsqrt(s_var * (1.0 / HEAD_DIM) + RWKV_GN_EPS) * gng_ref[...] + gnb_ref[...]
        gate = _dot1(hg_s[b], gup_ref[...])
        out_ref[b] = ((yn + s_bonus * v_s[b]) * gate).astype(out_ref.dtype)


def _rwkv_mixer(r, lw, k, v, kk, ka, hg, g_up, r_k, gn_g, gn_b, bsz, seq, tb):
    blk = lambda i: (0, i, 0)
    const = lambda i: (0, 0)
    vec = pl.BlockSpec((1, W_RWKV), const)
    tok = pl.BlockSpec((bsz, tb, W_RWKV), blk)
    slab = pltpu.VMEM((bsz, tb, W_RWKV), F32)
    seq_major = lambda a: a.reshape(bsz, seq, a.shape[-1])
    out = pl.pallas_call(
        _rwkv_kernel,
        grid=(seq // tb,),
        in_specs=[
            tok, tok, tok, tok, tok, tok,
            pl.BlockSpec((bsz, tb, RWKV_GATE_RANK), blk),
            pl.BlockSpec((RWKV_GATE_RANK, W_RWKV), const),
            vec, vec, vec,
        ],
        out_specs=pl.BlockSpec((bsz, tb, W_RWKV), blk),
        out_shape=jax.ShapeDtypeStruct((bsz, seq, W_RWKV), BF16),
        scratch_shapes=[
            pltpu.VMEM((bsz * N_RW_PAIRS, PAIR, PAIR), F32),
            slab,
            slab, slab, slab, slab, slab, slab, slab,
            pltpu.VMEM((bsz, tb // CHUNK * 8, W_RWKV), F32),
        ],
        compiler_params=pltpu.CompilerParams(
            dimension_semantics=("arbitrary",), vmem_limit_bytes=VMEM_LIMIT),
        name="rwkv7_mixer",
    )(*[seq_major(a) for a in (r, lw, k, v, kk, ka, hg)], g_up, r_k, gn_g, gn_b)
    return out.reshape(bsz * seq, W_RWKV)


def _post_kernel(alpha, ff_chunk, x_ref, ogl_ref, orw_ref, mod_ref, wout_ref, ln1g_ref, ln1b_ref,
                 wup_ref, wdn_ref, ln2g_ref, ln2b_ref, out_ref):
    m = mod_ref[0]
    gate1, shift2, scale2, gate2 = m[2:3, :], m[3:4, :], m[4:5, :], m[5:6, :]
    o = _mm(ogl_ref[...], wout_ref[0:W_GL, :]) + _mm(orw_ref[...], wout_ref[W_GL:W_GL + W_RWKV, :])
    x1 = _layer_norm(alpha * x_ref[...] + (1.0 + gate1) * o, ln1g_ref[...], ln1b_ref[...])
    h = (x1 * (1.0 + scale2) + shift2).astype(BF16)
    d_ff = wup_ref.shape[1]
    acc = jnp.zeros(x1.shape, F32)
    for j in range(d_ff // ff_chunk):
        cols = slice(j * ff_chunk, (j + 1) * ff_chunk)
        u = jnp.maximum(_mm(h, wup_ref[:, cols]), 0.0)
        acc = acc + _mm((u * u).astype(BF16), wdn_ref[cols, :])
    out_ref[...] = _layer_norm(alpha * x1 + (1.0 + gate2) * acc, ln2g_ref[...], ln2b_ref[...])


def _post_mixer(x2, ogl, orw, mod_l, w_out, ln1_g, ln1_b, w_up, w_dn, ln2_g, ln2_b, alpha, seq, tm, layer):
    m_rows, d = x2.shape
    d_ff = w_up.shape[2]
    steps_per_batch = seq // tm
    row = lambda i: (i, 0)
    const = lambda i: (0, 0)
    vec = pl.BlockSpec((1, d), const)
    resident = lambda shape: pl.BlockSpec((None,) + shape, lambda i: (layer, 0, 0), pipeline_mode=pl.Buffered(1))
    return pl.pallas_call(
        functools.partial(_post_kernel, alpha, 512),
        grid=(m_rows // tm,),
        in_specs=[
            pl.BlockSpec((tm, d), row),
            pl.BlockSpec((tm, W_GL), row),
            pl.BlockSpec((tm, W_RWKV), row),
            pl.BlockSpec((1, 6, d), lambda i: (i // steps_per_batch, 0, 0)),
            resident((W_GL + W_RWKV, d)),
            vec, vec,
            resident((d, d_ff)),
            resident((d_ff, d)),
            vec, vec,
        ],
        out_specs=pl.BlockSpec((tm, d), row),
        out_shape=jax.ShapeDtypeStruct((m_rows, d), F32),
        compiler_params=pltpu.CompilerParams(
            dimension_semantics=("arbitrary",), vmem_limit_bytes=VMEM_LIMIT),
        name="outproj_mlp",
    )(x2, ogl, orw, mod_l, w_out, ln1_g, ln1_b, w_up, w_dn, ln2_g, ln2_b)


def kernel(x, c, hgrn_lb_logits, ada_w, ada_b, w_in, hgrn_norm_g, gla_alpha_up, gla_alpha_b, gla_norm_g,
           rwkv_mu, rwkv_w0, rwkv_w_up, rwkv_a0, rwkv_a_up, rwkv_g_up, rwkv_k_k, rwkv_k_a, rwkv_r_k,
           rwkv_gn_g, rwkv_gn_b, w_out, ln1_g, ln1_b, mlp_w_up, mlp_w_down, ln2_g, ln2_b):
    bsz, seq, d = x.shape
    depth = w_in.shape[0]
    alpha = (2.0 * depth) ** 0.25
    tm = min(512, seq)
    tb = min(512, seq)
    assert seq % tm == 0 and seq % tb == 0 and tb % CHUNK == 0
    assert w_in.shape[2] == N_HGRN_COLS + N_GLA_COLS + N_RWKV_COLS

    mod = _modulation(c, ada_w, ada_b).reshape(depth, bsz, 6, d)
    lbs = _lower_bounds(hgrn_lb_logits)

    gla_end = N_HGRN_COLS + N_GLA_COLS
    w_in_b = w_in.astype(BF16)
    w_alpha = jnp.pad(w_in_b[:, :, gla_end - GLA_GATE_RANK:gla_end], ((0, 0), (0, 0), (0, LANES - GLA_GATE_RANK)))
    w_rwkv = w_in_b[:, :, gla_end:]
    alpha_up_p = jnp.concatenate(
        [gla_alpha_up, jnp.zeros((depth, LANES - GLA_GATE_RANK, W_GLA), gla_alpha_up.dtype)], axis=1)
    zeros_r = jnp.zeros((depth, RWKV_DECAY_RANK, W_RWKV), rwkv_w_up.dtype)
    wa_blk = jnp.concatenate(
        [jnp.concatenate([rwkv_w_up, zeros_r], axis=2), jnp.concatenate([zeros_r, rwkv_a_up], axis=2)], axis=1)
    norm_g = jnp.concatenate([hgrn_norm_g, gla_norm_g], axis=1)
    w_out_b = w_out.astype(BF16)
    w_up_b = mlp_w_up.astype(BF16)
    w_dn_b = mlp_w_down.astype(BF16)
    vec = lambda a, l: a[l].reshape(1, -1)

    x2 = x.reshape(bsz * seq, d)
    for l in range(depth):
        gq, gk, gv, glg, ggate, rr, rlw, rk, rv, rkk, rka, rhg = _in_projection(
            x2, mod[l], w_in_b, w_alpha, w_rwkv, vec(lbs, l), alpha_up_p[l], vec(gla_alpha_b, l), vec(rwkv_mu, l),
            vec(rwkv_w0, l), vec(rwkv_a0, l), wa_blk[l], vec(rwkv_k_k, l), vec(rwkv_k_a, l), seq, tm, l)
        ogl = _gl_mixer(gq, gk, gv, glg, ggate, vec(norm_g, l), bsz, seq, min(1024, seq))
        orw = _rwkv_mixer(rr, rlw, rk, rv, rkk, rka, rhg, rwkv_g_up[l], vec(rwkv_r_k, l), vec(rwkv_gn_g, l),
                          vec(rwkv_gn_b, l), bsz, seq, tb)
        x2 = _post_mixer(x2, ogl, orw, mod[l], w_out_b, vec(ln1_g, l), vec(ln1_b, l), w_up_b, w_dn_b,
                         vec(ln2_g, l), vec(ln2_b, l), alpha, seq, tm, l)
    return x2.reshape(bsz, seq, d)
```

```python
import functools

import numpy as np
import jax
import jax.numpy as jnp
from jax import lax
from jax.experimental import pallas as pl
from jax.experimental.pallas import tpu as pltpu

F32 = jnp.float32
BF16 = jnp.bfloat16

HEAD_DIM = 64
H_HGRN, H_GLA, H_RWKV = 4, 6, 6
W_HGRN, W_GLA, W_RWKV = H_HGRN * HEAD_DIM, H_GLA * HEAD_DIM, H_RWKV * HEAD_DIM
GLA_GATE_RANK = 16
GLA_GATE_NORMALIZER = 16.0
RWKV_DECAY_RANK, RWKV_ICLR_RANK, RWKV_GATE_RANK = 64, 64, 128
RWKV_GN_EPS = 64e-5
N_HGRN_COLS = 4 * W_HGRN
N_GLA_COLS = 4 * W_GLA + GLA_GATE_RANK
N_RWKV_COLS = 3 * W_RWKV + RWKV_DECAY_RANK + RWKV_ICLR_RANK + RWKV_GATE_RANK
CHUNK = 64
LN_EPS = 1e-5
RMS_EPS = 1e-5
F_MIN = 1e-30

LANES = 128
PAIR = 2 * HEAD_DIM
SUB = 8
LEVEL_HALVES = (SUB, 2 * SUB, 4 * SUB)
assert 2 * LEVEL_HALVES[-1] == CHUNK
A_GROUP_GL = 4
LOG2E = 1.4426950408889634
VMEM_LIMIT = 56 * 1024 * 1024

W_GL = W_HGRN + W_GLA
N_GL_PAIRS = W_GL // PAIR
N_RW_PAIRS = W_RWKV // PAIR

NN = (((1,), (0,)), ((), ()))
NT = (((1,), (1,)), ((), ()))
TN = (((0,), (0,)), ((), ()))


def _mm(a, b, dims=NN):
    return lax.dot_general(a, b, dims, preferred_element_type=F32)


def _split2(x):
    hi = x.astype(BF16)
    lo = (x - hi.astype(F32)).astype(BF16)
    return hi, lo


def _bf16_parts(x, n):
    parts = []
    for i in range(n):
        p = x.astype(BF16)
        parts.append(p)
        if i + 1 < n:
            x = x - p.astype(F32)
    return parts


def _dot1(a, b, dims=NN):
    return _mm(a.astype(BF16), b.astype(BF16), dims)


def _dot3(a, b, dims=NN):
    ah, al = _split2(a)
    bh, bl = _split2(b)
    return _mm(ah, bh, dims) + (_mm(ah, bl, dims) + _mm(al, bh, dims))


def _dot_exact_lhs(a_bf16, b, n):
    return _mm(jnp.concatenate([a_bf16] * n, axis=1), jnp.concatenate(_bf16_parts(b, n), axis=0))


def _sigmoid(x):
    return 1.0 / (1.0 + jnp.exp(-x))


def _silu(x):
    return x * _sigmoid(x)


def _softplus(x):
    return jnp.maximum(x, 0.0) + jnp.log1p(jnp.exp(-jnp.abs(x)))


def _log_sigmoid(x):
    return -_softplus(-x)


def _layer_norm(y, g, b):
    mu = jnp.mean(y, axis=-1, keepdims=True)
    d = y - mu
    var = jnp.mean(d * d, axis=-1, keepdims=True)
    return d * lax.rsqrt(var + LN_EPS) * g + b


def _iota(shape, axis):
    return lax.broadcasted_iota(jnp.int32, shape, axis)


def _stack_heads(x):
    lane = _iota(x.shape, 1)
    return jnp.concatenate([jnp.where(lane < HEAD_DIM, x, 0.0), jnp.where(lane >= HEAD_DIM, x, 0.0)], axis=0)


def _pair_masks():
    t = _iota((CHUNK, PAIR), 0)
    s = _iota((CHUNK, PAIR), 1) % HEAD_DIM
    r2 = _iota((PAIR, PAIR), 0) // HEAD_DIM
    c2 = _iota((PAIR, PAIR), 1) // HEAD_DIM
    return dict(
        strict=s < t,
        incl=s <= t,
        eye=s == t,
        diag=(s // SUB == t // SUB) & (s <= t),
        level={hs: (s // (2 * hs) == t // (2 * hs)) & ((t // hs) % 2 == 1) & ((s // hs) % 2 == 0)
               for hs in LEVEL_HALVES},
        bd=r2 == c2,
    )


def _tril_bf16():
    return jnp.where(_iota((CHUNK, CHUNK), 1) <= _iota((CHUNK, CHUNK), 0), 1.0, 0.0).astype(BF16)


def _seg_mean_matrix(width):
    r = _iota((width, width), 0) // HEAD_DIM
    c = _iota((width, width), 1) // HEAD_DIM
    return jnp.where(r == c, 1.0, 0.0).astype(BF16)


def _head_sums(xs):
    tiles = [x[:, j:j + LANES].astype(BF16) for x in xs for j in range(0, x.shape[1], LANES)]
    sums = []
    for i in range(0, len(tiles) - 1, 2):
        s = _mm(jnp.concatenate(tiles[i:i + 2], axis=1), _seg_mean_matrix(2 * LANES))
        sums += [s[:, 0:LANES], s[:, LANES:2 * LANES]]
    if len(tiles) % 2:
        sums.append(_mm(tiles[-1], _seg_mean_matrix(LANES)))
    out, i = [], 0
    for x in xs:
        n = x.shape[1] // LANES
        out.append(jnp.concatenate(sums[i:i + n], axis=1))
        i += n
    return out


def _mod_kernel(c_ref, w_ref, b_ref, o_ref):
    c = _silu(c_ref[...])
    o_ref[0] = _dot3(c, w_ref[0]) + b_ref[0]


def _modulation(c, ada_w, ada_b):
    depth, d, n = ada_w.shape
    bsz = c.shape[0]
    tn = 1536
    return pl.pallas_call(
        _mod_kernel,
        grid=(depth, n // tn),
        in_specs=[
            pl.BlockSpec((bsz, d), lambda l, j: (0, 0)),
            pl.BlockSpec((1, d, tn), lambda l, j: (l, 0, j)),
            pl.BlockSpec((1, 1, tn), lambda l, j: (l, 0, j)),
        ],
        out_specs=pl.BlockSpec((1, bsz, tn), lambda l, j: (l, 0, j)),
        out_shape=jax.ShapeDtypeStruct((depth, bsz, n), F32),
        compiler_params=pltpu.CompilerParams(
            dimension_semantics=("arbitrary", "arbitrary"), vmem_limit_bytes=VMEM_LIMIT),
        name="adaln_modulation",
    )(c, ada_w, ada_b.reshape(depth, 1, n))


def _lower_bound_kernel(x_ref, o_ref):
    depth = x_ref.shape[0]
    rows = [x_ref[l:l + 1, :] for l in range(depth)]
    m = functools.reduce(jnp.maximum, rows)
    e = [jnp.exp(r - m) for r in rows]
    tot = functools.reduce(lambda a, b: a + b, e)
    p = [ei / tot for ei in e]
    acc = jnp.zeros_like(p[0])
    for l in range(depth):
        acc = acc + p[l]
        o_ref[l:l + 1, :] = acc - p[0]


def _lower_bounds(logits):
    return pl.pallas_call(
        _lower_bound_kernel,
        out_shape=jax.ShapeDtypeStruct(logits.shape, F32),
        name="hgrn_lower_bounds",
    )(logits.astype(F32))


def _inproj_kernel(steps_per_batch, x_ref, mod_ref, w_ref, wal_ref, wr_ref, lb_ref, au_ref, ab_ref, mu_ref,
                   w0_ref, a0_ref, wa_ref, kk_ref, ka_ref,
                   gq_ref, gk_ref, gv_ref, gb_ref, gg_ref,
                   rr_ref, rlw_ref, rk_ref, rv_ref, rkk_ref, rka_ref, rhg_ref, carry_s):
    tm = x_ref.shape[0]
    w3 = W_RWKV

    @pl.when(pl.program_id(0) % steps_per_batch == 0)
    def _():
        carry_s[...] = jnp.zeros_like(carry_s)

    m = mod_ref[0]
    h = (x_ref[...] * (1.0 + m[1:2, :]) + m[0:1, :]).astype(BF16)

    def proj(c0, width):
        return _mm(h, w_ref[:, c0:c0 + width])

    first_row = _iota((tm, LANES), 0) == 0

    def shifted(c0, width):
        z = _mm(h, wr_ref[:, c0:c0 + width])
        z_prev = pltpu.roll(z, 1, axis=0)
        z_prev = jnp.where(jnp.concatenate([first_row] * (width // LANES), axis=1), carry_s[:, c0:c0 + width], z_prev)
        carry_s[:, c0:c0 + width] = z[tm - 1:tm, :]
        return z + (z_prev - z) * mu_ref[:, c0:c0 + width]

    z_v = shifted(2 * w3, w3 + LANES)
    z_rk = shifted(0, 2 * w3)
    z_hg = shifted(3 * w3 + LANES, RWKV_GATE_RANK)
    g0 = N_HGRN_COLS
    z_al = _mm(h, wal_ref[...])
    z_qk = proj(g0, 2 * W_GLA)
    z_vr = proj(g0 + 2 * W_GLA, 2 * W_GLA)
    zf = proj(W_HGRN, W_HGRN)
    zq = proj(0, W_HGRN)
    zg = proj(3 * W_HGRN, W_HGRN)
    zi = proj(2 * W_HGRN, W_HGRN)

    h_wa = z_v[:, w3:w3 + LANES]
    h_wa = jnp.where(_iota(h_wa.shape, 1) < RWKV_DECAY_RANK, jnp.tanh(h_wa), h_wa)
    wa = _dot1(h_wa, wa_ref[...])
    w_log = -_softplus(-(w0_ref[...] + wa[:, 0:w3])) - 0.5
    rlw_ref[...] = -jnp.exp(w_log)
    a = _sigmoid(a0_ref[...] + wa[:, w3:2 * w3])
    rv_ref[...] = z_v[:, 0:w3]
    k = z_rk[:, w3:2 * w3]
    kk = k * kk_ref[...]
    kk = kk / jnp.maximum(jnp.sqrt(_head_sums([kk * kk])[0]), 1e-12)
    rkk_ref[...] = kk
    rka_ref[...] = kk * a
    rk_ref[...] = k * (1.0 + (a - 1.0) * ka_ref[...])
    rr_ref[...] = z_rk[:, 0:w3]
    rhg_ref[...] = _sigmoid(z_hg).astype(BF16)

    logit = _dot1(z_al, au_ref[...]) + ab_ref[...]
    gb_ref[:, W_HGRN:W_GL] = _log_sigmoid(logit) / GLA_GATE_NORMALIZER
    for p in range(W_GLA // PAIR):
        gq_ref[W_HGRN // PAIR + p] = z_qk[:, p * PAIR:(p + 1) * PAIR] * HEAD_DIM ** -0.5
        gk_ref[W_HGRN // PAIR + p] = z_qk[:, W_GLA + p * PAIR:W_GLA + (p + 1) * PAIR]
    gv_ref[:, W_HGRN:W_GL] = z_vr[:, 0:W_GLA].astype(BF16)
    gg_ref[:, W_HGRN:W_GL] = _silu(z_vr[:, W_GLA:2 * W_GLA]).astype(BF16)

    lb = lb_ref[...]
    f = lb + (1.0 - lb) * _sigmoid(zf)
    gb_ref[:, 0:W_HGRN] = jnp.log(jnp.maximum(f, F_MIN))
    k_h = (1.0 - lb) * _sigmoid(-zf)
    q_h = _silu(zq) * HEAD_DIM ** -0.5
    for p in range(W_HGRN // PAIR):
        gk_ref[p] = k_h[:, p * PAIR:(p + 1) * PAIR]
        gq_ref[p] = q_h[:, p * PAIR:(p + 1) * PAIR]
    gg_ref[:, 0:W_HGRN] = _silu(zg).astype(BF16)
    gv_ref[:, 0:W_HGRN] = zi.astype(BF16)


def _in_projection(x2, mod_l, w_main, w_alpha, w_rwkv, lb_l, alpha_up_p, alpha_b, mu, w0, a0, wa_blk, k_k, k_a,
                   seq, tm, layer):
    m_rows, d = x2.shape
    resident = lambda n: pl.BlockSpec((None, d, n), lambda i: (layer, 0, 0), pipeline_mode=pl.Buffered(1))
    steps_per_batch = seq // tm
    row = lambda i: (i, 0)
    const = lambda i: (0, 0)
    vec = lambda n: pl.BlockSpec((1, n), const)
    out = lambda n, dt: (pl.BlockSpec((tm, n), row), jax.ShapeDtypeStruct((m_rows, n), dt))
    pair_slabs = (pl.BlockSpec((N_GL_PAIRS, tm, PAIR), lambda i: (0, i, 0)),
                  jax.ShapeDtypeStruct((N_GL_PAIRS, m_rows, PAIR), F32))
    outs = [pair_slabs, pair_slabs, out(W_GL, BF16), out(W_GL, F32), out(W_GL, BF16),
            out(W_RWKV, F32), out(W_RWKV, F32), out(W_RWKV, F32), out(W_RWKV, F32), out(W_RWKV, F32),
            out(W_RWKV, F32), out(RWKV_GATE_RANK, BF16)]
    return pl.pallas_call(
        functools.partial(_inproj_kernel, steps_per_batch),
        grid=(m_rows // tm,),
        in_specs=[
            pl.BlockSpec((tm, d), row),
            pl.BlockSpec((1, 6, d), lambda i: (i // steps_per_batch, 0, 0)),
            resident(N_HGRN_COLS + 4 * W_GLA), resident(LANES), resident(N_RWKV_COLS),
            vec(W_HGRN),
            pl.BlockSpec((LANES, W_GLA), const),
            vec(W_GLA),
            vec(N_RWKV_COLS), vec(W_RWKV), vec(W_RWKV),
            pl.BlockSpec((LANES, 2 * W_RWKV), const),
            vec(W_RWKV), vec(W_RWKV),
        ],
        out_specs=[o[0] for o in outs],
        out_shape=[o[1] for o in outs],
        scratch_shapes=[pltpu.VMEM((1, N_RWKV_COLS), F32)],
        compiler_params=pltpu.CompilerParams(
            dimension_semantics=("arbitrary",), vmem_limit_bytes=VMEM_LIMIT),
        name="in_projection",
    )(x2, mod_l, w_main, w_alpha, w_rwkv, lb_l, alpha_up_p, alpha_b, mu, w0, a0, wa_blk, k_k, k_a)


def _gl_kernel(q_s, k_s, v_s, lg_ref, gate_ref, ng_ref, ind_ref, out_ref,
               st_ref, b_s, ad_s, e_s, o_s, sc_s, eend_s):
    tb = q_s.shape[1]
    n_chunks = tb // CHUNK
    n_groups = tb // SUB

    @pl.when(pl.program_id(1) == 0)
    def _():
        st_ref[...] = jnp.zeros_like(st_ref)

    tril = _tril_bf16()
    for c in range(n_chunks):
        rows = slice(c * CHUNK, (c + 1) * CHUNK)
        b = _dot_exact_lhs(tril, lg_ref[rows, :], 2)
        for p in range(N_GL_PAIRS):
            b_s[p, rows, :] = b[:, p * PAIR:(p + 1) * PAIR]

    for t in range(SUB - 1):
        e_s[t * n_groups:(t + 1) * n_groups, (t + 1) * PAIR:] = jnp.zeros((n_groups, (SUB - 1 - t) * PAIR), BF16)
    for p in range(N_GL_PAIRS):
        by_pos = lambda ref, t: ref[p, pl.ds(t, n_groups, stride=SUB), :]
        qs = [by_pos(q_s, t) for t in range(SUB)]
        ks = [by_pos(k_s, t) for t in range(SUB)]
        bs = [by_pos(b_s, t) * LOG2E for t in range(SUB)]
        for t in range(SUB):
            for sg in range(t + 1):
                e = qs[t] * ks[sg]
                if sg < t:
                    e = e * jnp.exp2(jnp.minimum(bs[t] - bs[sg], 0.0))
                e_s[t * n_groups:(t + 1) * n_groups, sg * PAIR:(sg + 1) * PAIR] = e.astype(BF16)
        ad = [jnp.zeros((n_groups, PAIR), F32)] * SUB
        for sg0 in range(0, SUB, 2):
            part = _mm(e_s[sg0 * n_groups:, sg0 * PAIR:(sg0 + 2) * PAIR], ind_ref[sg0 * PAIR:(sg0 + 2) * PAIR, :])
            for t in range(sg0, SUB):
                ad[t] = ad[t] + part[(t - sg0) * n_groups:(t - sg0 + 1) * n_groups, :]
        for t in range(SUB):
            ad_s[p, pl.ds(t, n_groups, stride=SUB), :] = ad[t]

    masks = _pair_masks()
    t_idx = _iota((CHUNK, PAIR), 0)
    second_half = {hs: (t_idx // hs) % 2 == 1 for hs in LEVEL_HALVES}
    level_sign = {hs: jnp.where(second_half[hs], LOG2E, -LOG2E) for hs in LEVEL_HALVES}

    def group_units(i):
        units = []
        for cc in range(A_GROUP_GL):
            c = i * A_GROUP_GL + cc
            rows = pl.ds(pl.multiple_of(c * CHUNK, CHUNK), CHUNK)
            for p in range(N_GL_PAIRS):
                units.append(dict(p=p, rows=rows, lanes=slice(p * PAIR, (p + 1) * PAIR),
                                  e_rows=pl.ds(pl.multiple_of(c * 8, 8), 8),
                                  st_rows=pl.ds(pl.multiple_of((c * N_GL_PAIRS + p) * PAIR, PAIR), PAIR)))
        return units

    def increments(i, carry):
        units = group_units(i)
        for un in units:
            b = b_s[un["p"], un["rows"], :]
            b_end = b[CHUNK - 1:CHUNK, :]
            un["k_end"] = k_s[un["p"], un["rows"], :] * jnp.exp(b_end - b)
            eend_s[un["e_rows"], un["lanes"]] = jnp.broadcast_to(jnp.exp(b_end), (8, PAIR))
        for un in units:
            upd = _dot1(v_s[un["rows"], un["lanes"]], un["k_end"], TN)
            sc_s[un["st_rows"], :] = jnp.where(masks["bd"], upd, 0.0)
        return carry

    lax.fori_loop(0, n_chunks // A_GROUP_GL, increments, 0)

    sts = [st_ref[p] for p in range(N_GL_PAIRS)]
    for c in range(n_chunks):
        for p in range(N_GL_PAIRS):
            st_rows = slice((c * N_GL_PAIRS + p) * PAIR, (c * N_GL_PAIRS + p + 1) * PAIR)
            upd = sc_s[st_rows, :]
            sc_s[st_rows, :] = sts[p]
            sts[p] = sts[p] * eend_s[c * 8:c * 8 + 1, p * PAIR:(p + 1) * PAIR] + upd
    for p in range(N_GL_PAIRS):
        st_ref[p] = sts[p]

    def outputs(i, carry):
        units = group_units(i)
        for un in units:
            q = q_s[un["p"], un["rows"], :]
            k = k_s[un["p"], un["rows"], :]
            b = b_s[un["p"], un["rows"], :]
            un["lv"] = []
            for hs in LEVEL_HALVES:
                e_mid = jnp.concatenate(
                    [jnp.broadcast_to(b[m:m + 1, :], (2 * hs, PAIR)) for m in range(hs - 1, CHUNK, 2 * hs)], axis=0)
                dec = jnp.exp2(jnp.minimum((b - e_mid) * level_sign[hs], 0.0))
                un["lv"].append(jnp.where(second_half[hs], q, k) * dec)
            un["q_in"] = q * jnp.exp(b)
        for un in units:
            a = jnp.zeros((CHUNK, PAIR), F32)
            for hs, lv in zip(LEVEL_HALVES, un["lv"]):
                a = jnp.where(masks["level"][hs], _dot1(lv, jnp.transpose(_stack_heads(lv))), a)
            un["a"] = jnp.where(masks["diag"], ad_s[un["p"], un["rows"], :], a)
        for un in units:
            o_s[un["rows"], un["lanes"]] = (_dot1(un["a"], _stack_heads(v_s[un["rows"], un["lanes"]]))
                                            + _dot1(un["q_in"], sc_s[un["st_rows"], :], NT))
        return carry

    lax.fori_loop(0, n_chunks // A_GROUP_GL, outputs, 0)

    o = o_s[...]
    ms = _head_sums([o * o])[0] * (1.0 / HEAD_DIM)
    out_ref[...] = (o * lax.rsqrt(ms + RMS_EPS) * ng_ref[...] * gate_ref[...]).astype(out_ref.dtype)


def _gl_indicator():
    r = np.arange(SUB * PAIR)
    c = np.arange(PAIR)
    sg = r // PAIR
    h = (r % PAIR) // HEAD_DIM
    ind = (h[:, None] == (c // HEAD_DIM)[None, :]) & (sg[:, None] == (c % SUB)[None, :])
    return jnp.asarray(ind, dtype=BF16)


def _gl_mixer(q, k, v, lg, gate, norm_g, bsz, seq, tb):
    steps = seq // tb
    row = lambda b, i: (b * steps + i, 0)
    const = lambda b, i: (0, 0)
    slab = pl.BlockSpec((tb, W_GL), row)
    pair_slabs = pl.BlockSpec((N_GL_PAIRS, tb, PAIR), lambda b, i: (0, b * steps + i, 0))
    scratch_f32 = pltpu.VMEM((tb, W_GL), F32)
    scratch_pairs = pltpu.VMEM((N_GL_PAIRS, tb, PAIR), F32)
    return pl.pallas_call(
        _gl_kernel,
        grid=(bsz, steps),
        in_specs=[
            pair_slabs, pair_slabs, slab, slab, slab,
            pl.BlockSpec((1, W_GL), const),
            pl.BlockSpec((SUB * PAIR, PAIR), const),
        ],
        out_specs=pl.BlockSpec((tb, W_GL), row),
        out_shape=jax.ShapeDtypeStruct((bsz * seq, W_GL), BF16),
        scratch_shapes=[
            pltpu.VMEM((N_GL_PAIRS, PAIR, PAIR), F32),
            scratch_pairs, scratch_pairs,
            pltpu.VMEM((tb, SUB * PAIR), BF16),
            scratch_f32,
            pltpu.VMEM((tb // CHUNK * N_GL_PAIRS * PAIR, PAIR), F32),
            pltpu.VMEM((tb // CHUNK * 8, W_GL), F32),
        ],
        compiler_params=pltpu.CompilerParams(
            dimension_semantics=("arbitrary", "arbitrary"), vmem_limit_bytes=VMEM_LIMIT),
        name="hgrn_gla_mixer",
    )(q, k, v, lg, gate, norm_g, _gl_indicator())


def _rwkv_kernel(r_s, lw_s, k_s, v_s, kk_s, ka_s, hg_s, gup_ref, rk_ref, gng_ref, gnb_ref,
                 out_ref, st_ref, y_s, rt_s, bh_s, kh_s, nb_s, nk_s, w_s, u0_s, eend_s):
    bsz, tb = r_s.shape[0], r_s.shape[1]
    n_chunks = tb // CHUNK

    @pl.when(pl.program_id(0) == 0)
    def _():
        st_ref[...] = jnp.zeros_like(st_ref)

    masks = _pair_masks()
    tril = _tril_bf16()
    eye = jnp.where(masks["eye"], 1.0, 0.0)

    def phase_a(b, carry):
        units = []
        for c in range(n_chunks):
            rows = slice(c * CHUNK, (c + 1) * CHUNK)
            g_all = _dot_exact_lhs(tril, lw_s[b, rows, :], 2)
            for p in range(N_RW_PAIRS):
                lanes = slice(p * PAIR, (p + 1) * PAIR)
                units.append(dict(c=c, rows=rows, lanes=lanes, g=g_all[:, lanes]))
        for un in units:
            rows, lanes, g = un["rows"], un["lanes"], un["g"]
            kc = k_s[b, rows, lanes]
            kkc = kk_s[b, rows, lanes]
            ka = ka_s[b, rows, lanes]
            g_end = g[CHUNK - 1:CHUNK, :]
            e_neg = jnp.exp(-g)
            e_end = jnp.exp(g_end - g)
            un["at"] = -kkc * jnp.exp(g - lw_s[b, rows, lanes])
            rt = r_s[b, rows, lanes] * jnp.exp(g)
            un["rt"] = rt
            un["bt"] = ka * e_neg
            un["kt"] = kc * e_neg
            rt_s[b, rows, lanes] = rt
            bh_s[b, rows, lanes] = ka * e_end
            kh_s[b, rows, lanes] = kc * e_end
            eend_s[b, un["c"] * 8:un["c"] * 8 + 8, lanes] = jnp.broadcast_to(jnp.exp(g_end), (8, PAIR))
        for un in units:
            gram = _dot1(jnp.concatenate([un["at"], un["rt"]], axis=0),
                         jnp.concatenate([_stack_heads(un["bt"]), _stack_heads(un["kt"])], axis=0), NT)
            un["l"] = jnp.where(masks["strict"], gram[0:CHUNK, 0:PAIR], 0.0)
            un["mk"] = jnp.where(masks["strict"], gram[0:CHUNK, PAIR:2 * PAIR], 0.0)
            nb_s[b, un["rows"], un["lanes"]] = jnp.where(masks["incl"], gram[CHUNK:2 * CHUNK, 0:PAIR], 0.0)
            nk_s[b, un["rows"], un["lanes"]] = jnp.where(masks["incl"], gram[CHUNK:2 * CHUNK, PAIR:2 * PAIR], 0.0)
        for un in units:
            un["t"] = eye + un["l"]
            un["lp"] = _dot1(un["l"], _stack_heads(un["l"]))
        for _ in range(4):
            for un in units:
                both = _dot1(jnp.concatenate([un["lp"], un["t"]], axis=0), _stack_heads(un["lp"]))
                un["lp"] = both[0:CHUNK, :]
                un["t"] = un["t"] + both[CHUNK:2 * CHUNK, :]
        for un in units:
            un["t"] = un["t"] + _dot1(un["t"], _stack_heads(un["lp"]))
        for un in units:
            un["mv"] = _dot1(un["mk"], _stack_heads(v_s[b, un["rows"], un["lanes"]]))
        for un in units:
            wu = _dot1(un["t"], jnp.concatenate([_stack_heads(un["at"]), _stack_heads(un["mv"])], axis=1))
            w_s[b, un["rows"], un["lanes"]] = wu[:, 0:PAIR]
            u0_s[b, un["rows"], un["lanes"]] = wu[:, PAIR:2 * PAIR]
        return carry

    lax.fori_loop(0, bsz, phase_a, 0)

    def phase_b(c, carry):
        rows = pl.ds(pl.multiple_of(c * CHUNK, CHUNK), CHUNK)
        e_rows = pl.ds(pl.multiple_of(c * 8, 8), 8)
        seqs = [(b, p, slice(p * PAIR, (p + 1) * PAIR)) for b in range(bsz) for p in range(N_RW_PAIRS)]
        sts = [st_ref[b * N_RW_PAIRS + p] for b, p, _ in seqs]
        vs = [v_s[b, rows, lanes] for b, _, lanes in seqs]
        wrs = [_dot1(jnp.concatenate([w_s[b, rows, lanes], rt_s[b, rows, lanes]], axis=0), st, NT)
               for (b, _, lanes), st in zip(seqs, sts)]
        us = [wr[0:CHUNK, :] + u0_s[b, rows, lanes] for (b, _, lanes), wr in zip(seqs, wrs)]
        upds = [_dot1(jnp.concatenate([u, vc], axis=0),
                      jnp.concatenate([bh_s[b, rows, lanes], kh_s[b, rows, lanes]], axis=0), TN)
                for (b, _, lanes), u, vc in zip(seqs, us, vs)]
        for (b, p, lanes), st, upd in zip(seqs, sts, upds):
            e_end = eend_s[b, e_rows, lanes][0:1, :]
            st_ref[b * N_RW_PAIRS + p] = st * e_end + jnp.where(masks["bd"], upd, 0.0)
        for (b, _, lanes), wr, u, vc in zip(seqs, wrs, us, vs):
            y_s[b, rows, lanes] = (
                wr[CHUNK:2 * CHUNK, :]
                + _dot1(jnp.concatenate([nb_s[b, rows, lanes], nk_s[b, rows, lanes]], axis=1),
                        jnp.concatenate([_stack_heads(u), _stack_heads(vc)], axis=0)))
        return carry

    lax.fori_loop(0, n_chunks, phase_b, 0)

    for b in range(bsz):
        y = y_s[b]
        s_hi, s_lo = _head_sums(_bf16_parts(y, 2))
        d = y - (s_hi + s_lo) * (1.0 / HEAD_DIM)
        s_var, s_bonus = _head_sums([d * d, r_s[b] * k_s[b] * rk_ref[...]])
        yn = d * lax.rsqrt(s_var * (1.0 / HEAD_DIM) + RWKV_GN_EPS) * gng_ref[...] + gnb_ref[...]
        gate = _dot1(hg_s[b], gup_ref[...])
        out_ref[b] = ((yn + s_bonus * v_s[b]) * gate).astype(out_ref.dtype)


def _rwkv_mixer(r, lw, k, v, kk, ka, hg, g_up, r_k, gn_g, gn_b, bsz, seq, tb):
    blk = lambda i: (0, i, 0)
    const = lambda i: (0, 0)
    vec = pl.BlockSpec((1, W_RWKV), const)
    tok = pl.BlockSpec((bsz, tb, W_RWKV), blk)
    slab = pltpu.VMEM((bsz, tb, W_RWKV), F32)
    seq_major = lambda a: a.reshape(bsz, seq, a.shape[-1])
    out = pl.pallas_call(
        _rwkv_kernel,
        grid=(seq // tb,),
        in_specs=[
            tok, tok, tok, tok, tok, tok,
            pl.BlockSpec((bsz, tb, RWKV_GATE_RANK), blk),
            pl.BlockSpec((RWKV_GATE_RANK, W_RWKV), const),
            vec, vec, vec,
        ],
        out_specs=pl.BlockSpec((bsz, tb, W_RWKV), blk),
        out_shape=jax.ShapeDtypeStruct((bsz, seq, W_RWKV), BF16),
        scratch_shapes=[
            pltpu.VMEM((bsz * N_RW_PAIRS, PAIR, PAIR), F32),
            slab,
            slab, slab, slab, slab, slab, slab, slab,
            pltpu.VMEM((bsz, tb // CHUNK * 8, W_RWKV), F32),
        ],
        compiler_params=pltpu.CompilerParams(
            dimension_semantics=("arbitrary",), vmem_limit_bytes=VMEM_LIMIT),
        name="rwkv7_mixer",
    )(*[seq_major(a) for a in (r, lw, k, v, kk, ka, hg)], g_up, r_k, gn_g, gn_b)
    return out.reshape(bsz * seq, W_RWKV)


def _post_kernel(alpha, ff_chunk, x_ref, ogl_ref, orw_ref, mod_ref, wout_ref, ln1g_ref, ln1b_ref,
                 wup_ref, wdn_ref, ln2g_ref, ln2b_ref, out_ref):
    m = mod_ref[0]
    gate1, shift2, scale2, gate2 = m[2:3, :], m[3:4, :], m[4:5, :], m[5:6, :]
    o = _mm(ogl_ref[...], wout_ref[0:W_GL, :]) + _mm(orw_ref[...], wout_ref[W_GL:W_GL + W_RWKV, :])
    x1 = _layer_norm(alpha * x_ref[...] + (1.0 + gate1) * o, ln1g_ref[...], ln1b_ref[...])
    h = (x1 * (1.0 + scale2) + shift2).astype(BF16)
    d_ff = wup_ref.shape[1]
    acc = jnp.zeros(x1.shape, F32)
    for j in range(d_ff // ff_chunk):
        cols = slice(j * ff_chunk, (j + 1) * ff_chunk)
        u = jnp.maximum(_mm(h, wup_ref[:, cols]), 0.0)
        acc = acc + _mm((u * u).astype(BF16), wdn_ref[cols, :])
    out_ref[...] = _layer_norm(alpha * x1 + (1.0 + gate2) * acc, ln2g_ref[...], ln2b_ref[...])


def _post_mixer(x2, ogl, orw, mod_l, w_out, ln1_g, ln1_b, w_up, w_dn, ln2_g, ln2_b, alpha, seq, tm, layer):
    m_rows, d = x2.shape
    d_ff = w_up.shape[2]
    steps_per_batch = seq // tm
    row = lambda i: (i, 0)
    const = lambda i: (0, 0)
    vec = pl.BlockSpec((1, d), const)
    resident = lambda shape: pl.BlockSpec((None,) + shape, lambda i: (layer, 0, 0), pipeline_mode=pl.Buffered(1))
    return pl.pallas_call(
        functools.partial(_post_kernel, alpha, 512),
        grid=(m_rows // tm,),
        in_specs=[
            pl.BlockSpec((tm, d), row),
            pl.BlockSpec((tm, W_GL), row),
            pl.BlockSpec((tm, W_RWKV), row),
            pl.BlockSpec((1, 6, d), lambda i: (i // steps_per_batch, 0, 0)),
            resident((W_GL + W_RWKV, d)),
            vec, vec,
            resident((d, d_ff)),
            resident((d_ff, d)),
            vec, vec,
        ],
        out_specs=pl.BlockSpec((tm, d), row),
        out_shape=jax.ShapeDtypeStruct((m_rows, d), F32),
        compiler_params=pltpu.CompilerParams(
            dimension_semantics=("arbitrary",), vmem_limit_bytes=VMEM_LIMIT),
        name="outproj_mlp",
    )(x2, ogl, orw, mod_l, w_out, ln1_g, ln1_b, w_up, w_dn, ln2_g, ln2_b)


def kernel(x, c, hgrn_lb_logits, ada_w, ada_b, w_in, hgrn_norm_g, gla_alpha_up, gla_alpha_b, gla_norm_g,
           rwkv_mu, rwkv_w0, rwkv_w_up, rwkv_a0, rwkv_a_up, rwkv_g_up, rwkv_k_k, rwkv_k_a, rwkv_r_k,
           rwkv_gn_g, rwkv_gn_b, w_out, ln1_g, ln1_b, mlp_w_up, mlp_w_down, ln2_g, ln2_b):
    bsz, seq, d = x.shape
    depth = w_in.shape[0]
    alpha = (2.0 * depth) ** 0.25
    tm = min(512, seq)
    tb = min(512, seq)
    assert seq % tm == 0 and seq % tb == 0 and tb % CHUNK == 0
    assert w_in.shape[2] == N_HGRN_COLS + N_GLA_COLS + N_RWKV_COLS

    mod = _modulation(c, ada_w, ada_b).reshape(depth, bsz, 6, d)
    lbs = _lower_bounds(hgrn_lb_logits)

    gla_end = N_HGRN_COLS + N_GLA_COLS
    w_in_b = w_in.astype(BF16)
    w_alpha = jnp.pad(w_in_b[:, :, gla_end - GLA_GATE_RANK:gla_end], ((0, 0), (0, 0), (0, LANES - GLA_GATE_RANK)))
    w_rwkv = w_in_b[:, :, gla_end:]
    alpha_up_p = jnp.concatenate(
        [gla_alpha_up, jnp.zeros((depth, LANES - GLA_GATE_RANK, W_GLA), gla_alpha_up.dtype)], axis=1)
    zeros_r = jnp.zeros((depth, RWKV_DECAY_RANK, W_RWKV), rwkv_w_up.dtype)
    wa_blk = jnp.concatenate(
        [jnp.concatenate([rwkv_w_up, zeros_r], axis=2), jnp.concatenate([zeros_r, rwkv_a_up], axis=2)], axis=1)
    norm_g = jnp.concatenate([hgrn_norm_g, gla_norm_g], axis=1)
    w_out_b = w_out.astype(BF16)
    w_up_b = mlp_w_up.astype(BF16)
    w_dn_b = mlp_w_down.astype(BF16)
    vec = lambda a, l: a[l].reshape(1, -1)

    x2 = x.reshape(bsz * seq, d)
    for l in range(depth):
        gq, gk, gv, glg, ggate, rr, rlw, rk, rv, rkk, rka, rhg = _in_projection(
            x2, mod[l], w_in_b, w_alpha, w_rwkv, vec(lbs, l), alpha_up_p[l], vec(gla_alpha_b, l), vec(rwkv_mu, l),
            vec(rwkv_w0, l), vec(rwkv_a0, l), wa_blk[l], vec(rwkv_k_k, l), vec(rwkv_k_a, l), seq, tm, l)
        ogl = _gl_mixer(gq, gk, gv, glg, ggate, vec(norm_g, l), bsz, seq, min(1024, seq))
        orw = _rwkv_mixer(rr, rlw, rk, rv, rkk, rka, rhg, rwkv_g_up[l], vec(rwkv_r_k, l), vec(rwkv_gn_g, l),
                          vec(rwkv_gn_b, l), bsz, seq, tb)
        x2 = _post_mixer(x2, ogl, orw, mod[l], w_out_b, vec(ln1_g, l), vec(ln1_b, l), w_up_b, w_dn_b,
                         vec(ln2_g, l), vec(ln2_b, l), alpha, seq, tm, l)
    return x2.reshape(bsz, seq, d)
```

```python
import functools

import numpy as np
import jax
import jax.numpy as jnp
from jax import lax
from jax.experimental import pallas as pl
from jax.experimental.pallas import tpu as pltpu

F32 = jnp.float32
BF16 = jnp.bfloat16

HEAD_DIM = 64
H_HGRN, H_GLA, H_RWKV = 4, 6, 6
W_HGRN, W_GLA, W_RWKV = H_HGRN * HEAD_DIM, H_GLA * HEAD_DIM, H_RWKV * HEAD_DIM
GLA_GATE_RANK = 16
GLA_GATE_NORMALIZER = 16.0
RWKV_DECAY_RANK, RWKV_ICLR_RANK, RWKV_GATE_RANK = 64, 64, 128
RWKV_GN_EPS = 64e-5
N_HGRN_COLS = 4 * W_HGRN
N_GLA_COLS = 4 * W_GLA + GLA_GATE_RANK
N_RWKV_COLS = 3 * W_RWKV + RWKV_DECAY_RANK + RWKV_ICLR_RANK + RWKV_GATE_RANK
CHUNK = 64
LN_EPS = 1e-5
RMS_EPS = 1e-5
F_MIN = 1e-30

LANES = 128
PAIR = 2 * HEAD_DIM
SUB = 8
LEVEL_HALVES = (SUB, 2 * SUB, 4 * SUB)
assert 2 * LEVEL_HALVES[-1] == CHUNK
A_GROUP_GL = 4
LOG2E = 1.4426950408889634
VMEM_LIMIT = 56 * 1024 * 1024

W_GL = W_HGRN + W_GLA
N_GL_PAIRS = W_GL // PAIR
N_RW_PAIRS = W_RWKV // PAIR

NN = (((1,), (0,)), ((), ()))
NT = (((1,), (1,)), ((), ()))
TN = (((0,), (0,)), ((), ()))


def _mm(a, b, dims=NN):
    return lax.dot_general(a, b, dims, preferred_element_type=F32)


def _split2(x):
    hi = x.astype(BF16)
    lo = (x - hi.astype(F32)).astype(BF16)
    return hi, lo


def _bf16_parts(x, n):
    parts = []
    for i in range(n):
        p = x.astype(BF16)
        parts.append(p)
        if i + 1 < n:
            x = x - p.astype(F32)
    return parts


def _dot1(a, b, dims=NN):
    return _mm(a.astype(BF16), b.astype(BF16), dims)


def _dot3(a, b, dims=NN):
    ah, al = _split2(a)
    bh, bl = _split2(b)
    return _mm(ah, bh, dims) + (_mm(ah, bl, dims) + _mm(al, bh, dims))


def _dot_exact_lhs(a_bf16, b, n):
    return _mm(jnp.concatenate([a_bf16] * n, axis=1), jnp.concatenate(_bf16_parts(b, n), axis=0))


def _sigmoid(x):
    return 1.0 / (1.0 + jnp.exp(-x))


def _silu(x):
    return x * _sigmoid(x)


def _softplus(x):
    return jnp.maximum(x, 0.0) + jnp.log1p(jnp.exp(-jnp.abs(x)))


def _log_sigmoid(x):
    return -_softplus(-x)


def _layer_norm(y, g, b):
    mu = jnp.mean(y, axis=-1, keepdims=True)
    d = y - mu
    var = jnp.mean(d * d, axis=-1, keepdims=True)
    return d * lax.rsqrt(var + LN_EPS) * g + b


def _iota(shape, axis):
    return lax.broadcasted_iota(jnp.int32, shape, axis)


def _stack_heads(x):
    lane = _iota(x.shape, 1)
    return jnp.concatenate([jnp.where(lane < HEAD_DIM, x, 0.0), jnp.where(lane >= HEAD_DIM, x, 0.0)], axis=0)


def _pair_masks():
    t = _iota((CHUNK, PAIR), 0)
    s = _iota((CHUNK, PAIR), 1) % HEAD_DIM
    r2 = _iota((PAIR, PAIR), 0) // HEAD_DIM
    c2 = _iota((PAIR, PAIR), 1) // HEAD_DIM
    return dict(
        strict=s < t,
        incl=s <= t,
        eye=s == t,
        diag=(s // SUB == t // SUB) & (s <= t),
        level={hs: (s // (2 * hs) == t // (2 * hs)) & ((t // hs) % 2 == 1) & ((s // hs) % 2 == 0)
               for hs in LEVEL_HALVES},
        bd=r2 == c2,
    )


def _tril_bf16():
    return jnp.where(_iota((CHUNK, CHUNK), 1) <= _iota((CHUNK, CHUNK), 0), 1.0, 0.0).astype(BF16)


def _seg_mean_matrix(width):
    r = _iota((width, width), 0) // HEAD_DIM
    c = _iota((width, width), 1) // HEAD_DIM
    return jnp.where(r == c, 1.0, 0.0).astype(BF16)


def _head_sums(xs):
    tiles = [x[:, j:j + LANES].astype(BF16) for x in xs for j in range(0, x.shape[1], LANES)]
    sums = []
    for i in range(0, len(tiles) - 1, 2):
        s = _mm(jnp.concatenate(tiles[i:i + 2], axis=1), _seg_mean_matrix(2 * LANES))
        sums += [s[:, 0:LANES], s[:, LANES:2 * LANES]]
    if len(tiles) % 2:
        sums.append(_mm(tiles[-1], _seg_mean_matrix(LANES)))
    out, i = [], 0
    for x in xs:
        n = x.shape[1] // LANES
        out.append(jnp.concatenate(sums[i:i + n], axis=1))
        i += n
    return out


def _mod_kernel(c_ref, w_ref, b_ref, o_ref):
    c = _silu(c_ref[...])
    o_ref[0] = _dot3(c, w_ref[0]) + b_ref[0]


def _modulation(c, ada_w, ada_b):
    depth, d, n = ada_w.shape
    bsz = c.shape[0]
    tn = 1536
    return pl.pallas_call(
        _mod_kernel,
        grid=(depth, n // tn),
        in_specs=[
            pl.BlockSpec((bsz, d), lambda l, j: (0, 0)),
            pl.BlockSpec((1, d, tn), lambda l, j: (l, 0, j)),
            pl.BlockSpec((1, 1, tn), lambda l, j: (l, 0, j)),
        ],
        out_specs=pl.BlockSpec((1, bsz, tn), lambda l, j: (l, 0, j)),
        out_shape=jax.ShapeDtypeStruct((depth, bsz, n), F32),
        compiler_params=pltpu.CompilerParams(
            dimension_semantics=("arbitrary", "arbitrary"), vmem_limit_bytes=VMEM_LIMIT),
        name="adaln_modulation",
    )(c, ada_w, ada_b.reshape(depth, 1, n))


def _lower_bound_kernel(x_ref, o_ref):
    depth = x_ref.shape[0]
    rows = [x_ref[l:l + 1, :] for l in range(depth)]
    m = functools.reduce(jnp.maximum, rows)
    e = [jnp.exp(r - m) for r in rows]
    tot = functools.reduce(lambda a, b: a + b, e)
    p = [ei / tot for ei in e]
    acc = jnp.zeros_like(p[0])
    for l in range(depth):
        acc = acc + p[l]
        o_ref[l:l + 1, :] = acc - p[0]


def _lower_bounds(logits):
    return pl.pallas_call(
        _lower_bound_kernel,
        out_shape=jax.ShapeDtypeStruct(logits.shape, F32),
        name="hgrn_lower_bounds",
    )(logits.astype(F32))


def _inproj_kernel(steps_per_batch, x_ref, mod_ref, w_ref, wal_ref, wr_ref, lb_ref, au_ref, ab_ref, mu_ref,
                   w0_ref, a0_ref, wa_ref, kk_ref, ka_ref,
                   gq_ref, gk_ref, gv_ref, gb_ref, gg_ref,
                   rr_ref, rlw_ref, rk_ref, rv_ref, rkk_ref, rka_ref, rhg_ref, carry_s):
    tm = x_ref.shape[0]
    w3 = W_RWKV

    @pl.when(pl.program_id(0) % steps_per_batch == 0)
    def _():
        carry_s[...] = jnp.zeros_like(carry_s)

    m = mod_ref[0]
    h = (x_ref[...] * (1.0 + m[1:2, :]) + m[0:1, :]).astype(BF16)

    def proj(c0, width):
        return _mm(h, w_ref[:, c0:c0 + width])

    first_row = _iota((tm, LANES), 0) == 0

    def shifted(c0, width):
        z = _mm(h, wr_ref[:, c0:c0 + width])
        z_prev = pltpu.roll(z, 1, axis=0)
        z_prev = jnp.where(jnp.concatenate([first_row] * (width // LANES), axis=1), carry_s[:, c0:c0 + width], z_prev)
        carry_s[:, c0:c0 + width] = z[tm - 1:tm, :]
        return z + (z_prev - z) * mu_ref[:, c0:c0 + width]

    g0 = N_HGRN_COLS
    n_rp = w3 // PAIR
    tile = lambda j: slice(j * PAIR, (j + 1) * PAIR)
    z_v = shifted(2 * w3, w3 + LANES)
    z_rk = shifted(0, 2 * w3)

    h_wa = z_v[:, w3:w3 + LANES]
    h_wa = jnp.where(_iota(h_wa.shape, 1) < RWKV_DECAY_RANK, jnp.tanh(h_wa), h_wa)
    wa = _dot1(h_wa, wa_ref[...])
    rv_ref[...] = z_v[:, 0:w3]
    rr_ref[...] = z_rk[:, 0:w3]

    later = [lambda: proj(W_HGRN, W_HGRN), lambda: proj(0, W_HGRN),
             lambda: shifted(3 * w3 + LANES, RWKV_GATE_RANK)]
    issued, a_tiles = [], []
    for j in range(n_rp):
        issued.append(later[j]())
        w_log = -_softplus(-(w0_ref[:, tile(j)] + wa[:, tile(j)])) - 0.5
        rlw_ref[:, tile(j)] = -jnp.exp(w_log)
        a_tiles.append(_sigmoid(a0_ref[:, tile(j)] + wa[:, w3 + j * PAIR:w3 + (j + 1) * PAIR]))
    zf, zq, z_hg = issued

    z_al = _mm(h, wal_ref[...])
    k = z_rk[:, w3:2 * w3]
    kk = k * kk_ref[...]
    kk = kk / jnp.maximum(jnp.sqrt(_head_sums([kk * kk])[0]), 1e-12)
    rkk_ref[...] = kk

    z_qk = proj(g0, 2 * W_GLA)
    for j in range(n_rp):
        rka_ref[:, tile(j)] = kk[:, tile(j)] * a_tiles[j]
        rk_ref[:, tile(j)] = k[:, tile(j)] * (1.0 + (a_tiles[j] - 1.0) * ka_ref[:, tile(j)])
    lb = lb_ref[...]
    f = lb + (1.0 - lb) * _sigmoid(zf)
    gb_ref[:, 0:W_HGRN] = jnp.log(jnp.maximum(f, F_MIN))
    k_h = (1.0 - lb) * _sigmoid(-zf)
    q_h = _silu(zq) * HEAD_DIM ** -0.5
    for p in range(W_HGRN // PAIR):
        gk_ref[p] = k_h[:, tile(p)]
        gq_ref[p] = q_h[:, tile(p)]

    z_vr = proj(g0 + 2 * W_GLA, 2 * W_GLA)
    rhg_ref[...] = _sigmoid(z_hg).astype(BF16)
    logit = _dot1(z_al, au_ref[...]) + ab_ref[...]
    gb_ref[:, W_HGRN:W_GL] = _log_sigmoid(logit) / GLA_GATE_NORMALIZER
    for p in range(W_GLA // PAIR):
        gq_ref[W_HGRN // PAIR + p] = z_qk[:, tile(p)] * HEAD_DIM ** -0.5
        gk_ref[W_HGRN // PAIR + p] = z_qk[:, W_GLA + p * PAIR:W_GLA + (p + 1) * PAIR]

    zg = proj(3 * W_HGRN, W_HGRN)
    gv_ref[:, W_HGRN:W_GL] = z_vr[:, 0:W_GLA].astype(BF16)
    gg_ref[:, W_HGRN:W_GL] = _silu(z_vr[:, W_GLA:2 * W_GLA]).astype(BF16)
    zi = proj(2 * W_HGRN, W_HGRN)
    gg_ref[:, 0:W_HGRN] = _silu(zg).astype(BF16)
    gv_ref[:, 0:W_HGRN] = zi.astype(BF16)


def _in_projection(x2, mod_l, w_main, w_alpha, w_rwkv, lb_l, alpha_up_p, alpha_b, mu, w0, a0, wa_blk, k_k, k_a,
                   seq, tm, layer):
    m_rows, d = x2.shape
    resident = lambda n: pl.BlockSpec((None, d, n), lambda i: (layer, 0, 0), pipeline_mode=pl.Buffered(1))
    steps_per_batch = seq // tm
    row = lambda i: (i, 0)
    const = lambda i: (0, 0)
    vec = lambda n: pl.BlockSpec((1, n), const)
    out = lambda n, dt: (pl.BlockSpec((tm, n), row), jax.ShapeDtypeStruct((m_rows, n), dt))
    pair_slabs = (pl.BlockSpec((N_GL_PAIRS, tm, PAIR), lambda i: (0, i, 0)),
                  jax.ShapeDtypeStruct((N_GL_PAIRS, m_rows, PAIR), F32))
    outs = [pair_slabs, pair_slabs, out(W_GL, BF16), out(W_GL, F32), out(W_GL, BF16),
            out(W_RWKV, F32), out(W_RWKV, F32), out(W_RWKV, F32), out(W_RWKV, F32), out(W_RWKV, F32),
            out(W_RWKV, F32), out(RWKV_GATE_RANK, BF16)]
    return pl.pallas_call(
        functools.partial(_inproj_kernel, steps_per_batch),
        grid=(m_rows // tm,),
        in_specs=[
            pl.BlockSpec((tm, d), row),
            pl.BlockSpec((1, 6, d), lambda i: (i // steps_per_batch, 0, 0)),
            resident(N_HGRN_COLS + 4 * W_GLA), resident(LANES), resident(N_RWKV_COLS),
            vec(W_HGRN),
            pl.BlockSpec((LANES, W_GLA), const),
            vec(W_GLA),
            vec(N_RWKV_COLS), vec(W_RWKV), vec(W_RWKV),
            pl.BlockSpec((LANES, 2 * W_RWKV), const),
            vec(W_RWKV), vec(W_RWKV),
        ],
        out_specs=[o[0] for o in outs],
        out_shape=[o[1] for o in outs],
        scratch_shapes=[pltpu.VMEM((1, N_RWKV_COLS), F32)],
        compiler_params=pltpu.CompilerParams(
            dimension_semantics=("arbitrary",), vmem_limit_bytes=VMEM_LIMIT),
        name="in_projection",
    )(x2, mod_l, w_main, w_alpha, w_rwkv, lb_l, alpha_up_p, alpha_b, mu, w0, a0, wa_blk, k_k, k_a)


def _gl_kernel(q_s, k_s, v_s, lg_ref, gate_ref, ng_ref, ind_ref, out_ref,
               st_ref, b_s, ad_s, e_s, o_s, sc_s, eend_s):
    tb = q_s.shape[1]
    n_chunks = tb // CHUNK
    n_groups = tb // SUB

    @pl.when(pl.program_id(1) == 0)
    def _():
        st_ref[...] = jnp.zeros_like(st_ref)

    tril = _tril_bf16()
    for c in range(n_chunks):
        rows = slice(c * CHUNK, (c + 1) * CHUNK)
        b = _dot_exact_lhs(tril, lg_ref[rows, :], 2)
        for p in range(N_GL_PAIRS):
            b_s[p, rows, :] = b[:, p * PAIR:(p + 1) * PAIR]

    for t in range(SUB - 1):
        e_s[t * n_groups:(t + 1) * n_groups, (t + 1) * PAIR:] = jnp.zeros((n_groups, (SUB - 1 - t) * PAIR), BF16)
    for p in range(N_GL_PAIRS):
        by_pos = lambda ref, t: ref[p, pl.ds(t, n_groups, stride=SUB), :]
        qs = [by_pos(q_s, t) for t in range(SUB)]
        ks = [by_pos(k_s, t) for t in range(SUB)]
        bs = [by_pos(b_s, t) * LOG2E for t in range(SUB)]
        for t in range(SUB):
            for sg in range(t + 1):
                e = qs[t] * ks[sg]
                if sg < t:
                    e = e * jnp.exp2(jnp.minimum(bs[t] - bs[sg], 0.0))
                e_s[t * n_groups:(t + 1) * n_groups, sg * PAIR:(sg + 1) * PAIR] = e.astype(BF16)
        ad = [jnp.zeros((n_groups, PAIR), F32)] * SUB
        for sg0 in range(0, SUB, 2):
            part = _mm(e_s[sg0 * n_groups:, sg0 * PAIR:(sg0 + 2) * PAIR], ind_ref[sg0 * PAIR:(sg0 + 2) * PAIR, :])
            for t in range(sg0, SUB):
                ad[t] = ad[t] + part[(t - sg0) * n_groups:(t - sg0 + 1) * n_groups, :]
        for t in range(SUB):
            ad_s[p, pl.ds(t, n_groups, stride=SUB), :] = ad[t]

    masks = _pair_masks()
    t_idx = _iota((CHUNK, PAIR), 0)
    second_half = {hs: (t_idx // hs) % 2 == 1 for hs in LEVEL_HALVES}
    level_sign = {hs: jnp.where(second_half[hs], LOG2E, -LOG2E) for hs in LEVEL_HALVES}

    def group_units(i):
        units = []
        for cc in range(A_GROUP_GL):
            c = i * A_GROUP_GL + cc
            rows = pl.ds(pl.multiple_of(c * CHUNK, CHUNK), CHUNK)
            for p in range(N_GL_PAIRS):
                units.append(dict(p=p, rows=rows, lanes=slice(p * PAIR, (p + 1) * PAIR),
                                  e_rows=pl.ds(pl.multiple_of(c * 8, 8), 8),
                                  st_rows=pl.ds(pl.multiple_of((c * N_GL_PAIRS + p) * PAIR, PAIR), PAIR)))
        return units

    def increments(i, carry):
        units = group_units(i)
        for un in units:
            b = b_s[un["p"], un["rows"], :]
            b_end = b[CHUNK - 1:CHUNK, :]
            un["k_end"] = k_s[un["p"], un["rows"], :] * jnp.exp(b_end - b)
            eend_s[un["e_rows"], un["lanes"]] = jnp.broadcast_to(jnp.exp(b_end), (8, PAIR))
        for un in units:
            upd = _dot1(v_s[un["rows"], un["lanes"]], un["k_end"], TN)
            sc_s[un["st_rows"], :] = jnp.where(masks["bd"], upd, 0.0)
        return carry

    lax.fori_loop(0, n_chunks // A_GROUP_GL, increments, 0)

    sts = [st_ref[p] for p in range(N_GL_PAIRS)]
    for c in range(n_chunks):
        for p in range(N_GL_PAIRS):
            st_rows = slice((c * N_GL_PAIRS + p) * PAIR, (c * N_GL_PAIRS + p + 1) * PAIR)
            upd = sc_s[st_rows, :]
            sc_s[st_rows, :] = sts[p]
            sts[p] = sts[p] * eend_s[c * 8:c * 8 + 1, p * PAIR:(p + 1) * PAIR] + upd
    for p in range(N_GL_PAIRS):
        st_ref[p] = sts[p]

    def outputs(i, carry):
        units = group_units(i)
        for un in units:
            q = q_s[un["p"], un["rows"], :]
            k = k_s[un["p"], un["rows"], :]
            b = b_s[un["p"], un["rows"], :]
            un["lv"] = []
            for hs in LEVEL_HALVES:
                e_mid = jnp.concatenate(
                    [jnp.broadcast_to(b[m:m + 1, :], (2 * hs, PAIR)) for m in range(hs - 1, CHUNK, 2 * hs)], axis=0)
                dec = jnp.exp2(jnp.minimum((b - e_mid) * level_sign[hs], 0.0))
                un["lv"].append(jnp.where(second_half[hs], q, k) * dec)
            un["q_in"] = q * jnp.exp(b)
        for un in units:
            a = jnp.zeros((CHUNK, PAIR), F32)
            for hs, lv in zip(LEVEL_HALVES, un["lv"]):
                a = jnp.where(masks["level"][hs], _dot1(lv, jnp.transpose(_stack_heads(lv))), a)
            un["a"] = jnp.where(masks["diag"], ad_s[un["p"], un["rows"], :], a)
        for un in units:
            o_s[un["rows"], un["lanes"]] = (_dot1(un["a"], _stack_heads(v_s[un["rows"], un["lanes"]]))
                                            + _dot1(un["q_in"], sc_s[un["st_rows"], :], NT))
        return carry

    lax.fori_loop(0, n_chunks // A_GROUP_GL, outputs, 0)

    o = o_s[...]
    ms = _head_sums([o * o])[0] * (1.0 / HEAD_DIM)
    out_ref[...] = (o * lax.rsqrt(ms + RMS_EPS) * ng_ref[...] * gate_ref[...]).astype(out_ref.dtype)


def _gl_indicator():
    r = np.arange(SUB * PAIR)
    c = np.arange(PAIR)
    sg = r // PAIR
    h = (r % PAIR) // HEAD_DIM
    ind = (h[:, None] == (c // HEAD_DIM)[None, :]) & (sg[:, None] == (c % SUB)[None, :])
    return jnp.asarray(ind, dtype=BF16)


def _gl_mixer(q, k, v, lg, gate, norm_g, bsz, seq, tb):
    steps = seq // tb
    row = lambda b, i: (b * steps + i, 0)
    const = lambda b, i: (0, 0)
    slab = pl.BlockSpec((tb, W_GL), row)
    pair_slabs = pl.BlockSpec((N_GL_PAIRS, tb, PAIR), lambda b, i: (0, b * steps + i, 0))
    scratch_f32 = pltpu.VMEM((tb, W_GL), F32)
    scratch_pairs = pltpu.VMEM((N_GL_PAIRS, tb, PAIR), F32)
    return pl.pallas_call(
        _gl_kernel,
        grid=(bsz, steps),
        in_specs=[
            pair_slabs, pair_slabs, slab, slab, slab,
            pl.BlockSpec((1, W_GL), const),
            pl.BlockSpec((SUB * PAIR, PAIR), const),
        ],
        out_specs=pl.BlockSpec((tb, W_GL), row),
        out_shape=jax.ShapeDtypeStruct((bsz * seq, W_GL), BF16),
        scratch_shapes=[
            pltpu.VMEM((N_GL_PAIRS, PAIR, PAIR), F32),
            scratch_pairs, scratch_pairs,
            pltpu.VMEM((tb, SUB * PAIR), BF16),
            scratch_f32,
            pltpu.VMEM((tb // CHUNK * N_GL_PAIRS * PAIR, PAIR), F32),
            pltpu.VMEM((tb // CHUNK * 8, W_GL), F32),
        ],
        compiler_params=pltpu.CompilerParams(
            dimension_semantics=("arbitrary", "arbitrary"), vmem_limit_bytes=VMEM_LIMIT),
        name="hgrn_gla_mixer",
    )(q, k, v, lg, gate, norm_g, _gl_indicator())


def _rwkv_kernel(r_s, lw_s, k_s, v_s, kk_s, ka_s, hg_s, gup_ref, rk_ref, gng_ref, gnb_ref,
                 out_ref, st_ref, y_s, rt_s, bh_s, kh_s, nb_s, nk_s, w_s, u0_s, eend_s):
    bsz, tb = r_s.shape[0], r_s.shape[1]
    n_chunks = tb // CHUNK

    @pl.when(pl.program_id(0) == 0)
    def _():
        st_ref[...] = jnp.zeros_like(st_ref)

    masks = _pair_masks()
    tril = _tril_bf16()
    eye = jnp.where(masks["eye"], 1.0, 0.0)

    def phase_a(b, carry):
        units = []
        for c in range(n_chunks):
            rows = slice(c * CHUNK, (c + 1) * CHUNK)
            g_all = _dot_exact_lhs(tril, lw_s[b, rows, :], 2)
            for p in range(N_RW_PAIRS):
                lanes = slice(p * PAIR, (p + 1) * PAIR)
                units.append(dict(c=c, rows=rows, lanes=lanes, g=g_all[:, lanes]))
        for un in units:
            rows, lanes, g = un["rows"], un["lanes"], un["g"]
            kc = k_s[b, rows, lanes]
            kkc = kk_s[b, rows, lanes]
            ka = ka_s[b, rows, lanes]
            g_end = g[CHUNK - 1:CHUNK, :]
            e_neg = jnp.exp(-g)
            e_end = jnp.exp(g_end - g)
            un["at"] = -kkc * jnp.exp(g - lw_s[b, rows, lanes])
            rt = r_s[b, rows, lanes] * jnp.exp(g)
            un["rt"] = rt
            un["bt"] = ka * e_neg
            un["kt"] = kc * e_neg
            rt_s[b, rows, lanes] = rt
            bh_s[b, rows, lanes] = ka * e_end
            kh_s[b, rows, lanes] = kc * e_end
            eend_s[b, un["c"] * 8:un["c"] * 8 + 8, lanes] = jnp.broadcast_to(jnp.exp(g_end), (8, PAIR))
        for un in units:
            gram = _dot1(jnp.concatenate([un["at"], un["rt"]], axis=0),
                         jnp.concatenate([_stack_heads(un["bt"]), _stack_heads(un["kt"])], axis=0), NT)
            un["l"] = jnp.where(masks["strict"], gram[0:CHUNK, 0:PAIR], 0.0)
            un["mk"] = jnp.where(masks["strict"], gram[0:CHUNK, PAIR:2 * PAIR], 0.0)
            nb_s[b, un["rows"], un["lanes"]] = jnp.where(masks["incl"], gram[CHUNK:2 * CHUNK, 0:PAIR], 0.0)
            nk_s[b, un["rows"], un["lanes"]] = jnp.where(masks["incl"], gram[CHUNK:2 * CHUNK, PAIR:2 * PAIR], 0.0)
        for un in units:
            un["t"] = eye + un["l"]
            un["lp"] = _dot1(un["l"], _stack_heads(un["l"]))
        for _ in range(4):
            for un in units:
                both = _dot1(jnp.concatenate([un["lp"], un["t"]], axis=0), _stack_heads(un["lp"]))
                un["lp"] = both[0:CHUNK, :]
                un["t"] = un["t"] + both[CHUNK:2 * CHUNK, :]
        for un in units:
            un["t"] = un["t"] + _dot1(un["t"], _stack_heads(un["lp"]))
        for un in units:
            un["mv"] = _dot1(un["mk"], _stack_heads(v_s[b, un["rows"], un["lanes"]]))
        for un in units:
            wu = _dot1(un["t"], jnp.concatenate([_stack_heads(un["at"]), _stack_heads(un["mv"])], axis=1))
            w_s[b, un["rows"], un["lanes"]] = wu[:, 0:PAIR]
            u0_s[b, un["rows"], un["lanes"]] = wu[:, PAIR:2 * PAIR]
        return carry

    lax.fori_loop(0, bsz, phase_a, 0)

    def phase_b(c, carry):
        rows = pl.ds(pl.multiple_of(c * CHUNK, CHUNK), CHUNK)
        e_rows = pl.ds(pl.multiple_of(c * 8, 8), 8)
        seqs = [(b, p, slice(p * PAIR, (p + 1) * PAIR)) for b in range(bsz) for p in range(N_RW_PAIRS)]
        sts = [st_ref[b * N_RW_PAIRS + p] for b, p, _ in seqs]
        vs = [v_s[b, rows, lanes] for b, _, lanes in seqs]
        wrs = [_dot1(jnp.concatenate([w_s[b, rows, lanes], rt_s[b, rows, lanes]], axis=0), st, NT)
               for (b, _, lanes), st in zip(seqs, sts)]
        us = [wr[0:CHUNK, :] + u0_s[b, rows, lanes] for (b, _, lanes), wr in zip(seqs, wrs)]
        upds = [_dot1(jnp.concatenate([u, vc], axis=0),
                      jnp.concatenate([bh_s[b, rows, lanes], kh_s[b, rows, lanes]], axis=0), TN)
                for (b, _, lanes), u, vc in zip(seqs, us, vs)]
        for (b, p, lanes), st, upd in zip(seqs, sts, upds):
            e_end = eend_s[b, e_rows, lanes][0:1, :]
            st_ref[b * N_RW_PAIRS + p] = st * e_end + jnp.where(masks["bd"], upd, 0.0)
        for (b, _, lanes), wr, u, vc in zip(seqs, wrs, us, vs):
            y_s[b, rows, lanes] = (
                wr[CHUNK:2 * CHUNK, :]
                + _dot1(jnp.concatenate([nb_s[b, rows, lanes], nk_s[b, rows, lanes]], axis=1),
                        jnp.concatenate([_stack_heads(u), _stack_heads(vc)], axis=0)))
        return carry

    lax.fori_loop(0, n_chunks, phase_b, 0)

    for b in range(bsz):
        y = y_s[b]
        s_hi, s_lo = _head_sums(_bf16_parts(y, 2))
        d = y - (s_hi + s_lo) * (1.0 / HEAD_DIM)
        s_var, s_bonus = _head_sums([d * d, r_s[b] * k_s[b] * rk_ref[...]])
        yn = d * lax.rsqrt(s_var * (1.0 / HEAD_DIM) + RWKV_GN_EPS) * gng_ref[...] + gnb_ref[...]
        gate = _dot1(hg_s[b], gup_ref[...])
        out_ref[b] = ((yn + s_bonus * v_s[b]) * gate).astype(out_ref.dtype)


def _rwkv_mixer(r, lw, k, v, kk, ka, hg, g_up, r_k, gn_g, gn_b, bsz, seq, tb):
    blk = lambda i: (0, i, 0)
    const = lambda i: (0, 0)
    vec = pl.BlockSpec((1, W_RWKV), const)
    tok = pl.BlockSpec((bsz, tb, W_RWKV), blk)
    slab = pltpu.VMEM((bsz, tb, W_RWKV), F32)
    seq_major = lambda a: a.reshape(bsz, seq, a.shape[-1])
    out = pl.pallas_call(
        _rwkv_kernel,
        grid=(seq // tb,),
        in_specs=[
            tok, tok, tok, tok, tok, tok,
            pl.BlockSpec((bsz, tb, RWKV_GATE_RANK), blk),
            pl.BlockSpec((RWKV_GATE_RANK, W_RWKV), const),
            vec, vec, vec,
        ],
        out_specs=pl.BlockSpec((bsz, tb, W_RWKV), blk),
        out_shape=jax.ShapeDtypeStruct((bsz, seq, W_RWKV), BF16),
        scratch_shapes=[
            pltpu.VMEM((bsz * N_RW_PAIRS, PAIR, PAIR), F32),
            slab,
            slab, slab, slab, slab, slab, slab, slab,
            pltpu.VMEM((bsz, tb // CHUNK * 8, W_RWKV), F32),
        ],
        compiler_params=pltpu.CompilerParams(
            dimension_semantics=("arbitrary",), vmem_limit_bytes=VMEM_LIMIT),
        name="rwkv7_mixer",
    )(*[seq_major(a) for a in (r, lw, k, v, kk, ka, hg)], g_up, r_k, gn_g, gn_b)
    return out.reshape(bsz * seq, W_RWKV)


def _post_kernel(alpha, ff_chunk, x_ref, ogl_ref, orw_ref, mod_ref, wout_ref, ln1g_ref, ln1b_ref,
                 wup_ref, wdn_ref, ln2g_ref, ln2b_ref, out_ref):
    m = mod_ref[0]
    gate1, shift2, scale2, gate2 = m[2:3, :], m[3:4, :], m[4:5, :], m[5:6, :]
    o = _mm(ogl_ref[...], wout_ref[0:W_GL, :]) + _mm(orw_ref[...], wout_ref[W_GL:W_GL + W_RWKV, :])
    x1 = _layer_norm(alpha * x_ref[...] + (1.0 + gate1) * o, ln1g_ref[...], ln1b_ref[...])
    h = (x1 * (1.0 + scale2) + shift2).astype(BF16)
    d_ff = wup_ref.shape[1]
    acc = jnp.zeros(x1.shape, F32)
    for j in range(d_ff // ff_chunk):
        cols = slice(j * ff_chunk, (j + 1) * ff_chunk)
        u = jnp.maximum(_mm(h, wup_ref[:, cols]), 0.0)
        acc = acc + _mm((u * u).astype(BF16), wdn_ref[cols, :])
    out_ref[...] = _layer_norm(alpha * x1 + (1.0 + gate2) * acc, ln2g_ref[...], ln2b_ref[...])


def _post_mixer(x2, ogl, orw, mod_l, w_out, ln1_g, ln1_b, w_up, w_dn, ln2_g, ln2_b, alpha, seq, tm, layer):
    m_rows, d = x2.shape
    d_ff = w_up.shape[2]
    steps_per_batch = seq // tm
    row = lambda i: (i, 0)
    const = lambda i: (0, 0)
    vec = pl.BlockSpec((1, d), const)
    resident = lambda shape: pl.BlockSpec((None,) + shape, lambda i: (layer, 0, 0), pipeline_mode=pl.Buffered(1))
    return pl.pallas_call(
        functools.partial(_post_kernel, alpha, 512),
        grid=(m_rows // tm,),
        in_specs=[
            pl.BlockSpec((tm, d), row),
            pl.BlockSpec((tm, W_GL), row),
            pl.BlockSpec((tm, W_RWKV), row),
            pl.BlockSpec((1, 6, d), lambda i: (i // steps_per_batch, 0, 0)),
            resident((W_GL + W_RWKV, d)),
            vec, vec,
            resident((d, d_ff)),
            resident((d_ff, d)),
            vec, vec,
        ],
        out_specs=pl.BlockSpec((tm, d), row),
        out_shape=jax.ShapeDtypeStruct((m_rows, d), F32),
        compiler_params=pltpu.CompilerParams(
            dimension_semantics=("arbitrary",), vmem_limit_bytes=VMEM_LIMIT),
        name="outproj_mlp",
    )(x2, ogl, orw, mod_l, w_out, ln1_g, ln1_b, w_up, w_dn, ln2_g, ln2_b)


def kernel(x, c, hgrn_lb_logits, ada_w, ada_b, w_in, hgrn_norm_g, gla_alpha_up, gla_alpha_b, gla_norm_g,
           rwkv_mu, rwkv_w0, rwkv_w_up, rwkv_a0, rwkv_a_up, rwkv_g_up, rwkv_k_k, rwkv_k_a, rwkv_r_k,
           rwkv_gn_g, rwkv_gn_b, w_out, ln1_g, ln1_b, mlp_w_up, mlp_w_down, ln2_g, ln2_b):
    bsz, seq, d = x.shape
    depth = w_in.shape[0]
    alpha = (2.0 * depth) ** 0.25
    tm = min(512, seq)
    tb = min(512, seq)
    assert seq % tm == 0 and seq % tb == 0 and tb % CHUNK == 0
    assert w_in.shape[2] == N_HGRN_COLS + N_GLA_COLS + N_RWKV_COLS

    mod = _modulation(c, ada_w, ada_b).reshape(depth, bsz, 6, d)
    lbs = _lower_bounds(hgrn_lb_logits)

    gla_end = N_HGRN_COLS + N_GLA_COLS
    w_in_b = w_in.astype(BF16)
    w_alpha = jnp.pad(w_in_b[:, :, gla_end - GLA_GATE_RANK:gla_end], ((0, 0), (0, 0), (0, LANES - GLA_GATE_RANK)))
    w_rwkv = w_in_b[:, :, gla_end:]
    alpha_up_p = jnp.concatenate(
        [gla_alpha_up, jnp.zeros((depth, LANES - GLA_GATE_RANK, W_GLA), gla_alpha_up.dtype)], axis=1)
    zeros_r = jnp.zeros((depth, RWKV_DECAY_RANK, W_RWKV), rwkv_w_up.dtype)
    wa_blk = jnp.concatenate(
        [jnp.concatenate([rwkv_w_up, zeros_r], axis=2), jnp.concatenate([zeros_r, rwkv_a_up], axis=2)], axis=1)
    norm_g = jnp.concatenate([hgrn_norm_g, gla_norm_g], axis=1)
    w_out_b = w_out.astype(BF16)
    w_up_b = mlp_w_up.astype(BF16)
    w_dn_b = mlp_w_down.astype(BF16)
    vec = lambda a, l: a[l].reshape(1, -1)

    x2 = x.reshape(bsz * seq, d)
    for l in range(depth):
        gq, gk, gv, glg, ggate, rr, rlw, rk, rv, rkk, rka, rhg = _in_projection(
            x2, mod[l], w_in_b, w_alpha, w_rwkv, vec(lbs, l), alpha_up_p[l], vec(gla_alpha_b, l), vec(rwkv_mu, l),
            vec(rwkv_w0, l), vec(rwkv_a0, l), wa_blk[l], vec(rwkv_k_k, l), vec(rwkv_k_a, l), seq, tm, l)
        ogl = _gl_mixer(gq, gk, gv, glg, ggate, vec(norm_g, l), bsz, seq, min(1024, seq))
        orw = _rwkv_mixer(rr, rlw, rk, rv, rkk, rka, rhg, rwkv_g_up[l], vec(rwkv_r_k, l), vec(rwkv_gn_g, l),
                          vec(rwkv_gn_b, l), bsz, seq, tb)
        x2 = _post_mixer(x2, ogl, orw, mod[l], w_out_b, vec(ln1_g, l), vec(ln1_b, l), w_up_b, w_dn_b,
                         vec(ln2_g, l), vec(ln2_b, l), alpha, seq, tm, l)
    return x2.reshape(bsz, seq, d)
```

```python
import functools

import numpy as np
import jax
import jax.numpy as jnp
from jax import lax
from jax.experimental import pallas as pl
from jax.experimental.pallas import tpu as pltpu

F32 = jnp.float32
BF16 = jnp.bfloat16

HEAD_DIM = 64
H_HGRN, H_GLA, H_RWKV = 4, 6, 6
W_HGRN, W_GLA, W_RWKV = H_HGRN * HEAD_DIM, H_GLA * HEAD_DIM, H_RWKV * HEAD_DIM
GLA_GATE_RANK = 16
GLA_GATE_NORMALIZER = 16.0
RWKV_DECAY_RANK, RWKV_ICLR_RANK, RWKV_GATE_RANK = 64, 64, 128
RWKV_GN_EPS = 64e-5
N_HGRN_COLS = 4 * W_HGRN
N_GLA_COLS = 4 * W_GLA + GLA_GATE_RANK
N_RWKV_COLS = 3 * W_RWKV + RWKV_DECAY_RANK + RWKV_ICLR_RANK + RWKV_GATE_RANK
CHUNK = 64
LN_EPS = 1e-5
RMS_EPS = 1e-5
F_MIN = 1e-30

LANES = 128
PAIR = 2 * HEAD_DIM
SUB = 8
LEVEL_HALVES = (SUB, 2 * SUB, 4 * SUB)
assert 2 * LEVEL_HALVES[-1] == CHUNK
A_GROUP_GL = 4
OUT_SKEW = 4
LOG2E = 1.4426950408889634
VMEM_LIMIT = 56 * 1024 * 1024

W_GL = W_HGRN + W_GLA
N_GL_PAIRS = W_GL // PAIR
N_RW_PAIRS = W_RWKV // PAIR

NN = (((1,), (0,)), ((), ()))
NT = (((1,), (1,)), ((), ()))
TN = (((0,), (0,)), ((), ()))


def _mm(a, b, dims=NN):
    return lax.dot_general(a, b, dims, preferred_element_type=F32)


def _split2(x):
    hi = x.astype(BF16)
    lo = (x - hi.astype(F32)).astype(BF16)
    return hi, lo


def _bf16_parts(x, n):
    parts = []
    for i in range(n):
        p = x.astype(BF16)
        parts.append(p)
        if i + 1 < n:
            x = x - p.astype(F32)
    return parts


def _dot1(a, b, dims=NN):
    return _mm(a.astype(BF16), b.astype(BF16), dims)


def _dot3(a, b, dims=NN):
    ah, al = _split2(a)
    bh, bl = _split2(b)
    return _mm(ah, bh, dims) + (_mm(ah, bl, dims) + _mm(al, bh, dims))


def _dot_exact_lhs(a_bf16, b, n):
    return _mm(jnp.concatenate([a_bf16] * n, axis=1), jnp.concatenate(_bf16_parts(b, n), axis=0))


def _sigmoid(x):
    return 1.0 / (1.0 + jnp.exp(-x))


def _silu(x):
    return x * _sigmoid(x)


def _softplus(x):
    return jnp.maximum(x, 0.0) + jnp.log1p(jnp.exp(-jnp.abs(x)))


def _log_sigmoid(x):
    return -_softplus(-x)


def _layer_norm(y, g, b):
    mu = jnp.mean(y, axis=-1, keepdims=True)
    d = y - mu
    var = jnp.mean(d * d, axis=-1, keepdims=True)
    return d * lax.rsqrt(var + LN_EPS) * g + b


def _iota(shape, axis):
    return lax.broadcasted_iota(jnp.int32, shape, axis)


def _stack_heads(x):
    lane = _iota(x.shape, 1)
    return jnp.concatenate([jnp.where(lane < HEAD_DIM, x, 0.0), jnp.where(lane >= HEAD_DIM, x, 0.0)], axis=0)


def _pair_masks():
    t = _iota((CHUNK, PAIR), 0)
    s = _iota((CHUNK, PAIR), 1) % HEAD_DIM
    r2 = _iota((PAIR, PAIR), 0) // HEAD_DIM
    c2 = _iota((PAIR, PAIR), 1) // HEAD_DIM
    return dict(
        strict=s < t,
        incl=s <= t,
        eye=s == t,
        diag=(s // SUB == t // SUB) & (s <= t),
        level={hs: (s // (2 * hs) == t // (2 * hs)) & ((t // hs) % 2 == 1) & ((s // hs) % 2 == 0)
               for hs in LEVEL_HALVES},
        bd=r2 == c2,
    )


def _tril_bf16():
    return jnp.where(_iota((CHUNK, CHUNK), 1) <= _iota((CHUNK, CHUNK), 0), 1.0, 0.0).astype(BF16)


def _seg_mean_matrix(width):
    r = _iota((width, width), 0) // HEAD_DIM
    c = _iota((width, width), 1) // HEAD_DIM
    return jnp.where(r == c, 1.0, 0.0).astype(BF16)


def _head_sums(xs):
    tiles = [x[:, j:j + LANES].astype(BF16) for x in xs for j in range(0, x.shape[1], LANES)]
    sums = []
    for i in range(0, len(tiles) - 1, 2):
        s = _mm(jnp.concatenate(tiles[i:i + 2], axis=1), _seg_mean_matrix(2 * LANES))
        sums += [s[:, 0:LANES], s[:, LANES:2 * LANES]]
    if len(tiles) % 2:
        sums.append(_mm(tiles[-1], _seg_mean_matrix(LANES)))
    out, i = [], 0
    for x in xs:
        n = x.shape[1] // LANES
        out.append(jnp.concatenate(sums[i:i + n], axis=1))
        i += n
    return out


def _mod_kernel(c_ref, w_ref, b_ref, o_ref):
    c = _silu(c_ref[...])
    o_ref[0] = _dot3(c, w_ref[0]) + b_ref[0]


def _modulation(c, ada_w, ada_b):
    depth, d, n = ada_w.shape
    bsz = c.shape[0]
    tn = 1536
    return pl.pallas_call(
        _mod_kernel,
        grid=(depth, n // tn),
        in_specs=[
            pl.BlockSpec((bsz, d), lambda l, j: (0, 0)),
            pl.BlockSpec((1, d, tn), lambda l, j: (l, 0, j)),
            pl.BlockSpec((1, 1, tn), lambda l, j: (l, 0, j)),
        ],
        out_specs=pl.BlockSpec((1, bsz, tn), lambda l, j: (l, 0, j)),
        out_shape=jax.ShapeDtypeStruct((depth, bsz, n), F32),
        compiler_params=pltpu.CompilerParams(
            dimension_semantics=("arbitrary", "arbitrary"), vmem_limit_bytes=VMEM_LIMIT),
        name="adaln_modulation",
    )(c, ada_w, ada_b.reshape(depth, 1, n))


def _lower_bound_kernel(x_ref, o_ref):
    depth = x_ref.shape[0]
    rows = [x_ref[l:l + 1, :] for l in range(depth)]
    m = functools.reduce(jnp.maximum, rows)
    e = [jnp.exp(r - m) for r in rows]
    tot = functools.reduce(lambda a, b: a + b, e)
    p = [ei / tot for ei in e]
    acc = jnp.zeros_like(p[0])
    for l in range(depth):
        acc = acc + p[l]
        o_ref[l:l + 1, :] = acc - p[0]


def _lower_bounds(logits):
    return pl.pallas_call(
        _lower_bound_kernel,
        out_shape=jax.ShapeDtypeStruct(logits.shape, F32),
        name="hgrn_lower_bounds",
    )(logits.astype(F32))


def _inproj_kernel(steps_per_batch, x_ref, mod_ref, w_ref, wal_ref, wr_ref, lb_ref, au_ref, ab_ref, mu_ref,
                   w0_ref, a0_ref, wa_ref, kk_ref, ka_ref,
                   gq_ref, gk_ref, gv_ref, gb_ref, gg_ref,
                   rr_ref, rlw_ref, rk_ref, rv_ref, rkk_ref, rka_ref, rhg_ref, carry_s):
    tm = x_ref.shape[0]
    w3 = W_RWKV

    @pl.when(pl.program_id(0) % steps_per_batch == 0)
    def _():
        carry_s[...] = jnp.zeros_like(carry_s)

    m = mod_ref[0]
    h = (x_ref[...] * (1.0 + m[1:2, :]) + m[0:1, :]).astype(BF16)

    def proj(c0, width):
        return _mm(h, w_ref[:, c0:c0 + width])

    first_row = _iota((tm, LANES), 0) == 0

    def shifted(c0, width):
        z = _mm(h, wr_ref[:, c0:c0 + width])
        z_prev = pltpu.roll(z, 1, axis=0)
        z_prev = jnp.where(jnp.concatenate([first_row] * (width // LANES), axis=1), carry_s[:, c0:c0 + width], z_prev)
        carry_s[:, c0:c0 + width] = z[tm - 1:tm, :]
        return z + (z_prev - z) * mu_ref[:, c0:c0 + width]

    g0 = N_HGRN_COLS
    n_rp = w3 // PAIR
    tile = lambda j: slice(j * PAIR, (j + 1) * PAIR)
    z_v = shifted(2 * w3, w3 + LANES)
    z_rk = shifted(0, 2 * w3)

    h_wa = z_v[:, w3:w3 + LANES]
    h_wa = jnp.where(_iota(h_wa.shape, 1) < RWKV_DECAY_RANK, jnp.tanh(h_wa), h_wa)
    wa = _dot1(h_wa, wa_ref[...])
    rv_ref[...] = z_v[:, 0:w3]
    rr_ref[...] = z_rk[:, 0:w3]

    later = [lambda: proj(W_HGRN, W_HGRN), lambda: proj(0, W_HGRN),
             lambda: shifted(3 * w3 + LANES, RWKV_GATE_RANK)]
    issued, a_tiles = [], []
    for j in range(n_rp):
        issued.append(later[j]())
        w_log = -_softplus(-(w0_ref[:, tile(j)] + wa[:, tile(j)])) - 0.5
        rlw_ref[:, tile(j)] = -jnp.exp(w_log)
        a_tiles.append(_sigmoid(a0_ref[:, tile(j)] + wa[:, w3 + j * PAIR:w3 + (j + 1) * PAIR]))
    zf, zq, z_hg = issued

    z_al = _mm(h, wal_ref[...])
    k = z_rk[:, w3:2 * w3]
    kk = k * kk_ref[...]
    kk = kk / jnp.maximum(jnp.sqrt(_head_sums([kk * kk])[0]), 1e-12)
    rkk_ref[...] = kk

    z_qk = proj(g0, 2 * W_GLA)
    for j in range(n_rp):
        rka_ref[:, tile(j)] = kk[:, tile(j)] * a_tiles[j]
        rk_ref[:, tile(j)] = k[:, tile(j)] * (1.0 + (a_tiles[j] - 1.0) * ka_ref[:, tile(j)])
    lb = lb_ref[...]
    f = lb + (1.0 - lb) * _sigmoid(zf)
    gb_ref[:, 0:W_HGRN] = jnp.log(jnp.maximum(f, F_MIN))
    k_h = (1.0 - lb) * _sigmoid(-zf)
    q_h = _silu(zq) * HEAD_DIM ** -0.5
    for p in range(W_HGRN // PAIR):
        gk_ref[p] = k_h[:, tile(p)]
        gq_ref[p] = q_h[:, tile(p)]

    z_vr = proj(g0 + 2 * W_GLA, 2 * W_GLA)
    rhg_ref[...] = _sigmoid(z_hg).astype(BF16)
    logit = _dot1(z_al, au_ref[...]) + ab_ref[...]
    gb_ref[:, W_HGRN:W_GL] = _log_sigmoid(logit) / GLA_GATE_NORMALIZER
    for p in range(W_GLA // PAIR):
        gq_ref[W_HGRN // PAIR + p] = z_qk[:, tile(p)] * HEAD_DIM ** -0.5
        gk_ref[W_HGRN // PAIR + p] = z_qk[:, W_GLA + p * PAIR:W_GLA + (p + 1) * PAIR]

    zg = proj(3 * W_HGRN, W_HGRN)
    gv_ref[:, W_HGRN:W_GL] = z_vr[:, 0:W_GLA].astype(BF16)
    gg_ref[:, W_HGRN:W_GL] = _silu(z_vr[:, W_GLA:2 * W_GLA]).astype(BF16)
    zi = proj(2 * W_HGRN, W_HGRN)
    gg_ref[:, 0:W_HGRN] = _silu(zg).astype(BF16)
    gv_ref[:, 0:W_HGRN] = zi.astype(BF16)


def _in_projection(x2, mod_l, w_main, w_alpha, w_rwkv, lb_l, alpha_up_p, alpha_b, mu, w0, a0, wa_blk, k_k, k_a,
                   seq, tm, layer):
    m_rows, d = x2.shape
    resident = lambda n: pl.BlockSpec((None, d, n), lambda i: (layer, 0, 0), pipeline_mode=pl.Buffered(1))
    steps_per_batch = seq // tm
    row = lambda i: (i, 0)
    const = lambda i: (0, 0)
    vec = lambda n: pl.BlockSpec((1, n), const)
    out = lambda n, dt: (pl.BlockSpec((tm, n), row), jax.ShapeDtypeStruct((m_rows, n), dt))
    pair_slabs = (pl.BlockSpec((N_GL_PAIRS, tm, PAIR), lambda i: (0, i, 0)),
                  jax.ShapeDtypeStruct((N_GL_PAIRS, m_rows, PAIR), F32))
    outs = [pair_slabs, pair_slabs, out(W_GL, BF16), out(W_GL, F32), out(W_GL, BF16),
            out(W_RWKV, F32), out(W_RWKV, F32), out(W_RWKV, F32), out(W_RWKV, F32), out(W_RWKV, F32),
            out(W_RWKV, F32), out(RWKV_GATE_RANK, BF16)]
    return pl.pallas_call(
        functools.partial(_inproj_kernel, steps_per_batch),
        grid=(m_rows // tm,),
        in_specs=[
            pl.BlockSpec((tm, d), row),
            pl.BlockSpec((1, 6, d), lambda i: (i // steps_per_batch, 0, 0)),
            resident(N_HGRN_COLS + 4 * W_GLA), resident(LANES), resident(N_RWKV_COLS),
            vec(W_HGRN),
            pl.BlockSpec((LANES, W_GLA), const),
            vec(W_GLA),
            vec(N_RWKV_COLS), vec(W_RWKV), vec(W_RWKV),
            pl.BlockSpec((LANES, 2 * W_RWKV), const),
            vec(W_RWKV), vec(W_RWKV),
        ],
        out_specs=[o[0] for o in outs],
        out_shape=[o[1] for o in outs],
        scratch_shapes=[pltpu.VMEM((1, N_RWKV_COLS), F32)],
        compiler_params=pltpu.CompilerParams(
            dimension_semantics=("arbitrary",), vmem_limit_bytes=VMEM_LIMIT),
        name="in_projection",
    )(x2, mod_l, w_main, w_alpha, w_rwkv, lb_l, alpha_up_p, alpha_b, mu, w0, a0, wa_blk, k_k, k_a)


def _gl_kernel(q_s, k_s, v_s, lg_ref, gate_ref, ng_ref, ind_ref, out_ref,
               st_ref, b_s, ad_s, e_s, o_s, sc_s, eend_s):
    tb = q_s.shape[1]
    n_chunks = tb // CHUNK
    n_groups = tb // SUB

    @pl.when(pl.program_id(1) == 0)
    def _():
        st_ref[...] = jnp.zeros_like(st_ref)

    tril = _tril_bf16()
    for c in range(n_chunks):
        rows = slice(c * CHUNK, (c + 1) * CHUNK)
        b = _dot_exact_lhs(tril, lg_ref[rows, :], 2)
        for p in range(N_GL_PAIRS):
            b_s[p, rows, :] = b[:, p * PAIR:(p + 1) * PAIR]

    for t in range(SUB - 1):
        e_s[t * n_groups:(t + 1) * n_groups, (t + 1) * PAIR:] = jnp.zeros((n_groups, (SUB - 1 - t) * PAIR), BF16)
    for p in range(N_GL_PAIRS):
        by_pos = lambda ref, t: ref[p, pl.ds(t, n_groups, stride=SUB), :]
        qs = [by_pos(q_s, t) for t in range(SUB)]
        ks = [by_pos(k_s, t) for t in range(SUB)]
        bs = [by_pos(b_s, t) * LOG2E for t in range(SUB)]
        for t in range(SUB):
            for sg in range(t + 1):
                e = qs[t] * ks[sg]
                if sg < t:
                    e = e * jnp.exp2(jnp.minimum(bs[t] - bs[sg], 0.0))
                e_s[t * n_groups:(t + 1) * n_groups, sg * PAIR:(sg + 1) * PAIR] = e.astype(BF16)
        ad = [jnp.zeros((n_groups, PAIR), F32)] * SUB
        for sg0 in range(0, SUB, 2):
            part = _mm(e_s[sg0 * n_groups:, sg0 * PAIR:(sg0 + 2) * PAIR], ind_ref[sg0 * PAIR:(sg0 + 2) * PAIR, :])
            for t in range(sg0, SUB):
                ad[t] = ad[t] + part[(t - sg0) * n_groups:(t - sg0 + 1) * n_groups, :]
        for t in range(SUB):
            ad_s[p, pl.ds(t, n_groups, stride=SUB), :] = ad[t]

    masks = _pair_masks()
    t_idx = _iota((CHUNK, PAIR), 0)
    second_half = {hs: (t_idx // hs) % 2 == 1 for hs in LEVEL_HALVES}
    level_sign = {hs: jnp.where(second_half[hs], LOG2E, -LOG2E) for hs in LEVEL_HALVES}

    def group_units(i):
        units = []
        for cc in range(A_GROUP_GL):
            c = i * A_GROUP_GL + cc
            rows = pl.ds(pl.multiple_of(c * CHUNK, CHUNK), CHUNK)
            for p in range(N_GL_PAIRS):
                units.append(dict(p=p, rows=rows, lanes=slice(p * PAIR, (p + 1) * PAIR),
                                  e_rows=pl.ds(pl.multiple_of(c * 8, 8), 8),
                                  st_rows=pl.ds(pl.multiple_of((c * N_GL_PAIRS + p) * PAIR, PAIR), PAIR)))
        return units

    def increments(i, carry):
        units = group_units(i)
        for un in units:
            b = b_s[un["p"], un["rows"], :]
            b_end = b[CHUNK - 1:CHUNK, :]
            un["k_end"] = k_s[un["p"], un["rows"], :] * jnp.exp(b_end - b)
            eend_s[un["e_rows"], un["lanes"]] = jnp.broadcast_to(jnp.exp(b_end), (8, PAIR))
        for un in units:
            upd = _dot1(v_s[un["rows"], un["lanes"]], un["k_end"], TN)
            sc_s[un["st_rows"], :] = jnp.where(masks["bd"], upd, 0.0)
        return carry

    lax.fori_loop(0, n_chunks // A_GROUP_GL, increments, 0)

    sts = [st_ref[p] for p in range(N_GL_PAIRS)]
    for c in range(n_chunks):
        for p in range(N_GL_PAIRS):
            st_rows = slice((c * N_GL_PAIRS + p) * PAIR, (c * N_GL_PAIRS + p + 1) * PAIR)
            upd = sc_s[st_rows, :]
            sc_s[st_rows, :] = sts[p]
            sts[p] = sts[p] * eend_s[c * 8:c * 8 + 1, p * PAIR:(p + 1) * PAIR] + upd
    for p in range(N_GL_PAIRS):
        st_ref[p] = sts[p]

    def outputs(i, carry):
        units = group_units(i)

        def decays(un):
            q = q_s[un["p"], un["rows"], :]
            k = k_s[un["p"], un["rows"], :]
            b = b_s[un["p"], un["rows"], :]
            un["lv"] = []
            for hs in LEVEL_HALVES:
                e_mid = jnp.concatenate(
                    [jnp.broadcast_to(b[m:m + 1, :], (2 * hs, PAIR)) for m in range(hs - 1, CHUNK, 2 * hs)], axis=0)
                dec = jnp.exp2(jnp.minimum((b - e_mid) * level_sign[hs], 0.0))
                un["lv"].append(jnp.where(second_half[hs], q, k) * dec)
            un["q_in"] = q * jnp.exp(b)

        def scores(un):
            a = jnp.zeros((CHUNK, PAIR), F32)
            for hs, lv in zip(LEVEL_HALVES, un["lv"]):
                a = jnp.where(masks["level"][hs], _dot1(lv, jnp.transpose(_stack_heads(lv))), a)
            un["a"] = jnp.where(masks["diag"], ad_s[un["p"], un["rows"], :], a)

        def combine(un):
            o_s[un["rows"], un["lanes"]] = (_dot1(un["a"], _stack_heads(v_s[un["rows"], un["lanes"]]))
                                            + _dot1(un["q_in"], sc_s[un["st_rows"], :], NT))

        stages = (decays, scores, combine)
        for step in range(len(units) + OUT_SKEW * (len(stages) - 1)):
            for s, stage in enumerate(stages):
                u = step - OUT_SKEW * s
                if 0 <= u < len(units):
                    stage(units[u])
        return carry

    lax.fori_loop(0, n_chunks // A_GROUP_GL, outputs, 0)

    o = o_s[...]
    ms = _head_sums([o * o])[0] * (1.0 / HEAD_DIM)
    out_ref[...] = (o * lax.rsqrt(ms + RMS_EPS) * ng_ref[...] * gate_ref[...]).astype(out_ref.dtype)


def _gl_indicator():
    r = np.arange(SUB * PAIR)
    c = np.arange(PAIR)
    sg = r // PAIR
    h = (r % PAIR) // HEAD_DIM
    ind = (h[:, None] == (c // HEAD_DIM)[None, :]) & (sg[:, None] == (c % SUB)[None, :])
    return jnp.asarray(ind, dtype=BF16)


def _gl_mixer(q, k, v, lg, gate, norm_g, bsz, seq, tb):
    steps = seq // tb
    row = lambda b, i: (b * steps + i, 0)
    const = lambda b, i: (0, 0)
    slab = pl.BlockSpec((tb, W_GL), row)
    pair_slabs = pl.BlockSpec((N_GL_PAIRS, tb, PAIR), lambda b, i: (0, b * steps + i, 0))
    scratch_f32 = pltpu.VMEM((tb, W_GL), F32)
    scratch_pairs = pltpu.VMEM((N_GL_PAIRS, tb, PAIR), F32)
    return pl.pallas_call(
        _gl_kernel,
        grid=(bsz, steps),
        in_specs=[
            pair_slabs, pair_slabs, slab, slab, slab,
            pl.BlockSpec((1, W_GL), const),
            pl.BlockSpec((SUB * PAIR, PAIR), const),
        ],
        out_specs=pl.BlockSpec((tb, W_GL), row),
        out_shape=jax.ShapeDtypeStruct((bsz * seq, W_GL), BF16),
        scratch_shapes=[
            pltpu.VMEM((N_GL_PAIRS, PAIR, PAIR), F32),
            scratch_pairs, scratch_pairs,
            pltpu.VMEM((tb, SUB * PAIR), BF16),
            scratch_f32,
            pltpu.VMEM((tb // CHUNK * N_GL_PAIRS * PAIR, PAIR), F32),
            pltpu.VMEM((tb // CHUNK * 8, W_GL), F32),
        ],
        compiler_params=pltpu.CompilerParams(
            dimension_semantics=("arbitrary", "arbitrary"), vmem_limit_bytes=VMEM_LIMIT),
        name="hgrn_gla_mixer",
    )(q, k, v, lg, gate, norm_g, _gl_indicator())


def _rwkv_kernel(r_s, lw_s, k_s, v_s, kk_s, ka_s, hg_s, gup_ref, rk_ref, gng_ref, gnb_ref,
                 out_ref, st_ref, y_s, rt_s, bh_s, kh_s, nb_s, nk_s, w_s, u0_s, eend_s):
    bsz, tb = r_s.shape[0], r_s.shape[1]
    n_chunks = tb // CHUNK

    @pl.when(pl.program_id(0) == 0)
    def _():
        st_ref[...] = jnp.zeros_like(st_ref)

    masks = _pair_masks()
    tril = _tril_bf16()
    eye = jnp.where(masks["eye"], 1.0, 0.0)

    def phase_a(b, carry):
        units = []
        for c in range(n_chunks):
            rows = slice(c * CHUNK, (c + 1) * CHUNK)
            g_all = _dot_exact_lhs(tril, lw_s[b, rows, :], 2)
            for p in range(N_RW_PAIRS):
                lanes = slice(p * PAIR, (p + 1) * PAIR)
                units.append(dict(c=c, rows=rows, lanes=lanes, g=g_all[:, lanes]))
        def scale(un):
            rows, lanes, g = un["rows"], un["lanes"], un["g"]
            kc = k_s[b, rows, lanes]
            kkc = kk_s[b, rows, lanes]
            ka = ka_s[b, rows, lanes]
            g_end = g[CHUNK - 1:CHUNK, :]
            e_neg = jnp.exp(-g)
            e_end = jnp.exp(g_end - g)
            un["at"] = -kkc * jnp.exp(g - lw_s[b, rows, lanes])
            rt = r_s[b, rows, lanes] * jnp.exp(g)
            un["rt"] = rt
            un["bt"] = ka * e_neg
            un["kt"] = kc * e_neg
            rt_s[b, rows, lanes] = rt
            bh_s[b, rows, lanes] = ka * e_end
            kh_s[b, rows, lanes] = kc * e_end
            eend_s[b, un["c"] * 8:un["c"] * 8 + 8, lanes] = jnp.broadcast_to(jnp.exp(g_end), (8, PAIR))

        def gram(un):
            g2 = _dot1(jnp.concatenate([un["at"], un["rt"]], axis=0),
                       jnp.concatenate([_stack_heads(un["bt"]), _stack_heads(un["kt"])], axis=0), NT)
            un["l"] = jnp.where(masks["strict"], g2[0:CHUNK, 0:PAIR], 0.0)
            un["mk"] = jnp.where(masks["strict"], g2[0:CHUNK, PAIR:2 * PAIR], 0.0)
            nb_s[b, un["rows"], un["lanes"]] = jnp.where(masks["incl"], g2[CHUNK:2 * CHUNK, 0:PAIR], 0.0)
            nk_s[b, un["rows"], un["lanes"]] = jnp.where(masks["incl"], g2[CHUNK:2 * CHUNK, PAIR:2 * PAIR], 0.0)

        def inv_first(un):
            un["t"] = eye + un["l"]
            un["lp"] = _dot1(un["l"], _stack_heads(un["l"]))
            un["mv"] = _dot1(un["mk"], _stack_heads(v_s[b, un["rows"], un["lanes"]]))

        def inv_step(un):
            both = _dot1(jnp.concatenate([un["lp"], un["t"]], axis=0), _stack_heads(un["lp"]))
            un["lp"] = both[0:CHUNK, :]
            un["t"] = un["t"] + both[CHUNK:2 * CHUNK, :]

        def inv_last(un):
            un["t"] = un["t"] + _dot1(un["t"], _stack_heads(un["lp"]))

        def apply_inv(un):
            wu = _dot1(un["t"], jnp.concatenate([_stack_heads(un["at"]), _stack_heads(un["mv"])], axis=1))
            w_s[b, un["rows"], un["lanes"]] = wu[:, 0:PAIR]
            u0_s[b, un["rows"], un["lanes"]] = wu[:, PAIR:2 * PAIR]

        for stage in (scale, gram, inv_first, inv_step, inv_step, inv_step, inv_step, inv_last, apply_inv):
            for un in units:
                stage(un)
        return carry

    lax.fori_loop(0, bsz, phase_a, 0)

    def phase_b(c, carry):
        rows = pl.ds(pl.multiple_of(c * CHUNK, CHUNK), CHUNK)
        e_rows = pl.ds(pl.multiple_of(c * 8, 8), 8)
        seqs = [(b, p, slice(p * PAIR, (p + 1) * PAIR)) for b in range(bsz) for p in range(N_RW_PAIRS)]
        sts = [st_ref[b * N_RW_PAIRS + p] for b, p, _ in seqs]
        vs = [v_s[b, rows, lanes] for b, _, lanes in seqs]
        wrs = [_dot1(jnp.concatenate([w_s[b, rows, lanes], rt_s[b, rows, lanes]], axis=0), st, NT)
               for (b, _, lanes), st in zip(seqs, sts)]
        us = [wr[0:CHUNK, :] + u0_s[b, rows, lanes] for (b, _, lanes), wr in zip(seqs, wrs)]
        upds = [_dot1(jnp.concatenate([u, vc], axis=0),
                      jnp.concatenate([bh_s[b, rows, lanes], kh_s[b, rows, lanes]], axis=0), TN)
                for (b, _, lanes), u, vc in zip(seqs, us, vs)]
        for (b, p, lanes), st, upd in zip(seqs, sts, upds):
            e_end = eend_s[b, e_rows, lanes][0:1, :]
            st_ref[b * N_RW_PAIRS + p] = st * e_end + jnp.where(masks["bd"], upd, 0.0)
        for (b, _, lanes), wr, u, vc in zip(seqs, wrs, us, vs):
            y_s[b, rows, lanes] = (
                wr[CHUNK:2 * CHUNK, :]
                + _dot1(jnp.concatenate([nb_s[b, rows, lanes], nk_s[b, rows, lanes]], axis=1),
                        jnp.concatenate([_stack_heads(u), _stack_heads(vc)], axis=0)))
        return carry

    lax.fori_loop(0, n_chunks, phase_b, 0)

    for b in range(bsz):
        y = y_s[b]
        s_hi, s_lo = _head_sums(_bf16_parts(y, 2))
        d = y - (s_hi + s_lo) * (1.0 / HEAD_DIM)
        s_var, s_bonus = _head_sums([d * d, r_s[b] * k_s[b] * rk_ref[...]])
        yn = d * lax.rsqrt(s_var * (1.0 / HEAD_DIM) + RWKV_GN_EPS) * gng_ref[...] + gnb_ref[...]
        gate = _dot1(hg_s[b], gup_ref[...])
        out_ref[b] = ((yn + s_bonus * v_s[b]) * gate).astype(out_ref.dtype)


def _rwkv_mixer(r, lw, k, v, kk, ka, hg, g_up, r_k, gn_g, gn_b, bsz, seq, tb):
    blk = lambda i: (0, i, 0)
    const = lambda i: (0, 0)
    vec = pl.BlockSpec((1, W_RWKV), const)
    tok = pl.BlockSpec((bsz, tb, W_RWKV), blk)
    slab = pltpu.VMEM((bsz, tb, W_RWKV), F32)
    seq_major = lambda a: a.reshape(bsz, seq, a.shape[-1])
    out = pl.pallas_call(
        _rwkv_kernel,
        grid=(seq // tb,),
        in_specs=[
            tok, tok, tok, tok, tok, tok,
            pl.BlockSpec((bsz, tb, RWKV_GATE_RANK), blk),
            pl.BlockSpec((RWKV_GATE_RANK, W_RWKV), const),
            vec, vec, vec,
        ],
        out_specs=pl.BlockSpec((bsz, tb, W_RWKV), blk),
        out_shape=jax.ShapeDtypeStruct((bsz, seq, W_RWKV), BF16),
        scratch_shapes=[
            pltpu.VMEM((bsz * N_RW_PAIRS, PAIR, PAIR), F32),
            slab,
            slab, slab, slab, slab, slab, slab, slab,
            pltpu.VMEM((bsz, tb // CHUNK * 8, W_RWKV), F32),
        ],
        compiler_params=pltpu.CompilerParams(
            dimension_semantics=("arbitrary",), vmem_limit_bytes=VMEM_LIMIT),
        name="rwkv7_mixer",
    )(*[seq_major(a) for a in (r, lw, k, v, kk, ka, hg)], g_up, r_k, gn_g, gn_b)
    return out.reshape(bsz * seq, W_RWKV)


def _post_kernel(alpha, ff_chunk, x_ref, ogl_ref, orw_ref, mod_ref, wout_ref, ln1g_ref, ln1b_ref,
                 wup_ref, wdn_ref, ln2g_ref, ln2b_ref, out_ref):
    m = mod_ref[0]
    gate1, shift2, scale2, gate2 = m[2:3, :], m[3:4, :], m[4:5, :], m[5:6, :]
    o = _mm(ogl_ref[...], wout_ref[0:W_GL, :]) + _mm(orw_ref[...], wout_ref[W_GL:W_GL + W_RWKV, :])
    x1 = _layer_norm(alpha * x_ref[...] + (1.0 + gate1) * o, ln1g_ref[...], ln1b_ref[...])
    h = (x1 * (1.0 + scale2) + shift2).astype(BF16)
    d_ff = wup_ref.shape[1]
    acc = jnp.zeros(x1.shape, F32)
    for j in range(d_ff // ff_chunk):
        cols = slice(j * ff_chunk, (j + 1) * ff_chunk)
        u = jnp.maximum(_mm(h, wup_ref[:, cols]), 0.0)
        acc = acc + _mm((u * u).astype(BF16), wdn_ref[cols, :])
    out_ref[...] = _layer_norm(alpha * x1 + (1.0 + gate2) * acc, ln2g_ref[...], ln2b_ref[...])


def _post_mixer(x2, ogl, orw, mod_l, w_out, ln1_g, ln1_b, w_up, w_dn, ln2_g, ln2_b, alpha, seq, tm, layer):
    m_rows, d = x2.shape
    d_ff = w_up.shape[2]
    steps_per_batch = seq // tm
    row = lambda i: (i, 0)
    const = lambda i: (0, 0)
    vec = pl.BlockSpec((1, d), const)
    resident = lambda shape: pl.BlockSpec((None,) + shape, lambda i: (layer, 0, 0), pipeline_mode=pl.Buffered(1))
    return pl.pallas_call(
        functools.partial(_post_kernel, alpha, 512),
        grid=(m_rows // tm,),
        in_specs=[
            pl.BlockSpec((tm, d), row),
            pl.BlockSpec((tm, W_GL), row),
            pl.BlockSpec((tm, W_RWKV), row),
            pl.BlockSpec((1, 6, d), lambda i: (i // steps_per_batch, 0, 0)),
            resident((W_GL + W_RWKV, d)),
            vec, vec,
            resident((d, d_ff)),
            resident((d_ff, d)),
            vec, vec,
        ],
        out_specs=pl.BlockSpec((tm, d), row),
        out_shape=jax.ShapeDtypeStruct((m_rows, d), F32),
        compiler_params=pltpu.CompilerParams(
            dimension_semantics=("arbitrary",), vmem_limit_bytes=VMEM_LIMIT),
        name="outproj_mlp",
    )(x2, ogl, orw, mod_l, w_out, ln1_g, ln1_b, w_up, w_dn, ln2_g, ln2_b)


def kernel(x, c, hgrn_lb_logits, ada_w, ada_b, w_in, hgrn_norm_g, gla_alpha_up, gla_alpha_b, gla_norm_g,
           rwkv_mu, rwkv_w0, rwkv_w_up, rwkv_a0, rwkv_a_up, rwkv_g_up, rwkv_k_k, rwkv_k_a, rwkv_r_k,
           rwkv_gn_g, rwkv_gn_b, w_out, ln1_g, ln1_b, mlp_w_up, mlp_w_down, ln2_g, ln2_b):
    bsz, seq, d = x.shape
    depth = w_in.shape[0]
    alpha = (2.0 * depth) ** 0.25
    tm = min(512, seq)
    tb = min(512, seq)
    assert seq % tm == 0 and seq % tb == 0 and tb % CHUNK == 0
    assert w_in.shape[2] == N_HGRN_COLS + N_GLA_COLS + N_RWKV_COLS

    mod = _modulation(c, ada_w, ada_b).reshape(depth, bsz, 6, d)
    lbs = _lower_bounds(hgrn_lb_logits)

    gla_end = N_HGRN_COLS + N_GLA_COLS
    w_in_b = w_in.astype(BF16)
    w_alpha = jnp.pad(w_in_b[:, :, gla_end - GLA_GATE_RANK:gla_end], ((0, 0), (0, 0), (0, LANES - GLA_GATE_RANK)))
    w_rwkv = w_in_b[:, :, gla_end:]
    alpha_up_p = jnp.concatenate(
        [gla_alpha_up, jnp.zeros((depth, LANES - GLA_GATE_RANK, W_GLA), gla_alpha_up.dtype)], axis=1)
    zeros_r = jnp.zeros((depth, RWKV_DECAY_RANK, W_RWKV), rwkv_w_up.dtype)
    wa_blk = jnp.concatenate(
        [jnp.concatenate([rwkv_w_up, zeros_r], axis=2), jnp.concatenate([zeros_r, rwkv_a_up], axis=2)], axis=1)
    norm_g = jnp.concatenate([hgrn_norm_g, gla_norm_g], axis=1)
    w_out_b = w_out.astype(BF16)
    w_up_b = mlp_w_up.astype(BF16)
    w_dn_b = mlp_w_down.astype(BF16)
    vec = lambda a, l: a[l].reshape(1, -1)

    x2 = x.reshape(bsz * seq, d)
    for l in range(depth):
        gq, gk, gv, glg, ggate, rr, rlw, rk, rv, rkk, rka, rhg = _in_projection(
            x2, mod[l], w_in_b, w_alpha, w_rwkv, vec(lbs, l), alpha_up_p[l], vec(gla_alpha_b, l), vec(rwkv_mu, l),
            vec(rwkv_w0, l), vec(rwkv_a0, l), wa_blk[l], vec(rwkv_k_k, l), vec(rwkv_k_a, l), seq, tm, l)
        ogl = _gl_mixer(gq, gk, gv, glg, ggate, vec(norm_g, l), bsz, seq, min(1024, seq))
        orw = _rwkv_mixer(rr, rlw, rk, rv, rkk, rka, rhg, rwkv_g_up[l], vec(rwkv_r_k, l), vec(rwkv_gn_g, l),
                          vec(rwkv_gn_b, l), bsz, seq, tb)
        x2 = _post_mixer(x2, ogl, orw, mod[l], w_out_b, vec(ln1_g, l), vec(ln1_b, l), w_up_b, w_dn_b,
                         vec(ln2_g, l), vec(ln2_b, l), alpha, seq, tm, l)
    return x2.reshape(bsz, seq, d)
```

```python
import functools

import numpy as np
import jax
import jax.numpy as jnp
from jax import lax
from jax.experimental import pallas as pl
from jax.experimental.pallas import tpu as pltpu

F32 = jnp.float32
BF16 = jnp.bfloat16

HEAD_DIM = 64
H_HGRN, H_GLA, H_RWKV = 4, 6, 6
W_HGRN, W_GLA, W_RWKV = H_HGRN * HEAD_DIM, H_GLA * HEAD_DIM, H_RWKV * HEAD_DIM
GLA_GATE_RANK = 16
GLA_GATE_NORMALIZER = 16.0
RWKV_DECAY_RANK, RWKV_ICLR_RANK, RWKV_GATE_RANK = 64, 64, 128
RWKV_GN_EPS = 64e-5
N_HGRN_COLS = 4 * W_HGRN
N_GLA_COLS = 4 * W_GLA + GLA_GATE_RANK
N_RWKV_COLS = 3 * W_RWKV + RWKV_DECAY_RANK + RWKV_ICLR_RANK + RWKV_GATE_RANK
CHUNK = 64
LN_EPS = 1e-5
RMS_EPS = 1e-5
F_MIN = 1e-30

LANES = 128
PAIR = 2 * HEAD_DIM
SUB = 8
LEVEL_HALVES = (SUB, 2 * SUB, 4 * SUB)
assert 2 * LEVEL_HALVES[-1] == CHUNK
A_GROUP_GL = 4
OUT_SKEW = 4
LOG2E = 1.4426950408889634
VMEM_LIMIT = 56 * 1024 * 1024

W_GL = W_HGRN + W_GLA
N_GL_PAIRS = W_GL // PAIR
N_RW_PAIRS = W_RWKV // PAIR

NN = (((1,), (0,)), ((), ()))
NT = (((1,), (1,)), ((), ()))
TN = (((0,), (0,)), ((), ()))


def _mm(a, b, dims=NN):
    return lax.dot_general(a, b, dims, preferred_element_type=F32)


def _split2(x):
    hi = x.astype(BF16)
    lo = (x - hi.astype(F32)).astype(BF16)
    return hi, lo


def _bf16_parts(x, n):
    parts = []
    for i in range(n):
        p = x.astype(BF16)
        parts.append(p)
        if i + 1 < n:
            x = x - p.astype(F32)
    return parts


def _dot1(a, b, dims=NN):
    return _mm(a.astype(BF16), b.astype(BF16), dims)


def _dot3(a, b, dims=NN):
    ah, al = _split2(a)
    bh, bl = _split2(b)
    return _mm(ah, bh, dims) + (_mm(ah, bl, dims) + _mm(al, bh, dims))


def _dot_exact_lhs(a_bf16, b, n):
    return _mm(jnp.concatenate([a_bf16] * n, axis=1), jnp.concatenate(_bf16_parts(b, n), axis=0))


def _sigmoid(x):
    return 1.0 / (1.0 + jnp.exp(-x))


def _silu(x):
    return x * _sigmoid(x)


def _softplus(x):
    return jnp.maximum(x, 0.0) + jnp.log1p(jnp.exp(-jnp.abs(x)))


def _log_sigmoid(x):
    return -_softplus(-x)


def _layer_norm(y, g, b):
    mu = jnp.mean(y, axis=-1, keepdims=True)
    d = y - mu
    var = jnp.mean(d * d, axis=-1, keepdims=True)
    return d * lax.rsqrt(var + LN_EPS) * g + b


def _iota(shape, axis):
    return lax.broadcasted_iota(jnp.int32, shape, axis)


def _stack_heads(x):
    lane = _iota(x.shape, 1)
    return jnp.concatenate([jnp.where(lane < HEAD_DIM, x, 0.0), jnp.where(lane >= HEAD_DIM, x, 0.0)], axis=0)


def _pair_masks():
    t = _iota((CHUNK, PAIR), 0)
    s = _iota((CHUNK, PAIR), 1) % HEAD_DIM
    r2 = _iota((PAIR, PAIR), 0) // HEAD_DIM
    c2 = _iota((PAIR, PAIR), 1) // HEAD_DIM
    return dict(
        strict=s < t,
        incl=s <= t,
        eye=s == t,
        diag=(s // SUB == t // SUB) & (s <= t),
        level={hs: (s // (2 * hs) == t // (2 * hs)) & ((t // hs) % 2 == 1) & ((s // hs) % 2 == 0)
               for hs in LEVEL_HALVES},
        bd=r2 == c2,
    )


def _tril_bf16():
    return jnp.where(_iota((CHUNK, CHUNK), 1) <= _iota((CHUNK, CHUNK), 0), 1.0, 0.0).astype(BF16)


def _seg_mean_matrix(width):
    r = _iota((width, width), 0) // HEAD_DIM
    c = _iota((width, width), 1) // HEAD_DIM
    return jnp.where(r == c, 1.0, 0.0).astype(BF16)


def _head_sums(xs):
    tiles = [x[:, j:j + LANES].astype(BF16) for x in xs for j in range(0, x.shape[1], LANES)]
    sums = []
    for i in range(0, len(tiles) - 1, 2):
        s = _mm(jnp.concatenate(tiles[i:i + 2], axis=1), _seg_mean_matrix(2 * LANES))
        sums += [s[:, 0:LANES], s[:, LANES:2 * LANES]]
    if len(tiles) % 2:
        sums.append(_mm(tiles[-1], _seg_mean_matrix(LANES)))
    out, i = [], 0
    for x in xs:
        n = x.shape[1] // LANES
        out.append(jnp.concatenate(sums[i:i + n], axis=1))
        i += n
    return out


def _mod_kernel(c_ref, w_ref, b_ref, o_ref):
    c = _silu(c_ref[...])
    o_ref[0] = _dot3(c, w_ref[0]) + b_ref[0]


def _modulation(c, ada_w, ada_b):
    depth, d, n = ada_w.shape
    bsz = c.shape[0]
    tn = 1536
    return pl.pallas_call(
        _mod_kernel,
        grid=(depth, n // tn),
        in_specs=[
            pl.BlockSpec((bsz, d), lambda l, j: (0, 0)),
            pl.BlockSpec((1, d, tn), lambda l, j: (l, 0, j)),
            pl.BlockSpec((1, 1, tn), lambda l, j: (l, 0, j)),
        ],
        out_specs=pl.BlockSpec((1, bsz, tn), lambda l, j: (l, 0, j)),
        out_shape=jax.ShapeDtypeStruct((depth, bsz, n), F32),
        compiler_params=pltpu.CompilerParams(
            dimension_semantics=("arbitrary", "arbitrary"), vmem_limit_bytes=VMEM_LIMIT),
        name="adaln_modulation",
    )(c, ada_w, ada_b.reshape(depth, 1, n))


def _lower_bound_kernel(x_ref, o_ref):
    depth = x_ref.shape[0]
    rows = [x_ref[l:l + 1, :] for l in range(depth)]
    m = functools.reduce(jnp.maximum, rows)
    e = [jnp.exp(r - m) for r in rows]
    tot = functools.reduce(lambda a, b: a + b, e)
    p = [ei / tot for ei in e]
    acc = jnp.zeros_like(p[0])
    for l in range(depth):
        acc = acc + p[l]
        o_ref[l:l + 1, :] = acc - p[0]


def _lower_bounds(logits):
    return pl.pallas_call(
        _lower_bound_kernel,
        out_shape=jax.ShapeDtypeStruct(logits.shape, F32),
        name="hgrn_lower_bounds",
    )(logits.astype(F32))


def _inproj_kernel(steps_per_batch, x_ref, mod_ref, w_ref, wal_ref, wr_ref, lb_ref, au_ref, ab_ref, mu_ref,
                   w0_ref, a0_ref, wa_ref, kk_ref, ka_ref,
                   gq_ref, gk_ref, gv_ref, gb_ref, gg_ref,
                   rr_ref, rlw_ref, rk_ref, rv_ref, rkk_ref, rka_ref, rhg_ref, carry_s):
    tm = x_ref.shape[0]
    w3 = W_RWKV

    @pl.when(pl.program_id(0) % steps_per_batch == 0)
    def _():
        carry_s[...] = jnp.zeros_like(carry_s)

    m = mod_ref[0]
    h = (x_ref[...] * (1.0 + m[1:2, :]) + m[0:1, :]).astype(BF16)

    def proj(c0, width):
        return _mm(h, w_ref[:, c0:c0 + width])

    first_row = _iota((tm, LANES), 0) == 0

    def shifted(c0, width):
        z = _mm(h, wr_ref[:, c0:c0 + width])
        z_prev = pltpu.roll(z, 1, axis=0)
        z_prev = jnp.where(jnp.concatenate([first_row] * (width // LANES), axis=1), carry_s[:, c0:c0 + width], z_prev)
        carry_s[:, c0:c0 + width] = z[tm - 1:tm, :]
        return z + (z_prev - z) * mu_ref[:, c0:c0 + width]

    g0 = N_HGRN_COLS
    n_rp = w3 // PAIR
    tile = lambda j: slice(j * PAIR, (j + 1) * PAIR)
    z_v = shifted(2 * w3, w3 + LANES)
    z_rk = shifted(0, 2 * w3)

    h_wa = z_v[:, w3:w3 + LANES]
    h_wa = jnp.where(_iota(h_wa.shape, 1) < RWKV_DECAY_RANK, jnp.tanh(h_wa), h_wa)
    wa = _dot1(h_wa, wa_ref[...])
    rv_ref[...] = z_v[:, 0:w3]
    rr_ref[...] = z_rk[:, 0:w3]

    later = [lambda: proj(W_HGRN, W_HGRN), lambda: proj(0, W_HGRN),
             lambda: shifted(3 * w3 + LANES, RWKV_GATE_RANK)]
    issued, a_tiles = [], []
    for j in range(n_rp):
        issued.append(later[j]())
        w_log = -_softplus(-(w0_ref[:, tile(j)] + wa[:, tile(j)])) - 0.5
        rlw_ref[:, tile(j)] = -jnp.exp(w_log)
        a_tiles.append(_sigmoid(a0_ref[:, tile(j)] + wa[:, w3 + j * PAIR:w3 + (j + 1) * PAIR]))
    zf, zq, z_hg = issued

    z_al = _mm(h, wal_ref[...])
    k = z_rk[:, w3:2 * w3]
    kk = k * kk_ref[...]
    kk = kk / jnp.maximum(jnp.sqrt(_head_sums([kk * kk])[0]), 1e-12)
    rkk_ref[...] = kk

    z_qk = proj(g0, 2 * W_GLA)
    for j in range(n_rp):
        rka_ref[:, tile(j)] = kk[:, tile(j)] * a_tiles[j]
        rk_ref[:, tile(j)] = k[:, tile(j)] * (1.0 + (a_tiles[j] - 1.0) * ka_ref[:, tile(j)])
    lb = lb_ref[...]
    f = lb + (1.0 - lb) * _sigmoid(zf)
    gb_ref[:, 0:W_HGRN] = jnp.log(jnp.maximum(f, F_MIN))
    k_h = (1.0 - lb) * _sigmoid(-zf)
    q_h = _silu(zq) * HEAD_DIM ** -0.5
    for p in range(W_HGRN // PAIR):
        gk_ref[p] = k_h[:, tile(p)]
        gq_ref[p] = q_h[:, tile(p)]

    z_vr = proj(g0 + 2 * W_GLA, 2 * W_GLA)
    rhg_ref[...] = _sigmoid(z_hg).astype(BF16)
    logit = _dot1(z_al, au_ref[...]) + ab_ref[...]
    gb_ref[:, W_HGRN:W_GL] = _log_sigmoid(logit) / GLA_GATE_NORMALIZER
    for p in range(W_GLA // PAIR):
        gq_ref[W_HGRN // PAIR + p] = z_qk[:, tile(p)] * HEAD_DIM ** -0.5
        gk_ref[W_HGRN // PAIR + p] = z_qk[:, W_GLA + p * PAIR:W_GLA + (p + 1) * PAIR]

    zg = proj(3 * W_HGRN, W_HGRN)
    gv_ref[:, W_HGRN:W_GL] = z_vr[:, 0:W_GLA].astype(BF16)
    gg_ref[:, W_HGRN:W_GL] = _silu(z_vr[:, W_GLA:2 * W_GLA]).astype(BF16)
    zi = proj(2 * W_HGRN, W_HGRN)
    gg_ref[:, 0:W_HGRN] = _silu(zg).astype(BF16)
    gv_ref[:, 0:W_HGRN] = zi.astype(BF16)


def _in_projection(x2, mod_l, w_main, w_alpha, w_rwkv, lb_l, alpha_up_p, alpha_b, mu, w0, a0, wa_blk, k_k, k_a,
                   seq, tm, layer):
    m_rows, d = x2.shape
    resident = lambda n: pl.BlockSpec((None, d, n), lambda i: (layer, 0, 0), pipeline_mode=pl.Buffered(1))
    steps_per_batch = seq // tm
    row = lambda i: (i, 0)
    const = lambda i: (0, 0)
    vec = lambda n: pl.BlockSpec((1, n), const)
    out = lambda n, dt: (pl.BlockSpec((tm, n), row), jax.ShapeDtypeStruct((m_rows, n), dt))
    pair_slabs = (pl.BlockSpec((N_GL_PAIRS, tm, PAIR), lambda i: (0, i, 0)),
                  jax.ShapeDtypeStruct((N_GL_PAIRS, m_rows, PAIR), F32))
    outs = [pair_slabs, pair_slabs, out(W_GL, BF16), out(W_GL, F32), out(W_GL, BF16),
            out(W_RWKV, F32), out(W_RWKV, F32), out(W_RWKV, F32), out(W_RWKV, F32), out(W_RWKV, F32),
            out(W_RWKV, F32), out(RWKV_GATE_RANK, BF16)]
    return pl.pallas_call(
        functools.partial(_inproj_kernel, steps_per_batch),
        grid=(m_rows // tm,),
        in_specs=[
            pl.BlockSpec((tm, d), row),
            pl.BlockSpec((1, 6, d), lambda i: (i // steps_per_batch, 0, 0)),
            resident(N_HGRN_COLS + 4 * W_GLA), resident(LANES), resident(N_RWKV_COLS),
            vec(W_HGRN),
            pl.BlockSpec((LANES, W_GLA), const),
            vec(W_GLA),
            vec(N_RWKV_COLS), vec(W_RWKV), vec(W_RWKV),
            pl.BlockSpec((LANES, 2 * W_RWKV), const),
            vec(W_RWKV), vec(W_RWKV),
        ],
        out_specs=[o[0] for o in outs],
        out_shape=[o[1] for o in outs],
        scratch_shapes=[pltpu.VMEM((1, N_RWKV_COLS), F32)],
        compiler_params=pltpu.CompilerParams(
            dimension_semantics=("arbitrary",), vmem_limit_bytes=VMEM_LIMIT),
        name="in_projection",
    )(x2, mod_l, w_main, w_alpha, w_rwkv, lb_l, alpha_up_p, alpha_b, mu, w0, a0, wa_blk, k_k, k_a)


def _gl_kernel(q_s, k_s, v_s, lg_ref, gate_ref, ng_ref, ind_ref, out_ref,
               st_ref, b_s, ad_s, e_s, o_s, sc_s, eend_s):
    tb = q_s.shape[1]
    n_chunks = tb // CHUNK
    n_groups = tb // SUB

    @pl.when(pl.program_id(1) == 0)
    def _():
        st_ref[...] = jnp.zeros_like(st_ref)

    tril = _tril_bf16()
    for c in range(n_chunks):
        rows = slice(c * CHUNK, (c + 1) * CHUNK)
        b = _dot_exact_lhs(tril, lg_ref[rows, :], 2)
        for p in range(N_GL_PAIRS):
            b_s[p, rows, :] = b[:, p * PAIR:(p + 1) * PAIR]

    for t in range(SUB - 1):
        e_s[t * n_groups:(t + 1) * n_groups, (t + 1) * PAIR:] = jnp.zeros((n_groups, (SUB - 1 - t) * PAIR), BF16)
    for p in range(N_GL_PAIRS):
        by_pos = lambda ref, t: ref[p, pl.ds(t, n_groups, stride=SUB), :]
        qs = [by_pos(q_s, t) for t in range(SUB)]
        ks = [by_pos(k_s, t) for t in range(SUB)]
        bs = [by_pos(b_s, t) * LOG2E for t in range(SUB)]
        for t in range(SUB):
            for sg in range(t + 1):
                e = qs[t] * ks[sg]
                if sg < t:
                    e = e * jnp.exp2(jnp.minimum(bs[t] - bs[sg], 0.0))
                e_s[t * n_groups:(t + 1) * n_groups, sg * PAIR:(sg + 1) * PAIR] = e.astype(BF16)
        ad = [jnp.zeros((n_groups, PAIR), F32)] * SUB
        for sg0 in range(0, SUB, 2):
            part = _mm(e_s[sg0 * n_groups:, sg0 * PAIR:(sg0 + 2) * PAIR], ind_ref[sg0 * PAIR:(sg0 + 2) * PAIR, :])
            for t in range(sg0, SUB):
                ad[t] = ad[t] + part[(t - sg0) * n_groups:(t - sg0 + 1) * n_groups, :]
        for t in range(SUB):
            ad_s[p, pl.ds(t, n_groups, stride=SUB), :] = ad[t]

    masks = _pair_masks()
    t_idx = _iota((CHUNK, PAIR), 0)
    second_half = {hs: (t_idx // hs) % 2 == 1 for hs in LEVEL_HALVES}
    level_sign = {hs: jnp.where(second_half[hs], LOG2E, -LOG2E) for hs in LEVEL_HALVES}

    def group_units(i):
        units = []
        for cc in range(A_GROUP_GL):
            c = i * A_GROUP_GL + cc
            rows = pl.ds(pl.multiple_of(c * CHUNK, CHUNK), CHUNK)
            for p in range(N_GL_PAIRS):
                units.append(dict(p=p, rows=rows, lanes=slice(p * PAIR, (p + 1) * PAIR),
                                  e_rows=pl.ds(pl.multiple_of(c * 8, 8), 8),
                                  st_rows=pl.ds(pl.multiple_of((c * N_GL_PAIRS + p) * PAIR, PAIR), PAIR)))
        return units

    def increments(i, carry):
        units = group_units(i)
        for un in units:
            b = b_s[un["p"], un["rows"], :]
            b_end = b[CHUNK - 1:CHUNK, :]
            un["k_end"] = k_s[un["p"], un["rows"], :] * jnp.exp(b_end - b)
            eend_s[un["e_rows"], un["lanes"]] = jnp.broadcast_to(jnp.exp(b_end), (8, PAIR))
        for un in units:
            upd = _dot1(v_s[un["rows"], un["lanes"]], un["k_end"], TN)
            sc_s[un["st_rows"], :] = jnp.where(masks["bd"], upd, 0.0)
        return carry

    lax.fori_loop(0, n_chunks // A_GROUP_GL, increments, 0)

    sts = [st_ref[p] for p in range(N_GL_PAIRS)]
    for c in range(n_chunks):
        for p in range(N_GL_PAIRS):
            st_rows = slice((c * N_GL_PAIRS + p) * PAIR, (c * N_GL_PAIRS + p + 1) * PAIR)
            upd = sc_s[st_rows, :]
            sc_s[st_rows, :] = sts[p]
            sts[p] = sts[p] * eend_s[c * 8:c * 8 + 1, p * PAIR:(p + 1) * PAIR] + upd
    for p in range(N_GL_PAIRS):
        st_ref[p] = sts[p]

    def outputs(i, carry):
        units = group_units(i)

        def decays(un):
            q = q_s[un["p"], un["rows"], :]
            k = k_s[un["p"], un["rows"], :]
            b = b_s[un["p"], un["rows"], :]
            un["lv"] = []
            for hs in LEVEL_HALVES:
                e_mid = jnp.concatenate(
                    [jnp.broadcast_to(b[m:m + 1, :], (2 * hs, PAIR)) for m in range(hs - 1, CHUNK, 2 * hs)], axis=0)
                dec = jnp.exp2(jnp.minimum((b - e_mid) * level_sign[hs], 0.0))
                un["lv"].append(jnp.where(second_half[hs], q, k) * dec)
            un["q_in"] = q * jnp.exp(b)

        def scores(un):
            a = jnp.zeros((CHUNK, PAIR), F32)
            for hs, lv in zip(LEVEL_HALVES, un["lv"]):
                a = jnp.where(masks["level"][hs], _dot1(lv, jnp.transpose(_stack_heads(lv))), a)
            un["a"] = jnp.where(masks["diag"], ad_s[un["p"], un["rows"], :], a)

        def combine(un):
            o_s[un["rows"], un["lanes"]] = (_dot1(un["a"], _stack_heads(v_s[un["rows"], un["lanes"]]))
                                            + _dot1(un["q_in"], sc_s[un["st_rows"], :], NT))

        stages = (decays, scores, combine)
        for step in range(len(units) + OUT_SKEW * (len(stages) - 1)):
            for s, stage in enumerate(stages):
                u = step - OUT_SKEW * s
                if 0 <= u < len(units):
                    stage(units[u])
        return carry

    lax.fori_loop(0, n_chunks // A_GROUP_GL, outputs, 0)

    o = o_s[...]
    ms = _head_sums([o * o])[0] * (1.0 / HEAD_DIM)
    out_ref[...] = (o * lax.rsqrt(ms + RMS_EPS) * ng_ref[...] * gate_ref[...]).astype(out_ref.dtype)


def _gl_indicator():
    r = np.arange(SUB * PAIR)
    c = np.arange(PAIR)
    sg = r // PAIR
    h = (r % PAIR) // HEAD_DIM
    ind = (h[:, None] == (c // HEAD_DIM)[None, :]) & (sg[:, None] == (c % SUB)[None, :])
    return jnp.asarray(ind, dtype=BF16)


def _gl_mixer(q, k, v, lg, gate, norm_g, bsz, seq, tb):
    steps = seq // tb
    row = lambda b, i: (b * steps + i, 0)
    const = lambda b, i: (0, 0)
    slab = pl.BlockSpec((tb, W_GL), row)
    pair_slabs = pl.BlockSpec((N_GL_PAIRS, tb, PAIR), lambda b, i: (0, b * steps + i, 0))
    scratch_f32 = pltpu.VMEM((tb, W_GL), F32)
    scratch_pairs = pltpu.VMEM((N_GL_PAIRS, tb, PAIR), F32)
    return pl.pallas_call(
        _gl_kernel,
        grid=(bsz, steps),
        in_specs=[
            pair_slabs, pair_slabs, slab, slab, slab,
            pl.BlockSpec((1, W_GL), const),
            pl.BlockSpec((SUB * PAIR, PAIR), const),
        ],
        out_specs=pl.BlockSpec((tb, W_GL), row),
        out_shape=jax.ShapeDtypeStruct((bsz * seq, W_GL), BF16),
        scratch_shapes=[
            pltpu.VMEM((N_GL_PAIRS, PAIR, PAIR), F32),
            scratch_pairs, scratch_pairs,
            pltpu.VMEM((tb, SUB * PAIR), BF16),
            scratch_f32,
            pltpu.VMEM((tb // CHUNK * N_GL_PAIRS * PAIR, PAIR), F32),
            pltpu.VMEM((tb // CHUNK * 8, W_GL), F32),
        ],
        compiler_params=pltpu.CompilerParams(
            dimension_semantics=("arbitrary", "arbitrary"), vmem_limit_bytes=VMEM_LIMIT),
        name="hgrn_gla_mixer",
    )(q, k, v, lg, gate, norm_g, _gl_indicator())


def _rwkv_kernel(r_s, lw_s, k_s, v_s, kk_s, ka_s, hg_s, gup_ref, rk_ref, gng_ref, gnb_ref,
                 out_ref, st_ref, y_s, rt_s, bh_s, kh_s, nb_s, nk_s, w_s, u0_s, eend_s):
    bsz, tb = r_s.shape[0], r_s.shape[1]
    n_chunks = tb // CHUNK

    @pl.when(pl.program_id(0) == 0)
    def _():
        st_ref[...] = jnp.zeros_like(st_ref)

    masks = _pair_masks()
    tril = _tril_bf16()
    eye = jnp.where(masks["eye"], 1.0, 0.0)

    def phase_a(b, carry):
        units = []
        for c in range(n_chunks):
            rows = slice(c * CHUNK, (c + 1) * CHUNK)
            g_all = _dot_exact_lhs(tril, lw_s[b, rows, :], 2)
            for p in range(N_RW_PAIRS):
                lanes = slice(p * PAIR, (p + 1) * PAIR)
                units.append(dict(c=c, rows=rows, lanes=lanes, g=g_all[:, lanes]))
        def scale(un):
            rows, lanes, g = un["rows"], un["lanes"], un["g"]
            kc = k_s[b, rows, lanes]
            kkc = kk_s[b, rows, lanes]
            ka = ka_s[b, rows, lanes]
            g_end = g[CHUNK - 1:CHUNK, :]
            e_neg = jnp.exp(-g)
            e_end = jnp.exp(g_end - g)
            un["at"] = -kkc * jnp.exp(g - lw_s[b, rows, lanes])
            rt = r_s[b, rows, lanes] * jnp.exp(g)
            un["rt"] = rt
            un["bt"] = ka * e_neg
            un["kt"] = kc * e_neg
            rt_s[b, rows, lanes] = rt
            bh_s[b, rows, lanes] = ka * e_end
            kh_s[b, rows, lanes] = kc * e_end
            eend_s[b, un["c"] * 8:un["c"] * 8 + 8, lanes] = jnp.broadcast_to(jnp.exp(g_end), (8, PAIR))

        def gram(un):
            g2 = _dot1(jnp.concatenate([un["at"], un["rt"]], axis=0),
                       jnp.concatenate([_stack_heads(un["bt"]), _stack_heads(un["kt"])], axis=0), NT)
            un["l"] = jnp.where(masks["strict"], g2[0:CHUNK, 0:PAIR], 0.0)
            un["mk"] = jnp.where(masks["strict"], g2[0:CHUNK, PAIR:2 * PAIR], 0.0)
            nb_s[b, un["rows"], un["lanes"]] = jnp.where(masks["incl"], g2[CHUNK:2 * CHUNK, 0:PAIR], 0.0)
            nk_s[b, un["rows"], un["lanes"]] = jnp.where(masks["incl"], g2[CHUNK:2 * CHUNK, PAIR:2 * PAIR], 0.0)

        def inv_first(un):
            un["t"] = eye + un["l"]
            un["lp"] = _dot1(un["l"], _stack_heads(un["l"]))
            un["mv"] = _dot1(un["mk"], _stack_heads(v_s[b, un["rows"], un["lanes"]]))

        def inv_step(un):
            both = _dot1(jnp.concatenate([un["lp"], un["t"]], axis=0), _stack_heads(un["lp"]))
            un["lp"] = both[0:CHUNK, :]
            un["t"] = un["t"] + both[CHUNK:2 * CHUNK, :]

        def inv_last(un):
            un["t"] = un["t"] + _dot1(un["t"], _stack_heads(un["lp"]))

        def apply_inv(un):
            wu = _dot1(un["t"], jnp.concatenate([_stack_heads(un["at"]), _stack_heads(un["mv"])], axis=1))
            w_s[b, un["rows"], un["lanes"]] = wu[:, 0:PAIR]
            u0_s[b, un["rows"], un["lanes"]] = wu[:, PAIR:2 * PAIR]

        for stage in (scale, gram, inv_first, inv_step, inv_step, inv_step, inv_step, inv_last, apply_inv):
            for un in units:
                stage(un)
        return carry

    lax.fori_loop(0, bsz, phase_a, 0)

    def phase_b(c, carry):
        rows = pl.ds(pl.multiple_of(c * CHUNK, CHUNK), CHUNK)
        e_rows = pl.ds(pl.multiple_of(c * 8, 8), 8)
        seqs = [(b, p, slice(p * PAIR, (p + 1) * PAIR)) for b in range(bsz) for p in range(N_RW_PAIRS)]
        sts = [st_ref[b * N_RW_PAIRS + p] for b, p, _ in seqs]
        vs = [v_s[b, rows, lanes] for b, _, lanes in seqs]
        wrs = [_dot1(jnp.concatenate([w_s[b, rows, lanes], rt_s[b, rows, lanes]], axis=0), st, NT)
               for (b, _, lanes), st in zip(seqs, sts)]
        us = [wr[0:CHUNK, :] + u0_s[b, rows, lanes] for (b, _, lanes), wr in zip(seqs, wrs)]
        upds = [_dot1(jnp.concatenate([u, vc], axis=0),
                      jnp.concatenate([bh_s[b, rows, lanes], kh_s[b, rows, lanes]], axis=0), TN)
                for (b, _, lanes), u, vc in zip(seqs, us, vs)]
        for (b, p, lanes), st, upd in zip(seqs, sts, upds):
            e_end = eend_s[b, e_rows, lanes][0:1, :]
            st_ref[b * N_RW_PAIRS + p] = st * e_end + jnp.where(masks["bd"], upd, 0.0)
        for (b, _, lanes), wr, u, vc in zip(seqs, wrs, us, vs):
            y_s[b, rows, lanes] = (
                wr[CHUNK:2 * CHUNK, :]
                + _dot1(jnp.concatenate([nb_s[b, rows, lanes], nk_s[b, rows, lanes]], axis=1),
                        jnp.concatenate([_stack_heads(u), _stack_heads(vc)], axis=0)))
        return carry

    lax.fori_loop(0, n_chunks, phase_b, 0)

    for b in range(bsz):
        y = y_s[b]
        s_hi, s_lo = _head_sums(_bf16_parts(y, 2))
        d = y - (s_hi + s_lo) * (1.0 / HEAD_DIM)
        s_var, s_bonus = _head_sums([d * d, r_s[b] * k_s[b] * rk_ref[...]])
        yn = d * lax.rsqrt(s_var * (1.0 / HEAD_DIM) + RWKV_GN_EPS) * gng_ref[...] + gnb_ref[...]
        gate = _dot1(hg_s[b], gup_ref[...])
        out_ref[b] = ((yn + s_bonus * v_s[b]) * gate).astype(out_ref.dtype)


def _rwkv_mixer(r, lw, k, v, kk, ka, hg, g_up, r_k, gn_g, gn_b, bsz, seq, tb):
    blk = lambda i: (0, i, 0)
    const = lambda i: (0, 0)
    vec = pl.BlockSpec((1, W_RWKV), const)
    tok = pl.BlockSpec((bsz, tb, W_RWKV), blk)
    slab = pltpu.VMEM((bsz, tb, W_RWKV), F32)
    seq_major = lambda a: a.reshape(bsz, seq, a.shape[-1])
    out = pl.pallas_call(
        _rwkv_kernel,
        grid=(seq // tb,),
        in_specs=[
            tok, tok, tok, tok, tok, tok,
            pl.BlockSpec((bsz, tb, RWKV_GATE_RANK), blk),
            pl.BlockSpec((RWKV_GATE_RANK, W_RWKV), const),
            vec, vec, vec,
        ],
        out_specs=pl.BlockSpec((bsz, tb, W_RWKV), blk),
        out_shape=jax.ShapeDtypeStruct((bsz, seq, W_RWKV), BF16),
        scratch_shapes=[
            pltpu.VMEM((bsz * N_RW_PAIRS, PAIR, PAIR), F32),
            slab,
            slab, slab, slab, slab, slab, slab, slab,
            pltpu.VMEM((bsz, tb // CHUNK * 8, W_RWKV), F32),
        ],
        compiler_params=pltpu.CompilerParams(
            dimension_semantics=("arbitrary",), vmem_limit_bytes=VMEM_LIMIT),
        name="rwkv7_mixer",
    )(*[seq_major(a) for a in (r, lw, k, v, kk, ka, hg)], g_up, r_k, gn_g, gn_b)
    return out.reshape(bsz * seq, W_RWKV)


def _post_kernel(alpha, ff_chunk, x_ref, ogl_ref, orw_ref, mod_ref, wout_ref, ln1g_ref, ln1b_ref,
                 wup_ref, wdn_ref, ln2g_ref, ln2b_ref, out_ref):
    m = mod_ref[0]
    gate1, shift2, scale2, gate2 = m[2:3, :], m[3:4, :], m[4:5, :], m[5:6, :]
    d_ff = wup_ref.shape[1]
    n_ff = d_ff // ff_chunk
    half = x_ref.shape[0] // 2
    groups = [slice(0, half), slice(half, 2 * half)]

    def out_proj(rows):
        return (_mm(ogl_ref[rows, :], wout_ref[0:W_GL, :])
                + _mm(orw_ref[rows, :], wout_ref[W_GL:W_GL + W_RWKV, :]))

    def norm1(rows, o):
        x1 = _layer_norm(alpha * x_ref[rows, :] + (1.0 + gate1) * o, ln1g_ref[...], ln1b_ref[...])
        return x1, (x1 * (1.0 + scale2) + shift2).astype(BF16)

    def mlp(h, acc, js):
        for j in js:
            cols = slice(j * ff_chunk, (j + 1) * ff_chunk)
            u = jnp.maximum(_mm(h, wup_ref[:, cols]), 0.0)
            acc = acc + _mm((u * u).astype(BF16), wdn_ref[cols, :])
        return acc

    def norm2(rows, x1, acc):
        out_ref[rows, :] = _layer_norm(alpha * x1 + (1.0 + gate2) * acc, ln2g_ref[...], ln2b_ref[...])

    o_a, o_b = out_proj(groups[0]), out_proj(groups[1])
    x1_a, h_a = norm1(groups[0], o_a)
    acc_a = mlp(h_a, jnp.zeros(x1_a.shape, F32), range(0, 2))
    x1_b, h_b = norm1(groups[1], o_b)
    acc_a = mlp(h_a, acc_a, range(2, n_ff))
    acc_b = mlp(h_b, jnp.zeros(x1_b.shape, F32), range(0, 2))
    norm2(groups[0], x1_a, acc_a)
    acc_b = mlp(h_b, acc_b, range(2, n_ff))
    norm2(groups[1], x1_b, acc_b)


def _post_mixer(x2, ogl, orw, mod_l, w_out, ln1_g, ln1_b, w_up, w_dn, ln2_g, ln2_b, alpha, seq, tm, layer):
    m_rows, d = x2.shape
    d_ff = w_up.shape[2]
    steps_per_batch = seq // tm
    row = lambda i: (i, 0)
    const = lambda i: (0, 0)
    vec = pl.BlockSpec((1, d), const)
    resident = lambda shape: pl.BlockSpec((None,) + shape, lambda i: (layer, 0, 0), pipeline_mode=pl.Buffered(1))
    return pl.pallas_call(
        functools.partial(_post_kernel, alpha, 512),
        grid=(m_rows // tm,),
        in_specs=[
            pl.BlockSpec((tm, d), row),
            pl.BlockSpec((tm, W_GL), row),
            pl.BlockSpec((tm, W_RWKV), row),
            pl.BlockSpec((1, 6, d), lambda i: (i // steps_per_batch, 0, 0)),
            resident((W_GL + W_RWKV, d)),
            vec, vec,
            resident((d, d_ff)),
            resident((d_ff, d)),
            vec, vec,
        ],
        out_specs=pl.BlockSpec((tm, d), row),
        out_shape=jax.ShapeDtypeStruct((m_rows, d), F32),
        compiler_params=pltpu.CompilerParams(
            dimension_semantics=("arbitrary",), vmem_limit_bytes=VMEM_LIMIT),
        name="outproj_mlp",
    )(x2, ogl, orw, mod_l, w_out, ln1_g, ln1_b, w_up, w_dn, ln2_g, ln2_b)


def kernel(x, c, hgrn_lb_logits, ada_w, ada_b, w_in, hgrn_norm_g, gla_alpha_up, gla_alpha_b, gla_norm_g,
           rwkv_mu, rwkv_w0, rwkv_w_up, rwkv_a0, rwkv_a_up, rwkv_g_up, rwkv_k_k, rwkv_k_a, rwkv_r_k,
           rwkv_gn_g, rwkv_gn_b, w_out, ln1_g, ln1_b, mlp_w_up, mlp_w_down, ln2_g, ln2_b):
    bsz, seq, d = x.shape
    depth = w_in.shape[0]
    alpha = (2.0 * depth) ** 0.25
    tm = min(512, seq)
    tb = min(512, seq)
    assert seq % tm == 0 and seq % tb == 0 and tb % CHUNK == 0
    assert w_in.shape[2] == N_HGRN_COLS + N_GLA_COLS + N_RWKV_COLS

    mod = _modulation(c, ada_w, ada_b).reshape(depth, bsz, 6, d)
    lbs = _lower_bounds(hgrn_lb_logits)

    gla_end = N_HGRN_COLS + N_GLA_COLS
    w_in_b = w_in.astype(BF16)
    w_alpha = jnp.pad(w_in_b[:, :, gla_end - GLA_GATE_RANK:gla_end], ((0, 0), (0, 0), (0, LANES - GLA_GATE_RANK)))
    w_rwkv = w_in_b[:, :, gla_end:]
    alpha_up_p = jnp.concatenate(
        [gla_alpha_up, jnp.zeros((depth, LANES - GLA_GATE_RANK, W_GLA), gla_alpha_up.dtype)], axis=1)
    zeros_r = jnp.zeros((depth, RWKV_DECAY_RANK, W_RWKV), rwkv_w_up.dtype)
    wa_blk = jnp.concatenate(
        [jnp.concatenate([rwkv_w_up, zeros_r], axis=2), jnp.concatenate([zeros_r, rwkv_a_up], axis=2)], axis=1)
    norm_g = jnp.concatenate([hgrn_norm_g, gla_norm_g], axis=1)
    w_out_b = w_out.astype(BF16)
    w_up_b = mlp_w_up.astype(BF16)
    w_dn_b = mlp_w_down.astype(BF16)
    vec = lambda a, l: a[l].reshape(1, -1)

    x2 = x.reshape(bsz * seq, d)
    for l in range(depth):
        gq, gk, gv, glg, ggate, rr, rlw, rk, rv, rkk, rka, rhg = _in_projection(
            x2, mod[l], w_in_b, w_alpha, w_rwkv, vec(lbs, l), alpha_up_p[l], vec(gla_alpha_b, l), vec(rwkv_mu, l),
            vec(rwkv_w0, l), vec(rwkv_a0, l), wa_blk[l], vec(rwkv_k_k, l), vec(rwkv_k_a, l), seq, tm, l)
        ogl = _gl_mixer(gq, gk, gv, glg, ggate, vec(norm_g, l), bsz, seq, min(1024, seq))
        orw = _rwkv_mixer(rr, rlw, rk, rv, rkk, rka, rhg, rwkv_g_up[l], vec(rwkv_r_k, l), vec(rwkv_gn_g, l),
                          vec(rwkv_gn_b, l), bsz, seq, tb)
        x2 = _post_mixer(x2, ogl, orw, mod[l], w_out_b, vec(ln1_g, l), vec(ln1_b, l), w_up_b, w_dn_b,
                         vec(ln2_g, l), vec(ln2_b, l), alpha, seq, min(1024, seq), l)
    return x2.reshape(bsz, seq, d)
```

```python
import functools

import numpy as np
import jax
import jax.numpy as jnp
from jax import lax
from jax.experimental import pallas as pl
from jax.experimental.pallas import tpu as pltpu

F32 = jnp.float32
BF16 = jnp.bfloat16

HEAD_DIM = 64
H_HGRN, H_GLA, H_RWKV = 4, 6, 6
W_HGRN, W_GLA, W_RWKV = H_HGRN * HEAD_DIM, H_GLA * HEAD_DIM, H_RWKV * HEAD_DIM
GLA_GATE_RANK = 16
GLA_GATE_NORMALIZER = 16.0
RWKV_DECAY_RANK, RWKV_ICLR_RANK, RWKV_GATE_RANK = 64, 64, 128
RWKV_GN_EPS = 64e-5
N_HGRN_COLS = 4 * W_HGRN
N_GLA_COLS = 4 * W_GLA + GLA_GATE_RANK
N_RWKV_COLS = 3 * W_RWKV + RWKV_DECAY_RANK + RWKV_ICLR_RANK + RWKV_GATE_RANK
CHUNK = 64
LN_EPS = 1e-5
RMS_EPS = 1e-5
F_MIN = 1e-30

LANES = 128
PAIR = 2 * HEAD_DIM
SUB = 8
LEVEL_HALVES = (SUB, 2 * SUB, 4 * SUB)
assert 2 * LEVEL_HALVES[-1] == CHUNK
A_GROUP_GL = 4
OUT_SKEW = 4
LOG2E = 1.4426950408889634
VMEM_LIMIT = 56 * 1024 * 1024

W_GL = W_HGRN + W_GLA
N_GL_PAIRS = W_GL // PAIR
N_RW_PAIRS = W_RWKV // PAIR

NN = (((1,), (0,)), ((), ()))
NT = (((1,), (1,)), ((), ()))
TN = (((0,), (0,)), ((), ()))


def _mm(a, b, dims=NN):
    return lax.dot_general(a, b, dims, preferred_element_type=F32)


def _split2(x):
    hi = x.astype(BF16)
    lo = (x - hi.astype(F32)).astype(BF16)
    return hi, lo


def _bf16_parts(x, n):
    parts = []
    for i in range(n):
        p = x.astype(BF16)
        parts.append(p)
        if i + 1 < n:
            x = x - p.astype(F32)
    return parts


def _dot1(a, b, dims=NN):
    return _mm(a.astype(BF16), b.astype(BF16), dims)


def _dot3(a, b, dims=NN):
    ah, al = _split2(a)
    bh, bl = _split2(b)
    return _mm(ah, bh, dims) + (_mm(ah, bl, dims) + _mm(al, bh, dims))


def _dot_exact_lhs(a_bf16, b, n):
    return _mm(jnp.concatenate([a_bf16] * n, axis=1), jnp.concatenate(_bf16_parts(b, n), axis=0))


def _sigmoid(x):
    return 1.0 / (1.0 + jnp.exp(-x))


def _silu(x):
    return x * _sigmoid(x)


def _softplus(x):
    return jnp.maximum(x, 0.0) + jnp.log1p(jnp.exp(-jnp.abs(x)))


def _log_sigmoid(x):
    return -_softplus(-x)


def _layer_norm(y, g, b):
    mu = jnp.mean(y, axis=-1, keepdims=True)
    d = y - mu
    var = jnp.mean(d * d, axis=-1, keepdims=True)
    return d * lax.rsqrt(var + LN_EPS) * g + b


def _iota(shape, axis):
    return lax.broadcasted_iota(jnp.int32, shape, axis)


def _stack_heads(x):
    lane = _iota(x.shape, 1)
    return jnp.concatenate([jnp.where(lane < HEAD_DIM, x, 0.0), jnp.where(lane >= HEAD_DIM, x, 0.0)], axis=0)


def _pair_masks():
    t = _iota((CHUNK, PAIR), 0)
    s = _iota((CHUNK, PAIR), 1) % HEAD_DIM
    r2 = _iota((PAIR, PAIR), 0) // HEAD_DIM
    c2 = _iota((PAIR, PAIR), 1) // HEAD_DIM
    return dict(
        strict=s < t,
        incl=s <= t,
        eye=s == t,
        diag=(s // SUB == t // SUB) & (s <= t),
        level={hs: (s // (2 * hs) == t // (2 * hs)) & ((t // hs) % 2 == 1) & ((s // hs) % 2 == 0)
               for hs in LEVEL_HALVES},
        bd=r2 == c2,
    )


def _trace_skewed(stages, units, skew):
    for step in range(len(units) + skew * (len(stages) - 1)):
        for s, stage in enumerate(stages):
            u = step - skew * s
            if 0 <= u < len(units):
                stage(units[u])


def _tril_bf16():
    return jnp.where(_iota((CHUNK, CHUNK), 1) <= _iota((CHUNK, CHUNK), 0), 1.0, 0.0).astype(BF16)


def _seg_mean_matrix(width):
    r = _iota((width, width), 0) // HEAD_DIM
    c = _iota((width, width), 1) // HEAD_DIM
    return jnp.where(r == c, 1.0, 0.0).astype(BF16)


def _head_sums(xs):
    tiles = [x[:, j:j + LANES].astype(BF16) for x in xs for j in range(0, x.shape[1], LANES)]
    sums = []
    for i in range(0, len(tiles) - 1, 2):
        s = _mm(jnp.concatenate(tiles[i:i + 2], axis=1), _seg_mean_matrix(2 * LANES))
        sums += [s[:, 0:LANES], s[:, LANES:2 * LANES]]
    if len(tiles) % 2:
        sums.append(_mm(tiles[-1], _seg_mean_matrix(LANES)))
    out, i = [], 0
    for x in xs:
        n = x.shape[1] // LANES
        out.append(jnp.concatenate(sums[i:i + n], axis=1))
        i += n
    return out


def _mod_kernel(c_ref, w_ref, b_ref, o_ref):
    c = _silu(c_ref[...])
    o_ref[0] = _dot3(c, w_ref[0]) + b_ref[0]


def _modulation(c, ada_w, ada_b):
    depth, d, n = ada_w.shape
    bsz = c.shape[0]
    tn = 1536
    return pl.pallas_call(
        _mod_kernel,
        grid=(depth, n // tn),
        in_specs=[
            pl.BlockSpec((bsz, d), lambda l, j: (0, 0)),
            pl.BlockSpec((1, d, tn), lambda l, j: (l, 0, j)),
            pl.BlockSpec((1, 1, tn), lambda l, j: (l, 0, j)),
        ],
        out_specs=pl.BlockSpec((1, bsz, tn), lambda l, j: (l, 0, j)),
        out_shape=jax.ShapeDtypeStruct((depth, bsz, n), F32),
        compiler_params=pltpu.CompilerParams(
            dimension_semantics=("arbitrary", "arbitrary"), vmem_limit_bytes=VMEM_LIMIT),
        name="adaln_modulation",
    )(c, ada_w, ada_b.reshape(depth, 1, n))


def _lower_bound_kernel(x_ref, o_ref):
    depth = x_ref.shape[0]
    rows = [x_ref[l:l + 1, :] for l in range(depth)]
    m = functools.reduce(jnp.maximum, rows)
    e = [jnp.exp(r - m) for r in rows]
    tot = functools.reduce(lambda a, b: a + b, e)
    p = [ei / tot for ei in e]
    acc = jnp.zeros_like(p[0])
    for l in range(depth):
        acc = acc + p[l]
        o_ref[l:l + 1, :] = acc - p[0]


def _lower_bounds(logits):
    return pl.pallas_call(
        _lower_bound_kernel,
        out_shape=jax.ShapeDtypeStruct(logits.shape, F32),
        name="hgrn_lower_bounds",
    )(logits.astype(F32))


def _inproj_kernel(steps_per_batch, x_ref, mod_ref, w_ref, wal_ref, wr_ref, lb_ref, au_ref, ab_ref, mu_ref,
                   w0_ref, a0_ref, wa_ref, kk_ref, ka_ref,
                   gq_ref, gk_ref, gv_ref, gb_ref, gg_ref,
                   rr_ref, rlw_ref, rk_ref, rv_ref, rkk_ref, rka_ref, rhg_ref, carry_s):
    tm = x_ref.shape[0]
    w3 = W_RWKV

    @pl.when(pl.program_id(0) % steps_per_batch == 0)
    def _():
        carry_s[...] = jnp.zeros_like(carry_s)

    m = mod_ref[0]
    h = (x_ref[...] * (1.0 + m[1:2, :]) + m[0:1, :]).astype(BF16)

    def proj(c0, width):
        return _mm(h, w_ref[:, c0:c0 + width])

    first_row = _iota((tm, LANES), 0) == 0

    def shifted(c0, width):
        z = _mm(h, wr_ref[:, c0:c0 + width])
        z_prev = pltpu.roll(z, 1, axis=0)
        z_prev = jnp.where(jnp.concatenate([first_row] * (width // LANES), axis=1), carry_s[:, c0:c0 + width], z_prev)
        carry_s[:, c0:c0 + width] = z[tm - 1:tm, :]
        return z + (z_prev - z) * mu_ref[:, c0:c0 + width]

    g0 = N_HGRN_COLS
    n_rp = w3 // PAIR
    tile = lambda j: slice(j * PAIR, (j + 1) * PAIR)
    z_v = shifted(2 * w3, w3 + LANES)
    z_rk = shifted(0, 2 * w3)

    h_wa = z_v[:, w3:w3 + LANES]
    h_wa = jnp.where(_iota(h_wa.shape, 1) < RWKV_DECAY_RANK, jnp.tanh(h_wa), h_wa)
    wa = _dot1(h_wa, wa_ref[...])
    rv_ref[...] = z_v[:, 0:w3]
    rr_ref[...] = z_rk[:, 0:w3]

    later = [lambda: proj(W_HGRN, W_HGRN), lambda: proj(0, W_HGRN),
             lambda: shifted(3 * w3 + LANES, RWKV_GATE_RANK)]
    issued, a_tiles = [], []
    for j in range(n_rp):
        issued.append(later[j]())
        w_log = -_softplus(-(w0_ref[:, tile(j)] + wa[:, tile(j)])) - 0.5
        rlw_ref[:, tile(j)] = -jnp.exp(w_log)
        a_tiles.append(_sigmoid(a0_ref[:, tile(j)] + wa[:, w3 + j * PAIR:w3 + (j + 1) * PAIR]))
    zf, zq, z_hg = issued

    z_al = _mm(h, wal_ref[...])
    k = z_rk[:, w3:2 * w3]
    kk = k * kk_ref[...]
    kk = kk / jnp.maximum(jnp.sqrt(_head_sums([kk * kk])[0]), 1e-12)
    rkk_ref[...] = kk

    z_qk = proj(g0, 2 * W_GLA)
    for j in range(n_rp):
        rka_ref[:, tile(j)] = kk[:, tile(j)] * a_tiles[j]
        rk_ref[:, tile(j)] = k[:, tile(j)] * (1.0 + (a_tiles[j] - 1.0) * ka_ref[:, tile(j)])
    lb = lb_ref[...]
    f = lb + (1.0 - lb) * _sigmoid(zf)
    gb_ref[:, 0:W_HGRN] = jnp.log(jnp.maximum(f, F_MIN))
    k_h = (1.0 - lb) * _sigmoid(-zf)
    q_h = _silu(zq) * HEAD_DIM ** -0.5
    for p in range(W_HGRN // PAIR):
        gk_ref[p] = k_h[:, tile(p)]
        gq_ref[p] = q_h[:, tile(p)]

    z_vr = proj(g0 + 2 * W_GLA, 2 * W_GLA)
    rhg_ref[...] = _sigmoid(z_hg).astype(BF16)
    logit = _dot1(z_al, au_ref[...]) + ab_ref[...]
    gb_ref[:, W_HGRN:W_GL] = _log_sigmoid(logit) / GLA_GATE_NORMALIZER
    for p in range(W_GLA // PAIR):
        gq_ref[W_HGRN // PAIR + p] = z_qk[:, tile(p)] * HEAD_DIM ** -0.5
        gk_ref[W_HGRN // PAIR + p] = z_qk[:, W_GLA + p * PAIR:W_GLA + (p + 1) * PAIR]

    zg = proj(3 * W_HGRN, W_HGRN)
    gv_ref[:, W_HGRN:W_GL] = z_vr[:, 0:W_GLA].astype(BF16)
    gg_ref[:, W_HGRN:W_GL] = _silu(z_vr[:, W_GLA:2 * W_GLA]).astype(BF16)
    zi = proj(2 * W_HGRN, W_HGRN)
    gg_ref[:, 0:W_HGRN] = _silu(zg).astype(BF16)
    gv_ref[:, 0:W_HGRN] = zi.astype(BF16)


def _in_projection(x2, mod_l, w_main, w_alpha, w_rwkv, lb_l, alpha_up_p, alpha_b, mu, w0, a0, wa_blk, k_k, k_a,
                   seq, tm, layer):
    m_rows, d = x2.shape
    resident = lambda n: pl.BlockSpec((None, d, n), lambda i: (layer, 0, 0), pipeline_mode=pl.Buffered(1))
    steps_per_batch = seq // tm
    row = lambda i: (i, 0)
    const = lambda i: (0, 0)
    vec = lambda n: pl.BlockSpec((1, n), const)
    out = lambda n, dt: (pl.BlockSpec((tm, n), row), jax.ShapeDtypeStruct((m_rows, n), dt))
    pair_slabs = (pl.BlockSpec((N_GL_PAIRS, tm, PAIR), lambda i: (0, i, 0)),
                  jax.ShapeDtypeStruct((N_GL_PAIRS, m_rows, PAIR), F32))
    outs = [pair_slabs, pair_slabs, out(W_GL, BF16), out(W_GL, F32), out(W_GL, BF16),
            out(W_RWKV, F32), out(W_RWKV, F32), out(W_RWKV, F32), out(W_RWKV, F32), out(W_RWKV, F32),
            out(W_RWKV, F32), out(RWKV_GATE_RANK, BF16)]
    return pl.pallas_call(
        functools.partial(_inproj_kernel, steps_per_batch),
        grid=(m_rows // tm,),
        in_specs=[
            pl.BlockSpec((tm, d), row),
            pl.BlockSpec((1, 6, d), lambda i: (i // steps_per_batch, 0, 0)),
            resident(N_HGRN_COLS + 4 * W_GLA), resident(LANES), resident(N_RWKV_COLS),
            vec(W_HGRN),
            pl.BlockSpec((LANES, W_GLA), const),
            vec(W_GLA),
            vec(N_RWKV_COLS), vec(W_RWKV), vec(W_RWKV),
            pl.BlockSpec((LANES, 2 * W_RWKV), const),
            vec(W_RWKV), vec(W_RWKV),
        ],
        out_specs=[o[0] for o in outs],
        out_shape=[o[1] for o in outs],
        scratch_shapes=[pltpu.VMEM((1, N_RWKV_COLS), F32)],
        compiler_params=pltpu.CompilerParams(
            dimension_semantics=("arbitrary",), vmem_limit_bytes=VMEM_LIMIT),
        name="in_projection",
    )(x2, mod_l, w_main, w_alpha, w_rwkv, lb_l, alpha_up_p, alpha_b, mu, w0, a0, wa_blk, k_k, k_a)


def _gl_kernel(q_s, k_s, v_s, lg_ref, gate_ref, ng_ref, ind_ref, out_ref,
               st_ref, b_s, ad_s, e_s, o_s, sc_s, eend_s):
    tb = q_s.shape[1]
    n_chunks = tb // CHUNK
    n_groups = tb // SUB

    @pl.when(pl.program_id(1) == 0)
    def _():
        st_ref[...] = jnp.zeros_like(st_ref)

    tril = _tril_bf16()
    for c in range(n_chunks):
        rows = slice(c * CHUNK, (c + 1) * CHUNK)
        b = _dot_exact_lhs(tril, lg_ref[rows, :], 2)
        for p in range(N_GL_PAIRS):
            b_s[p, rows, :] = b[:, p * PAIR:(p + 1) * PAIR]

    for t in range(SUB - 1):
        e_s[t * n_groups:(t + 1) * n_groups, (t + 1) * PAIR:] = jnp.zeros((n_groups, (SUB - 1 - t) * PAIR), BF16)
    for p in range(N_GL_PAIRS):
        by_pos = lambda ref, t: ref[p, pl.ds(t, n_groups, stride=SUB), :]
        qs = [by_pos(q_s, t) for t in range(SUB)]
        ks = [by_pos(k_s, t) for t in range(SUB)]
        bs = [by_pos(b_s, t) * LOG2E for t in range(SUB)]
        for t in range(SUB):
            for sg in range(t + 1):
                e = qs[t] * ks[sg]
                if sg < t:
                    e = e * jnp.exp2(jnp.minimum(bs[t] - bs[sg], 0.0))
                e_s[t * n_groups:(t + 1) * n_groups, sg * PAIR:(sg + 1) * PAIR] = e.astype(BF16)
        ad = [jnp.zeros((n_groups, PAIR), F32)] * SUB
        for sg0 in range(0, SUB, 2):
            part = _mm(e_s[sg0 * n_groups:, sg0 * PAIR:(sg0 + 2) * PAIR], ind_ref[sg0 * PAIR:(sg0 + 2) * PAIR, :])
            for t in range(sg0, SUB):
                ad[t] = ad[t] + part[(t - sg0) * n_groups:(t - sg0 + 1) * n_groups, :]
        for t in range(SUB):
            ad_s[p, pl.ds(t, n_groups, stride=SUB), :] = ad[t]

    masks = _pair_masks()
    t_idx = _iota((CHUNK, PAIR), 0)
    second_half = {hs: (t_idx // hs) % 2 == 1 for hs in LEVEL_HALVES}
    level_sign = {hs: jnp.where(second_half[hs], LOG2E, -LOG2E) for hs in LEVEL_HALVES}

    def group_units(i):
        units = []
        for cc in range(A_GROUP_GL):
            c = i * A_GROUP_GL + cc
            rows = pl.ds(pl.multiple_of(c * CHUNK, CHUNK), CHUNK)
            for p in range(N_GL_PAIRS):
                units.append(dict(p=p, rows=rows, lanes=slice(p * PAIR, (p + 1) * PAIR),
                                  e_rows=pl.ds(pl.multiple_of(c * 8, 8), 8),
                                  st_rows=pl.ds(pl.multiple_of((c * N_GL_PAIRS + p) * PAIR, PAIR), PAIR)))
        return units

    def increments(i, carry):
        units = group_units(i)

        def decay_keys(un):
            b = b_s[un["p"], un["rows"], :]
            b_end = b[CHUNK - 1:CHUNK, :]
            un["k_end"] = k_s[un["p"], un["rows"], :] * jnp.exp(b_end - b)
            eend_s[un["e_rows"], un["lanes"]] = jnp.broadcast_to(jnp.exp(b_end), (8, PAIR))

        def outer(un):
            upd = _dot1(v_s[un["rows"], un["lanes"]], un["k_end"], TN)
            sc_s[un["st_rows"], :] = jnp.where(masks["bd"], upd, 0.0)

        _trace_skewed((decay_keys, outer), units, OUT_SKEW)
        return carry

    lax.fori_loop(0, n_chunks // A_GROUP_GL, increments, 0)

    sts = [st_ref[p] for p in range(N_GL_PAIRS)]
    for c in range(n_chunks):
        for p in range(N_GL_PAIRS):
            st_rows = slice((c * N_GL_PAIRS + p) * PAIR, (c * N_GL_PAIRS + p + 1) * PAIR)
            upd = sc_s[st_rows, :]
            sc_s[st_rows, :] = sts[p]
            sts[p] = sts[p] * eend_s[c * 8:c * 8 + 1, p * PAIR:(p + 1) * PAIR] + upd
    for p in range(N_GL_PAIRS):
        st_ref[p] = sts[p]

    def outputs(i, carry):
        units = group_units(i)

        def decays(un):
            q = q_s[un["p"], un["rows"], :]
            k = k_s[un["p"], un["rows"], :]
            b = b_s[un["p"], un["rows"], :]
            un["lv"] = []
            for hs in LEVEL_HALVES:
                e_mid = jnp.concatenate(
                    [jnp.broadcast_to(b[m:m + 1, :], (2 * hs, PAIR)) for m in range(hs - 1, CHUNK, 2 * hs)], axis=0)
                dec = jnp.exp2(jnp.minimum((b - e_mid) * level_sign[hs], 0.0))
                un["lv"].append(jnp.where(second_half[hs], q, k) * dec)
            un["q_in"] = q * jnp.exp(b)

        def scores(un):
            a = jnp.zeros((CHUNK, PAIR), F32)
            for hs, lv in zip(LEVEL_HALVES, un["lv"]):
                a = jnp.where(masks["level"][hs], _dot1(lv, jnp.transpose(_stack_heads(lv))), a)
            un["a"] = jnp.where(masks["diag"], ad_s[un["p"], un["rows"], :], a)

        def combine(un):
            o_s[un["rows"], un["lanes"]] = (_dot1(un["a"], _stack_heads(v_s[un["rows"], un["lanes"]]))
                                            + _dot1(un["q_in"], sc_s[un["st_rows"], :], NT))

        _trace_skewed((decays, scores, combine), units, OUT_SKEW)
        return carry

    lax.fori_loop(0, n_chunks // A_GROUP_GL, outputs, 0)

    o = o_s[...]
    ms = _head_sums([o * o])[0] * (1.0 / HEAD_DIM)
    out_ref[...] = (o * lax.rsqrt(ms + RMS_EPS) * ng_ref[...] * gate_ref[...]).astype(out_ref.dtype)


def _gl_indicator():
    r = np.arange(SUB * PAIR)
    c = np.arange(PAIR)
    sg = r // PAIR
    h = (r % PAIR) // HEAD_DIM
    ind = (h[:, None] == (c // HEAD_DIM)[None, :]) & (sg[:, None] == (c % SUB)[None, :])
    return jnp.asarray(ind, dtype=BF16)


def _gl_mixer(q, k, v, lg, gate, norm_g, bsz, seq, tb):
    steps = seq // tb
    row = lambda b, i: (b * steps + i, 0)
    const = lambda b, i: (0, 0)
    slab = pl.BlockSpec((tb, W_GL), row)
    pair_slabs = pl.BlockSpec((N_GL_PAIRS, tb, PAIR), lambda b, i: (0, b * steps + i, 0))
    scratch_f32 = pltpu.VMEM((tb, W_GL), F32)
    scratch_pairs = pltpu.VMEM((N_GL_PAIRS, tb, PAIR), F32)
    return pl.pallas_call(
        _gl_kernel,
        grid=(bsz, steps),
        in_specs=[
            pair_slabs, pair_slabs, slab, slab, slab,
            pl.BlockSpec((1, W_GL), const),
            pl.BlockSpec((SUB * PAIR, PAIR), const),
        ],
        out_specs=pl.BlockSpec((tb, W_GL), row),
        out_shape=jax.ShapeDtypeStruct((bsz * seq, W_GL), BF16),
        scratch_shapes=[
            pltpu.VMEM((N_GL_PAIRS, PAIR, PAIR), F32),
            scratch_pairs, scratch_pairs,
            pltpu.VMEM((tb, SUB * PAIR), BF16),
            scratch_f32,
            pltpu.VMEM((tb // CHUNK * N_GL_PAIRS * PAIR, PAIR), F32),
            pltpu.VMEM((tb // CHUNK * 8, W_GL), F32),
        ],
        compiler_params=pltpu.CompilerParams(
            dimension_semantics=("arbitrary", "arbitrary"), vmem_limit_bytes=VMEM_LIMIT),
        name="hgrn_gla_mixer",
    )(q, k, v, lg, gate, norm_g, _gl_indicator())


def _rwkv_kernel(r_s, lw_s, k_s, v_s, kk_s, ka_s, hg_s, gup_ref, rk_ref, gng_ref, gnb_ref,
                 out_ref, st_ref, y_s, rt_s, bh_s, kh_s, nb_s, nk_s, w_s, u0_s, eend_s):
    bsz, tb = r_s.shape[0], r_s.shape[1]
    n_chunks = tb // CHUNK

    @pl.when(pl.program_id(0) == 0)
    def _():
        st_ref[...] = jnp.zeros_like(st_ref)

    masks = _pair_masks()
    tril = _tril_bf16()
    eye = jnp.where(masks["eye"], 1.0, 0.0)

    def phase_a(b, carry):
        units = []
        for c in range(n_chunks):
            rows = slice(c * CHUNK, (c + 1) * CHUNK)
            g_all = _dot_exact_lhs(tril, lw_s[b, rows, :], 2)
            for p in range(N_RW_PAIRS):
                lanes = slice(p * PAIR, (p + 1) * PAIR)
                units.append(dict(c=c, rows=rows, lanes=lanes, g=g_all[:, lanes]))
        def scale(un):
            rows, lanes, g = un["rows"], un["lanes"], un["g"]
            kc = k_s[b, rows, lanes]
            kkc = kk_s[b, rows, lanes]
            ka = ka_s[b, rows, lanes]
            g_end = g[CHUNK - 1:CHUNK, :]
            e_neg = jnp.exp(-g)
            e_end = jnp.exp(g_end - g)
            un["at"] = -kkc * jnp.exp(g - lw_s[b, rows, lanes])
            rt = r_s[b, rows, lanes] * jnp.exp(g)
            un["rt"] = rt
            un["bt"] = ka * e_neg
            un["kt"] = kc * e_neg
            rt_s[b, rows, lanes] = rt
            bh_s[b, rows, lanes] = ka * e_end
            kh_s[b, rows, lanes] = kc * e_end
            eend_s[b, un["c"] * 8:un["c"] * 8 + 8, lanes] = jnp.broadcast_to(jnp.exp(g_end), (8, PAIR))

        def gram(un):
            g2 = _dot1(jnp.concatenate([un["at"], un["rt"]], axis=0),
                       jnp.concatenate([_stack_heads(un["bt"]), _stack_heads(un["kt"])], axis=0), NT)
            un["l"] = jnp.where(masks["strict"], g2[0:CHUNK, 0:PAIR], 0.0)
            un["mk"] = jnp.where(masks["strict"], g2[0:CHUNK, PAIR:2 * PAIR], 0.0)
            nb_s[b, un["rows"], un["lanes"]] = jnp.where(masks["incl"], g2[CHUNK:2 * CHUNK, 0:PAIR], 0.0)
            nk_s[b, un["rows"], un["lanes"]] = jnp.where(masks["incl"], g2[CHUNK:2 * CHUNK, PAIR:2 * PAIR], 0.0)

        def inv_first(un):
            un["t"] = eye + un["l"]
            un["lp"] = _dot1(un["l"], _stack_heads(un["l"]))
            un["mv"] = _dot1(un["mk"], _stack_heads(v_s[b, un["rows"], un["lanes"]]))

        def inv_step(un):
            both = _dot1(jnp.concatenate([un["lp"], un["t"]], axis=0), _stack_heads(un["lp"]))
            un["lp"] = both[0:CHUNK, :]
            un["t"] = un["t"] + both[CHUNK:2 * CHUNK, :]

        def inv_last(un):
            un["t"] = un["t"] + _dot1(un["t"], _stack_heads(un["lp"]))

        def apply_inv(un):
            wu = _dot1(un["t"], jnp.concatenate([_stack_heads(un["at"]), _stack_heads(un["mv"])], axis=1))
            w_s[b, un["rows"], un["lanes"]] = wu[:, 0:PAIR]
            u0_s[b, un["rows"], un["lanes"]] = wu[:, PAIR:2 * PAIR]

        for stage in (scale, gram, inv_first, inv_step, inv_step, inv_step, inv_step, inv_last, apply_inv):
            for un in units:
                stage(un)
        return carry

    lax.fori_loop(0, bsz, phase_a, 0)

    def phase_b(c, carry):
        rows = pl.ds(pl.multiple_of(c * CHUNK, CHUNK), CHUNK)
        e_rows = pl.ds(pl.multiple_of(c * 8, 8), 8)
        seqs = [(b, p, slice(p * PAIR, (p + 1) * PAIR)) for b in range(bsz) for p in range(N_RW_PAIRS)]
        sts = [st_ref[b * N_RW_PAIRS + p] for b, p, _ in seqs]
        vs = [v_s[b, rows, lanes] for b, _, lanes in seqs]
        wrs = [_dot1(jnp.concatenate([w_s[b, rows, lanes], rt_s[b, rows, lanes]], axis=0), st, NT)
               for (b, _, lanes), st in zip(seqs, sts)]
        us = [wr[0:CHUNK, :] + u0_s[b, rows, lanes] for (b, _, lanes), wr in zip(seqs, wrs)]
        upds = [_dot1(jnp.concatenate([u, vc], axis=0),
                      jnp.concatenate([bh_s[b, rows, lanes], kh_s[b, rows, lanes]], axis=0), TN)
                for (b, _, lanes), u, vc in zip(seqs, us, vs)]
        for (b, p, lanes), st, upd in zip(seqs, sts, upds):
            e_end = eend_s[b, e_rows, lanes][0:1, :]
            st_ref[b * N_RW_PAIRS + p] = st * e_end + jnp.where(masks["bd"], upd, 0.0)
        for (b, _, lanes), wr, u, vc in zip(seqs, wrs, us, vs):
            y_s[b, rows, lanes] = (
                wr[CHUNK:2 * CHUNK, :]
                + _dot1(jnp.concatenate([nb_s[b, rows, lanes], nk_s[b, rows, lanes]], axis=1),
                        jnp.concatenate([_stack_heads(u), _stack_heads(vc)], axis=0)))
        return carry

    lax.fori_loop(0, n_chunks, phase_b, 0)

    for b in range(bsz):
        y = y_s[b]
        s_hi, s_lo = _head_sums(_bf16_parts(y, 2))
        d = y - (s_hi + s_lo) * (1.0 / HEAD_DIM)
        s_var, s_bonus = _head_sums([d * d, r_s[b] * k_s[b] * rk_ref[...]])
        yn = d * lax.rsqrt(s_var * (1.0 / HEAD_DIM) + RWKV_GN_EPS) * gng_ref[...] + gnb_ref[...]
        gate = _dot1(hg_s[b], gup_ref[...])
        out_ref[b] = ((yn + s_bonus * v_s[b]) * gate).astype(out_ref.dtype)


def _rwkv_mixer(r, lw, k, v, kk, ka, hg, g_up, r_k, gn_g, gn_b, bsz, seq, tb):
    blk = lambda i: (0, i, 0)
    const = lambda i: (0, 0)
    vec = pl.BlockSpec((1, W_RWKV), const)
    tok = pl.BlockSpec((bsz, tb, W_RWKV), blk)
    slab = pltpu.VMEM((bsz, tb, W_RWKV), F32)
    seq_major = lambda a: a.reshape(bsz, seq, a.shape[-1])
    out = pl.pallas_call(
        _rwkv_kernel,
        grid=(seq // tb,),
        in_specs=[
            tok, tok, tok, tok, tok, tok,
            pl.BlockSpec((bsz, tb, RWKV_GATE_RANK), blk),
            pl.BlockSpec((RWKV_GATE_RANK, W_RWKV), const),
            vec, vec, vec,
        ],
        out_specs=pl.BlockSpec((bsz, tb, W_RWKV), blk),
        out_shape=jax.ShapeDtypeStruct((bsz, seq, W_RWKV), BF16),
        scratch_shapes=[
            pltpu.VMEM((bsz * N_RW_PAIRS, PAIR, PAIR), F32),
            slab,
            slab, slab, slab, slab, slab, slab, slab,
            pltpu.VMEM((bsz, tb // CHUNK * 8, W_RWKV), F32),
        ],
        compiler_params=pltpu.CompilerParams(
            dimension_semantics=("arbitrary",), vmem_limit_bytes=VMEM_LIMIT),
        name="rwkv7_mixer",
    )(*[seq_major(a) for a in (r, lw, k, v, kk, ka, hg)], g_up, r_k, gn_g, gn_b)
    return out.reshape(bsz * seq, W_RWKV)


def _post_kernel(alpha, ff_chunk, x_ref, ogl_ref, orw_ref, mod_ref, wout_ref, ln1g_ref, ln1b_ref,
                 wup_ref, wdn_ref, ln2g_ref, ln2b_ref, out_ref):
    m = mod_ref[0]
    gate1, shift2, scale2, gate2 = m[2:3, :], m[3:4, :], m[4:5, :], m[5:6, :]
    d_ff = wup_ref.shape[1]
    n_ff = d_ff // ff_chunk
    half = x_ref.shape[0] // 2
    groups = [slice(0, half), slice(half, 2 * half)]

    def out_proj(rows):
        return (_mm(ogl_ref[rows, :], wout_ref[0:W_GL, :])
                + _mm(orw_ref[rows, :], wout_ref[W_GL:W_GL + W_RWKV, :]))

    def norm1(rows, o):
        x1 = _layer_norm(alpha * x_ref[rows, :] + (1.0 + gate1) * o, ln1g_ref[...], ln1b_ref[...])
        return x1, (x1 * (1.0 + scale2) + shift2).astype(BF16)

    def mlp(h, acc, js):
        for j in js:
            cols = slice(j * ff_chunk, (j + 1) * ff_chunk)
            u = jnp.maximum(_mm(h, wup_ref[:, cols]), 0.0)
            acc = acc + _mm((u * u).astype(BF16), wdn_ref[cols, :])
        return acc

    def norm2(rows, x1, acc):
        out_ref[rows, :] = _layer_norm(alpha * x1 + (1.0 + gate2) * acc, ln2g_ref[...], ln2b_ref[...])

    o_a, o_b = out_proj(groups[0]), out_proj(groups[1])
    x1_a, h_a = norm1(groups[0], o_a)
    acc_a = mlp(h_a, jnp.zeros(x1_a.shape, F32), range(0, 2))
    x1_b, h_b = norm1(groups[1], o_b)
    acc_a = mlp(h_a, acc_a, range(2, n_ff))
    acc_b = mlp(h_b, jnp.zeros(x1_b.shape, F32), range(0, 2))
    norm2(groups[0], x1_a, acc_a)
    acc_b = mlp(h_b, acc_b, range(2, n_ff))
    norm2(groups[1], x1_b, acc_b)


def _post_mixer(x2, ogl, orw, mod_l, w_out, ln1_g, ln1_b, w_up, w_dn, ln2_g, ln2_b, alpha, seq, tm, layer):
    m_rows, d = x2.shape
    d_ff = w_up.shape[2]
    steps_per_batch = seq // tm
    row = lambda i: (i, 0)
    const = lambda i: (0, 0)
    vec = pl.BlockSpec((1, d), const)
    resident = lambda shape: pl.BlockSpec((None,) + shape, lambda i: (layer, 0, 0), pipeline_mode=pl.Buffered(1))
    return pl.pallas_call(
        functools.partial(_post_kernel, alpha, 512),
        grid=(m_rows // tm,),
        in_specs=[
            pl.BlockSpec((tm, d), row),
            pl.BlockSpec((tm, W_GL), row),
            pl.BlockSpec((tm, W_RWKV), row),
            pl.BlockSpec((1, 6, d), lambda i: (i // steps_per_batch, 0, 0)),
            resident((W_GL + W_RWKV, d)),
            vec, vec,
            resident((d, d_ff)),
            resident((d_ff, d)),
            vec, vec,
        ],
        out_specs=pl.BlockSpec((tm, d), row),
        out_shape=jax.ShapeDtypeStruct((m_rows, d), F32),
        compiler_params=pltpu.CompilerParams(
            dimension_semantics=("arbitrary",), vmem_limit_bytes=VMEM_LIMIT),
        name="outproj_mlp",
    )(x2, ogl, orw, mod_l, w_out, ln1_g, ln1_b, w_up, w_dn, ln2_g, ln2_b)


def kernel(x, c, hgrn_lb_logits, ada_w, ada_b, w_in, hgrn_norm_g, gla_alpha_up, gla_alpha_b, gla_norm_g,
           rwkv_mu, rwkv_w0, rwkv_w_up, rwkv_a0, rwkv_a_up, rwkv_g_up, rwkv_k_k, rwkv_k_a, rwkv_r_k,
           rwkv_gn_g, rwkv_gn_b, w_out, ln1_g, ln1_b, mlp_w_up, mlp_w_down, ln2_g, ln2_b):
    bsz, seq, d = x.shape
    depth = w_in.shape[0]
    alpha = (2.0 * depth) ** 0.25
    tm = min(512, seq)
    tb = min(512, seq)
    tb_gl = min(1024, seq)
    tm_post = min(1024, seq)
    assert all(seq % t == 0 for t in (tm, tb, tb_gl, tm_post)) and tb % CHUNK == 0
    assert tb_gl % (A_GROUP_GL * CHUNK) == 0 and tm_post % 16 == 0
    assert w_in.shape[2] == N_HGRN_COLS + N_GLA_COLS + N_RWKV_COLS

    mod = _modulation(c, ada_w, ada_b).reshape(depth, bsz, 6, d)
    lbs = _lower_bounds(hgrn_lb_logits)

    gla_end = N_HGRN_COLS + N_GLA_COLS
    w_in_b = w_in.astype(BF16)
    w_alpha = jnp.pad(w_in_b[:, :, gla_end - GLA_GATE_RANK:gla_end], ((0, 0), (0, 0), (0, LANES - GLA_GATE_RANK)))
    w_rwkv = w_in_b[:, :, gla_end:]
    alpha_up_p = jnp.concatenate(
        [gla_alpha_up, jnp.zeros((depth, LANES - GLA_GATE_RANK, W_GLA), gla_alpha_up.dtype)], axis=1)
    zeros_r = jnp.zeros((depth, RWKV_DECAY_RANK, W_RWKV), rwkv_w_up.dtype)
    wa_blk = jnp.concatenate(
        [jnp.concatenate([rwkv_w_up, zeros_r], axis=2), jnp.concatenate([zeros_r, rwkv_a_up], axis=2)], axis=1)
    norm_g = jnp.concatenate([hgrn_norm_g, gla_norm_g], axis=1)
    w_out_b = w_out.astype(BF16)
    w_up_b = mlp_w_up.astype(BF16)
    w_dn_b = mlp_w_down.astype(BF16)
    vec = lambda a, l: a[l].reshape(1, -1)

    x2 = x.reshape(bsz * seq, d)
    for l in range(depth):
        gq, gk, gv, glg, ggate, rr, rlw, rk, rv, rkk, rka, rhg = _in_projection(
            x2, mod[l], w_in_b, w_alpha, w_rwkv, vec(lbs, l), alpha_up_p[l], vec(gla_alpha_b, l), vec(rwkv_mu, l),
            vec(rwkv_w0, l), vec(rwkv_a0, l), wa_blk[l], vec(rwkv_k_k, l), vec(rwkv_k_a, l), seq, tm, l)
        ogl = _gl_mixer(gq, gk, gv, glg, ggate, vec(norm_g, l), bsz, seq, tb_gl)
        orw = _rwkv_mixer(rr, rlw, rk, rv, rkk, rka, rhg, rwkv_g_up[l], vec(rwkv_r_k, l), vec(rwkv_gn_g, l),
                          vec(rwkv_gn_b, l), bsz, seq, tb)
        x2 = _post_mixer(x2, ogl, orw, mod[l], w_out_b, vec(ln1_g, l), vec(ln1_b, l), w_up_b, w_dn_b,
                         vec(ln2_g, l), vec(ln2_b, l), alpha, seq, tm_post, l)
    return x2.reshape(bsz, seq, d)
```

```python
import functools

import numpy as np
import jax
import jax.numpy as jnp
from jax import lax
from jax.experimental import pallas as pl
from jax.experimental.pallas import tpu as pltpu

F32 = jnp.float32
BF16 = jnp.bfloat16

HEAD_DIM = 64
H_HGRN, H_GLA, H_RWKV = 4, 6, 6
W_HGRN, W_GLA, W_RWKV = H_HGRN * HEAD_DIM, H_GLA * HEAD_DIM, H_RWKV * HEAD_DIM
GLA_GATE_RANK = 16
GLA_GATE_NORMALIZER = 16.0
RWKV_DECAY_RANK, RWKV_ICLR_RANK, RWKV_GATE_RANK = 64, 64, 128
RWKV_GN_EPS = 64e-5
N_HGRN_COLS = 4 * W_HGRN
N_GLA_COLS = 4 * W_GLA + GLA_GATE_RANK
N_RWKV_COLS = 3 * W_RWKV + RWKV_DECAY_RANK + RWKV_ICLR_RANK + RWKV_GATE_RANK
CHUNK = 64
LN_EPS = 1e-5
RMS_EPS = 1e-5
F_MIN = 1e-30

LANES = 128
PAIR = 2 * HEAD_DIM
SUB = 8
LEVEL_HALVES = (SUB, 2 * SUB, 4 * SUB)
assert 2 * LEVEL_HALVES[-1] == CHUNK
A_GROUP_GL = 4
OUT_SKEW = 4
LOG2E = 1.4426950408889634
VMEM_LIMIT = 56 * 1024 * 1024

W_GL = W_HGRN + W_GLA
N_GL_PAIRS = W_GL // PAIR
N_RW_PAIRS = W_RWKV // PAIR

NN = (((1,), (0,)), ((), ()))
NT = (((1,), (1,)), ((), ()))
TN = (((0,), (0,)), ((), ()))


def _mm(a, b, dims=NN):
    return lax.dot_general(a, b, dims, preferred_element_type=F32)


def _split2(x):
    hi = x.astype(BF16)
    lo = (x - hi.astype(F32)).astype(BF16)
    return hi, lo


def _bf16_parts(x, n):
    parts = []
    for i in range(n):
        p = x.astype(BF16)
        parts.append(p)
        if i + 1 < n:
            x = x - p.astype(F32)
    return parts


def _dot1(a, b, dims=NN):
    return _mm(a.astype(BF16), b.astype(BF16), dims)


def _dot3(a, b, dims=NN):
    ah, al = _split2(a)
    bh, bl = _split2(b)
    return _mm(ah, bh, dims) + (_mm(ah, bl, dims) + _mm(al, bh, dims))


def _dot_exact_lhs(a_bf16, b, n):
    return _mm(jnp.concatenate([a_bf16] * n, axis=1), jnp.concatenate(_bf16_parts(b, n), axis=0))


def _sigmoid(x):
    return 1.0 / (1.0 + jnp.exp(-x))


def _silu(x):
    return x * _sigmoid(x)


def _softplus(x):
    return jnp.maximum(x, 0.0) + jnp.log1p(jnp.exp(-jnp.abs(x)))


def _log_sigmoid(x):
    return -_softplus(-x)


def _layer_norm(y, g, b):
    mu = jnp.mean(y, axis=-1, keepdims=True)
    d = y - mu
    var = jnp.mean(d * d, axis=-1, keepdims=True)
    return d * lax.rsqrt(var + LN_EPS) * g + b


def _iota(shape, axis):
    return lax.broadcasted_iota(jnp.int32, shape, axis)


def _stack_heads(x):
    lane = _iota(x.shape, 1)
    return jnp.concatenate([jnp.where(lane < HEAD_DIM, x, 0.0), jnp.where(lane >= HEAD_DIM, x, 0.0)], axis=0)


def _pair_masks():
    t = _iota((CHUNK, PAIR), 0)
    s = _iota((CHUNK, PAIR), 1) % HEAD_DIM
    r2 = _iota((PAIR, PAIR), 0) // HEAD_DIM
    c2 = _iota((PAIR, PAIR), 1) // HEAD_DIM
    return dict(
        strict=s < t,
        incl=s <= t,
        eye=s == t,
        diag=(s // SUB == t // SUB) & (s <= t),
        level={hs: (s // (2 * hs) == t // (2 * hs)) & ((t // hs) % 2 == 1) & ((s // hs) % 2 == 0)
               for hs in LEVEL_HALVES},
        bd=r2 == c2,
    )


def _trace_skewed(stages, units, skew):
    for step in range(len(units) + skew * (len(stages) - 1)):
        for s, stage in enumerate(stages):
            u = step - skew * s
            if 0 <= u < len(units):
                stage(units[u])


def _tril_bf16():
    return jnp.where(_iota((CHUNK, CHUNK), 1) <= _iota((CHUNK, CHUNK), 0), 1.0, 0.0).astype(BF16)


def _seg_mean_matrix(width):
    r = _iota((width, width), 0) // HEAD_DIM
    c = _iota((width, width), 1) // HEAD_DIM
    return jnp.where(r == c, 1.0, 0.0).astype(BF16)


def _head_sums(xs):
    tiles = [x[:, j:j + LANES].astype(BF16) for x in xs for j in range(0, x.shape[1], LANES)]
    sums = []
    for i in range(0, len(tiles) - 1, 2):
        s = _mm(jnp.concatenate(tiles[i:i + 2], axis=1), _seg_mean_matrix(2 * LANES))
        sums += [s[:, 0:LANES], s[:, LANES:2 * LANES]]
    if len(tiles) % 2:
        sums.append(_mm(tiles[-1], _seg_mean_matrix(LANES)))
    out, i = [], 0
    for x in xs:
        n = x.shape[1] // LANES
        out.append(jnp.concatenate(sums[i:i + n], axis=1))
        i += n
    return out


def _mod_kernel(c_ref, w_ref, b_ref, o_ref):
    c = _silu(c_ref[...])
    o_ref[0] = _dot3(c, w_ref[0]) + b_ref[0]


def _modulation(c, ada_w, ada_b):
    depth, d, n = ada_w.shape
    bsz = c.shape[0]
    tn = 1536
    return pl.pallas_call(
        _mod_kernel,
        grid=(depth, n // tn),
        in_specs=[
            pl.BlockSpec((bsz, d), lambda l, j: (0, 0)),
            pl.BlockSpec((1, d, tn), lambda l, j: (l, 0, j)),
            pl.BlockSpec((1, 1, tn), lambda l, j: (l, 0, j)),
        ],
        out_specs=pl.BlockSpec((1, bsz, tn), lambda l, j: (l, 0, j)),
        out_shape=jax.ShapeDtypeStruct((depth, bsz, n), F32),
        compiler_params=pltpu.CompilerParams(
            dimension_semantics=("arbitrary", "arbitrary"), vmem_limit_bytes=VMEM_LIMIT),
        name="adaln_modulation",
    )(c, ada_w, ada_b.reshape(depth, 1, n))


def _lower_bound_kernel(x_ref, o_ref):
    depth = x_ref.shape[0]
    rows = [x_ref[l:l + 1, :] for l in range(depth)]
    m = functools.reduce(jnp.maximum, rows)
    e = [jnp.exp(r - m) for r in rows]
    tot = functools.reduce(lambda a, b: a + b, e)
    p = [ei / tot for ei in e]
    acc = jnp.zeros_like(p[0])
    for l in range(depth):
        acc = acc + p[l]
        o_ref[l:l + 1, :] = acc - p[0]


def _lower_bounds(logits):
    return pl.pallas_call(
        _lower_bound_kernel,
        out_shape=jax.ShapeDtypeStruct(logits.shape, F32),
        name="hgrn_lower_bounds",
    )(logits.astype(F32))


def _inproj_kernel(steps_per_batch, x_ref, mod_ref, w_ref, wal_ref, wr_ref, lb_ref, au_ref, ab_ref, mu_ref,
                   w0_ref, a0_ref, wa_ref, kk_ref, ka_ref,
                   gq_ref, gk_ref, gv_ref, gb_ref, gg_ref,
                   rr_ref, rlw_ref, rk_ref, rv_ref, rkk_ref, rka_ref, rhg_ref, carry_s):
    tm = x_ref.shape[0]
    w3 = W_RWKV

    @pl.when(pl.program_id(0) % steps_per_batch == 0)
    def _():
        carry_s[...] = jnp.zeros_like(carry_s)

    m = mod_ref[0]
    h = (x_ref[...] * (1.0 + m[1:2, :]) + m[0:1, :]).astype(BF16)

    def proj(c0, width):
        return _mm(h, w_ref[:, c0:c0 + width])

    first_row = _iota((tm, LANES), 0) == 0

    def shifted(c0, width):
        z = _mm(h, wr_ref[:, c0:c0 + width])
        z_prev = pltpu.roll(z, 1, axis=0)
        z_prev = jnp.where(jnp.concatenate([first_row] * (width // LANES), axis=1), carry_s[:, c0:c0 + width], z_prev)
        carry_s[:, c0:c0 + width] = z[tm - 1:tm, :]
        return z + (z_prev - z) * mu_ref[:, c0:c0 + width]

    g0 = N_HGRN_COLS
    n_rp = w3 // PAIR
    tile = lambda j: slice(j * PAIR, (j + 1) * PAIR)
    z_v = shifted(2 * w3, w3 + LANES)
    z_rk = shifted(0, 2 * w3)

    h_wa = z_v[:, w3:w3 + LANES]
    h_wa = jnp.where(_iota(h_wa.shape, 1) < RWKV_DECAY_RANK, jnp.tanh(h_wa), h_wa)
    wa = _dot1(h_wa, wa_ref[...])
    rv_ref[...] = z_v[:, 0:w3]
    rr_ref[...] = z_rk[:, 0:w3]

    later = [lambda: proj(W_HGRN, W_HGRN), lambda: proj(0, W_HGRN),
             lambda: shifted(3 * w3 + LANES, RWKV_GATE_RANK)]
    issued, a_tiles = [], []
    for j in range(n_rp):
        issued.append(later[j]())
        w_log = -_softplus(-(w0_ref[:, tile(j)] + wa[:, tile(j)])) - 0.5
        rlw_ref[:, tile(j)] = -jnp.exp(w_log)
        a_tiles.append(_sigmoid(a0_ref[:, tile(j)] + wa[:, w3 + j * PAIR:w3 + (j + 1) * PAIR]))
    zf, zq, z_hg = issued

    z_al = _mm(h, wal_ref[...])
    k = z_rk[:, w3:2 * w3]
    kk = k * kk_ref[...]
    kk = kk / jnp.maximum(jnp.sqrt(_head_sums([kk * kk])[0]), 1e-12)
    rkk_ref[...] = kk

    z_qk = proj(g0, 2 * W_GLA)
    for j in range(n_rp):
        rka_ref[:, tile(j)] = kk[:, tile(j)] * a_tiles[j]
        rk_ref[:, tile(j)] = k[:, tile(j)] * (1.0 + (a_tiles[j] - 1.0) * ka_ref[:, tile(j)])
    lb = lb_ref[...]
    f = lb + (1.0 - lb) * _sigmoid(zf)
    gb_ref[:, 0:W_HGRN] = jnp.log(jnp.maximum(f, F_MIN))
    k_h = (1.0 - lb) * _sigmoid(-zf)
    q_h = _silu(zq) * HEAD_DIM ** -0.5
    for p in range(W_HGRN // PAIR):
        gk_ref[p] = k_h[:, tile(p)]
        gq_ref[p] = q_h[:, tile(p)]

    z_vr = proj(g0 + 2 * W_GLA, 2 * W_GLA)
    rhg_ref[...] = _sigmoid(z_hg).astype(BF16)
    logit = _dot1(z_al, au_ref[...]) + ab_ref[...]
    gb_ref[:, W_HGRN:W_GL] = _log_sigmoid(logit) / GLA_GATE_NORMALIZER
    for p in range(W_GLA // PAIR):
        gq_ref[W_HGRN // PAIR + p] = z_qk[:, tile(p)] * HEAD_DIM ** -0.5
        gk_ref[W_HGRN // PAIR + p] = z_qk[:, W_GLA + p * PAIR:W_GLA + (p + 1) * PAIR]

    zg = proj(3 * W_HGRN, W_HGRN)
    gv_ref[:, W_HGRN:W_GL] = z_vr[:, 0:W_GLA].astype(BF16)
    gg_ref[:, W_HGRN:W_GL] = _silu(z_vr[:, W_GLA:2 * W_GLA]).astype(BF16)
    zi = proj(2 * W_HGRN, W_HGRN)
    gg_ref[:, 0:W_HGRN] = _silu(zg).astype(BF16)
    gv_ref[:, 0:W_HGRN] = zi.astype(BF16)


def _in_projection(x2, mod_l, w_main, w_alpha, w_rwkv, lb_l, alpha_up_p, alpha_b, mu, w0, a0, wa_blk, k_k, k_a,
                   seq, tm, layer):
    m_rows, d = x2.shape
    resident = lambda n: pl.BlockSpec((None, d, n), lambda i: (layer, 0, 0), pipeline_mode=pl.Buffered(1))
    steps_per_batch = seq // tm
    row = lambda i: (i, 0)
    const = lambda i: (0, 0)
    vec = lambda n: pl.BlockSpec((1, n), const)
    out = lambda n, dt: (pl.BlockSpec((tm, n), row), jax.ShapeDtypeStruct((m_rows, n), dt))
    pair_slabs = (pl.BlockSpec((N_GL_PAIRS, tm, PAIR), lambda i: (0, i, 0)),
                  jax.ShapeDtypeStruct((N_GL_PAIRS, m_rows, PAIR), F32))
    outs = [pair_slabs, pair_slabs, out(W_GL, BF16), out(W_GL, F32), out(W_GL, BF16),
            out(W_RWKV, F32), out(W_RWKV, F32), out(W_RWKV, F32), out(W_RWKV, F32), out(W_RWKV, F32),
            out(W_RWKV, F32), out(RWKV_GATE_RANK, BF16)]
    return pl.pallas_call(
        functools.partial(_inproj_kernel, steps_per_batch),
        grid=(m_rows // tm,),
        in_specs=[
            pl.BlockSpec((tm, d), row),
            pl.BlockSpec((1, 6, d), lambda i: (i // steps_per_batch, 0, 0)),
            resident(N_HGRN_COLS + 4 * W_GLA), resident(LANES), resident(N_RWKV_COLS),
            vec(W_HGRN),
            pl.BlockSpec((LANES, W_GLA), const),
            vec(W_GLA),
            vec(N_RWKV_COLS), vec(W_RWKV), vec(W_RWKV),
            pl.BlockSpec((LANES, 2 * W_RWKV), const),
            vec(W_RWKV), vec(W_RWKV),
        ],
        out_specs=[o[0] for o in outs],
        out_shape=[o[1] for o in outs],
        scratch_shapes=[pltpu.VMEM((1, N_RWKV_COLS), F32)],
        compiler_params=pltpu.CompilerParams(
            dimension_semantics=("arbitrary",), vmem_limit_bytes=VMEM_LIMIT),
        name="in_projection",
    )(x2, mod_l, w_main, w_alpha, w_rwkv, lb_l, alpha_up_p, alpha_b, mu, w0, a0, wa_blk, k_k, k_a)


def _gl_kernel(q_s, k_s, v_s, lg_ref, gate_ref, ng_ref, ind_ref, out_ref,
               st_ref, b_s, ad_s, e_s, o_s, sc_s, eend_s):
    tb = q_s.shape[1]
    n_chunks = tb // CHUNK
    n_groups = tb // SUB

    @pl.when(pl.program_id(1) == 0)
    def _():
        st_ref[...] = jnp.zeros_like(st_ref)

    tril = _tril_bf16()
    for c in range(n_chunks):
        rows = slice(c * CHUNK, (c + 1) * CHUNK)
        b = _dot_exact_lhs(tril, lg_ref[rows, :], 2)
        for p in range(N_GL_PAIRS):
            b_s[p, rows, :] = b[:, p * PAIR:(p + 1) * PAIR]

    for t in range(SUB - 1):
        e_s[t * n_groups:(t + 1) * n_groups, (t + 1) * PAIR:] = jnp.zeros((n_groups, (SUB - 1 - t) * PAIR), BF16)
    for p in range(N_GL_PAIRS):
        by_pos = lambda ref, t: ref[p, pl.ds(t, n_groups, stride=SUB), :]
        qs = [by_pos(q_s, t) for t in range(SUB)]
        ks = [by_pos(k_s, t) for t in range(SUB)]
        bs = [by_pos(b_s, t) * LOG2E for t in range(SUB)]
        for t in range(SUB):
            for sg in range(t + 1):
                e = qs[t] * ks[sg]
                if sg < t:
                    e = e * jnp.exp2(jnp.minimum(bs[t] - bs[sg], 0.0))
                e_s[t * n_groups:(t + 1) * n_groups, sg * PAIR:(sg + 1) * PAIR] = e.astype(BF16)
        ad = [jnp.zeros((n_groups, PAIR), F32)] * SUB
        for sg0 in range(0, SUB, 2):
            part = _mm(e_s[sg0 * n_groups:, sg0 * PAIR:(sg0 + 2) * PAIR], ind_ref[sg0 * PAIR:(sg0 + 2) * PAIR, :])
            for t in range(sg0, SUB):
                ad[t] = ad[t] + part[(t - sg0) * n_groups:(t - sg0 + 1) * n_groups, :]
        for t in range(SUB):
            ad_s[p, pl.ds(t, n_groups, stride=SUB), :] = ad[t]

    masks = _pair_masks()
    t_idx = _iota((CHUNK, PAIR), 0)
    second_half = {hs: (t_idx // hs) % 2 == 1 for hs in LEVEL_HALVES}
    level_sign = {hs: jnp.where(second_half[hs], LOG2E, -LOG2E) for hs in LEVEL_HALVES}

    def group_units(i):
        units = []
        for cc in range(A_GROUP_GL):
            c = i * A_GROUP_GL + cc
            rows = pl.ds(pl.multiple_of(c * CHUNK, CHUNK), CHUNK)
            for p in range(N_GL_PAIRS):
                units.append(dict(p=p, rows=rows, lanes=slice(p * PAIR, (p + 1) * PAIR),
                                  e_rows=pl.ds(pl.multiple_of(c * 8, 8), 8),
                                  st_rows=pl.ds(pl.multiple_of((c * N_GL_PAIRS + p) * PAIR, PAIR), PAIR)))
        return units

    def increments(i, carry):
        units = group_units(i)

        def decay_keys(un):
            b = b_s[un["p"], un["rows"], :]
            b_end = b[CHUNK - 1:CHUNK, :]
            un["k_end"] = k_s[un["p"], un["rows"], :] * jnp.exp(b_end - b)
            eend_s[un["e_rows"], un["lanes"]] = jnp.broadcast_to(jnp.exp(b_end), (8, PAIR))

        def outer(un):
            upd = _dot1(v_s[un["rows"], un["lanes"]], un["k_end"], TN)
            sc_s[un["st_rows"], :] = jnp.where(masks["bd"], upd, 0.0)

        _trace_skewed((decay_keys, outer), units, OUT_SKEW)
        return carry

    lax.fori_loop(0, n_chunks // A_GROUP_GL, increments, 0)

    sts = [st_ref[p] for p in range(N_GL_PAIRS)]
    for c in range(n_chunks):
        for p in range(N_GL_PAIRS):
            st_rows = slice((c * N_GL_PAIRS + p) * PAIR, (c * N_GL_PAIRS + p + 1) * PAIR)
            upd = sc_s[st_rows, :]
            sc_s[st_rows, :] = sts[p]
            sts[p] = sts[p] * eend_s[c * 8:c * 8 + 1, p * PAIR:(p + 1) * PAIR] + upd
    for p in range(N_GL_PAIRS):
        st_ref[p] = sts[p]

    def outputs(i, carry):
        units = group_units(i)

        def decays(un):
            q = q_s[un["p"], un["rows"], :]
            k = k_s[un["p"], un["rows"], :]
            b = b_s[un["p"], un["rows"], :]
            un["lv"] = []
            for hs in LEVEL_HALVES:
                e_mid = jnp.concatenate(
                    [jnp.broadcast_to(b[m:m + 1, :], (2 * hs, PAIR)) for m in range(hs - 1, CHUNK, 2 * hs)], axis=0)
                dec = jnp.exp2(jnp.minimum((b - e_mid) * level_sign[hs], 0.0))
                un["lv"].append(jnp.where(second_half[hs], q, k) * dec)
            un["q_in"] = q * jnp.exp(b)

        def scores(un):
            a = jnp.zeros((CHUNK, PAIR), F32)
            for hs, lv in zip(LEVEL_HALVES, un["lv"]):
                a = jnp.where(masks["level"][hs], _dot1(lv, jnp.transpose(_stack_heads(lv))), a)
            un["a"] = jnp.where(masks["diag"], ad_s[un["p"], un["rows"], :], a)

        def combine(un):
            o_s[un["rows"], un["lanes"]] = (_dot1(un["a"], _stack_heads(v_s[un["rows"], un["lanes"]]))
                                            + _dot1(un["q_in"], sc_s[un["st_rows"], :], NT))

        _trace_skewed((decays, scores, combine), units, OUT_SKEW)
        return carry

    lax.fori_loop(0, n_chunks // A_GROUP_GL, outputs, 0)

    o = o_s[...]
    ms = _head_sums([o * o])[0] * (1.0 / HEAD_DIM)
    out_ref[...] = (o * lax.rsqrt(ms + RMS_EPS) * ng_ref[...] * gate_ref[...]).astype(out_ref.dtype)


def _gl_indicator():
    r = np.arange(SUB * PAIR)
    c = np.arange(PAIR)
    sg = r // PAIR
    h = (r % PAIR) // HEAD_DIM
    ind = (h[:, None] == (c // HEAD_DIM)[None, :]) & (sg[:, None] == (c % SUB)[None, :])
    return jnp.asarray(ind, dtype=BF16)


def _gl_mixer(q, k, v, lg, gate, norm_g, bsz, seq, tb):
    steps = seq // tb
    row = lambda b, i: (b * steps + i, 0)
    const = lambda b, i: (0, 0)
    slab = pl.BlockSpec((tb, W_GL), row)
    pair_slabs = pl.BlockSpec((N_GL_PAIRS, tb, PAIR), lambda b, i: (0, b * steps + i, 0))
    scratch_f32 = pltpu.VMEM((tb, W_GL), F32)
    scratch_pairs = pltpu.VMEM((N_GL_PAIRS, tb, PAIR), F32)
    return pl.pallas_call(
        _gl_kernel,
        grid=(bsz, steps),
        in_specs=[
            pair_slabs, pair_slabs, slab, slab, slab,
            pl.BlockSpec((1, W_GL), const),
            pl.BlockSpec((SUB * PAIR, PAIR), const),
        ],
        out_specs=pl.BlockSpec((tb, W_GL), row),
        out_shape=jax.ShapeDtypeStruct((bsz * seq, W_GL), BF16),
        scratch_shapes=[
            pltpu.VMEM((N_GL_PAIRS, PAIR, PAIR), F32),
            scratch_pairs, scratch_pairs,
            pltpu.VMEM((tb, SUB * PAIR), BF16),
            scratch_f32,
            pltpu.VMEM((tb // CHUNK * N_GL_PAIRS * PAIR, PAIR), F32),
            pltpu.VMEM((tb // CHUNK * 8, W_GL), F32),
        ],
        compiler_params=pltpu.CompilerParams(
            dimension_semantics=("arbitrary", "arbitrary"), vmem_limit_bytes=VMEM_LIMIT),
        name="hgrn_gla_mixer",
    )(q, k, v, lg, gate, norm_g, _gl_indicator())


def _rwkv_kernel(r_s, lw_s, k_s, v_s, kk_s, ka_s, hg_s, gup_ref, rk_ref, gng_ref, gnb_ref,
                 out_ref, st_ref, y_s, rt_s, nb_s, nk_s, w_s, u0_s, bkt_s, ecol_s):
    bsz, tb = r_s.shape[0], r_s.shape[1]
    n_chunks = tb // CHUNK

    @pl.when(pl.program_id(0) == 0)
    def _():
        st_ref[...] = jnp.zeros_like(st_ref)

    masks = _pair_masks()
    tril = _tril_bf16()
    eye = jnp.where(masks["eye"], 1.0, 0.0)

    def phase_a(b, carry):
        units = []
        for c in range(n_chunks):
            rows = slice(c * CHUNK, (c + 1) * CHUNK)
            g_all = _dot_exact_lhs(tril, lw_s[b, rows, :], 2)
            for p in range(N_RW_PAIRS):
                lanes = slice(p * PAIR, (p + 1) * PAIR)
                units.append(dict(c=c, p=p, rows=rows, lanes=lanes, g=g_all[:, lanes]))
        def scale(un):
            rows, lanes, g = un["rows"], un["lanes"], un["g"]
            kc = k_s[b, rows, lanes]
            kkc = kk_s[b, rows, lanes]
            ka = ka_s[b, rows, lanes]
            g_end = g[CHUNK - 1:CHUNK, :]
            e_neg = jnp.exp(-g)
            e_end = jnp.exp(g_end - g)
            un["at"] = -kkc * jnp.exp(g - lw_s[b, rows, lanes])
            rt = r_s[b, rows, lanes] * jnp.exp(g)
            un["rt"] = rt
            un["bt"] = ka * e_neg
            un["kt"] = kc * e_neg
            rt_s[b, rows, lanes] = rt
            slot = un["c"] * N_RW_PAIRS + un["p"]
            bkt_s[b, slot] = jnp.transpose(jnp.concatenate([ka * e_end, kc * e_end], axis=0))
            ecol_s[b, slot] = jnp.transpose(jnp.broadcast_to(jnp.exp(g_end), (PAIR, PAIR)))

        def gram(un):
            g2 = _dot1(jnp.concatenate([un["at"], un["rt"]], axis=0),
                       jnp.concatenate([_stack_heads(un["bt"]), _stack_heads(un["kt"])], axis=0), NT)
            un["l"] = jnp.where(masks["strict"], g2[0:CHUNK, 0:PAIR], 0.0)
            un["mk"] = jnp.where(masks["strict"], g2[0:CHUNK, PAIR:2 * PAIR], 0.0)
            nb_s[b, un["rows"], un["lanes"]] = jnp.where(masks["incl"], g2[CHUNK:2 * CHUNK, 0:PAIR], 0.0)
            nk_s[b, un["rows"], un["lanes"]] = jnp.where(masks["incl"], g2[CHUNK:2 * CHUNK, PAIR:2 * PAIR], 0.0)

        def inv_first(un):
            un["t"] = eye + un["l"]
            un["lp"] = _dot1(un["l"], _stack_heads(un["l"]))
            un["mv"] = _dot1(un["mk"], _stack_heads(v_s[b, un["rows"], un["lanes"]]))

        def inv_step(un):
            both = _dot1(jnp.concatenate([un["lp"], un["t"]], axis=0), _stack_heads(un["lp"]))
            un["lp"] = both[0:CHUNK, :]
            un["t"] = un["t"] + both[CHUNK:2 * CHUNK, :]

        def inv_last(un):
            un["t"] = un["t"] + _dot1(un["t"], _stack_heads(un["lp"]))

        def apply_inv(un):
            wu = _dot1(un["t"], jnp.concatenate([_stack_heads(un["at"]), _stack_heads(un["mv"])], axis=1))
            w_s[b, un["rows"], un["lanes"]] = wu[:, 0:PAIR]
            u0_s[b, un["rows"], un["lanes"]] = wu[:, PAIR:2 * PAIR]

        for stage in (scale, gram, inv_first, inv_step, inv_step, inv_step, inv_step, inv_last, apply_inv):
            for un in units:
                stage(un)
        return carry

    lax.fori_loop(0, bsz, phase_a, 0)

    def phase_b(c, carry):
        rows = pl.ds(pl.multiple_of(c * CHUNK, CHUNK), CHUNK)
        seqs = [(b, p, slice(p * PAIR, (p + 1) * PAIR)) for b in range(bsz) for p in range(N_RW_PAIRS)]
        sts = [st_ref[b * N_RW_PAIRS + p] for b, p, _ in seqs]
        vs = [v_s[b, rows, lanes] for b, _, lanes in seqs]
        wrs = [_dot1(jnp.concatenate([w_s[b, rows, lanes], rt_s[b, rows, lanes]], axis=0), st)
               for (b, _, lanes), st in zip(seqs, sts)]
        us = [wr[0:CHUNK, :] + u0_s[b, rows, lanes] for (b, _, lanes), wr in zip(seqs, wrs)]
        upds = [_dot1(bkt_s[b, c * N_RW_PAIRS + p], jnp.concatenate([u, vc], axis=0))
                for (b, p, _), u, vc in zip(seqs, us, vs)]
        for (b, p, _), st, upd in zip(seqs, sts, upds):
            st_ref[b * N_RW_PAIRS + p] = st * ecol_s[b, c * N_RW_PAIRS + p] + jnp.where(masks["bd"], upd, 0.0)
        for (b, _, lanes), wr, u, vc in zip(seqs, wrs, us, vs):
            y_s[b, rows, lanes] = (
                wr[CHUNK:2 * CHUNK, :]
                + _dot1(jnp.concatenate([nb_s[b, rows, lanes], nk_s[b, rows, lanes]], axis=1),
                        jnp.concatenate([_stack_heads(u), _stack_heads(vc)], axis=0)))
        return carry

    lax.fori_loop(0, n_chunks, phase_b, 0)

    for b in range(bsz):
        y = y_s[b]
        s_hi, s_lo = _head_sums(_bf16_parts(y, 2))
        d = y - (s_hi + s_lo) * (1.0 / HEAD_DIM)
        s_var, s_bonus = _head_sums([d * d, r_s[b] * k_s[b] * rk_ref[...]])
        yn = d * lax.rsqrt(s_var * (1.0 / HEAD_DIM) + RWKV_GN_EPS) * gng_ref[...] + gnb_ref[...]
        gate = _dot1(hg_s[b], gup_ref[...])
        out_ref[b] = ((yn + s_bonus * v_s[b]) * gate).astype(out_ref.dtype)


def _rwkv_mixer(r, lw, k, v, kk, ka, hg, g_up, r_k, gn_g, gn_b, bsz, seq, tb):
    blk = lambda i: (0, i, 0)
    const = lambda i: (0, 0)
    vec = pl.BlockSpec((1, W_RWKV), const)
    tok = pl.BlockSpec((bsz, tb, W_RWKV), blk)
    slab = pltpu.VMEM((bsz, tb, W_RWKV), F32)
    unit_tiles = pltpu.VMEM((bsz, tb // CHUNK * N_RW_PAIRS, PAIR, PAIR), F32)
    seq_major = lambda a: a.reshape(bsz, seq, a.shape[-1])
    out = pl.pallas_call(
        _rwkv_kernel,
        grid=(seq // tb,),
        in_specs=[
            tok, tok, tok, tok, tok, tok,
            pl.BlockSpec((bsz, tb, RWKV_GATE_RANK), blk),
            pl.BlockSpec((RWKV_GATE_RANK, W_RWKV), const),
            vec, vec, vec,
        ],
        out_specs=pl.BlockSpec((bsz, tb, W_RWKV), blk),
        out_shape=jax.ShapeDtypeStruct((bsz, seq, W_RWKV), BF16),
        scratch_shapes=[
            pltpu.VMEM((bsz * N_RW_PAIRS, PAIR, PAIR), F32),
            slab,
            slab, slab, slab, slab, slab,
            unit_tiles, unit_tiles,
        ],
        compiler_params=pltpu.CompilerParams(
            dimension_semantics=("arbitrary",), vmem_limit_bytes=VMEM_LIMIT),
        name="rwkv7_mixer",
    )(*[seq_major(a) for a in (r, lw, k, v, kk, ka, hg)], g_up, r_k, gn_g, gn_b)
    return out.reshape(bsz * seq, W_RWKV)


def _post_kernel(alpha, ff_chunk, x_ref, ogl_ref, orw_ref, mod_ref, wout_ref, ln1g_ref, ln1b_ref,
                 wup_ref, wdn_ref, ln2g_ref, ln2b_ref, out_ref):
    m = mod_ref[0]
    gate1, shift2, scale2, gate2 = m[2:3, :], m[3:4, :], m[4:5, :], m[5:6, :]
    d_ff = wup_ref.shape[1]
    n_ff = d_ff // ff_chunk
    half = x_ref.shape[0] // 2
    groups = [slice(0, half), slice(half, 2 * half)]

    def out_proj(rows):
        return (_mm(ogl_ref[rows, :], wout_ref[0:W_GL, :])
                + _mm(orw_ref[rows, :], wout_ref[W_GL:W_GL + W_RWKV, :]))

    def norm1(rows, o):
        x1 = _layer_norm(alpha * x_ref[rows, :] + (1.0 + gate1) * o, ln1g_ref[...], ln1b_ref[...])
        return x1, (x1 * (1.0 + scale2) + shift2).astype(BF16)

    def mlp(h, acc, js):
        for j in js:
            cols = slice(j * ff_chunk, (j + 1) * ff_chunk)
            u = jnp.maximum(_mm(h, wup_ref[:, cols]), 0.0)
            acc = acc + _mm((u * u).astype(BF16), wdn_ref[cols, :])
        return acc

    def norm2(rows, x1, acc):
        out_ref[rows, :] = _layer_norm(alpha * x1 + (1.0 + gate2) * acc, ln2g_ref[...], ln2b_ref[...])

    o_a, o_b = out_proj(groups[0]), out_proj(groups[1])
    x1_a, h_a = norm1(groups[0], o_a)
    acc_a = mlp(h_a, jnp.zeros(x1_a.shape, F32), range(0, 2))
    x1_b, h_b = norm1(groups[1], o_b)
    acc_a = mlp(h_a, acc_a, range(2, n_ff))
    acc_b = mlp(h_b, jnp.zeros(x1_b.shape, F32), range(0, 2))
    norm2(groups[0], x1_a, acc_a)
    acc_b = mlp(h_b, acc_b, range(2, n_ff))
    norm2(groups[1], x1_b, acc_b)


def _post_mixer(x2, ogl, orw, mod_l, w_out, ln1_g, ln1_b, w_up, w_dn, ln2_g, ln2_b, alpha, seq, tm, layer):
    m_rows, d = x2.shape
    d_ff = w_up.shape[2]
    steps_per_batch = seq // tm
    row = lambda i: (i, 0)
    const = lambda i: (0, 0)
    vec = pl.BlockSpec((1, d), const)
    resident = lambda shape: pl.BlockSpec((None,) + shape, lambda i: (layer, 0, 0), pipeline_mode=pl.Buffered(1))
    return pl.pallas_call(
        functools.partial(_post_kernel, alpha, 512),
        grid=(m_rows // tm,),
        in_specs=[
            pl.BlockSpec((tm, d), row),
            pl.BlockSpec((tm, W_GL), row),
            pl.BlockSpec((tm, W_RWKV), row),
            pl.BlockSpec((1, 6, d), lambda i: (i // steps_per_batch, 0, 0)),
            resident((W_GL + W_RWKV, d)),
            vec, vec,
            resident((d, d_ff)),
            resident((d_ff, d)),
            vec, vec,
        ],
        out_specs=pl.BlockSpec((tm, d), row),
        out_shape=jax.ShapeDtypeStruct((m_rows, d), F32),
        compiler_params=pltpu.CompilerParams(
            dimension_semantics=("arbitrary",), vmem_limit_bytes=VMEM_LIMIT),
        name="outproj_mlp",
    )(x2, ogl, orw, mod_l, w_out, ln1_g, ln1_b, w_up, w_dn, ln2_g, ln2_b)


def kernel(x, c, hgrn_lb_logits, ada_w, ada_b, w_in, hgrn_norm_g, gla_alpha_up, gla_alpha_b, gla_norm_g,
           rwkv_mu, rwkv_w0, rwkv_w_up, rwkv_a0, rwkv_a_up, rwkv_g_up, rwkv_k_k, rwkv_k_a, rwkv_r_k,
           rwkv_gn_g, rwkv_gn_b, w_out, ln1_g, ln1_b, mlp_w_up, mlp_w_down, ln2_g, ln2_b):
    bsz, seq, d = x.shape
    depth = w_in.shape[0]
    alpha = (2.0 * depth) ** 0.25
    tm = min(512, seq)
    tb = min(512, seq)
    tb_gl = min(1024, seq)
    tm_post = min(1024, seq)
    assert all(seq % t == 0 for t in (tm, tb, tb_gl, tm_post)) and tb % CHUNK == 0
    assert tb_gl % (A_GROUP_GL * CHUNK) == 0 and tm_post % 16 == 0
    assert w_in.shape[2] == N_HGRN_COLS + N_GLA_COLS + N_RWKV_COLS

    mod = _modulation(c, ada_w, ada_b).reshape(depth, bsz, 6, d)
    lbs = _lower_bounds(hgrn_lb_logits)

    gla_end = N_HGRN_COLS + N_GLA_COLS
    w_in_b = w_in.astype(BF16)
    w_alpha = jnp.pad(w_in_b[:, :, gla_end - GLA_GATE_RANK:gla_end], ((0, 0), (0, 0), (0, LANES - GLA_GATE_RANK)))
    w_rwkv = w_in_b[:, :, gla_end:]
    alpha_up_p = jnp.concatenate(
        [gla_alpha_up, jnp.zeros((depth, LANES - GLA_GATE_RANK, W_GLA), gla_alpha_up.dtype)], axis=1)
    zeros_r = jnp.zeros((depth, RWKV_DECAY_RANK, W_RWKV), rwkv_w_up.dtype)
    wa_blk = jnp.concatenate(
        [jnp.concatenate([rwkv_w_up, zeros_r], axis=2), jnp.concatenate([zeros_r, rwkv_a_up], axis=2)], axis=1)
    norm_g = jnp.concatenate([hgrn_norm_g, gla_norm_g], axis=1)
    w_out_b = w_out.astype(BF16)
    w_up_b = mlp_w_up.astype(BF16)
    w_dn_b = mlp_w_down.astype(BF16)
    vec = lambda a, l: a[l].reshape(1, -1)

    x2 = x.reshape(bsz * seq, d)
    for l in range(depth):
        gq, gk, gv, glg, ggate, rr, rlw, rk, rv, rkk, rka, rhg = _in_projection(
            x2, mod[l], w_in_b, w_alpha, w_rwkv, vec(lbs, l), alpha_up_p[l], vec(gla_alpha_b, l), vec(rwkv_mu, l),
            vec(rwkv_w0, l), vec(rwkv_a0, l), wa_blk[l], vec(rwkv_k_k, l), vec(rwkv_k_a, l), seq, tm, l)
        ogl = _gl_mixer(gq, gk, gv, glg, ggate, vec(norm_g, l), bsz, seq, tb_gl)
        orw = _rwkv_mixer(rr, rlw, rk, rv, rkk, rka, rhg, rwkv_g_up[l], vec(rwkv_r_k, l), vec(rwkv_gn_g, l),
                          vec(rwkv_gn_b, l), bsz, seq, tb)
        x2 = _post_mixer(x2, ogl, orw, mod[l], w_out_b, vec(ln1_g, l), vec(ln1_b, l), w_up_b, w_dn_b,
                         vec(ln2_g, l), vec(ln2_b, l), alpha, seq, tm_post, l)
    return x2.reshape(bsz, seq, d)
```

```python
import functools

import numpy as np
import jax
import jax.numpy as jnp
from jax import lax
from jax.experimental import pallas as pl
from jax.experimental.pallas import tpu as pltpu

F32 = jnp.float32
BF16 = jnp.bfloat16

HEAD_DIM = 64
H_HGRN, H_GLA, H_RWKV = 4, 6, 6
W_HGRN, W_GLA, W_RWKV = H_HGRN * HEAD_DIM, H_GLA * HEAD_DIM, H_RWKV * HEAD_DIM
GLA_GATE_RANK = 16
GLA_GATE_NORMALIZER = 16.0
RWKV_DECAY_RANK, RWKV_ICLR_RANK, RWKV_GATE_RANK = 64, 64, 128
RWKV_GN_EPS = 64e-5
N_HGRN_COLS = 4 * W_HGRN
N_GLA_COLS = 4 * W_GLA + GLA_GATE_RANK
N_RWKV_COLS = 3 * W_RWKV + RWKV_DECAY_RANK + RWKV_ICLR_RANK + RWKV_GATE_RANK
CHUNK = 64
LN_EPS = 1e-5
RMS_EPS = 1e-5
F_MIN = 1e-30

LANES = 128
PAIR = 2 * HEAD_DIM
SUB = 8
LEVEL_HALVES = (SUB, 2 * SUB, 4 * SUB)
assert 2 * LEVEL_HALVES[-1] == CHUNK
A_GROUP_GL = 8
OUT_SKEW = 4
LOG2E = 1.4426950408889634
VMEM_LIMIT = 56 * 1024 * 1024

W_GL = W_HGRN + W_GLA
N_GL_PAIRS = W_GL // PAIR
N_RW_PAIRS = W_RWKV // PAIR

NN = (((1,), (0,)), ((), ()))
NT = (((1,), (1,)), ((), ()))
TN = (((0,), (0,)), ((), ()))


def _mm(a, b, dims=NN):
    return lax.dot_general(a, b, dims, preferred_element_type=F32)


def _split2(x):
    hi = x.astype(BF16)
    lo = (x - hi.astype(F32)).astype(BF16)
    return hi, lo


def _bf16_parts(x, n):
    parts = []
    for i in range(n):
        p = x.astype(BF16)
        parts.append(p)
        if i + 1 < n:
            x = x - p.astype(F32)
    return parts


def _dot1(a, b, dims=NN):
    return _mm(a.astype(BF16), b.astype(BF16), dims)


def _dot3(a, b, dims=NN):
    ah, al = _split2(a)
    bh, bl = _split2(b)
    return _mm(ah, bh, dims) + (_mm(ah, bl, dims) + _mm(al, bh, dims))


def _dot_exact_lhs(a_bf16, b, n):
    return _mm(jnp.concatenate([a_bf16] * n, axis=1), jnp.concatenate(_bf16_parts(b, n), axis=0))


def _sigmoid(x):
    return 1.0 / (1.0 + jnp.exp(-x))


def _silu(x):
    return x * _sigmoid(x)


def _softplus(x):
    return jnp.maximum(x, 0.0) + jnp.log1p(jnp.exp(-jnp.abs(x)))


def _log_sigmoid(x):
    return -_softplus(-x)


def _layer_norm(y, g, b):
    mu = jnp.mean(y, axis=-1, keepdims=True)
    d = y - mu
    var = jnp.mean(d * d, axis=-1, keepdims=True)
    return d * lax.rsqrt(var + LN_EPS) * g + b


def _iota(shape, axis):
    return lax.broadcasted_iota(jnp.int32, shape, axis)


def _stack_heads(x):
    lane = _iota(x.shape, 1)
    return jnp.concatenate([jnp.where(lane < HEAD_DIM, x, 0.0), jnp.where(lane >= HEAD_DIM, x, 0.0)], axis=0)


def _pair_masks():
    t = _iota((CHUNK, PAIR), 0)
    s = _iota((CHUNK, PAIR), 1) % HEAD_DIM
    r2 = _iota((PAIR, PAIR), 0) // HEAD_DIM
    c2 = _iota((PAIR, PAIR), 1) // HEAD_DIM
    return dict(
        strict=s < t,
        incl=s <= t,
        eye=s == t,
        diag=(s // SUB == t // SUB) & (s <= t),
        level={hs: (s // (2 * hs) == t // (2 * hs)) & ((t // hs) % 2 == 1) & ((s // hs) % 2 == 0)
               for hs in LEVEL_HALVES},
        bd=r2 == c2,
    )


def _trace_skewed(stages, units, skew):
    for step in range(len(units) + skew * (len(stages) - 1)):
        for s, stage in enumerate(stages):
            u = step - skew * s
            if 0 <= u < len(units):
                stage(units[u])


def _tril_bf16():
    return jnp.where(_iota((CHUNK, CHUNK), 1) <= _iota((CHUNK, CHUNK), 0), 1.0, 0.0).astype(BF16)


def _seg_mean_matrix(width):
    r = _iota((width, width), 0) // HEAD_DIM
    c = _iota((width, width), 1) // HEAD_DIM
    return jnp.where(r == c, 1.0, 0.0).astype(BF16)


def _head_sums(xs):
    tiles = [x[:, j:j + LANES].astype(BF16) for x in xs for j in range(0, x.shape[1], LANES)]
    sums = []
    for i in range(0, len(tiles) - 1, 2):
        s = _mm(jnp.concatenate(tiles[i:i + 2], axis=1), _seg_mean_matrix(2 * LANES))
        sums += [s[:, 0:LANES], s[:, LANES:2 * LANES]]
    if len(tiles) % 2:
        sums.append(_mm(tiles[-1], _seg_mean_matrix(LANES)))
    out, i = [], 0
    for x in xs:
        n = x.shape[1] // LANES
        out.append(jnp.concatenate(sums[i:i + n], axis=1))
        i += n
    return out


def _mod_kernel(c_ref, w_ref, b_ref, o_ref):
    c = _silu(c_ref[...])
    o_ref[0] = _dot3(c, w_ref[0]) + b_ref[0]


def _modulation(c, ada_w, ada_b):
    depth, d, n = ada_w.shape
    bsz = c.shape[0]
    tn = 1536
    return pl.pallas_call(
        _mod_kernel,
        grid=(depth, n // tn),
        in_specs=[
            pl.BlockSpec((bsz, d), lambda l, j: (0, 0)),
            pl.BlockSpec((1, d, tn), lambda l, j: (l, 0, j)),
            pl.BlockSpec((1, 1, tn), lambda l, j: (l, 0, j)),
        ],
        out_specs=pl.BlockSpec((1, bsz, tn), lambda l, j: (l, 0, j)),
        out_shape=jax.ShapeDtypeStruct((depth, bsz, n), F32),
        compiler_params=pltpu.CompilerParams(
            dimension_semantics=("arbitrary", "arbitrary"), vmem_limit_bytes=VMEM_LIMIT),
        name="adaln_modulation",
    )(c, ada_w, ada_b.reshape(depth, 1, n))


def _lower_bound_kernel(x_ref, o_ref):
    depth = x_ref.shape[0]
    rows = [x_ref[l:l + 1, :] for l in range(depth)]
    m = functools.reduce(jnp.maximum, rows)
    e = [jnp.exp(r - m) for r in rows]
    tot = functools.reduce(lambda a, b: a + b, e)
    p = [ei / tot for ei in e]
    acc = jnp.zeros_like(p[0])
    for l in range(depth):
        acc = acc + p[l]
        o_ref[l:l + 1, :] = acc - p[0]


def _lower_bounds(logits):
    return pl.pallas_call(
        _lower_bound_kernel,
        out_shape=jax.ShapeDtypeStruct(logits.shape, F32),
        name="hgrn_lower_bounds",
    )(logits.astype(F32))


def _inproj_kernel(steps_per_batch, x_ref, mod_ref, w_ref, wal_ref, wr_ref, lb_ref, au_ref, ab_ref, mu_ref,
                   w0_ref, a0_ref, wa_ref, kk_ref, ka_ref,
                   gq_ref, gk_ref, gv_ref, gb_ref, gg_ref,
                   rr_ref, rlw_ref, rk_ref, rv_ref, rkk_ref, rka_ref, rhg_ref, carry_s):
    tm = x_ref.shape[0]
    w3 = W_RWKV

    @pl.when(pl.program_id(0) % steps_per_batch == 0)
    def _():
        carry_s[...] = jnp.zeros_like(carry_s)

    m = mod_ref[0]
    h = (x_ref[...] * (1.0 + m[1:2, :]) + m[0:1, :]).astype(BF16)

    def proj(c0, width):
        return _mm(h, w_ref[:, c0:c0 + width])

    first_row = _iota((tm, LANES), 0) == 0

    def shifted(c0, width):
        z = _mm(h, wr_ref[:, c0:c0 + width])
        z_prev = pltpu.roll(z, 1, axis=0)
        z_prev = jnp.where(jnp.concatenate([first_row] * (width // LANES), axis=1), carry_s[:, c0:c0 + width], z_prev)
        carry_s[:, c0:c0 + width] = z[tm - 1:tm, :]
        return z + (z_prev - z) * mu_ref[:, c0:c0 + width]

    g0 = N_HGRN_COLS
    n_rp = w3 // PAIR
    tile = lambda j: slice(j * PAIR, (j + 1) * PAIR)
    z_v = shifted(2 * w3, w3 + LANES)
    z_rk = shifted(0, 2 * w3)

    h_wa = z_v[:, w3:w3 + LANES]
    h_wa = jnp.where(_iota(h_wa.shape, 1) < RWKV_DECAY_RANK, jnp.tanh(h_wa), h_wa)
    wa = _dot1(h_wa, wa_ref[...])
    rv_ref[...] = z_v[:, 0:w3]
    rr_ref[...] = z_rk[:, 0:w3]

    later = [lambda: proj(W_HGRN, W_HGRN), lambda: proj(0, W_HGRN),
             lambda: shifted(3 * w3 + LANES, RWKV_GATE_RANK)]
    issued, a_tiles = [], []
    for j in range(n_rp):
        issued.append(later[j]())
        w_log = -_softplus(-(w0_ref[:, tile(j)] + wa[:, tile(j)])) - 0.5
        rlw_ref[:, tile(j)] = -jnp.exp(w_log)
        a_tiles.append(_sigmoid(a0_ref[:, tile(j)] + wa[:, w3 + j * PAIR:w3 + (j + 1) * PAIR]))
    zf, zq, z_hg = issued

    z_al = _mm(h, wal_ref[...])
    k = z_rk[:, w3:2 * w3]
    kk = k * kk_ref[...]
    kk = kk / jnp.maximum(jnp.sqrt(_head_sums([kk * kk])[0]), 1e-12)
    rkk_ref[...] = kk

    z_qk = proj(g0, 2 * W_GLA)
    for j in range(n_rp):
        rka_ref[:, tile(j)] = kk[:, tile(j)] * a_tiles[j]
        rk_ref[:, tile(j)] = k[:, tile(j)] * (1.0 + (a_tiles[j] - 1.0) * ka_ref[:, tile(j)])
    lb = lb_ref[...]
    f = lb + (1.0 - lb) * _sigmoid(zf)
    gb_ref[:, 0:W_HGRN] = jnp.log(jnp.maximum(f, F_MIN))
    k_h = (1.0 - lb) * _sigmoid(-zf)
    q_h = _silu(zq) * HEAD_DIM ** -0.5
    for p in range(W_HGRN // PAIR):
        gk_ref[p] = k_h[:, tile(p)]
        gq_ref[p] = q_h[:, tile(p)]

    z_vr = proj(g0 + 2 * W_GLA, 2 * W_GLA)
    rhg_ref[...] = _sigmoid(z_hg).astype(BF16)
    logit = _dot1(z_al, au_ref[...]) + ab_ref[...]
    gb_ref[:, W_HGRN:W_GL] = _log_sigmoid(logit) / GLA_GATE_NORMALIZER
    for p in range(W_GLA // PAIR):
        gq_ref[W_HGRN // PAIR + p] = z_qk[:, tile(p)] * HEAD_DIM ** -0.5
        gk_ref[W_HGRN // PAIR + p] = z_qk[:, W_GLA + p * PAIR:W_GLA + (p + 1) * PAIR]

    zg = proj(3 * W_HGRN, W_HGRN)
    gv_ref[:, W_HGRN:W_GL] = z_vr[:, 0:W_GLA].astype(BF16)
    gg_ref[:, W_HGRN:W_GL] = _silu(z_vr[:, W_GLA:2 * W_GLA]).astype(BF16)
    zi = proj(2 * W_HGRN, W_HGRN)
    gg_ref[:, 0:W_HGRN] = _silu(zg).astype(BF16)
    gv_ref[:, 0:W_HGRN] = zi.astype(BF16)


def _in_projection(x2, mod_l, w_main, w_alpha, w_rwkv, lb_l, alpha_up_p, alpha_b, mu, w0, a0, wa_blk, k_k, k_a,
                   seq, tm, layer):
    m_rows, d = x2.shape
    resident = lambda n: pl.BlockSpec((None, d, n), lambda i: (layer, 0, 0), pipeline_mode=pl.Buffered(1))
    steps_per_batch = seq // tm
    row = lambda i: (i, 0)
    const = lambda i: (0, 0)
    vec = lambda n: pl.BlockSpec((1, n), const)
    out = lambda n, dt: (pl.BlockSpec((tm, n), row), jax.ShapeDtypeStruct((m_rows, n), dt))
    pair_slabs = (pl.BlockSpec((N_GL_PAIRS, tm, PAIR), lambda i: (0, i, 0)),
                  jax.ShapeDtypeStruct((N_GL_PAIRS, m_rows, PAIR), F32))
    outs = [pair_slabs, pair_slabs, out(W_GL, BF16), out(W_GL, F32), out(W_GL, BF16),
            out(W_RWKV, F32), out(W_RWKV, F32), out(W_RWKV, F32), out(W_RWKV, F32), out(W_RWKV, F32),
            out(W_RWKV, F32), out(RWKV_GATE_RANK, BF16)]
    return pl.pallas_call(
        functools.partial(_inproj_kernel, steps_per_batch),
        grid=(m_rows // tm,),
        in_specs=[
            pl.BlockSpec((tm, d), row),
            pl.BlockSpec((1, 6, d), lambda i: (i // steps_per_batch, 0, 0)),
            resident(N_HGRN_COLS + 4 * W_GLA), resident(LANES), resident(N_RWKV_COLS),
            vec(W_HGRN),
            pl.BlockSpec((LANES, W_GLA), const),
            vec(W_GLA),
            vec(N_RWKV_COLS), vec(W_RWKV), vec(W_RWKV),
            pl.BlockSpec((LANES, 2 * W_RWKV), const),
            vec(W_RWKV), vec(W_RWKV),
        ],
        out_specs=[o[0] for o in outs],
        out_shape=[o[1] for o in outs],
        scratch_shapes=[pltpu.VMEM((1, N_RWKV_COLS), F32)],
        compiler_params=pltpu.CompilerParams(
            dimension_semantics=("arbitrary",), vmem_limit_bytes=VMEM_LIMIT),
        name="in_projection",
    )(x2, mod_l, w_main, w_alpha, w_rwkv, lb_l, alpha_up_p, alpha_b, mu, w0, a0, wa_blk, k_k, k_a)


def _gl_kernel(q_s, k_s, v_s, lg_ref, gate_ref, ng_ref, ind_ref, out_ref,
               st_ref, b_s, ad_s, e_s, o_s, sc_s, eend_s):
    tb = q_s.shape[1]
    n_chunks = tb // CHUNK
    n_groups = tb // SUB

    @pl.when(pl.program_id(1) == 0)
    def _():
        st_ref[...] = jnp.zeros_like(st_ref)

    tril = _tril_bf16()
    for c in range(n_chunks):
        rows = slice(c * CHUNK, (c + 1) * CHUNK)
        b = _dot_exact_lhs(tril, lg_ref[rows, :], 2)
        for p in range(N_GL_PAIRS):
            b_s[p, rows, :] = b[:, p * PAIR:(p + 1) * PAIR]

    for t in range(SUB - 1):
        e_s[t * n_groups:(t + 1) * n_groups, (t + 1) * PAIR:] = jnp.zeros((n_groups, (SUB - 1 - t) * PAIR), BF16)
    for p in range(N_GL_PAIRS):
        by_pos = lambda ref, t: ref[p, pl.ds(t, n_groups, stride=SUB), :]
        qs = [by_pos(q_s, t) for t in range(SUB)]
        ks = [by_pos(k_s, t) for t in range(SUB)]
        bs = [by_pos(b_s, t) * LOG2E for t in range(SUB)]
        for t in range(SUB):
            for sg in range(t + 1):
                e = qs[t] * ks[sg]
                if sg < t:
                    e = e * jnp.exp2(jnp.minimum(bs[t] - bs[sg], 0.0))
                e_s[t * n_groups:(t + 1) * n_groups, sg * PAIR:(sg + 1) * PAIR] = e.astype(BF16)
        ad = [jnp.zeros((n_groups, PAIR), F32)] * SUB
        for sg0 in range(0, SUB, 2):
            part = _mm(e_s[sg0 * n_groups:, sg0 * PAIR:(sg0 + 2) * PAIR], ind_ref[sg0 * PAIR:(sg0 + 2) * PAIR, :])
            for t in range(sg0, SUB):
                ad[t] = ad[t] + part[(t - sg0) * n_groups:(t - sg0 + 1) * n_groups, :]
        for t in range(SUB):
            ad_s[p, pl.ds(t, n_groups, stride=SUB), :] = ad[t]

    masks = _pair_masks()
    t_idx = _iota((CHUNK, PAIR), 0)
    second_half = {hs: (t_idx // hs) % 2 == 1 for hs in LEVEL_HALVES}
    level_sign = {hs: jnp.where(second_half[hs], LOG2E, -LOG2E) for hs in LEVEL_HALVES}

    def group_units(i):
        units = []
        for cc in range(A_GROUP_GL):
            c = i * A_GROUP_GL + cc
            rows = pl.ds(pl.multiple_of(c * CHUNK, CHUNK), CHUNK)
            for p in range(N_GL_PAIRS):
                units.append(dict(p=p, rows=rows, lanes=slice(p * PAIR, (p + 1) * PAIR),
                                  e_rows=pl.ds(pl.multiple_of(c * 8, 8), 8),
                                  st_rows=pl.ds(pl.multiple_of((c * N_GL_PAIRS + p) * PAIR, PAIR), PAIR)))
        return units

    def increments(i, carry):
        units = group_units(i)

        def decay_keys(un):
            b = b_s[un["p"], un["rows"], :]
            b_end = b[CHUNK - 1:CHUNK, :]
            un["k_end"] = k_s[un["p"], un["rows"], :] * jnp.exp(b_end - b)
            eend_s[un["e_rows"], un["lanes"]] = jnp.broadcast_to(jnp.exp(b_end), (8, PAIR))

        def outer(un):
            upd = _dot1(v_s[un["rows"], un["lanes"]], un["k_end"], TN)
            sc_s[un["st_rows"], :] = jnp.where(masks["bd"], upd, 0.0)

        _trace_skewed((decay_keys, outer), units, OUT_SKEW)
        return carry

    lax.fori_loop(0, n_chunks // A_GROUP_GL, increments, 0)

    sts = [st_ref[p] for p in range(N_GL_PAIRS)]
    for c in range(n_chunks):
        for p in range(N_GL_PAIRS):
            st_rows = slice((c * N_GL_PAIRS + p) * PAIR, (c * N_GL_PAIRS + p + 1) * PAIR)
            upd = sc_s[st_rows, :]
            sc_s[st_rows, :] = sts[p]
            sts[p] = sts[p] * eend_s[c * 8:c * 8 + 1, p * PAIR:(p + 1) * PAIR] + upd
    for p in range(N_GL_PAIRS):
        st_ref[p] = sts[p]

    def outputs(i, carry):
        units = group_units(i)

        def decays(un):
            q = q_s[un["p"], un["rows"], :]
            k = k_s[un["p"], un["rows"], :]
            b = b_s[un["p"], un["rows"], :]
            un["lv"] = []
            for hs in LEVEL_HALVES:
                e_mid = jnp.concatenate(
                    [jnp.broadcast_to(b[m:m + 1, :], (2 * hs, PAIR)) for m in range(hs - 1, CHUNK, 2 * hs)], axis=0)
                dec = jnp.exp2(jnp.minimum((b - e_mid) * level_sign[hs], 0.0))
                un["lv"].append(jnp.where(second_half[hs], q, k) * dec)
            un["q_in"] = q * jnp.exp(b)

        def scores(un):
            a = jnp.zeros((CHUNK, PAIR), F32)
            for hs, lv in zip(LEVEL_HALVES, un["lv"]):
                a = jnp.where(masks["level"][hs], _dot1(lv, jnp.transpose(_stack_heads(lv))), a)
            un["a"] = jnp.where(masks["diag"], ad_s[un["p"], un["rows"], :], a)

        def combine(un):
            o_s[un["rows"], un["lanes"]] = (_dot1(un["a"], _stack_heads(v_s[un["rows"], un["lanes"]]))
                                            + _dot1(un["q_in"], sc_s[un["st_rows"], :], NT))

        _trace_skewed((decays, scores, combine), units, OUT_SKEW)
        return carry

    lax.fori_loop(0, n_chunks // A_GROUP_GL, outputs, 0)

    o = o_s[...]
    ms = _head_sums([o * o])[0] * (1.0 / HEAD_DIM)
    out_ref[...] = (o * lax.rsqrt(ms + RMS_EPS) * ng_ref[...] * gate_ref[...]).astype(out_ref.dtype)


def _gl_indicator():
    r = np.arange(SUB * PAIR)
    c = np.arange(PAIR)
    sg = r // PAIR
    h = (r % PAIR) // HEAD_DIM
    ind = (h[:, None] == (c // HEAD_DIM)[None, :]) & (sg[:, None] == (c % SUB)[None, :])
    return jnp.asarray(ind, dtype=BF16)


def _gl_mixer(q, k, v, lg, gate, norm_g, bsz, seq, tb):
    steps = seq // tb
    row = lambda b, i: (b * steps + i, 0)
    const = lambda b, i: (0, 0)
    slab = pl.BlockSpec((tb, W_GL), row)
    pair_slabs = pl.BlockSpec((N_GL_PAIRS, tb, PAIR), lambda b, i: (0, b * steps + i, 0))
    scratch_f32 = pltpu.VMEM((tb, W_GL), F32)
    scratch_pairs = pltpu.VMEM((N_GL_PAIRS, tb, PAIR), F32)
    return pl.pallas_call(
        _gl_kernel,
        grid=(bsz, steps),
        in_specs=[
            pair_slabs, pair_slabs, slab, slab, slab,
            pl.BlockSpec((1, W_GL), const),
            pl.BlockSpec((SUB * PAIR, PAIR), const),
        ],
        out_specs=pl.BlockSpec((tb, W_GL), row),
        out_shape=jax.ShapeDtypeStruct((bsz * seq, W_GL), BF16),
        scratch_shapes=[
            pltpu.VMEM((N_GL_PAIRS, PAIR, PAIR), F32),
            scratch_pairs, scratch_pairs,
            pltpu.VMEM((tb, SUB * PAIR), BF16),
            scratch_f32,
            pltpu.VMEM((tb // CHUNK * N_GL_PAIRS * PAIR, PAIR), F32),
            pltpu.VMEM((tb // CHUNK * 8, W_GL), F32),
        ],
        compiler_params=pltpu.CompilerParams(
            dimension_semantics=("arbitrary", "arbitrary"), vmem_limit_bytes=VMEM_LIMIT),
        name="hgrn_gla_mixer",
    )(q, k, v, lg, gate, norm_g, _gl_indicator())


def _rwkv_kernel(r_s, lw_s, k_s, v_s, kk_s, ka_s, hg_s, gup_ref, rk_ref, gng_ref, gnb_ref,
                 out_ref, st_ref, y_s, rt_s, nb_s, nk_s, w_s, u0_s, bkt_s, ecol_s):
    bsz, tb = r_s.shape[0], r_s.shape[1]
    n_chunks = tb // CHUNK

    @pl.when(pl.program_id(0) == 0)
    def _():
        st_ref[...] = jnp.zeros_like(st_ref)

    masks = _pair_masks()
    tril = _tril_bf16()
    eye = jnp.where(masks["eye"], 1.0, 0.0)

    def phase_a(b, carry):
        units = []
        for c in range(n_chunks):
            rows = slice(c * CHUNK, (c + 1) * CHUNK)
            g_all = _dot_exact_lhs(tril, lw_s[b, rows, :], 2)
            for p in range(N_RW_PAIRS):
                lanes = slice(p * PAIR, (p + 1) * PAIR)
                units.append(dict(c=c, p=p, rows=rows, lanes=lanes, g=g_all[:, lanes]))
        def scale(un):
            rows, lanes, g = un["rows"], un["lanes"], un["g"]
            kc = k_s[b, rows, lanes]
            kkc = kk_s[b, rows, lanes]
            ka = ka_s[b, rows, lanes]
            g_end = g[CHUNK - 1:CHUNK, :]
            e_neg = jnp.exp(-g)
            e_end = jnp.exp(g_end - g)
            un["at"] = -kkc * jnp.exp(g - lw_s[b, rows, lanes])
            rt = r_s[b, rows, lanes] * jnp.exp(g)
            un["rt"] = rt
            un["bt"] = ka * e_neg
            un["kt"] = kc * e_neg
            rt_s[b, rows, lanes] = rt
            slot = un["c"] * N_RW_PAIRS + un["p"]
            bkt_s[b, slot] = jnp.transpose(jnp.concatenate([ka * e_end, kc * e_end], axis=0))
            ecol_s[b, slot] = jnp.transpose(jnp.broadcast_to(jnp.exp(g_end), (PAIR, PAIR)))

        def gram(un):
            g2 = _dot1(jnp.concatenate([un["at"], un["rt"]], axis=0),
                       jnp.concatenate([_stack_heads(un["bt"]), _stack_heads(un["kt"])], axis=0), NT)
            un["l"] = jnp.where(masks["strict"], g2[0:CHUNK, 0:PAIR], 0.0)
            un["mk"] = jnp.where(masks["strict"], g2[0:CHUNK, PAIR:2 * PAIR], 0.0)
            nb_s[b, un["rows"], un["lanes"]] = jnp.where(masks["incl"], g2[CHUNK:2 * CHUNK, 0:PAIR], 0.0)
            nk_s[b, un["rows"], un["lanes"]] = jnp.where(masks["incl"], g2[CHUNK:2 * CHUNK, PAIR:2 * PAIR], 0.0)

        def inv_first(un):
            un["t"] = eye + un["l"]
            un["lp"] = _dot1(un["l"], _stack_heads(un["l"]))
            un["mv"] = _dot1(un["mk"], _stack_heads(v_s[b, un["rows"], un["lanes"]]))

        def inv_step(un):
            both = _dot1(jnp.concatenate([un["lp"], un["t"]], axis=0), _stack_heads(un["lp"]))
            un["lp"] = both[0:CHUNK, :]
            un["t"] = un["t"] + both[CHUNK:2 * CHUNK, :]

        def inv_last(un):
            un["t"] = un["t"] + _dot1(un["t"], _stack_heads(un["lp"]))

        def apply_inv(un):
            wu = _dot1(un["t"], jnp.concatenate([_stack_heads(un["at"]), _stack_heads(un["mv"])], axis=1))
            w_s[b, un["rows"], un["lanes"]] = wu[:, 0:PAIR]
            u0_s[b, un["rows"], un["lanes"]] = wu[:, PAIR:2 * PAIR]

        for stage in (scale, gram, inv_first, inv_step, inv_step, inv_step, inv_step, inv_last, apply_inv):
            for un in units:
                stage(un)
        return carry

    lax.fori_loop(0, bsz, phase_a, 0)

    def phase_b(c, carry):
        rows = pl.ds(pl.multiple_of(c * CHUNK, CHUNK), CHUNK)
        seqs = [(b, p, slice(p * PAIR, (p + 1) * PAIR)) for b in range(bsz) for p in range(N_RW_PAIRS)]
        sts = [st_ref[b * N_RW_PAIRS + p] for b, p, _ in seqs]
        vs = [v_s[b, rows, lanes] for b, _, lanes in seqs]
        wrs = [_dot1(jnp.concatenate([w_s[b, rows, lanes], rt_s[b, rows, lanes]], axis=0), st)
               for (b, _, lanes), st in zip(seqs, sts)]
        us = [wr[0:CHUNK, :] + u0_s[b, rows, lanes] for (b, _, lanes), wr in zip(seqs, wrs)]
        upds = [_dot1(bkt_s[b, c * N_RW_PAIRS + p], jnp.concatenate([u, vc], axis=0))
                for (b, p, _), u, vc in zip(seqs, us, vs)]
        for (b, p, _), st, upd in zip(seqs, sts, upds):
            st_ref[b * N_RW_PAIRS + p] = st * ecol_s[b, c * N_RW_PAIRS + p] + jnp.where(masks["bd"], upd, 0.0)
        for (b, _, lanes), wr, u, vc in zip(seqs, wrs, us, vs):
            y_s[b, rows, lanes] = (
                wr[CHUNK:2 * CHUNK, :]
                + _dot1(jnp.concatenate([nb_s[b, rows, lanes], nk_s[b, rows, lanes]], axis=1),
                        jnp.concatenate([_stack_heads(u), _stack_heads(vc)], axis=0)))
        return carry

    lax.fori_loop(0, n_chunks, phase_b, 0)

    for b in range(bsz):
        y = y_s[b]
        s_hi, s_lo = _head_sums(_bf16_parts(y, 2))
        d = y - (s_hi + s_lo) * (1.0 / HEAD_DIM)
        s_var, s_bonus = _head_sums([d * d, r_s[b] * k_s[b] * rk_ref[...]])
        yn = d * lax.rsqrt(s_var * (1.0 / HEAD_DIM) + RWKV_GN_EPS) * gng_ref[...] + gnb_ref[...]
        gate = _dot1(hg_s[b], gup_ref[...])
        out_ref[b] = ((yn + s_bonus * v_s[b]) * gate).astype(out_ref.dtype)


def _rwkv_mixer(r, lw, k, v, kk, ka, hg, g_up, r_k, gn_g, gn_b, bsz, seq, tb):
    blk = lambda i: (0, i, 0)
    const = lambda i: (0, 0)
    vec = pl.BlockSpec((1, W_RWKV), const)
    tok = pl.BlockSpec((bsz, tb, W_RWKV), blk)
    slab = pltpu.VMEM((bsz, tb, W_RWKV), F32)
    unit_tiles = pltpu.VMEM((bsz, tb // CHUNK * N_RW_PAIRS, PAIR, PAIR), F32)
    seq_major = lambda a: a.reshape(bsz, seq, a.shape[-1])
    out = pl.pallas_call(
        _rwkv_kernel,
        grid=(seq // tb,),
        in_specs=[
            tok, tok, tok, tok, tok, tok,
            pl.BlockSpec((bsz, tb, RWKV_GATE_RANK), blk),
            pl.BlockSpec((RWKV_GATE_RANK, W_RWKV), const),
            vec, vec, vec,
        ],
        out_specs=pl.BlockSpec((bsz, tb, W_RWKV), blk),
        out_shape=jax.ShapeDtypeStruct((bsz, seq, W_RWKV), BF16),
        scratch_shapes=[
            pltpu.VMEM((bsz * N_RW_PAIRS, PAIR, PAIR), F32),
            slab,
            slab, slab, slab, slab, slab,
            unit_tiles, unit_tiles,
        ],
        compiler_params=pltpu.CompilerParams(
            dimension_semantics=("arbitrary",), vmem_limit_bytes=VMEM_LIMIT),
        name="rwkv7_mixer",
    )(*[seq_major(a) for a in (r, lw, k, v, kk, ka, hg)], g_up, r_k, gn_g, gn_b)
    return out.reshape(bsz * seq, W_RWKV)


def _post_kernel(alpha, ff_chunk, x_ref, ogl_ref, orw_ref, mod_ref, wout_ref, ln1g_ref, ln1b_ref,
                 wup_ref, wdn_ref, ln2g_ref, ln2b_ref, out_ref):
    m = mod_ref[0]
    gate1, shift2, scale2, gate2 = m[2:3, :], m[3:4, :], m[4:5, :], m[5:6, :]
    d_ff = wup_ref.shape[1]
    n_ff = d_ff // ff_chunk
    half = x_ref.shape[0] // 2
    groups = [slice(0, half), slice(half, 2 * half)]

    def out_proj(rows):
        return (_mm(ogl_ref[rows, :], wout_ref[0:W_GL, :])
                + _mm(orw_ref[rows, :], wout_ref[W_GL:W_GL + W_RWKV, :]))

    def norm1(rows, o):
        x1 = _layer_norm(alpha * x_ref[rows, :] + (1.0 + gate1) * o, ln1g_ref[...], ln1b_ref[...])
        return x1, (x1 * (1.0 + scale2) + shift2).astype(BF16)

    def mlp(h, acc, js):
        for j in js:
            cols = slice(j * ff_chunk, (j + 1) * ff_chunk)
            u = jnp.maximum(_mm(h, wup_ref[:, cols]), 0.0)
            acc = acc + _mm((u * u).astype(BF16), wdn_ref[cols, :])
        return acc

    def norm2(rows, x1, acc):
        out_ref[rows, :] = _layer_norm(alpha * x1 + (1.0 + gate2) * acc, ln2g_ref[...], ln2b_ref[...])

    o_a, o_b = out_proj(groups[0]), out_proj(groups[1])
    x1_a, h_a = norm1(groups[0], o_a)
    acc_a = mlp(h_a, jnp.zeros(x1_a.shape, F32), range(0, 2))
    x1_b, h_b = norm1(groups[1], o_b)
    acc_a = mlp(h_a, acc_a, range(2, n_ff))
    acc_b = mlp(h_b, jnp.zeros(x1_b.shape, F32), range(0, 2))
    norm2(groups[0], x1_a, acc_a)
    acc_b = mlp(h_b, acc_b, range(2, n_ff))
    norm2(groups[1], x1_b, acc_b)


def _post_mixer(x2, ogl, orw, mod_l, w_out, ln1_g, ln1_b, w_up, w_dn, ln2_g, ln2_b, alpha, seq, tm, layer):
    m_rows, d = x2.shape
    d_ff = w_up.shape[2]
    steps_per_batch = seq // tm
    row = lambda i: (i, 0)
    const = lambda i: (0, 0)
    vec = pl.BlockSpec((1, d), const)
    resident = lambda shape: pl.BlockSpec((None,) + shape, lambda i: (layer, 0, 0), pipeline_mode=pl.Buffered(1))
    return pl.pallas_call(
        functools.partial(_post_kernel, alpha, 512),
        grid=(m_rows // tm,),
        in_specs=[
            pl.BlockSpec((tm, d), row),
            pl.BlockSpec((tm, W_GL), row),
            pl.BlockSpec((tm, W_RWKV), row),
            pl.BlockSpec((1, 6, d), lambda i: (i // steps_per_batch, 0, 0)),
            resident((W_GL + W_RWKV, d)),
            vec, vec,
            resident((d, d_ff)),
            resident((d_ff, d)),
            vec, vec,
        ],
        out_specs=pl.BlockSpec((tm, d), row),
        out_shape=jax.ShapeDtypeStruct((m_rows, d), F32),
        compiler_params=pltpu.CompilerParams(
            dimension_semantics=("arbitrary",), vmem_limit_bytes=VMEM_LIMIT),
        name="outproj_mlp",
    )(x2, ogl, orw, mod_l, w_out, ln1_g, ln1_b, w_up, w_dn, ln2_g, ln2_b)


def kernel(x, c, hgrn_lb_logits, ada_w, ada_b, w_in, hgrn_norm_g, gla_alpha_up, gla_alpha_b, gla_norm_g,
           rwkv_mu, rwkv_w0, rwkv_w_up, rwkv_a0, rwkv_a_up, rwkv_g_up, rwkv_k_k, rwkv_k_a, rwkv_r_k,
           rwkv_gn_g, rwkv_gn_b, w_out, ln1_g, ln1_b, mlp_w_up, mlp_w_down, ln2_g, ln2_b):
    bsz, seq, d = x.shape
    depth = w_in.shape[0]
    alpha = (2.0 * depth) ** 0.25
    tm = min(512, seq)
    tb = min(512, seq)
    tb_gl = min(1024, seq)
    tm_post = min(1024, seq)
    assert all(seq % t == 0 for t in (tm, tb, tb_gl, tm_post)) and tb % CHUNK == 0
    assert tb_gl % (A_GROUP_GL * CHUNK) == 0 and tm_post % 16 == 0
    assert w_in.shape[2] == N_HGRN_COLS + N_GLA_COLS + N_RWKV_COLS

    mod = _modulation(c, ada_w, ada_b).reshape(depth, bsz, 6, d)
    lbs = _lower_bounds(hgrn_lb_logits)

    gla_end = N_HGRN_COLS + N_GLA_COLS
    w_in_b = w_in.astype(BF16)
    w_alpha = jnp.pad(w_in_b[:, :, gla_end - GLA_GATE_RANK:gla_end], ((0, 0), (0, 0), (0, LANES - GLA_GATE_RANK)))
    w_rwkv = w_in_b[:, :, gla_end:]
    alpha_up_p = jnp.concatenate(
        [gla_alpha_up, jnp.zeros((depth, LANES - GLA_GATE_RANK, W_GLA), gla_alpha_up.dtype)], axis=1)
    zeros_r = jnp.zeros((depth, RWKV_DECAY_RANK, W_RWKV), rwkv_w_up.dtype)
    wa_blk = jnp.concatenate(
        [jnp.concatenate([rwkv_w_up, zeros_r], axis=2), jnp.concatenate([zeros_r, rwkv_a_up], axis=2)], axis=1)
    norm_g = jnp.concatenate([hgrn_norm_g, gla_norm_g], axis=1)
    w_out_b = w_out.astype(BF16)
    w_up_b = mlp_w_up.astype(BF16)
    w_dn_b = mlp_w_down.astype(BF16)
    vec = lambda a, l: a[l].reshape(1, -1)

    x2 = x.reshape(bsz * seq, d)
    for l in range(depth):
        gq, gk, gv, glg, ggate, rr, rlw, rk, rv, rkk, rka, rhg = _in_projection(
            x2, mod[l], w_in_b, w_alpha, w_rwkv, vec(lbs, l), alpha_up_p[l], vec(gla_alpha_b, l), vec(rwkv_mu, l),
            vec(rwkv_w0, l), vec(rwkv_a0, l), wa_blk[l], vec(rwkv_k_k, l), vec(rwkv_k_a, l), seq, tm, l)
        ogl = _gl_mixer(gq, gk, gv, glg, ggate, vec(norm_g, l), bsz, seq, tb_gl)
        orw = _rwkv_mixer(rr, rlw, rk, rv, rkk, rka, rhg, rwkv_g_up[l], vec(rwkv_r_k, l), vec(rwkv_gn_g, l),
                          vec(rwkv_gn_b, l), bsz, seq, tb)
        x2 = _post_mixer(x2, ogl, orw, mod[l], w_out_b, vec(ln1_g, l), vec(ln1_b, l), w_up_b, w_dn_b,
                         vec(ln2_g, l), vec(ln2_b, l), alpha, seq, tm_post, l)
    return x2.reshape(bsz, seq, d)
```

```python
import functools

import numpy as np
import jax
import jax.numpy as jnp
from jax import lax
from jax.experimental import pallas as pl
from jax.experimental.pallas import tpu as pltpu

F32 = jnp.float32
BF16 = jnp.bfloat16

HEAD_DIM = 64
H_HGRN, H_GLA, H_RWKV = 4, 6, 6
W_HGRN, W_GLA, W_RWKV = H_HGRN * HEAD_DIM, H_GLA * HEAD_DIM, H_RWKV * HEAD_DIM
GLA_GATE_RANK = 16
GLA_GATE_NORMALIZER = 16.0
RWKV_DECAY_RANK, RWKV_ICLR_RANK, RWKV_GATE_RANK = 64, 64, 128
RWKV_GN_EPS = 64e-5
N_HGRN_COLS = 4 * W_HGRN
N_GLA_COLS = 4 * W_GLA + GLA_GATE_RANK
N_RWKV_COLS = 3 * W_RWKV + RWKV_DECAY_RANK + RWKV_ICLR_RANK + RWKV_GATE_RANK
CHUNK = 64
LN_EPS = 1e-5
RMS_EPS = 1e-5
F_MIN = 1e-30

LANES = 128
PAIR = 2 * HEAD_DIM
SUB = 8
LEVEL_HALVES = (SUB, 2 * SUB, 4 * SUB)
assert 2 * LEVEL_HALVES[-1] == CHUNK
A_GROUP_GL = 16
OUT_SKEW = 4
LOG2E = 1.4426950408889634
VMEM_LIMIT = 56 * 1024 * 1024

W_GL = W_HGRN + W_GLA
N_GL_PAIRS = W_GL // PAIR
N_RW_PAIRS = W_RWKV // PAIR

NN = (((1,), (0,)), ((), ()))
NT = (((1,), (1,)), ((), ()))
TN = (((0,), (0,)), ((), ()))


def _mm(a, b, dims=NN):
    return lax.dot_general(a, b, dims, preferred_element_type=F32)


def _split2(x):
    hi = x.astype(BF16)
    lo = (x - hi.astype(F32)).astype(BF16)
    return hi, lo


def _bf16_parts(x, n):
    parts = []
    for i in range(n):
        p = x.astype(BF16)
        parts.append(p)
        if i + 1 < n:
            x = x - p.astype(F32)
    return parts


def _dot1(a, b, dims=NN):
    return _mm(a.astype(BF16), b.astype(BF16), dims)


def _dot3(a, b, dims=NN):
    ah, al = _split2(a)
    bh, bl = _split2(b)
    return _mm(ah, bh, dims) + (_mm(ah, bl, dims) + _mm(al, bh, dims))


def _dot_exact_lhs(a_bf16, b, n):
    return _mm(jnp.concatenate([a_bf16] * n, axis=1), jnp.concatenate(_bf16_parts(b, n), axis=0))


def _sigmoid(x):
    return 1.0 / (1.0 + jnp.exp(-x))


def _silu(x):
    return x * _sigmoid(x)


def _softplus(x):
    return jnp.maximum(x, 0.0) + jnp.log1p(jnp.exp(-jnp.abs(x)))


def _log_sigmoid(x):
    return -_softplus(-x)


def _layer_norm(y, g, b):
    mu = jnp.mean(y, axis=-1, keepdims=True)
    d = y - mu
    var = jnp.mean(d * d, axis=-1, keepdims=True)
    return d * lax.rsqrt(var + LN_EPS) * g + b


def _iota(shape, axis):
    return lax.broadcasted_iota(jnp.int32, shape, axis)


def _stack_heads(x):
    lane = _iota(x.shape, 1)
    return jnp.concatenate([jnp.where(lane < HEAD_DIM, x, 0.0), jnp.where(lane >= HEAD_DIM, x, 0.0)], axis=0)


def _pair_masks():
    t = _iota((CHUNK, PAIR), 0)
    s = _iota((CHUNK, PAIR), 1) % HEAD_DIM
    r2 = _iota((PAIR, PAIR), 0) // HEAD_DIM
    c2 = _iota((PAIR, PAIR), 1) // HEAD_DIM
    return dict(
        strict=s < t,
        incl=s <= t,
        eye=s == t,
        diag=(s // SUB == t // SUB) & (s <= t),
        level={hs: (s // (2 * hs) == t // (2 * hs)) & ((t // hs) % 2 == 1) & ((s // hs) % 2 == 0)
               for hs in LEVEL_HALVES},
        bd=r2 == c2,
    )


def _trace_skewed(stages, units, skew):
    for step in range(len(units) + skew * (len(stages) - 1)):
        for s, stage in enumerate(stages):
            u = step - skew * s
            if 0 <= u < len(units):
                stage(units[u])


def _tril_bf16():
    return jnp.where(_iota((CHUNK, CHUNK), 1) <= _iota((CHUNK, CHUNK), 0), 1.0, 0.0).astype(BF16)


def _seg_mean_matrix(width):
    r = _iota((width, width), 0) // HEAD_DIM
    c = _iota((width, width), 1) // HEAD_DIM
    return jnp.where(r == c, 1.0, 0.0).astype(BF16)


def _head_sums(xs):
    tiles = [x[:, j:j + LANES].astype(BF16) for x in xs for j in range(0, x.shape[1], LANES)]
    sums = []
    for i in range(0, len(tiles) - 1, 2):
        s = _mm(jnp.concatenate(tiles[i:i + 2], axis=1), _seg_mean_matrix(2 * LANES))
        sums += [s[:, 0:LANES], s[:, LANES:2 * LANES]]
    if len(tiles) % 2:
        sums.append(_mm(tiles[-1], _seg_mean_matrix(LANES)))
    out, i = [], 0
    for x in xs:
        n = x.shape[1] // LANES
        out.append(jnp.concatenate(sums[i:i + n], axis=1))
        i += n
    return out


def _mod_kernel(c_ref, w_ref, b_ref, o_ref):
    c = _silu(c_ref[...])
    o_ref[0] = _dot3(c, w_ref[0]) + b_ref[0]


def _modulation(c, ada_w, ada_b):
    depth, d, n = ada_w.shape
    bsz = c.shape[0]
    tn = 1536
    return pl.pallas_call(
        _mod_kernel,
        grid=(depth, n // tn),
        in_specs=[
            pl.BlockSpec((bsz, d), lambda l, j: (0, 0)),
            pl.BlockSpec((1, d, tn), lambda l, j: (l, 0, j)),
            pl.BlockSpec((1, 1, tn), lambda l, j: (l, 0, j)),
        ],
        out_specs=pl.BlockSpec((1, bsz, tn), lambda l, j: (l, 0, j)),
        out_shape=jax.ShapeDtypeStruct((depth, bsz, n), F32),
        compiler_params=pltpu.CompilerParams(
            dimension_semantics=("arbitrary", "arbitrary"), vmem_limit_bytes=VMEM_LIMIT),
        name="adaln_modulation",
    )(c, ada_w, ada_b.reshape(depth, 1, n))


def _lower_bound_kernel(x_ref, o_ref):
    depth = x_ref.shape[0]
    rows = [x_ref[l:l + 1, :] for l in range(depth)]
    m = functools.reduce(jnp.maximum, rows)
    e = [jnp.exp(r - m) for r in rows]
    tot = functools.reduce(lambda a, b: a + b, e)
    p = [ei / tot for ei in e]
    acc = jnp.zeros_like(p[0])
    for l in range(depth):
        acc = acc + p[l]
        o_ref[l:l + 1, :] = acc - p[0]


def _lower_bounds(logits):
    return pl.pallas_call(
        _lower_bound_kernel,
        out_shape=jax.ShapeDtypeStruct(logits.shape, F32),
        name="hgrn_lower_bounds",
    )(logits.astype(F32))


def _inproj_kernel(steps_per_batch, x_ref, mod_ref, w_ref, wal_ref, wr_ref, lb_ref, au_ref, ab_ref, mu_ref,
                   w0_ref, a0_ref, wa_ref, kk_ref, ka_ref,
                   gq_ref, gk_ref, gv_ref, gb_ref, gg_ref,
                   rr_ref, rlw_ref, rk_ref, rv_ref, rkk_ref, rka_ref, rhg_ref, carry_s):
    tm = x_ref.shape[0]
    w3 = W_RWKV

    @pl.when(pl.program_id(0) % steps_per_batch == 0)
    def _():
        carry_s[...] = jnp.zeros_like(carry_s)

    m = mod_ref[0]
    h = (x_ref[...] * (1.0 + m[1:2, :]) + m[0:1, :]).astype(BF16)

    def proj(c0, width):
        return _mm(h, w_ref[:, c0:c0 + width])

    first_row = _iota((tm, LANES), 0) == 0

    def shifted(c0, width):
        z = _mm(h, wr_ref[:, c0:c0 + width])
        z_prev = pltpu.roll(z, 1, axis=0)
        z_prev = jnp.where(jnp.concatenate([first_row] * (width // LANES), axis=1), carry_s[:, c0:c0 + width], z_prev)
        carry_s[:, c0:c0 + width] = z[tm - 1:tm, :]
        return z + (z_prev - z) * mu_ref[:, c0:c0 + width]

    g0 = N_HGRN_COLS
    n_rp = w3 // PAIR
    tile = lambda j: slice(j * PAIR, (j + 1) * PAIR)
    z_v = shifted(2 * w3, w3 + LANES)
    z_rk = shifted(0, 2 * w3)

    h_wa = z_v[:, w3:w3 + LANES]
    h_wa = jnp.where(_iota(h_wa.shape, 1) < RWKV_DECAY_RANK, jnp.tanh(h_wa), h_wa)
    wa = _dot1(h_wa, wa_ref[...])
    rv_ref[...] = z_v[:, 0:w3]
    rr_ref[...] = z_rk[:, 0:w3]

    later = [lambda: proj(W_HGRN, W_HGRN), lambda: proj(0, W_HGRN),
             lambda: shifted(3 * w3 + LANES, RWKV_GATE_RANK)]
    issued, a_tiles = [], []
    for j in range(n_rp):
        issued.append(later[j]())
        w_log = -_softplus(-(w0_ref[:, tile(j)] + wa[:, tile(j)])) - 0.5
        rlw_ref[:, tile(j)] = -jnp.exp(w_log)
        a_tiles.append(_sigmoid(a0_ref[:, tile(j)] + wa[:, w3 + j * PAIR:w3 + (j + 1) * PAIR]))
    zf, zq, z_hg = issued

    z_al = _mm(h, wal_ref[...])
    k = z_rk[:, w3:2 * w3]
    kk = k * kk_ref[...]
    kk = kk / jnp.maximum(jnp.sqrt(_head_sums([kk * kk])[0]), 1e-12)
    rkk_ref[...] = kk

    z_qk = proj(g0, 2 * W_GLA)
    for j in range(n_rp):
        rka_ref[:, tile(j)] = kk[:, tile(j)] * a_tiles[j]
        rk_ref[:, tile(j)] = k[:, tile(j)] * (1.0 + (a_tiles[j] - 1.0) * ka_ref[:, tile(j)])
    lb = lb_ref[...]
    f = lb + (1.0 - lb) * _sigmoid(zf)
    gb_ref[:, 0:W_HGRN] = jnp.log(jnp.maximum(f, F_MIN))
    k_h = (1.0 - lb) * _sigmoid(-zf)
    q_h = _silu(zq) * HEAD_DIM ** -0.5
    for p in range(W_HGRN // PAIR):
        gk_ref[p] = k_h[:, tile(p)]
        gq_ref[p] = q_h[:, tile(p)]

    z_vr = proj(g0 + 2 * W_GLA, 2 * W_GLA)
    rhg_ref[...] = _sigmoid(z_hg).astype(BF16)
    logit = _dot1(z_al, au_ref[...]) + ab_ref[...]
    gb_ref[:, W_HGRN:W_GL] = _log_sigmoid(logit) / GLA_GATE_NORMALIZER
    for p in range(W_GLA // PAIR):
        gq_ref[W_HGRN // PAIR + p] = z_qk[:, tile(p)] * HEAD_DIM ** -0.5
        gk_ref[W_HGRN // PAIR + p] = z_qk[:, W_GLA + p * PAIR:W_GLA + (p + 1) * PAIR]

    zg = proj(3 * W_HGRN, W_HGRN)
    gv_ref[:, W_HGRN:W_GL] = z_vr[:, 0:W_GLA].astype(BF16)
    gg_ref[:, W_HGRN:W_GL] = _silu(z_vr[:, W_GLA:2 * W_GLA]).astype(BF16)
    zi = proj(2 * W_HGRN, W_HGRN)
    gg_ref[:, 0:W_HGRN] = _silu(zg).astype(BF16)
    gv_ref[:, 0:W_HGRN] = zi.astype(BF16)


def _in_projection(x2, mod_l, w_main, w_alpha, w_rwkv, lb_l, alpha_up_p, alpha_b, mu, w0, a0, wa_blk, k_k, k_a,
                   seq, tm, layer):
    m_rows, d = x2.shape
    resident = lambda n: pl.BlockSpec((None, d, n), lambda i: (layer, 0, 0), pipeline_mode=pl.Buffered(1))
    steps_per_batch = seq // tm
    row = lambda i: (i, 0)
    const = lambda i: (0, 0)
    vec = lambda n: pl.BlockSpec((1, n), const)
    out = lambda n, dt: (pl.BlockSpec((tm, n), row), jax.ShapeDtypeStruct((m_rows, n), dt))
    pair_slabs = (pl.BlockSpec((N_GL_PAIRS, tm, PAIR), lambda i: (0, i, 0)),
                  jax.ShapeDtypeStruct((N_GL_PAIRS, m_rows, PAIR), F32))
    outs = [pair_slabs, pair_slabs, out(W_GL, BF16), out(W_GL, F32), out(W_GL, BF16),
            out(W_RWKV, F32), out(W_RWKV, F32), out(W_RWKV, F32), out(W_RWKV, F32), out(W_RWKV, F32),
            out(W_RWKV, F32), out(RWKV_GATE_RANK, BF16)]
    return pl.pallas_call(
        functools.partial(_inproj_kernel, steps_per_batch),
        grid=(m_rows // tm,),
        in_specs=[
            pl.BlockSpec((tm, d), row),
            pl.BlockSpec((1, 6, d), lambda i: (i // steps_per_batch, 0, 0)),
            resident(N_HGRN_COLS + 4 * W_GLA), resident(LANES), resident(N_RWKV_COLS),
            vec(W_HGRN),
            pl.BlockSpec((LANES, W_GLA), const),
            vec(W_GLA),
            vec(N_RWKV_COLS), vec(W_RWKV), vec(W_RWKV),
            pl.BlockSpec((LANES, 2 * W_RWKV), const),
            vec(W_RWKV), vec(W_RWKV),
        ],
        out_specs=[o[0] for o in outs],
        out_shape=[o[1] for o in outs],
        scratch_shapes=[pltpu.VMEM((1, N_RWKV_COLS), F32)],
        compiler_params=pltpu.CompilerParams(
            dimension_semantics=("arbitrary",), vmem_limit_bytes=VMEM_LIMIT),
        name="in_projection",
    )(x2, mod_l, w_main, w_alpha, w_rwkv, lb_l, alpha_up_p, alpha_b, mu, w0, a0, wa_blk, k_k, k_a)


def _gl_kernel(q_s, k_s, v_s, lg_ref, gate_ref, ng_ref, ind_ref, out_ref,
               st_ref, b_s, ad_s, e_s, o_s, sc_s, eend_s):
    tb = q_s.shape[1]
    n_chunks = tb // CHUNK
    n_groups = tb // SUB

    @pl.when(pl.program_id(1) == 0)
    def _():
        st_ref[...] = jnp.zeros_like(st_ref)

    tril = _tril_bf16()
    for c in range(n_chunks):
        rows = slice(c * CHUNK, (c + 1) * CHUNK)
        b = _dot_exact_lhs(tril, lg_ref[rows, :], 2)
        for p in range(N_GL_PAIRS):
            b_s[p, rows, :] = b[:, p * PAIR:(p + 1) * PAIR]

    for t in range(SUB - 1):
        e_s[t * n_groups:(t + 1) * n_groups, (t + 1) * PAIR:] = jnp.zeros((n_groups, (SUB - 1 - t) * PAIR), BF16)
    for p in range(N_GL_PAIRS):
        by_pos = lambda ref, t: ref[p, pl.ds(t, n_groups, stride=SUB), :]
        qs = [by_pos(q_s, t) for t in range(SUB)]
        ks = [by_pos(k_s, t) for t in range(SUB)]
        bs = [by_pos(b_s, t) * LOG2E for t in range(SUB)]
        for t in range(SUB):
            for sg in range(t + 1):
                e = qs[t] * ks[sg]
                if sg < t:
                    e = e * jnp.exp2(jnp.minimum(bs[t] - bs[sg], 0.0))
                e_s[t * n_groups:(t + 1) * n_groups, sg * PAIR:(sg + 1) * PAIR] = e.astype(BF16)
        ad = [jnp.zeros((n_groups, PAIR), F32)] * SUB
        for sg0 in range(0, SUB, 2):
            part = _mm(e_s[sg0 * n_groups:, sg0 * PAIR:(sg0 + 2) * PAIR], ind_ref[sg0 * PAIR:(sg0 + 2) * PAIR, :])
            for t in range(sg0, SUB):
                ad[t] = ad[t] + part[(t - sg0) * n_groups:(t - sg0 + 1) * n_groups, :]
        for t in range(SUB):
            ad_s[p, pl.ds(t, n_groups, stride=SUB), :] = ad[t]

    masks = _pair_masks()
    t_idx = _iota((CHUNK, PAIR), 0)
    second_half = {hs: (t_idx // hs) % 2 == 1 for hs in LEVEL_HALVES}
    level_sign = {hs: jnp.where(second_half[hs], LOG2E, -LOG2E) for hs in LEVEL_HALVES}

    def group_units(i):
        units = []
        for cc in range(A_GROUP_GL):
            c = i * A_GROUP_GL + cc
            rows = pl.ds(pl.multiple_of(c * CHUNK, CHUNK), CHUNK)
            for p in range(N_GL_PAIRS):
                units.append(dict(p=p, rows=rows, lanes=slice(p * PAIR, (p + 1) * PAIR),
                                  e_rows=pl.ds(pl.multiple_of(c * 8, 8), 8),
                                  st_rows=pl.ds(pl.multiple_of((c * N_GL_PAIRS + p) * PAIR, PAIR), PAIR)))
        return units

    def increments(i, carry):
        units = group_units(i)

        def decay_keys(un):
            b = b_s[un["p"], un["rows"], :]
            b_end = b[CHUNK - 1:CHUNK, :]
            un["k_end"] = k_s[un["p"], un["rows"], :] * jnp.exp(b_end - b)
            eend_s[un["e_rows"], un["lanes"]] = jnp.broadcast_to(jnp.exp(b_end), (8, PAIR))

        def outer(un):
            upd = _dot1(v_s[un["rows"], un["lanes"]], un["k_end"], TN)
            sc_s[un["st_rows"], :] = jnp.where(masks["bd"], upd, 0.0)

        _trace_skewed((decay_keys, outer), units, OUT_SKEW)
        return carry

    lax.fori_loop(0, n_chunks // A_GROUP_GL, increments, 0)

    sts = [st_ref[p] for p in range(N_GL_PAIRS)]
    for c in range(n_chunks):
        for p in range(N_GL_PAIRS):
            st_rows = slice((c * N_GL_PAIRS + p) * PAIR, (c * N_GL_PAIRS + p + 1) * PAIR)
            upd = sc_s[st_rows, :]
            sc_s[st_rows, :] = sts[p]
            sts[p] = sts[p] * eend_s[c * 8:c * 8 + 1, p * PAIR:(p + 1) * PAIR] + upd
    for p in range(N_GL_PAIRS):
        st_ref[p] = sts[p]

    def outputs(i, carry):
        units = group_units(i)

        def decays(un):
            q = q_s[un["p"], un["rows"], :]
            k = k_s[un["p"], un["rows"], :]
            b = b_s[un["p"], un["rows"], :]
            un["lv"] = []
            for hs in LEVEL_HALVES:
                e_mid = jnp.concatenate(
                    [jnp.broadcast_to(b[m:m + 1, :], (2 * hs, PAIR)) for m in range(hs - 1, CHUNK, 2 * hs)], axis=0)
                dec = jnp.exp2(jnp.minimum((b - e_mid) * level_sign[hs], 0.0))
                un["lv"].append(jnp.where(second_half[hs], q, k) * dec)
            un["q_in"] = q * jnp.exp(b)

        def scores(un):
            a = jnp.zeros((CHUNK, PAIR), F32)
            for hs, lv in zip(LEVEL_HALVES, un["lv"]):
                a = jnp.where(masks["level"][hs], _dot1(lv, jnp.transpose(_stack_heads(lv))), a)
            un["a"] = jnp.where(masks["diag"], ad_s[un["p"], un["rows"], :], a)

        def combine(un):
            o_s[un["rows"], un["lanes"]] = (_dot1(un["a"], _stack_heads(v_s[un["rows"], un["lanes"]]))
                                            + _dot1(un["q_in"], sc_s[un["st_rows"], :], NT))

        _trace_skewed((decays, scores, combine), units, OUT_SKEW)
        return carry

    lax.fori_loop(0, n_chunks // A_GROUP_GL, outputs, 0)

    o = o_s[...]
    ms = _head_sums([o * o])[0] * (1.0 / HEAD_DIM)
    out_ref[...] = (o * lax.rsqrt(ms + RMS_EPS) * ng_ref[...] * gate_ref[...]).astype(out_ref.dtype)


def _gl_indicator():
    r = np.arange(SUB * PAIR)
    c = np.arange(PAIR)
    sg = r // PAIR
    h = (r % PAIR) // HEAD_DIM
    ind = (h[:, None] == (c // HEAD_DIM)[None, :]) & (sg[:, None] == (c % SUB)[None, :])
    return jnp.asarray(ind, dtype=BF16)


def _gl_mixer(q, k, v, lg, gate, norm_g, bsz, seq, tb):
    steps = seq // tb
    row = lambda b, i: (b * steps + i, 0)
    const = lambda b, i: (0, 0)
    slab = pl.BlockSpec((tb, W_GL), row)
    pair_slabs = pl.BlockSpec((N_GL_PAIRS, tb, PAIR), lambda b, i: (0, b * steps + i, 0))
    scratch_f32 = pltpu.VMEM((tb, W_GL), F32)
    scratch_pairs = pltpu.VMEM((N_GL_PAIRS, tb, PAIR), F32)
    return pl.pallas_call(
        _gl_kernel,
        grid=(bsz, steps),
        in_specs=[
            pair_slabs, pair_slabs, slab, slab, slab,
            pl.BlockSpec((1, W_GL), const),
            pl.BlockSpec((SUB * PAIR, PAIR), const),
        ],
        out_specs=pl.BlockSpec((tb, W_GL), row),
        out_shape=jax.ShapeDtypeStruct((bsz * seq, W_GL), BF16),
        scratch_shapes=[
            pltpu.VMEM((N_GL_PAIRS, PAIR, PAIR), F32),
            scratch_pairs, scratch_pairs,
            pltpu.VMEM((tb, SUB * PAIR), BF16),
            scratch_f32,
            pltpu.VMEM((tb // CHUNK * N_GL_PAIRS * PAIR, PAIR), F32),
            pltpu.VMEM((tb // CHUNK * 8, W_GL), F32),
        ],
        compiler_params=pltpu.CompilerParams(
            dimension_semantics=("arbitrary", "arbitrary"), vmem_limit_bytes=VMEM_LIMIT),
        name="hgrn_gla_mixer",
    )(q, k, v, lg, gate, norm_g, _gl_indicator())


def _rwkv_kernel(r_s, lw_s, k_s, v_s, kk_s, ka_s, hg_s, gup_ref, rk_ref, gng_ref, gnb_ref,
                 out_ref, st_ref, y_s, rt_s, nb_s, nk_s, w_s, u0_s, bkt_s, ecol_s):
    bsz, tb = r_s.shape[0], r_s.shape[1]
    n_chunks = tb // CHUNK

    @pl.when(pl.program_id(0) == 0)
    def _():
        st_ref[...] = jnp.zeros_like(st_ref)

    masks = _pair_masks()
    tril = _tril_bf16()
    eye = jnp.where(masks["eye"], 1.0, 0.0)

    def phase_a(b, carry):
        units = []
        for c in range(n_chunks):
            rows = slice(c * CHUNK, (c + 1) * CHUNK)
            g_all = _dot_exact_lhs(tril, lw_s[b, rows, :], 2)
            for p in range(N_RW_PAIRS):
                lanes = slice(p * PAIR, (p + 1) * PAIR)
                units.append(dict(c=c, p=p, rows=rows, lanes=lanes, g=g_all[:, lanes]))
        def scale(un):
            rows, lanes, g = un["rows"], un["lanes"], un["g"]
            kc = k_s[b, rows, lanes]
            kkc = kk_s[b, rows, lanes]
            ka = ka_s[b, rows, lanes]
            g_end = g[CHUNK - 1:CHUNK, :]
            e_neg = jnp.exp(-g)
            e_end = jnp.exp(g_end - g)
            un["at"] = -kkc * jnp.exp(g - lw_s[b, rows, lanes])
            rt = r_s[b, rows, lanes] * jnp.exp(g)
            un["rt"] = rt
            un["bt"] = ka * e_neg
            un["kt"] = kc * e_neg
            rt_s[b, rows, lanes] = rt
            slot = un["c"] * N_RW_PAIRS + un["p"]
            bkt_s[b, slot] = jnp.transpose(jnp.concatenate([ka * e_end, kc * e_end], axis=0))
            ecol_s[b, slot] = jnp.transpose(jnp.broadcast_to(jnp.exp(g_end), (PAIR, PAIR)))

        def gram(un):
            g2 = _dot1(jnp.concatenate([un["at"], un["rt"]], axis=0),
                       jnp.concatenate([_stack_heads(un["bt"]), _stack_heads(un["kt"])], axis=0), NT)
            un["l"] = jnp.where(masks["strict"], g2[0:CHUNK, 0:PAIR], 0.0)
            un["mk"] = jnp.where(masks["strict"], g2[0:CHUNK, PAIR:2 * PAIR], 0.0)
            nb_s[b, un["rows"], un["lanes"]] = jnp.where(masks["incl"], g2[CHUNK:2 * CHUNK, 0:PAIR], 0.0)
            nk_s[b, un["rows"], un["lanes"]] = jnp.where(masks["incl"], g2[CHUNK:2 * CHUNK, PAIR:2 * PAIR], 0.0)

        def inv_first(un):
            un["t"] = eye + un["l"]
            un["lp"] = _dot1(un["l"], _stack_heads(un["l"]))
            un["mv"] = _dot1(un["mk"], _stack_heads(v_s[b, un["rows"], un["lanes"]]))

        def inv_step(un):
            both = _dot1(jnp.concatenate([un["lp"], un["t"]], axis=0), _stack_heads(un["lp"]))
            un["lp"] = both[0:CHUNK, :]
            un["t"] = un["t"] + both[CHUNK:2 * CHUNK, :]

        def inv_last(un):
            un["t"] = un["t"] + _dot1(un["t"], _stack_heads(un["lp"]))

        def apply_inv(un):
            wu = _dot1(un["t"], jnp.concatenate([_stack_heads(un["at"]), _stack_heads(un["mv"])], axis=1))
            w_s[b, un["rows"], un["lanes"]] = wu[:, 0:PAIR]
            u0_s[b, un["rows"], un["lanes"]] = wu[:, PAIR:2 * PAIR]

        for stage in (scale, gram, inv_first, inv_step, inv_step, inv_step, inv_step, inv_last, apply_inv):
            for un in units:
                stage(un)
        return carry

    lax.fori_loop(0, bsz, phase_a, 0)

    def phase_b(c, carry):
        rows = pl.ds(pl.multiple_of(c * CHUNK, CHUNK), CHUNK)
        seqs = [(b, p, slice(p * PAIR, (p + 1) * PAIR)) for b in range(bsz) for p in range(N_RW_PAIRS)]
        sts = [st_ref[b * N_RW_PAIRS + p] for b, p, _ in seqs]
        vs = [v_s[b, rows, lanes] for b, _, lanes in seqs]
        wrs = [_dot1(jnp.concatenate([w_s[b, rows, lanes], rt_s[b, rows, lanes]], axis=0), st)
               for (b, _, lanes), st in zip(seqs, sts)]
        us = [wr[0:CHUNK, :] + u0_s[b, rows, lanes] for (b, _, lanes), wr in zip(seqs, wrs)]
        upds = [_dot1(bkt_s[b, c * N_RW_PAIRS + p], jnp.concatenate([u, vc], axis=0))
                for (b, p, _), u, vc in zip(seqs, us, vs)]
        for (b, p, _), st, upd in zip(seqs, sts, upds):
            st_ref[b * N_RW_PAIRS + p] = st * ecol_s[b, c * N_RW_PAIRS + p] + jnp.where(masks["bd"], upd, 0.0)
        for (b, _, lanes), wr, u, vc in zip(seqs, wrs, us, vs):
            y_s[b, rows, lanes] = (
                wr[CHUNK:2 * CHUNK, :]
                + _dot1(jnp.concatenate([nb_s[b, rows, lanes], nk_s[b, rows, lanes]], axis=1),
                        jnp.concatenate([_stack_heads(u), _stack_heads(vc)], axis=0)))
        return carry

    lax.fori_loop(0, n_chunks, phase_b, 0)

    for b in range(bsz):
        y = y_s[b]
        s_hi, s_lo = _head_sums(_bf16_parts(y, 2))
        d = y - (s_hi + s_lo) * (1.0 / HEAD_DIM)
        s_var, s_bonus = _head_sums([d * d, r_s[b] * k_s[b] * rk_ref[...]])
        yn = d * lax.rsqrt(s_var * (1.0 / HEAD_DIM) + RWKV_GN_EPS) * gng_ref[...] + gnb_ref[...]
        gate = _dot1(hg_s[b], gup_ref[...])
        out_ref[b] = ((yn + s_bonus * v_s[b]) * gate).astype(out_ref.dtype)


def _rwkv_mixer(r, lw, k, v, kk, ka, hg, g_up, r_k, gn_g, gn_b, bsz, seq, tb):
    blk = lambda i: (0, i, 0)
    const = lambda i: (0, 0)
    vec = pl.BlockSpec((1, W_RWKV), const)
    tok = pl.BlockSpec((bsz, tb, W_RWKV), blk)
    slab = pltpu.VMEM((bsz, tb, W_RWKV), F32)
    unit_tiles = pltpu.VMEM((bsz, tb // CHUNK * N_RW_PAIRS, PAIR, PAIR), F32)
    seq_major = lambda a: a.reshape(bsz, seq, a.shape[-1])
    out = pl.pallas_call(
        _rwkv_kernel,
        grid=(seq // tb,),
        in_specs=[
            tok, tok, tok, tok, tok, tok,
            pl.BlockSpec((bsz, tb, RWKV_GATE_RANK), blk),
            pl.BlockSpec((RWKV_GATE_RANK, W_RWKV), const),
            vec, vec, vec,
        ],
        out_specs=pl.BlockSpec((bsz, tb, W_RWKV), blk),
        out_shape=jax.ShapeDtypeStruct((bsz, seq, W_RWKV), BF16),
        scratch_shapes=[
            pltpu.VMEM((bsz * N_RW_PAIRS, PAIR, PAIR), F32),
            slab,
            slab, slab, slab, slab, slab,
            unit_tiles, unit_tiles,
        ],
        compiler_params=pltpu.CompilerParams(
            dimension_semantics=("arbitrary",), vmem_limit_bytes=VMEM_LIMIT),
        name="rwkv7_mixer",
    )(*[seq_major(a) for a in (r, lw, k, v, kk, ka, hg)], g_up, r_k, gn_g, gn_b)
    return out.reshape(bsz * seq, W_RWKV)


def _post_kernel(alpha, ff_chunk, x_ref, ogl_ref, orw_ref, mod_ref, wout_ref, ln1g_ref, ln1b_ref,
                 wup_ref, wdn_ref, ln2g_ref, ln2b_ref, out_ref):
    m = mod_ref[0]
    gate1, shift2, scale2, gate2 = m[2:3, :], m[3:4, :], m[4:5, :], m[5:6, :]
    d_ff = wup_ref.shape[1]
    n_ff = d_ff // ff_chunk
    half = x_ref.shape[0] // 2
    groups = [slice(0, half), slice(half, 2 * half)]

    def out_proj(rows):
        return (_mm(ogl_ref[rows, :], wout_ref[0:W_GL, :])
                + _mm(orw_ref[rows, :], wout_ref[W_GL:W_GL + W_RWKV, :]))

    def norm1(rows, o):
        x1 = _layer_norm(alpha * x_ref[rows, :] + (1.0 + gate1) * o, ln1g_ref[...], ln1b_ref[...])
        return x1, (x1 * (1.0 + scale2) + shift2).astype(BF16)

    def mlp(h, acc, js):
        for j in js:
            cols = slice(j * ff_chunk, (j + 1) * ff_chunk)
            u = jnp.maximum(_mm(h, wup_ref[:, cols]), 0.0)
            acc = acc + _mm((u * u).astype(BF16), wdn_ref[cols, :])
        return acc

    def norm2(rows, x1, acc):
        out_ref[rows, :] = _layer_norm(alpha * x1 + (1.0 + gate2) * acc, ln2g_ref[...], ln2b_ref[...])

    o_a, o_b = out_proj(groups[0]), out_proj(groups[1])
    x1_a, h_a = norm1(groups[0], o_a)
    acc_a = mlp(h_a, jnp.zeros(x1_a.shape, F32), range(0, 2))
    x1_b, h_b = norm1(groups[1], o_b)
    acc_a = mlp(h_a, acc_a, range(2, n_ff))
    acc_b = mlp(h_b, jnp.zeros(x1_b.shape, F32), range(0, 2))
    norm2(groups[0], x1_a, acc_a)
    acc_b = mlp(h_b, acc_b, range(2, n_ff))
    norm2(groups[1], x1_b, acc_b)


def _post_mixer(x2, ogl, orw, mod_l, w_out, ln1_g, ln1_b, w_up, w_dn, ln2_g, ln2_b, alpha, seq, tm, layer):
    m_rows, d = x2.shape
    d_ff = w_up.shape[2]
    steps_per_batch = seq // tm
    row = lambda i: (i, 0)
    const = lambda i: (0, 0)
    vec = pl.BlockSpec((1, d), const)
    resident = lambda shape: pl.BlockSpec((None,) + shape, lambda i: (layer, 0, 0), pipeline_mode=pl.Buffered(1))
    return pl.pallas_call(
        functools.partial(_post_kernel, alpha, 512),
        grid=(m_rows // tm,),
        in_specs=[
            pl.BlockSpec((tm, d), row),
            pl.BlockSpec((tm, W_GL), row),
            pl.BlockSpec((tm, W_RWKV), row),
            pl.BlockSpec((1, 6, d), lambda i: (i // steps_per_batch, 0, 0)),
            resident((W_GL + W_RWKV, d)),
            vec, vec,
            resident((d, d_ff)),
            resident((d_ff, d)),
            vec, vec,
        ],
        out_specs=pl.BlockSpec((tm, d), row),
        out_shape=jax.ShapeDtypeStruct((m_rows, d), F32),
        compiler_params=pltpu.CompilerParams(
            dimension_semantics=("arbitrary",), vmem_limit_bytes=VMEM_LIMIT),
        name="outproj_mlp",
    )(x2, ogl, orw, mod_l, w_out, ln1_g, ln1_b, w_up, w_dn, ln2_g, ln2_b)


def kernel(x, c, hgrn_lb_logits, ada_w, ada_b, w_in, hgrn_norm_g, gla_alpha_up, gla_alpha_b, gla_norm_g,
           rwkv_mu, rwkv_w0, rwkv_w_up, rwkv_a0, rwkv_a_up, rwkv_g_up, rwkv_k_k, rwkv_k_a, rwkv_r_k,
           rwkv_gn_g, rwkv_gn_b, w_out, ln1_g, ln1_b, mlp_w_up, mlp_w_down, ln2_g, ln2_b):
    bsz, seq, d = x.shape
    depth = w_in.shape[0]
    alpha = (2.0 * depth) ** 0.25
    tm = min(512, seq)
    tb = min(512, seq)
    tb_gl = min(1024, seq)
    tm_post = min(1024, seq)
    assert all(seq % t == 0 for t in (tm, tb, tb_gl, tm_post)) and tb % CHUNK == 0
    assert tb_gl % (A_GROUP_GL * CHUNK) == 0 and tm_post % 16 == 0
    assert w_in.shape[2] == N_HGRN_COLS + N_GLA_COLS + N_RWKV_COLS

    mod = _modulation(c, ada_w, ada_b).reshape(depth, bsz, 6, d)
    lbs = _lower_bounds(hgrn_lb_logits)

    gla_end = N_HGRN_COLS + N_GLA_COLS
    w_in_b = w_in.astype(BF16)
    w_alpha = jnp.pad(w_in_b[:, :, gla_end - GLA_GATE_RANK:gla_end], ((0, 0), (0, 0), (0, LANES - GLA_GATE_RANK)))
    w_rwkv = w_in_b[:, :, gla_end:]
    alpha_up_p = jnp.concatenate(
        [gla_alpha_up, jnp.zeros((depth, LANES - GLA_GATE_RANK, W_GLA), gla_alpha_up.dtype)], axis=1)
    zeros_r = jnp.zeros((depth, RWKV_DECAY_RANK, W_RWKV), rwkv_w_up.dtype)
    wa_blk = jnp.concatenate(
        [jnp.concatenate([rwkv_w_up, zeros_r], axis=2), jnp.concatenate([zeros_r, rwkv_a_up], axis=2)], axis=1)
    norm_g = jnp.concatenate([hgrn_norm_g, gla_norm_g], axis=1)
    w_out_b = w_out.astype(BF16)
    w_up_b = mlp_w_up.astype(BF16)
    w_dn_b = mlp_w_down.astype(BF16)
    vec = lambda a, l: a[l].reshape(1, -1)

    x2 = x.reshape(bsz * seq, d)
    for l in range(depth):
        gq, gk, gv, glg, ggate, rr, rlw, rk, rv, rkk, rka, rhg = _in_projection(
            x2, mod[l], w_in_b, w_alpha, w_rwkv, vec(lbs, l), alpha_up_p[l], vec(gla_alpha_b, l), vec(rwkv_mu, l),
            vec(rwkv_w0, l), vec(rwkv_a0, l), wa_blk[l], vec(rwkv_k_k, l), vec(rwkv_k_a, l), seq, tm, l)
        ogl = _gl_mixer(gq, gk, gv, glg, ggate, vec(norm_g, l), bsz, seq, tb_gl)
        orw = _rwkv_mixer(rr, rlw, rk, rv, rkk, rka, rhg, rwkv_g_up[l], vec(rwkv_r_k, l), vec(rwkv_gn_g, l),
                          vec(rwkv_gn_b, l), bsz, seq, tb)
        x2 = _post_mixer(x2, ogl, orw, mod[l], w_out_b, vec(ln1_g, l), vec(ln1_b, l), w_up_b, w_dn_b,
                         vec(ln2_g, l), vec(ln2_b, l), alpha, seq, tm_post, l)
    return x2.reshape(bsz, seq, d)
```

```python
import functools

import numpy as np
import jax
import jax.numpy as jnp
from jax import lax
from jax.experimental import pallas as pl
from jax.experimental.pallas import tpu as pltpu

F32 = jnp.float32
BF16 = jnp.bfloat16

HEAD_DIM = 64
H_HGRN, H_GLA, H_RWKV = 4, 6, 6
W_HGRN, W_GLA, W_RWKV = H_HGRN * HEAD_DIM, H_GLA * HEAD_DIM, H_RWKV * HEAD_DIM
GLA_GATE_RANK = 16
GLA_GATE_NORMALIZER = 16.0
RWKV_DECAY_RANK, RWKV_ICLR_RANK, RWKV_GATE_RANK = 64, 64, 128
RWKV_GN_EPS = 64e-5
N_HGRN_COLS = 4 * W_HGRN
N_GLA_COLS = 4 * W_GLA + GLA_GATE_RANK
N_RWKV_COLS = 3 * W_RWKV + RWKV_DECAY_RANK + RWKV_ICLR_RANK + RWKV_GATE_RANK
CHUNK = 64
LN_EPS = 1e-5
RMS_EPS = 1e-5
F_MIN = 1e-30

LANES = 128
PAIR = 2 * HEAD_DIM
SUB = 8
LEVEL_HALVES = (SUB, 2 * SUB, 4 * SUB)
assert 2 * LEVEL_HALVES[-1] == CHUNK
A_GROUP_GL = 16
OUT_SKEW = 4
LOG2E = 1.4426950408889634
VMEM_LIMIT = 56 * 1024 * 1024

W_GL = W_HGRN + W_GLA
N_GL_PAIRS = W_GL // PAIR
N_RW_PAIRS = W_RWKV // PAIR

NN = (((1,), (0,)), ((), ()))
NT = (((1,), (1,)), ((), ()))
TN = (((0,), (0,)), ((), ()))


def _mm(a, b, dims=NN):
    return lax.dot_general(a, b, dims, preferred_element_type=F32)


def _split2(x):
    hi = x.astype(BF16)
    lo = (x - hi.astype(F32)).astype(BF16)
    return hi, lo


def _bf16_parts(x, n):
    parts = []
    for i in range(n):
        p = x.astype(BF16)
        parts.append(p)
        if i + 1 < n:
            x = x - p.astype(F32)
    return parts


def _dot1(a, b, dims=NN):
    return _mm(a.astype(BF16), b.astype(BF16), dims)


def _dot3(a, b, dims=NN):
    ah, al = _split2(a)
    bh, bl = _split2(b)
    return _mm(ah, bh, dims) + (_mm(ah, bl, dims) + _mm(al, bh, dims))


def _dot_exact_lhs(a_bf16, b, n):
    return _mm(jnp.concatenate([a_bf16] * n, axis=1), jnp.concatenate(_bf16_parts(b, n), axis=0))


def _sigmoid(x):
    return 1.0 / (1.0 + jnp.exp(-x))


def _silu(x):
    return x * _sigmoid(x)


def _softplus(x):
    return jnp.maximum(x, 0.0) + jnp.log1p(jnp.exp(-jnp.abs(x)))


def _log_sigmoid(x):
    return -_softplus(-x)


def _layer_norm(y, g, b):
    mu = jnp.mean(y, axis=-1, keepdims=True)
    d = y - mu
    var = jnp.mean(d * d, axis=-1, keepdims=True)
    return d * lax.rsqrt(var + LN_EPS) * g + b


def _iota(shape, axis):
    return lax.broadcasted_iota(jnp.int32, shape, axis)


def _stack_heads(x):
    lane = _iota(x.shape, 1)
    return jnp.concatenate([jnp.where(lane < HEAD_DIM, x, 0.0), jnp.where(lane >= HEAD_DIM, x, 0.0)], axis=0)


def _pair_masks():
    t = _iota((CHUNK, PAIR), 0)
    s = _iota((CHUNK, PAIR), 1) % HEAD_DIM
    r2 = _iota((PAIR, PAIR), 0) // HEAD_DIM
    c2 = _iota((PAIR, PAIR), 1) // HEAD_DIM
    return dict(
        strict=s < t,
        incl=s <= t,
        eye=s == t,
        diag=(s // SUB == t // SUB) & (s <= t),
        level={hs: (s // (2 * hs) == t // (2 * hs)) & ((t // hs) % 2 == 1) & ((s // hs) % 2 == 0)
               for hs in LEVEL_HALVES},
        bd=r2 == c2,
    )


def _trace_skewed(stages, units, skew):
    for step in range(len(units) + skew * (len(stages) - 1)):
        for s, stage in enumerate(stages):
            u = step - skew * s
            if 0 <= u < len(units):
                stage(units[u])


def _tril_bf16():
    return jnp.where(_iota((CHUNK, CHUNK), 1) <= _iota((CHUNK, CHUNK), 0), 1.0, 0.0).astype(BF16)


def _seg_mean_matrix(width):
    r = _iota((width, width), 0) // HEAD_DIM
    c = _iota((width, width), 1) // HEAD_DIM
    return jnp.where(r == c, 1.0, 0.0).astype(BF16)


def _head_sums(xs):
    tiles = [x[:, j:j + LANES].astype(BF16) for x in xs for j in range(0, x.shape[1], LANES)]
    sums = []
    for i in range(0, len(tiles) - 1, 2):
        s = _mm(jnp.concatenate(tiles[i:i + 2], axis=1), _seg_mean_matrix(2 * LANES))
        sums += [s[:, 0:LANES], s[:, LANES:2 * LANES]]
    if len(tiles) % 2:
        sums.append(_mm(tiles[-1], _seg_mean_matrix(LANES)))
    out, i = [], 0
    for x in xs:
        n = x.shape[1] // LANES
        out.append(jnp.concatenate(sums[i:i + n], axis=1))
        i += n
    return out


def _mod_kernel(c_ref, w_ref, b_ref, o_ref):
    c = _silu(c_ref[...])
    o_ref[0] = _dot3(c, w_ref[0]) + b_ref[0]


def _modulation(c, ada_w, ada_b):
    depth, d, n = ada_w.shape
    bsz = c.shape[0]
    tn = 3072
    return pl.pallas_call(
        _mod_kernel,
        grid=(depth, n // tn),
        in_specs=[
            pl.BlockSpec((bsz, d), lambda l, j: (0, 0)),
            pl.BlockSpec((1, d, tn), lambda l, j: (l, 0, j)),
            pl.BlockSpec((1, 1, tn), lambda l, j: (l, 0, j)),
        ],
        out_specs=pl.BlockSpec((1, bsz, tn), lambda l, j: (l, 0, j)),
        out_shape=jax.ShapeDtypeStruct((depth, bsz, n), F32),
        compiler_params=pltpu.CompilerParams(
            dimension_semantics=("arbitrary", "arbitrary"), vmem_limit_bytes=VMEM_LIMIT),
        name="adaln_modulation",
    )(c, ada_w, ada_b.reshape(depth, 1, n))


def _lower_bound_kernel(x_ref, o_ref):
    depth = x_ref.shape[0]
    rows = [x_ref[l:l + 1, :] for l in range(depth)]
    m = functools.reduce(jnp.maximum, rows)
    e = [jnp.exp(r - m) for r in rows]
    tot = functools.reduce(lambda a, b: a + b, e)
    p = [ei / tot for ei in e]
    acc = jnp.zeros_like(p[0])
    for l in range(depth):
        acc = acc + p[l]
        o_ref[l:l + 1, :] = acc - p[0]


def _lower_bounds(logits):
    return pl.pallas_call(
        _lower_bound_kernel,
        out_shape=jax.ShapeDtypeStruct(logits.shape, F32),
        name="hgrn_lower_bounds",
    )(logits.astype(F32))


def _inproj_kernel(steps_per_batch, x_ref, mod_ref, w_ref, wal_ref, wr_ref, lb_ref, au_ref, ab_ref, mu_ref,
                   w0_ref, a0_ref, wa_ref, kk_ref, ka_ref,
                   gq_ref, gk_ref, gv_ref, gb_ref, gg_ref,
                   rr_ref, rlw_ref, rk_ref, rv_ref, rkk_ref, rka_ref, rhg_ref, carry_s):
    tm = x_ref.shape[0]
    w3 = W_RWKV

    @pl.when(pl.program_id(0) % steps_per_batch == 0)
    def _():
        carry_s[...] = jnp.zeros_like(carry_s)

    m = mod_ref[0]
    h = (x_ref[...] * (1.0 + m[1:2, :]) + m[0:1, :]).astype(BF16)

    def proj(c0, width):
        return _mm(h, w_ref[:, c0:c0 + width])

    first_row = _iota((tm, LANES), 0) == 0

    def shifted(c0, width):
        z = _mm(h, wr_ref[:, c0:c0 + width])
        z_prev = pltpu.roll(z, 1, axis=0)
        z_prev = jnp.where(jnp.concatenate([first_row] * (width // LANES), axis=1), carry_s[:, c0:c0 + width], z_prev)
        carry_s[:, c0:c0 + width] = z[tm - 1:tm, :]
        return z + (z_prev - z) * mu_ref[:, c0:c0 + width]

    g0 = N_HGRN_COLS
    n_rp = w3 // PAIR
    tile = lambda j: slice(j * PAIR, (j + 1) * PAIR)
    z_v = shifted(2 * w3, w3 + LANES)
    z_rk = shifted(0, 2 * w3)

    h_wa = z_v[:, w3:w3 + LANES]
    h_wa = jnp.where(_iota(h_wa.shape, 1) < RWKV_DECAY_RANK, jnp.tanh(h_wa), h_wa)
    wa = _dot1(h_wa, wa_ref[...])
    rv_ref[...] = z_v[:, 0:w3]
    rr_ref[...] = z_rk[:, 0:w3]

    later = [lambda: proj(W_HGRN, W_HGRN), lambda: proj(0, W_HGRN),
             lambda: shifted(3 * w3 + LANES, RWKV_GATE_RANK)]
    issued, a_tiles = [], []
    for j in range(n_rp):
        issued.append(later[j]())
        w_log = -_softplus(-(w0_ref[:, tile(j)] + wa[:, tile(j)])) - 0.5
        rlw_ref[:, tile(j)] = -jnp.exp(w_log)
        a_tiles.append(_sigmoid(a0_ref[:, tile(j)] + wa[:, w3 + j * PAIR:w3 + (j + 1) * PAIR]))
    zf, zq, z_hg = issued

    z_al = _mm(h, wal_ref[...])
    k = z_rk[:, w3:2 * w3]
    kk = k * kk_ref[...]
    kk = kk / jnp.maximum(jnp.sqrt(_head_sums([kk * kk])[0]), 1e-12)
    rkk_ref[...] = kk

    z_qk = proj(g0, 2 * W_GLA)
    for j in range(n_rp):
        rka_ref[:, tile(j)] = kk[:, tile(j)] * a_tiles[j]
        rk_ref[:, tile(j)] = k[:, tile(j)] * (1.0 + (a_tiles[j] - 1.0) * ka_ref[:, tile(j)])
    lb = lb_ref[...]
    f = lb + (1.0 - lb) * _sigmoid(zf)
    gb_ref[:, 0:W_HGRN] = jnp.log(jnp.maximum(f, F_MIN))
    k_h = (1.0 - lb) * _sigmoid(-zf)
    q_h = _silu(zq) * HEAD_DIM ** -0.5
    for p in range(W_HGRN // PAIR):
        gk_ref[p] = k_h[:, tile(p)]
        gq_ref[p] = q_h[:, tile(p)]

    z_vr = proj(g0 + 2 * W_GLA, 2 * W_GLA)
    rhg_ref[...] = _sigmoid(z_hg).astype(BF16)
    logit = _dot1(z_al, au_ref[...]) + ab_ref[...]
    gb_ref[:, W_HGRN:W_GL] = _log_sigmoid(logit) / GLA_GATE_NORMALIZER
    for p in range(W_GLA // PAIR):
        gq_ref[W_HGRN // PAIR + p] = z_qk[:, tile(p)] * HEAD_DIM ** -0.5
        gk_ref[W_HGRN // PAIR + p] = z_qk[:, W_GLA + p * PAIR:W_GLA + (p + 1) * PAIR]

    zg = proj(3 * W_HGRN, W_HGRN)
    gv_ref[:, W_HGRN:W_GL] = z_vr[:, 0:W_GLA].astype(BF16)
    gg_ref[:, W_HGRN:W_GL] = _silu(z_vr[:, W_GLA:2 * W_GLA]).astype(BF16)
    zi = proj(2 * W_HGRN, W_HGRN)
    gg_ref[:, 0:W_HGRN] = _silu(zg).astype(BF16)
    gv_ref[:, 0:W_HGRN] = zi.astype(BF16)


def _in_projection(x2, mod_l, w_main, w_alpha, w_rwkv, lb_l, alpha_up_p, alpha_b, mu, w0, a0, wa_blk, k_k, k_a,
                   seq, tm, layer):
    m_rows, d = x2.shape
    resident = lambda n: pl.BlockSpec((None, d, n), lambda i: (layer, 0, 0), pipeline_mode=pl.Buffered(1))
    steps_per_batch = seq // tm
    row = lambda i: (i, 0)
    const = lambda i: (0, 0)
    vec = lambda n: pl.BlockSpec((1, n), const)
    out = lambda n, dt: (pl.BlockSpec((tm, n), row), jax.ShapeDtypeStruct((m_rows, n), dt))
    pair_slabs = (pl.BlockSpec((N_GL_PAIRS, tm, PAIR), lambda i: (0, i, 0)),
                  jax.ShapeDtypeStruct((N_GL_PAIRS, m_rows, PAIR), F32))
    outs = [pair_slabs, pair_slabs, out(W_GL, BF16), out(W_GL, F32), out(W_GL, BF16),
            out(W_RWKV, F32), out(W_RWKV, F32), out(W_RWKV, F32), out(W_RWKV, F32), out(W_RWKV, F32),
            out(W_RWKV, F32), out(RWKV_GATE_RANK, BF16)]
    return pl.pallas_call(
        functools.partial(_inproj_kernel, steps_per_batch),
        grid=(m_rows // tm,),
        in_specs=[
            pl.BlockSpec((tm, d), row),
            pl.BlockSpec((1, 6, d), lambda i: (i // steps_per_batch, 0, 0)),
            resident(N_HGRN_COLS + 4 * W_GLA), resident(LANES), resident(N_RWKV_COLS),
            vec(W_HGRN),
            pl.BlockSpec((LANES, W_GLA), const),
            vec(W_GLA),
            vec(N_RWKV_COLS), vec(W_RWKV), vec(W_RWKV),
            pl.BlockSpec((LANES, 2 * W_RWKV), const),
            vec(W_RWKV), vec(W_RWKV),
        ],
        out_specs=[o[0] for o in outs],
        out_shape=[o[1] for o in outs],
        scratch_shapes=[pltpu.VMEM((1, N_RWKV_COLS), F32)],
        compiler_params=pltpu.CompilerParams(
            dimension_semantics=("arbitrary",), vmem_limit_bytes=VMEM_LIMIT),
        name="in_projection",
    )(x2, mod_l, w_main, w_alpha, w_rwkv, lb_l, alpha_up_p, alpha_b, mu, w0, a0, wa_blk, k_k, k_a)


def _gl_kernel(q_s, k_s, v_s, lg_ref, gate_ref, ng_ref, ind_ref, out_ref,
               st_ref, b_s, ad_s, e_s, o_s, sc_s, eend_s):
    tb = q_s.shape[1]
    n_chunks = tb // CHUNK
    n_groups = tb // SUB

    @pl.when(pl.program_id(1) == 0)
    def _():
        st_ref[...] = jnp.zeros_like(st_ref)

    tril = _tril_bf16()
    for c in range(n_chunks):
        rows = slice(c * CHUNK, (c + 1) * CHUNK)
        b = _dot_exact_lhs(tril, lg_ref[rows, :], 2)
        for p in range(N_GL_PAIRS):
            b_s[p, rows, :] = b[:, p * PAIR:(p + 1) * PAIR]

    for t in range(SUB - 1):
        e_s[t * n_groups:(t + 1) * n_groups, (t + 1) * PAIR:] = jnp.zeros((n_groups, (SUB - 1 - t) * PAIR), BF16)
    for p in range(N_GL_PAIRS):
        by_pos = lambda ref, t: ref[p, pl.ds(t, n_groups, stride=SUB), :]
        qs = [by_pos(q_s, t) for t in range(SUB)]
        ks = [by_pos(k_s, t) for t in range(SUB)]
        bs = [by_pos(b_s, t) * LOG2E for t in range(SUB)]
        for t in range(SUB):
            for sg in range(t + 1):
                e = qs[t] * ks[sg]
                if sg < t:
                    e = e * jnp.exp2(jnp.minimum(bs[t] - bs[sg], 0.0))
                e_s[t * n_groups:(t + 1) * n_groups, sg * PAIR:(sg + 1) * PAIR] = e.astype(BF16)
        ad = [jnp.zeros((n_groups, PAIR), F32)] * SUB
        for sg0 in range(0, SUB, 2):
            part = _mm(e_s[sg0 * n_groups:, sg0 * PAIR:(sg0 + 2) * PAIR], ind_ref[sg0 * PAIR:(sg0 + 2) * PAIR, :])
            for t in range(sg0, SUB):
                ad[t] = ad[t] + part[(t - sg0) * n_groups:(t - sg0 + 1) * n_groups, :]
        for t in range(SUB):
            ad_s[p, pl.ds(t, n_groups, stride=SUB), :] = ad[t]

    masks = _pair_masks()
    t_idx = _iota((CHUNK, PAIR), 0)
    second_half = {hs: (t_idx // hs) % 2 == 1 for hs in LEVEL_HALVES}
    level_sign = {hs: jnp.where(second_half[hs], LOG2E, -LOG2E) for hs in LEVEL_HALVES}

    def group_units(i):
        units = []
        for cc in range(A_GROUP_GL):
            c = i * A_GROUP_GL + cc
            rows = pl.ds(pl.multiple_of(c * CHUNK, CHUNK), CHUNK)
            for p in range(N_GL_PAIRS):
                units.append(dict(p=p, rows=rows, lanes=slice(p * PAIR, (p + 1) * PAIR),
                                  e_rows=pl.ds(pl.multiple_of(c * 8, 8), 8),
                                  st_rows=pl.ds(pl.multiple_of((c * N_GL_PAIRS + p) * PAIR, PAIR), PAIR)))
        return units

    def increments(i, carry):
        units = group_units(i)

        def decay_keys(un):
            b = b_s[un["p"], un["rows"], :]
            b_end = b[CHUNK - 1:CHUNK, :]
            un["k_end"] = k_s[un["p"], un["rows"], :] * jnp.exp(b_end - b)
            eend_s[un["e_rows"], un["lanes"]] = jnp.broadcast_to(jnp.exp(b_end), (8, PAIR))

        def outer(un):
            upd = _dot1(v_s[un["rows"], un["lanes"]], un["k_end"], TN)
            sc_s[un["st_rows"], :] = jnp.where(masks["bd"], upd, 0.0)

        _trace_skewed((decay_keys, outer), units, OUT_SKEW)
        return carry

    lax.fori_loop(0, n_chunks // A_GROUP_GL, increments, 0)

    sts = [st_ref[p] for p in range(N_GL_PAIRS)]
    for c in range(n_chunks):
        for p in range(N_GL_PAIRS):
            st_rows = slice((c * N_GL_PAIRS + p) * PAIR, (c * N_GL_PAIRS + p + 1) * PAIR)
            upd = sc_s[st_rows, :]
            sc_s[st_rows, :] = sts[p]
            sts[p] = sts[p] * eend_s[c * 8:c * 8 + 1, p * PAIR:(p + 1) * PAIR] + upd
    for p in range(N_GL_PAIRS):
        st_ref[p] = sts[p]

    def outputs(i, carry):
        units = group_units(i)

        def decays(un):
            q = q_s[un["p"], un["rows"], :]
            k = k_s[un["p"], un["rows"], :]
            b = b_s[un["p"], un["rows"], :]
            un["lv"] = []
            for hs in LEVEL_HALVES:
                e_mid = jnp.concatenate(
                    [jnp.broadcast_to(b[m:m + 1, :], (2 * hs, PAIR)) for m in range(hs - 1, CHUNK, 2 * hs)], axis=0)
                dec = jnp.exp2(jnp.minimum((b - e_mid) * level_sign[hs], 0.0))
                un["lv"].append(jnp.where(second_half[hs], q, k) * dec)
            un["q_in"] = q * jnp.exp(b)

        def scores(un):
            a = jnp.zeros((CHUNK, PAIR), F32)
            for hs, lv in zip(LEVEL_HALVES, un["lv"]):
                a = jnp.where(masks["level"][hs], _dot1(lv, jnp.transpose(_stack_heads(lv))), a)
            un["a"] = jnp.where(masks["diag"], ad_s[un["p"], un["rows"], :], a)

        def combine(un):
            o_s[un["rows"], un["lanes"]] = (_dot1(un["a"], _stack_heads(v_s[un["rows"], un["lanes"]]))
                                            + _dot1(un["q_in"], sc_s[un["st_rows"], :], NT))

        _trace_skewed((decays, scores, combine), units, OUT_SKEW)
        return carry

    lax.fori_loop(0, n_chunks // A_GROUP_GL, outputs, 0)

    o = o_s[...]
    ms = _head_sums([o * o])[0] * (1.0 / HEAD_DIM)
    out_ref[...] = (o * lax.rsqrt(ms + RMS_EPS) * ng_ref[...] * gate_ref[...]).astype(out_ref.dtype)


def _gl_indicator():
    r = np.arange(SUB * PAIR)
    c = np.arange(PAIR)
    sg = r // PAIR
    h = (r % PAIR) // HEAD_DIM
    ind = (h[:, None] == (c // HEAD_DIM)[None, :]) & (sg[:, None] == (c % SUB)[None, :])
    return jnp.asarray(ind, dtype=BF16)


def _gl_mixer(q, k, v, lg, gate, norm_g, bsz, seq, tb):
    steps = seq // tb
    row = lambda b, i: (b * steps + i, 0)
    const = lambda b, i: (0, 0)
    slab = pl.BlockSpec((tb, W_GL), row)
    pair_slabs = pl.BlockSpec((N_GL_PAIRS, tb, PAIR), lambda b, i: (0, b * steps + i, 0))
    scratch_f32 = pltpu.VMEM((tb, W_GL), F32)
    scratch_pairs = pltpu.VMEM((N_GL_PAIRS, tb, PAIR), F32)
    return pl.pallas_call(
        _gl_kernel,
        grid=(bsz, steps),
        in_specs=[
            pair_slabs, pair_slabs, slab, slab, slab,
            pl.BlockSpec((1, W_GL), const),
            pl.BlockSpec((SUB * PAIR, PAIR), const),
        ],
        out_specs=pl.BlockSpec((tb, W_GL), row),
        out_shape=jax.ShapeDtypeStruct((bsz * seq, W_GL), BF16),
        scratch_shapes=[
            pltpu.VMEM((N_GL_PAIRS, PAIR, PAIR), F32),
            scratch_pairs, scratch_pairs,
            pltpu.VMEM((tb, SUB * PAIR), BF16),
            scratch_f32,
            pltpu.VMEM((tb // CHUNK * N_GL_PAIRS * PAIR, PAIR), F32),
            pltpu.VMEM((tb // CHUNK * 8, W_GL), F32),
        ],
        compiler_params=pltpu.CompilerParams(
            dimension_semantics=("arbitrary", "arbitrary"), vmem_limit_bytes=VMEM_LIMIT),
        name="hgrn_gla_mixer",
    )(q, k, v, lg, gate, norm_g, _gl_indicator())


def _rwkv_kernel(r_s, lw_s, k_s, v_s, kk_s, ka_s, hg_s, gup_ref, rk_ref, gng_ref, gnb_ref,
                 out_ref, st_ref, y_s, rt_s, nb_s, nk_s, w_s, u0_s, bkt_s, ecol_s):
    bsz, tb = r_s.shape[0], r_s.shape[1]
    n_chunks = tb // CHUNK

    @pl.when(pl.program_id(0) == 0)
    def _():
        st_ref[...] = jnp.zeros_like(st_ref)

    masks = _pair_masks()
    tril = _tril_bf16()
    eye = jnp.where(masks["eye"], 1.0, 0.0)

    def phase_a(b, carry):
        units = []
        for c in range(n_chunks):
            rows = slice(c * CHUNK, (c + 1) * CHUNK)
            g_all = _dot_exact_lhs(tril, lw_s[b, rows, :], 2)
            for p in range(N_RW_PAIRS):
                lanes = slice(p * PAIR, (p + 1) * PAIR)
                units.append(dict(c=c, p=p, rows=rows, lanes=lanes, g=g_all[:, lanes]))
        def scale(un):
            rows, lanes, g = un["rows"], un["lanes"], un["g"]
            kc = k_s[b, rows, lanes]
            kkc = kk_s[b, rows, lanes]
            ka = ka_s[b, rows, lanes]
            g_end = g[CHUNK - 1:CHUNK, :]
            e_neg = jnp.exp(-g)
            e_end = jnp.exp(g_end - g)
            un["at"] = -kkc * jnp.exp(g - lw_s[b, rows, lanes])
            rt = r_s[b, rows, lanes] * jnp.exp(g)
            un["rt"] = rt
            un["bt"] = ka * e_neg
            un["kt"] = kc * e_neg
            rt_s[b, rows, lanes] = rt
            slot = un["c"] * N_RW_PAIRS + un["p"]
            bkt_s[b, slot] = jnp.transpose(jnp.concatenate([ka * e_end, kc * e_end], axis=0))
            ecol_s[b, slot] = jnp.transpose(jnp.broadcast_to(jnp.exp(g_end), (PAIR, PAIR)))

        def gram(un):
            g2 = _dot1(jnp.concatenate([un["at"], un["rt"]], axis=0),
                       jnp.concatenate([_stack_heads(un["bt"]), _stack_heads(un["kt"])], axis=0), NT)
            un["l"] = jnp.where(masks["strict"], g2[0:CHUNK, 0:PAIR], 0.0)
            un["mk"] = jnp.where(masks["strict"], g2[0:CHUNK, PAIR:2 * PAIR], 0.0)
            nb_s[b, un["rows"], un["lanes"]] = jnp.where(masks["incl"], g2[CHUNK:2 * CHUNK, 0:PAIR], 0.0)
            nk_s[b, un["rows"], un["lanes"]] = jnp.where(masks["incl"], g2[CHUNK:2 * CHUNK, PAIR:2 * PAIR], 0.0)

        def inv_first(un):
            un["t"] = eye + un["l"]
            un["lp"] = _dot1(un["l"], _stack_heads(un["l"]))
            un["mv"] = _dot1(un["mk"], _stack_heads(v_s[b, un["rows"], un["lanes"]]))

        def inv_step(un):
            both = _dot1(jnp.concatenate([un["lp"], un["t"]], axis=0), _stack_heads(un["lp"]))
            un["lp"] = both[0:CHUNK, :]
            un["t"] = un["t"] + both[CHUNK:2 * CHUNK, :]

        def inv_last(un):
            un["t"] = un["t"] + _dot1(un["t"], _stack_heads(un["lp"]))

        def apply_inv(un):
            wu = _dot1(un["t"], jnp.concatenate([_stack_heads(un["at"]), _stack_heads(un["mv"])], axis=1))
            w_s[b, un["rows"], un["lanes"]] = wu[:, 0:PAIR]
            u0_s[b, un["rows"], un["lanes"]] = wu[:, PAIR:2 * PAIR]

        for stage in (scale, gram, inv_first, inv_step, inv_step, inv_step, inv_step, inv_last, apply_inv):
            for un in units:
                stage(un)
        return carry

    lax.fori_loop(0, bsz, phase_a, 0)

    def phase_b(c, carry):
        rows = pl.ds(pl.multiple_of(c * CHUNK, CHUNK), CHUNK)
        seqs = [(b, p, slice(p * PAIR, (p + 1) * PAIR)) for b in range(bsz) for p in range(N_RW_PAIRS)]
        sts = [st_ref[b * N_RW_PAIRS + p] for b, p, _ in seqs]
        vs = [v_s[b, rows, lanes] for b, _, lanes in seqs]
        wrs = [_dot1(jnp.concatenate([w_s[b, rows, lanes], rt_s[b, rows, lanes]], axis=0), st)
               for (b, _, lanes), st in zip(seqs, sts)]
        us = [wr[0:CHUNK, :] + u0_s[b, rows, lanes] for (b, _, lanes), wr in zip(seqs, wrs)]
        upds = [_dot1(bkt_s[b, c * N_RW_PAIRS + p], jnp.concatenate([u, vc], axis=0))
                for (b, p, _), u, vc in zip(seqs, us, vs)]
        for (b, p, _), st, upd in zip(seqs, sts, upds):
            st_ref[b * N_RW_PAIRS + p] = st * ecol_s[b, c * N_RW_PAIRS + p] + jnp.where(masks["bd"], upd, 0.0)
        for (b, _, lanes), wr, u, vc in zip(seqs, wrs, us, vs):
            y_s[b, rows, lanes] = (
                wr[CHUNK:2 * CHUNK, :]
                + _dot1(jnp.concatenate([nb_s[b, rows, lanes], nk_s[b, rows, lanes]], axis=1),
                        jnp.concatenate([_stack_heads(u), _stack_heads(vc)], axis=0)))
        return carry

    lax.fori_loop(0, n_chunks, phase_b, 0)

    for b in range(bsz):
        y = y_s[b]
        s_hi, s_lo = _head_sums(_bf16_parts(y, 2))
        d = y - (s_hi + s_lo) * (1.0 / HEAD_DIM)
        s_var, s_bonus = _head_sums([d * d, r_s[b] * k_s[b] * rk_ref[...]])
        yn = d * lax.rsqrt(s_var * (1.0 / HEAD_DIM) + RWKV_GN_EPS) * gng_ref[...] + gnb_ref[...]
        gate = _dot1(hg_s[b], gup_ref[...])
        out_ref[b] = ((yn + s_bonus * v_s[b]) * gate).astype(out_ref.dtype)


def _rwkv_mixer(r, lw, k, v, kk, ka, hg, g_up, r_k, gn_g, gn_b, bsz, seq, tb):
    blk = lambda i: (0, i, 0)
    const = lambda i: (0, 0)
    vec = pl.BlockSpec((1, W_RWKV), const)
    tok = pl.BlockSpec((bsz, tb, W_RWKV), blk)
    slab = pltpu.VMEM((bsz, tb, W_RWKV), F32)
    unit_tiles = pltpu.VMEM((bsz, tb // CHUNK * N_RW_PAIRS, PAIR, PAIR), F32)
    seq_major = lambda a: a.reshape(bsz, seq, a.shape[-1])
    out = pl.pallas_call(
        _rwkv_kernel,
        grid=(seq // tb,),
        in_specs=[
            tok, tok, tok, tok, tok, tok,
            pl.BlockSpec((bsz, tb, RWKV_GATE_RANK), blk),
            pl.BlockSpec((RWKV_GATE_RANK, W_RWKV), const),
            vec, vec, vec,
        ],
        out_specs=pl.BlockSpec((bsz, tb, W_RWKV), blk),
        out_shape=jax.ShapeDtypeStruct((bsz, seq, W_RWKV), BF16),
        scratch_shapes=[
            pltpu.VMEM((bsz * N_RW_PAIRS, PAIR, PAIR), F32),
            slab,
            slab, slab, slab, slab, slab,
            unit_tiles, unit_tiles,
        ],
        compiler_params=pltpu.CompilerParams(
            dimension_semantics=("arbitrary",), vmem_limit_bytes=VMEM_LIMIT),
        name="rwkv7_mixer",
    )(*[seq_major(a) for a in (r, lw, k, v, kk, ka, hg)], g_up, r_k, gn_g, gn_b)
    return out.reshape(bsz * seq, W_RWKV)


def _post_kernel(alpha, ff_chunk, x_ref, ogl_ref, orw_ref, mod_ref, wout_ref, ln1g_ref, ln1b_ref,
                 wup_ref, wdn_ref, ln2g_ref, ln2b_ref, out_ref):
    m = mod_ref[0]
    gate1, shift2, scale2, gate2 = m[2:3, :], m[3:4, :], m[4:5, :], m[5:6, :]
    d_ff = wup_ref.shape[1]
    n_ff = d_ff // ff_chunk
    half = x_ref.shape[0] // 2
    groups = [slice(0, half), slice(half, 2 * half)]

    def out_proj(rows):
        return (_mm(ogl_ref[rows, :], wout_ref[0:W_GL, :])
                + _mm(orw_ref[rows, :], wout_ref[W_GL:W_GL + W_RWKV, :]))

    def norm1(rows, o):
        x1 = _layer_norm(alpha * x_ref[rows, :] + (1.0 + gate1) * o, ln1g_ref[...], ln1b_ref[...])
        return x1, (x1 * (1.0 + scale2) + shift2).astype(BF16)

    def mlp(h, acc, js):
        for j in js:
            cols = slice(j * ff_chunk, (j + 1) * ff_chunk)
            u = jnp.maximum(_mm(h, wup_ref[:, cols]), 0.0)
            acc = acc + _mm((u * u).astype(BF16), wdn_ref[cols, :])
        return acc

    def norm2(rows, x1, acc):
        out_ref[rows, :] = _layer_norm(alpha * x1 + (1.0 + gate2) * acc, ln2g_ref[...], ln2b_ref[...])

    o_a, o_b = out_proj(groups[0]), out_proj(groups[1])
    x1_a, h_a = norm1(groups[0], o_a)
    acc_a = mlp(h_a, jnp.zeros(x1_a.shape, F32), range(0, 2))
    x1_b, h_b = norm1(groups[1], o_b)
    acc_a = mlp(h_a, acc_a, range(2, n_ff))
    acc_b = mlp(h_b, jnp.zeros(x1_b.shape, F32), range(0, 2))
    norm2(groups[0], x1_a, acc_a)
    acc_b = mlp(h_b, acc_b, range(2, n_ff))
    norm2(groups[1], x1_b, acc_b)


def _post_mixer(x2, ogl, orw, mod_l, w_out, ln1_g, ln1_b, w_up, w_dn, ln2_g, ln2_b, alpha, seq, tm, layer):
    m_rows, d = x2.shape
    d_ff = w_up.shape[2]
    steps_per_batch = seq // tm
    row = lambda i: (i, 0)
    const = lambda i: (0, 0)
    vec = pl.BlockSpec((1, d), const)
    resident = lambda shape: pl.BlockSpec((None,) + shape, lambda i: (layer, 0, 0), pipeline_mode=pl.Buffered(1))
    return pl.pallas_call(
        functools.partial(_post_kernel, alpha, 512),
        grid=(m_rows // tm,),
        in_specs=[
            pl.BlockSpec((tm, d), row),
            pl.BlockSpec((tm, W_GL), row),
            pl.BlockSpec((tm, W_RWKV), row),
            pl.BlockSpec((1, 6, d), lambda i: (i // steps_per_batch, 0, 0)),
            resident((W_GL + W_RWKV, d)),
            vec, vec,
            resident((d, d_ff)),
            resident((d_ff, d)),
            vec, vec,
        ],
        out_specs=pl.BlockSpec((tm, d), row),
        out_shape=jax.ShapeDtypeStruct((m_rows, d), F32),
        compiler_params=pltpu.CompilerParams(
            dimension_semantics=("arbitrary",), vmem_limit_bytes=VMEM_LIMIT),
        name="outproj_mlp",
    )(x2, ogl, orw, mod_l, w_out, ln1_g, ln1_b, w_up, w_dn, ln2_g, ln2_b)


def kernel(x, c, hgrn_lb_logits, ada_w, ada_b, w_in, hgrn_norm_g, gla_alpha_up, gla_alpha_b, gla_norm_g,
           rwkv_mu, rwkv_w0, rwkv_w_up, rwkv_a0, rwkv_a_up, rwkv_g_up, rwkv_k_k, rwkv_k_a, rwkv_r_k,
           rwkv_gn_g, rwkv_gn_b, w_out, ln1_g, ln1_b, mlp_w_up, mlp_w_down, ln2_g, ln2_b):
    bsz, seq, d = x.shape
    depth = w_in.shape[0]
    alpha = (2.0 * depth) ** 0.25
    tm = min(512, seq)
    tb = min(512, seq)
    tb_gl = min(1024, seq)
    tm_post = min(1024, seq)
    assert all(seq % t == 0 for t in (tm, tb, tb_gl, tm_post)) and tb % CHUNK == 0
    assert tb_gl % (A_GROUP_GL * CHUNK) == 0 and tm_post % 16 == 0
    assert w_in.shape[2] == N_HGRN_COLS + N_GLA_COLS + N_RWKV_COLS

    mod = _modulation(c, ada_w, ada_b).reshape(depth, bsz, 6, d)
    lbs = _lower_bounds(hgrn_lb_logits)

    gla_end = N_HGRN_COLS + N_GLA_COLS
    w_in_b = w_in.astype(BF16)
    w_alpha = jnp.pad(w_in_b[:, :, gla_end - GLA_GATE_RANK:gla_end], ((0, 0), (0, 0), (0, LANES - GLA_GATE_RANK)))
    w_rwkv = w_in_b[:, :, gla_end:]
    alpha_up_p = jnp.concatenate(
        [gla_alpha_up, jnp.zeros((depth, LANES - GLA_GATE_RANK, W_GLA), gla_alpha_up.dtype)], axis=1)
    zeros_r = jnp.zeros((depth, RWKV_DECAY_RANK, W_RWKV), rwkv_w_up.dtype)
    wa_blk = jnp.concatenate(
        [jnp.concatenate([rwkv_w_up, zeros_r], axis=2), jnp.concatenate([zeros_r, rwkv_a_up], axis=2)], axis=1)
    norm_g = jnp.concatenate([hgrn_norm_g, gla_norm_g], axis=1)
    w_out_b = w_out.astype(BF16)
    w_up_b = mlp_w_up.astype(BF16)
    w_dn_b = mlp_w_down.astype(BF16)
    vec = lambda a, l: a[l].reshape(1, -1)

    x2 = x.reshape(bsz * seq, d)
    for l in range(depth):
        gq, gk, gv, glg, ggate, rr, rlw, rk, rv, rkk, rka, rhg = _in_projection(
            x2, mod[l], w_in_b, w_alpha, w_rwkv, vec(lbs, l), alpha_up_p[l], vec(gla_alpha_b, l), vec(rwkv_mu, l),
            vec(rwkv_w0, l), vec(rwkv_a0, l), wa_blk[l], vec(rwkv_k_k, l), vec(rwkv_k_a, l), seq, tm, l)
        ogl = _gl_mixer(gq, gk, gv, glg, ggate, vec(norm_g, l), bsz, seq, tb_gl)
        orw = _rwkv_mixer(rr, rlw, rk, rv, rkk, rka, rhg, rwkv_g_up[l], vec(rwkv_r_k, l), vec(rwkv_gn_g, l),
                          vec(rwkv_gn_b, l), bsz, seq, tb)
        x2 = _post_mixer(x2, ogl, orw, mod[l], w_out_b, vec(ln1_g, l), vec(ln1_b, l), w_up_b, w_dn_b,
                         vec(ln2_g, l), vec(ln2_b, l), alpha, seq, tm_post, l)
    return x2.reshape(bsz, seq, d)
```

```python
import functools

import numpy as np
import jax
import jax.numpy as jnp
from jax import lax
from jax.experimental import pallas as pl
from jax.experimental.pallas import tpu as pltpu

F32 = jnp.float32
BF16 = jnp.bfloat16

HEAD_DIM = 64
H_HGRN, H_GLA, H_RWKV = 4, 6, 6
W_HGRN, W_GLA, W_RWKV = H_HGRN * HEAD_DIM, H_GLA * HEAD_DIM, H_RWKV * HEAD_DIM
GLA_GATE_RANK = 16
GLA_GATE_NORMALIZER = 16.0
RWKV_DECAY_RANK, RWKV_ICLR_RANK, RWKV_GATE_RANK = 64, 64, 128
RWKV_GN_EPS = 64e-5
N_HGRN_COLS = 4 * W_HGRN
N_GLA_COLS = 4 * W_GLA + GLA_GATE_RANK
N_RWKV_COLS = 3 * W_RWKV + RWKV_DECAY_RANK + RWKV_ICLR_RANK + RWKV_GATE_RANK
CHUNK = 64
LN_EPS = 1e-5
RMS_EPS = 1e-5
F_MIN = 1e-30

LANES = 128
PAIR = 2 * HEAD_DIM
SUB = 8
LEVEL_HALVES = (SUB, 2 * SUB, 4 * SUB)
assert 2 * LEVEL_HALVES[-1] == CHUNK
A_GROUP_GL = 16
OUT_SKEW = 3
LOG2E = 1.4426950408889634
VMEM_LIMIT = 56 * 1024 * 1024

W_GL = W_HGRN + W_GLA
N_GL_PAIRS = W_GL // PAIR
N_RW_PAIRS = W_RWKV // PAIR

NN = (((1,), (0,)), ((), ()))
NT = (((1,), (1,)), ((), ()))
TN = (((0,), (0,)), ((), ()))


def _mm(a, b, dims=NN):
    return lax.dot_general(a, b, dims, preferred_element_type=F32)


def _split2(x):
    hi = x.astype(BF16)
    lo = (x - hi.astype(F32)).astype(BF16)
    return hi, lo


def _bf16_parts(x, n):
    parts = []
    for i in range(n):
        p = x.astype(BF16)
        parts.append(p)
        if i + 1 < n:
            x = x - p.astype(F32)
    return parts


def _dot1(a, b, dims=NN):
    return _mm(a.astype(BF16), b.astype(BF16), dims)


def _dot3(a, b, dims=NN):
    ah, al = _split2(a)
    bh, bl = _split2(b)
    return _mm(ah, bh, dims) + (_mm(ah, bl, dims) + _mm(al, bh, dims))


def _dot_exact_lhs(a_bf16, b, n):
    return _mm(jnp.concatenate([a_bf16] * n, axis=1), jnp.concatenate(_bf16_parts(b, n), axis=0))


def _sigmoid(x):
    return 1.0 / (1.0 + jnp.exp(-x))


def _silu(x):
    return x * _sigmoid(x)


def _softplus(x):
    return jnp.maximum(x, 0.0) + jnp.log1p(jnp.exp(-jnp.abs(x)))


def _log_sigmoid(x):
    return -_softplus(-x)


def _layer_norm(y, g, b):
    mu = jnp.mean(y, axis=-1, keepdims=True)
    d = y - mu
    var = jnp.mean(d * d, axis=-1, keepdims=True)
    return d * lax.rsqrt(var + LN_EPS) * g + b


def _iota(shape, axis):
    return lax.broadcasted_iota(jnp.int32, shape, axis)


def _stack_heads(x):
    lane = _iota(x.shape, 1)
    return jnp.concatenate([jnp.where(lane < HEAD_DIM, x, 0.0), jnp.where(lane >= HEAD_DIM, x, 0.0)], axis=0)


def _pair_masks():
    t = _iota((CHUNK, PAIR), 0)
    s = _iota((CHUNK, PAIR), 1) % HEAD_DIM
    r2 = _iota((PAIR, PAIR), 0) // HEAD_DIM
    c2 = _iota((PAIR, PAIR), 1) // HEAD_DIM
    return dict(
        strict=s < t,
        incl=s <= t,
        eye=s == t,
        diag=(s // SUB == t // SUB) & (s <= t),
        level={hs: (s // (2 * hs) == t // (2 * hs)) & ((t // hs) % 2 == 1) & ((s // hs) % 2 == 0)
               for hs in LEVEL_HALVES},
        bd=r2 == c2,
    )


def _trace_skewed(stages, units, skew):
    for step in range(len(units) + skew * (len(stages) - 1)):
        for s, stage in enumerate(stages):
            u = step - skew * s
            if 0 <= u < len(units):
                stage(units[u])


def _tril_bf16():
    return jnp.where(_iota((CHUNK, CHUNK), 1) <= _iota((CHUNK, CHUNK), 0), 1.0, 0.0).astype(BF16)


def _seg_mean_matrix(width):
    r = _iota((width, width), 0) // HEAD_DIM
    c = _iota((width, width), 1) // HEAD_DIM
    return jnp.where(r == c, 1.0, 0.0).astype(BF16)


def _head_sums(xs):
    tiles = [x[:, j:j + LANES].astype(BF16) for x in xs for j in range(0, x.shape[1], LANES)]
    sums = []
    for i in range(0, len(tiles) - 1, 2):
        s = _mm(jnp.concatenate(tiles[i:i + 2], axis=1), _seg_mean_matrix(2 * LANES))
        sums += [s[:, 0:LANES], s[:, LANES:2 * LANES]]
    if len(tiles) % 2:
        sums.append(_mm(tiles[-1], _seg_mean_matrix(LANES)))
    out, i = [], 0
    for x in xs:
        n = x.shape[1] // LANES
        out.append(jnp.concatenate(sums[i:i + n], axis=1))
        i += n
    return out


def _mod_kernel(c_ref, w_ref, b_ref, o_ref):
    c = _silu(c_ref[...])
    o_ref[0] = _dot3(c, w_ref[0]) + b_ref[0]


def _modulation(c, ada_w, ada_b):
    depth, d, n = ada_w.shape
    bsz = c.shape[0]
    tn = 1536
    return pl.pallas_call(
        _mod_kernel,
        grid=(depth, n // tn),
        in_specs=[
            pl.BlockSpec((bsz, d), lambda l, j: (0, 0)),
            pl.BlockSpec((1, d, tn), lambda l, j: (l, 0, j)),
            pl.BlockSpec((1, 1, tn), lambda l, j: (l, 0, j)),
        ],
        out_specs=pl.BlockSpec((1, bsz, tn), lambda l, j: (l, 0, j)),
        out_shape=jax.ShapeDtypeStruct((depth, bsz, n), F32),
        compiler_params=pltpu.CompilerParams(
            dimension_semantics=("arbitrary", "arbitrary"), vmem_limit_bytes=VMEM_LIMIT),
        name="adaln_modulation",
    )(c, ada_w, ada_b.reshape(depth, 1, n))


def _lower_bound_kernel(x_ref, o_ref):
    depth = x_ref.shape[0]
    rows = [x_ref[l:l + 1, :] for l in range(depth)]
    m = functools.reduce(jnp.maximum, rows)
    e = [jnp.exp(r - m) for r in rows]
    tot = functools.reduce(lambda a, b: a + b, e)
    p = [ei / tot for ei in e]
    acc = jnp.zeros_like(p[0])
    for l in range(depth):
        acc = acc + p[l]
        o_ref[l:l + 1, :] = acc - p[0]


def _lower_bounds(logits):
    return pl.pallas_call(
        _lower_bound_kernel,
        out_shape=jax.ShapeDtypeStruct(logits.shape, F32),
        name="hgrn_lower_bounds",
    )(logits.astype(F32))


def _inproj_kernel(steps_per_batch, x_ref, mod_ref, w_ref, wal_ref, wr_ref, lb_ref, au_ref, ab_ref, mu_ref,
                   w0_ref, a0_ref, wa_ref, kk_ref, ka_ref,
                   gq_ref, gk_ref, gv_ref, gb_ref, gg_ref,
                   rr_ref, rlw_ref, rk_ref, rv_ref, rkk_ref, rka_ref, rhg_ref, carry_s):
    tm = x_ref.shape[0]
    w3 = W_RWKV

    @pl.when(pl.program_id(0) % steps_per_batch == 0)
    def _():
        carry_s[...] = jnp.zeros_like(carry_s)

    m = mod_ref[0]
    h = (x_ref[...] * (1.0 + m[1:2, :]) + m[0:1, :]).astype(BF16)

    def proj(c0, width):
        return _mm(h, w_ref[:, c0:c0 + width])

    first_row = _iota((tm, LANES), 0) == 0

    def shifted(c0, width):
        z = _mm(h, wr_ref[:, c0:c0 + width])
        z_prev = pltpu.roll(z, 1, axis=0)
        z_prev = jnp.where(jnp.concatenate([first_row] * (width // LANES), axis=1), carry_s[:, c0:c0 + width], z_prev)
        carry_s[:, c0:c0 + width] = z[tm - 1:tm, :]
        return z + (z_prev - z) * mu_ref[:, c0:c0 + width]

    g0 = N_HGRN_COLS
    n_rp = w3 // PAIR
    tile = lambda j: slice(j * PAIR, (j + 1) * PAIR)
    z_v = shifted(2 * w3, w3 + LANES)
    z_rk = shifted(0, 2 * w3)

    h_wa = z_v[:, w3:w3 + LANES]
    h_wa = jnp.where(_iota(h_wa.shape, 1) < RWKV_DECAY_RANK, jnp.tanh(h_wa), h_wa)
    wa = _dot1(h_wa, wa_ref[...])
    rv_ref[...] = z_v[:, 0:w3]
    rr_ref[...] = z_rk[:, 0:w3]

    later = [lambda: proj(W_HGRN, W_HGRN), lambda: proj(0, W_HGRN),
             lambda: shifted(3 * w3 + LANES, RWKV_GATE_RANK)]
    issued, a_tiles = [], []
    for j in range(n_rp):
        issued.append(later[j]())
        w_log = -_softplus(-(w0_ref[:, tile(j)] + wa[:, tile(j)])) - 0.5
        rlw_ref[:, tile(j)] = -jnp.exp(w_log)
        a_tiles.append(_sigmoid(a0_ref[:, tile(j)] + wa[:, w3 + j * PAIR:w3 + (j + 1) * PAIR]))
    zf, zq, z_hg = issued

    z_al = _mm(h, wal_ref[...])
    k = z_rk[:, w3:2 * w3]
    kk = k * kk_ref[...]
    kk = kk / jnp.maximum(jnp.sqrt(_head_sums([kk * kk])[0]), 1e-12)
    rkk_ref[...] = kk

    z_qk = proj(g0, 2 * W_GLA)
    for j in range(n_rp):
        rka_ref[:, tile(j)] = kk[:, tile(j)] * a_tiles[j]
        rk_ref[:, tile(j)] = k[:, tile(j)] * (1.0 + (a_tiles[j] - 1.0) * ka_ref[:, tile(j)])
    lb = lb_ref[...]
    f = lb + (1.0 - lb) * _sigmoid(zf)
    gb_ref[:, 0:W_HGRN] = jnp.log(jnp.maximum(f, F_MIN))
    k_h = (1.0 - lb) * _sigmoid(-zf)
    q_h = _silu(zq) * HEAD_DIM ** -0.5
    for p in range(W_HGRN // PAIR):
        gk_ref[p] = k_h[:, tile(p)]
        gq_ref[p] = q_h[:, tile(p)]

    z_vr = proj(g0 + 2 * W_GLA, 2 * W_GLA)
    rhg_ref[...] = _sigmoid(z_hg).astype(BF16)
    logit = _dot1(z_al, au_ref[...]) + ab_ref[...]
    gb_ref[:, W_HGRN:W_GL] = _log_sigmoid(logit) / GLA_GATE_NORMALIZER
    for p in range(W_GLA // PAIR):
        gq_ref[W_HGRN // PAIR + p] = z_qk[:, tile(p)] * HEAD_DIM ** -0.5
        gk_ref[W_HGRN // PAIR + p] = z_qk[:, W_GLA + p * PAIR:W_GLA + (p + 1) * PAIR]

    zg = proj(3 * W_HGRN, W_HGRN)
    gv_ref[:, W_HGRN:W_GL] = z_vr[:, 0:W_GLA].astype(BF16)
    gg_ref[:, W_HGRN:W_GL] = _silu(z_vr[:, W_GLA:2 * W_GLA]).astype(BF16)
    zi = proj(2 * W_HGRN, W_HGRN)
    gg_ref[:, 0:W_HGRN] = _silu(zg).astype(BF16)
    gv_ref[:, 0:W_HGRN] = zi.astype(BF16)


def _in_projection(x2, mod_l, w_main, w_alpha, w_rwkv, lb_l, alpha_up_p, alpha_b, mu, w0, a0, wa_blk, k_k, k_a,
                   seq, tm, layer):
    m_rows, d = x2.shape
    resident = lambda n: pl.BlockSpec((None, d, n), lambda i: (layer, 0, 0), pipeline_mode=pl.Buffered(1))
    steps_per_batch = seq // tm
    row = lambda i: (i, 0)
    const = lambda i: (0, 0)
    vec = lambda n: pl.BlockSpec((1, n), const)
    out = lambda n, dt: (pl.BlockSpec((tm, n), row), jax.ShapeDtypeStruct((m_rows, n), dt))
    pair_slabs = (pl.BlockSpec((N_GL_PAIRS, tm, PAIR), lambda i: (0, i, 0)),
                  jax.ShapeDtypeStruct((N_GL_PAIRS, m_rows, PAIR), F32))
    outs = [pair_slabs, pair_slabs, out(W_GL, BF16), out(W_GL, F32), out(W_GL, BF16),
            out(W_RWKV, F32), out(W_RWKV, F32), out(W_RWKV, F32), out(W_RWKV, F32), out(W_RWKV, F32),
            out(W_RWKV, F32), out(RWKV_GATE_RANK, BF16)]
    return pl.pallas_call(
        functools.partial(_inproj_kernel, steps_per_batch),
        grid=(m_rows // tm,),
        in_specs=[
            pl.BlockSpec((tm, d), row),
            pl.BlockSpec((1, 6, d), lambda i: (i // steps_per_batch, 0, 0)),
            resident(N_HGRN_COLS + 4 * W_GLA), resident(LANES), resident(N_RWKV_COLS),
            vec(W_HGRN),
            pl.BlockSpec((LANES, W_GLA), const),
            vec(W_GLA),
            vec(N_RWKV_COLS), vec(W_RWKV), vec(W_RWKV),
            pl.BlockSpec((LANES, 2 * W_RWKV), const),
            vec(W_RWKV), vec(W_RWKV),
        ],
        out_specs=[o[0] for o in outs],
        out_shape=[o[1] for o in outs],
        scratch_shapes=[pltpu.VMEM((1, N_RWKV_COLS), F32)],
        compiler_params=pltpu.CompilerParams(
            dimension_semantics=("arbitrary",), vmem_limit_bytes=VMEM_LIMIT),
        name="in_projection",
    )(x2, mod_l, w_main, w_alpha, w_rwkv, lb_l, alpha_up_p, alpha_b, mu, w0, a0, wa_blk, k_k, k_a)


def _gl_kernel(q_s, k_s, v_s, lg_ref, gate_ref, ng_ref, ind_ref, out_ref,
               st_ref, b_s, ad_s, e_s, o_s, sc_s, eend_s):
    tb = q_s.shape[1]
    n_chunks = tb // CHUNK
    n_groups = tb // SUB

    @pl.when(pl.program_id(1) == 0)
    def _():
        st_ref[...] = jnp.zeros_like(st_ref)

    tril = _tril_bf16()
    for c in range(n_chunks):
        rows = slice(c * CHUNK, (c + 1) * CHUNK)
        b = _dot_exact_lhs(tril, lg_ref[rows, :], 2)
        for p in range(N_GL_PAIRS):
            b_s[p, rows, :] = b[:, p * PAIR:(p + 1) * PAIR]

    for t in range(SUB - 1):
        e_s[t * n_groups:(t + 1) * n_groups, (t + 1) * PAIR:] = jnp.zeros((n_groups, (SUB - 1 - t) * PAIR), BF16)
    for p in range(N_GL_PAIRS):
        by_pos = lambda ref, t: ref[p, pl.ds(t, n_groups, stride=SUB), :]
        qs = [by_pos(q_s, t) for t in range(SUB)]
        ks = [by_pos(k_s, t) for t in range(SUB)]
        bs = [by_pos(b_s, t) * LOG2E for t in range(SUB)]
        for t in range(SUB):
            for sg in range(t + 1):
                e = qs[t] * ks[sg]
                if sg < t:
                    e = e * jnp.exp2(jnp.minimum(bs[t] - bs[sg], 0.0))
                e_s[t * n_groups:(t + 1) * n_groups, sg * PAIR:(sg + 1) * PAIR] = e.astype(BF16)
        ad = [jnp.zeros((n_groups, PAIR), F32)] * SUB
        for sg0 in range(0, SUB, 2):
            part = _mm(e_s[sg0 * n_groups:, sg0 * PAIR:(sg0 + 2) * PAIR], ind_ref[sg0 * PAIR:(sg0 + 2) * PAIR, :])
            for t in range(sg0, SUB):
                ad[t] = ad[t] + part[(t - sg0) * n_groups:(t - sg0 + 1) * n_groups, :]
        for t in range(SUB):
            ad_s[p, pl.ds(t, n_groups, stride=SUB), :] = ad[t]

    masks = _pair_masks()
    t_idx = _iota((CHUNK, PAIR), 0)
    second_half = {hs: (t_idx // hs) % 2 == 1 for hs in LEVEL_HALVES}
    level_sign = {hs: jnp.where(second_half[hs], LOG2E, -LOG2E) for hs in LEVEL_HALVES}

    def group_units(i):
        units = []
        for cc in range(A_GROUP_GL):
            c = i * A_GROUP_GL + cc
            rows = pl.ds(pl.multiple_of(c * CHUNK, CHUNK), CHUNK)
            for p in range(N_GL_PAIRS):
                units.append(dict(p=p, rows=rows, lanes=slice(p * PAIR, (p + 1) * PAIR),
                                  e_rows=pl.ds(pl.multiple_of(c * 8, 8), 8),
                                  st_rows=pl.ds(pl.multiple_of((c * N_GL_PAIRS + p) * PAIR, PAIR), PAIR)))
        return units

    def increments(i, carry):
        units = group_units(i)

        def decay_keys(un):
            b = b_s[un["p"], un["rows"], :]
            b_end = b[CHUNK - 1:CHUNK, :]
            un["k_end"] = k_s[un["p"], un["rows"], :] * jnp.exp(b_end - b)
            eend_s[un["e_rows"], un["lanes"]] = jnp.broadcast_to(jnp.exp(b_end), (8, PAIR))

        def outer(un):
            upd = _dot1(v_s[un["rows"], un["lanes"]], un["k_end"], TN)
            sc_s[un["st_rows"], :] = jnp.where(masks["bd"], upd, 0.0)

        _trace_skewed((decay_keys, outer), units, OUT_SKEW)
        return carry

    lax.fori_loop(0, n_chunks // A_GROUP_GL, increments, 0)

    sts = [st_ref[p] for p in range(N_GL_PAIRS)]
    for c in range(n_chunks):
        for p in range(N_GL_PAIRS):
            st_rows = slice((c * N_GL_PAIRS + p) * PAIR, (c * N_GL_PAIRS + p + 1) * PAIR)
            upd = sc_s[st_rows, :]
            sc_s[st_rows, :] = sts[p]
            sts[p] = sts[p] * eend_s[c * 8:c * 8 + 1, p * PAIR:(p + 1) * PAIR] + upd
    for p in range(N_GL_PAIRS):
        st_ref[p] = sts[p]

    def outputs(i, carry):
        units = group_units(i)

        def decays(un):
            q = q_s[un["p"], un["rows"], :]
            k = k_s[un["p"], un["rows"], :]
            b = b_s[un["p"], un["rows"], :]
            un["lv"] = []
            for hs in LEVEL_HALVES:
                e_mid = jnp.concatenate(
                    [jnp.broadcast_to(b[m:m + 1, :], (2 * hs, PAIR)) for m in range(hs - 1, CHUNK, 2 * hs)], axis=0)
                dec = jnp.exp2(jnp.minimum((b - e_mid) * level_sign[hs], 0.0))
                un["lv"].append(jnp.where(second_half[hs], q, k) * dec)
            un["q_in"] = q * jnp.exp(b)

        def scores(un):
            a = jnp.zeros((CHUNK, PAIR), F32)
            for hs, lv in zip(LEVEL_HALVES, un["lv"]):
                a = jnp.where(masks["level"][hs], _dot1(lv, jnp.transpose(_stack_heads(lv))), a)
            un["a"] = jnp.where(masks["diag"], ad_s[un["p"], un["rows"], :], a)

        def combine(un):
            o_s[un["rows"], un["lanes"]] = (_dot1(un["a"], _stack_heads(v_s[un["rows"], un["lanes"]]))
                                            + _dot1(un["q_in"], sc_s[un["st_rows"], :], NT))

        _trace_skewed((decays, scores, combine), units, OUT_SKEW)
        return carry

    lax.fori_loop(0, n_chunks // A_GROUP_GL, outputs, 0)

    o = o_s[...]
    ms = _head_sums([o * o])[0] * (1.0 / HEAD_DIM)
    out_ref[...] = (o * lax.rsqrt(ms + RMS_EPS) * ng_ref[...] * gate_ref[...]).astype(out_ref.dtype)


def _gl_indicator():
    r = np.arange(SUB * PAIR)
    c = np.arange(PAIR)
    sg = r // PAIR
    h = (r % PAIR) // HEAD_DIM
    ind = (h[:, None] == (c // HEAD_DIM)[None, :]) & (sg[:, None] == (c % SUB)[None, :])
    return jnp.asarray(ind, dtype=BF16)


def _gl_mixer(q, k, v, lg, gate, norm_g, bsz, seq, tb):
    steps = seq // tb
    row = lambda b, i: (b * steps + i, 0)
    const = lambda b, i: (0, 0)
    slab = pl.BlockSpec((tb, W_GL), row)
    pair_slabs = pl.BlockSpec((N_GL_PAIRS, tb, PAIR), lambda b, i: (0, b * steps + i, 0))
    scratch_f32 = pltpu.VMEM((tb, W_GL), F32)
    scratch_pairs = pltpu.VMEM((N_GL_PAIRS, tb, PAIR), F32)
    return pl.pallas_call(
        _gl_kernel,
        grid=(bsz, steps),
        in_specs=[
            pair_slabs, pair_slabs, slab, slab, slab,
            pl.BlockSpec((1, W_GL), const),
            pl.BlockSpec((SUB * PAIR, PAIR), const),
        ],
        out_specs=pl.BlockSpec((tb, W_GL), row),
        out_shape=jax.ShapeDtypeStruct((bsz * seq, W_GL), BF16),
        scratch_shapes=[
            pltpu.VMEM((N_GL_PAIRS, PAIR, PAIR), F32),
            scratch_pairs, scratch_pairs,
            pltpu.VMEM((tb, SUB * PAIR), BF16),
            scratch_f32,
            pltpu.VMEM((tb // CHUNK * N_GL_PAIRS * PAIR, PAIR), F32),
            pltpu.VMEM((tb // CHUNK * 8, W_GL), F32),
        ],
        compiler_params=pltpu.CompilerParams(
            dimension_semantics=("arbitrary", "arbitrary"), vmem_limit_bytes=VMEM_LIMIT),
        name="hgrn_gla_mixer",
    )(q, k, v, lg, gate, norm_g, _gl_indicator())


def _rwkv_kernel(r_s, lw_s, k_s, v_s, kk_s, ka_s, hg_s, gup_ref, rk_ref, gng_ref, gnb_ref,
                 out_ref, st_ref, y_s, rt_s, nb_s, nk_s, w_s, u0_s, bkt_s, ecol_s):
    bsz, tb = r_s.shape[0], r_s.shape[1]
    n_chunks = tb // CHUNK

    @pl.when(pl.program_id(0) == 0)
    def _():
        st_ref[...] = jnp.zeros_like(st_ref)

    masks = _pair_masks()
    tril = _tril_bf16()
    eye = jnp.where(masks["eye"], 1.0, 0.0)

    def phase_a(b, carry):
        units = []
        for c in range(n_chunks):
            rows = slice(c * CHUNK, (c + 1) * CHUNK)
            g_all = _dot_exact_lhs(tril, lw_s[b, rows, :], 2)
            for p in range(N_RW_PAIRS):
                lanes = slice(p * PAIR, (p + 1) * PAIR)
                units.append(dict(c=c, p=p, rows=rows, lanes=lanes, g=g_all[:, lanes]))
        def scale(un):
            rows, lanes, g = un["rows"], un["lanes"], un["g"]
            kc = k_s[b, rows, lanes]
            kkc = kk_s[b, rows, lanes]
            ka = ka_s[b, rows, lanes]
            g_end = g[CHUNK - 1:CHUNK, :]
            e_neg = jnp.exp(-g)
            e_end = jnp.exp(g_end - g)
            un["at"] = -kkc * jnp.exp(g - lw_s[b, rows, lanes])
            rt = r_s[b, rows, lanes] * jnp.exp(g)
            un["rt"] = rt
            un["bt"] = ka * e_neg
            un["kt"] = kc * e_neg
            rt_s[b, rows, lanes] = rt
            slot = un["c"] * N_RW_PAIRS + un["p"]
            bkt_s[b, slot] = jnp.transpose(jnp.concatenate([ka * e_end, kc * e_end], axis=0))
            ecol_s[b, slot] = jnp.transpose(jnp.broadcast_to(jnp.exp(g_end), (PAIR, PAIR)))

        def gram(un):
            g2 = _dot1(jnp.concatenate([un["at"], un["rt"]], axis=0),
                       jnp.concatenate([_stack_heads(un["bt"]), _stack_heads(un["kt"])], axis=0), NT)
            un["l"] = jnp.where(masks["strict"], g2[0:CHUNK, 0:PAIR], 0.0)
            un["mk"] = jnp.where(masks["strict"], g2[0:CHUNK, PAIR:2 * PAIR], 0.0)
            nb_s[b, un["rows"], un["lanes"]] = jnp.where(masks["incl"], g2[CHUNK:2 * CHUNK, 0:PAIR], 0.0)
            nk_s[b, un["rows"], un["lanes"]] = jnp.where(masks["incl"], g2[CHUNK:2 * CHUNK, PAIR:2 * PAIR], 0.0)

        def inv_first(un):
            un["t"] = eye + un["l"]
            un["lp"] = _dot1(un["l"], _stack_heads(un["l"]))
            un["mv"] = _dot1(un["mk"], _stack_heads(v_s[b, un["rows"], un["lanes"]]))

        def inv_step(un):
            both = _dot1(jnp.concatenate([un["lp"], un["t"]], axis=0), _stack_heads(un["lp"]))
            un["lp"] = both[0:CHUNK, :]
            un["t"] = un["t"] + both[CHUNK:2 * CHUNK, :]

        def inv_last(un):
            un["t"] = un["t"] + _dot1(un["t"], _stack_heads(un["lp"]))

        def apply_inv(un):
            wu = _dot1(un["t"], jnp.concatenate([_stack_heads(un["at"]), _stack_heads(un["mv"])], axis=1))
            w_s[b, un["rows"], un["lanes"]] = wu[:, 0:PAIR]
            u0_s[b, un["rows"], un["lanes"]] = wu[:, PAIR:2 * PAIR]

        for stage in (scale, gram, inv_first, inv_step, inv_step, inv_step, inv_step, inv_last, apply_inv):
            for un in units:
                stage(un)
        return carry

    lax.fori_loop(0, bsz, phase_a, 0)

    def phase_b(c, carry):
        rows = pl.ds(pl.multiple_of(c * CHUNK, CHUNK), CHUNK)
        seqs = [(b, p, slice(p * PAIR, (p + 1) * PAIR)) for b in range(bsz) for p in range(N_RW_PAIRS)]
        sts = [st_ref[b * N_RW_PAIRS + p] for b, p, _ in seqs]
        vs = [v_s[b, rows, lanes] for b, _, lanes in seqs]
        wrs = [_dot1(jnp.concatenate([w_s[b, rows, lanes], rt_s[b, rows, lanes]], axis=0), st)
               for (b, _, lanes), st in zip(seqs, sts)]
        us = [wr[0:CHUNK, :] + u0_s[b, rows, lanes] for (b, _, lanes), wr in zip(seqs, wrs)]
        upds = [_dot1(bkt_s[b, c * N_RW_PAIRS + p], jnp.concatenate([u, vc], axis=0))
                for (b, p, _), u, vc in zip(seqs, us, vs)]
        for (b, p, _), st, upd in zip(seqs, sts, upds):
            st_ref[b * N_RW_PAIRS + p] = st * ecol_s[b, c * N_RW_PAIRS + p] + jnp.where(masks["bd"], upd, 0.0)
        for (b, _, lanes), wr, u, vc in zip(seqs, wrs, us, vs):
            y_s[b, rows, lanes] = (
                wr[CHUNK:2 * CHUNK, :]
                + _dot1(jnp.concatenate([nb_s[b, rows, lanes], nk_s[b, rows, lanes]], axis=1),
                        jnp.concatenate([_stack_heads(u), _stack_heads(vc)], axis=0)))
        return carry

    lax.fori_loop(0, n_chunks, phase_b, 0)

    for b in range(bsz):
        y = y_s[b]
        s_hi, s_lo = _head_sums(_bf16_parts(y, 2))
        d = y - (s_hi + s_lo) * (1.0 / HEAD_DIM)
        s_var, s_bonus = _head_sums([d * d, r_s[b] * k_s[b] * rk_ref[...]])
        yn = d * lax.rsqrt(s_var * (1.0 / HEAD_DIM) + RWKV_GN_EPS) * gng_ref[...] + gnb_ref[...]
        gate = _dot1(hg_s[b], gup_ref[...])
        out_ref[b] = ((yn + s_bonus * v_s[b]) * gate).astype(out_ref.dtype)


def _rwkv_mixer(r, lw, k, v, kk, ka, hg, g_up, r_k, gn_g, gn_b, bsz, seq, tb):
    blk = lambda i: (0, i, 0)
    const = lambda i: (0, 0)
    vec = pl.BlockSpec((1, W_RWKV), const)
    tok = pl.BlockSpec((bsz, tb, W_RWKV), blk)
    slab = pltpu.VMEM((bsz, tb, W_RWKV), F32)
    unit_tiles = pltpu.VMEM((bsz, tb // CHUNK * N_RW_PAIRS, PAIR, PAIR), F32)
    seq_major = lambda a: a.reshape(bsz, seq, a.shape[-1])
    out = pl.pallas_call(
        _rwkv_kernel,
        grid=(seq // tb,),
        in_specs=[
            tok, tok, tok, tok, tok, tok,
            pl.BlockSpec((bsz, tb, RWKV_GATE_RANK), blk),
            pl.BlockSpec((RWKV_GATE_RANK, W_RWKV), const),
            vec, vec, vec,
        ],
        out_specs=pl.BlockSpec((bsz, tb, W_RWKV), blk),
        out_shape=jax.ShapeDtypeStruct((bsz, seq, W_RWKV), BF16),
        scratch_shapes=[
            pltpu.VMEM((bsz * N_RW_PAIRS, PAIR, PAIR), F32),
            slab,
            slab, slab, slab, slab, slab,
            unit_tiles, unit_tiles,
        ],
        compiler_params=pltpu.CompilerParams(
            dimension_semantics=("arbitrary",), vmem_limit_bytes=VMEM_LIMIT),
        name="rwkv7_mixer",
    )(*[seq_major(a) for a in (r, lw, k, v, kk, ka, hg)], g_up, r_k, gn_g, gn_b)
    return out.reshape(bsz * seq, W_RWKV)


def _post_kernel(alpha, ff_chunk, x_ref, ogl_ref, orw_ref, mod_ref, wout_ref, ln1g_ref, ln1b_ref,
                 wup_ref, wdn_ref, ln2g_ref, ln2b_ref, out_ref):
    m = mod_ref[0]
    gate1, shift2, scale2, gate2 = m[2:3, :], m[3:4, :], m[4:5, :], m[5:6, :]
    d_ff = wup_ref.shape[1]
    n_ff = d_ff // ff_chunk
    half = x_ref.shape[0] // 2
    groups = [slice(0, half), slice(half, 2 * half)]

    def out_proj(rows):
        return (_mm(ogl_ref[rows, :], wout_ref[0:W_GL, :])
                + _mm(orw_ref[rows, :], wout_ref[W_GL:W_GL + W_RWKV, :]))

    def norm1(rows, o):
        x1 = _layer_norm(alpha * x_ref[rows, :] + (1.0 + gate1) * o, ln1g_ref[...], ln1b_ref[...])
        return x1, (x1 * (1.0 + scale2) + shift2).astype(BF16)

    def mlp(h, acc, js):
        for j in js:
            cols = slice(j * ff_chunk, (j + 1) * ff_chunk)
            u = jnp.maximum(_mm(h, wup_ref[:, cols]), 0.0)
            acc = acc + _mm((u * u).astype(BF16), wdn_ref[cols, :])
        return acc

    def norm2(rows, x1, acc):
        out_ref[rows, :] = _layer_norm(alpha * x1 + (1.0 + gate2) * acc, ln2g_ref[...], ln2b_ref[...])

    o_a, o_b = out_proj(groups[0]), out_proj(groups[1])
    x1_a, h_a = norm1(groups[0], o_a)
    acc_a = mlp(h_a, jnp.zeros(x1_a.shape, F32), range(0, 2))
    x1_b, h_b = norm1(groups[1], o_b)
    acc_a = mlp(h_a, acc_a, range(2, n_ff))
    acc_b = mlp(h_b, jnp.zeros(x1_b.shape, F32), range(0, 2))
    norm2(groups[0], x1_a, acc_a)
    acc_b = mlp(h_b, acc_b, range(2, n_ff))
    norm2(groups[1], x1_b, acc_b)


def _post_mixer(x2, ogl, orw, mod_l, w_out, ln1_g, ln1_b, w_up, w_dn, ln2_g, ln2_b, alpha, seq, tm, layer):
    m_rows, d = x2.shape
    d_ff = w_up.shape[2]
    steps_per_batch = seq // tm
    row = lambda i: (i, 0)
    const = lambda i: (0, 0)
    vec = pl.BlockSpec((1, d), const)
    resident = lambda shape: pl.BlockSpec((None,) + shape, lambda i: (layer, 0, 0), pipeline_mode=pl.Buffered(1))
    return pl.pallas_call(
        functools.partial(_post_kernel, alpha, 512),
        grid=(m_rows // tm,),
        in_specs=[
            pl.BlockSpec((tm, d), row),
            pl.BlockSpec((tm, W_GL), row),
            pl.BlockSpec((tm, W_RWKV), row),
            pl.BlockSpec((1, 6, d), lambda i: (i // steps_per_batch, 0, 0)),
            resident((W_GL + W_RWKV, d)),
            vec, vec,
            resident((d, d_ff)),
            resident((d_ff, d)),
            vec, vec,
        ],
        out_specs=pl.BlockSpec((tm, d), row),
        out_shape=jax.ShapeDtypeStruct((m_rows, d), F32),
        compiler_params=pltpu.CompilerParams(
            dimension_semantics=("arbitrary",), vmem_limit_bytes=VMEM_LIMIT),
        name="outproj_mlp",
    )(x2, ogl, orw, mod_l, w_out, ln1_g, ln1_b, w_up, w_dn, ln2_g, ln2_b)


def kernel(x, c, hgrn_lb_logits, ada_w, ada_b, w_in, hgrn_norm_g, gla_alpha_up, gla_alpha_b, gla_norm_g,
           rwkv_mu, rwkv_w0, rwkv_w_up, rwkv_a0, rwkv_a_up, rwkv_g_up, rwkv_k_k, rwkv_k_a, rwkv_r_k,
           rwkv_gn_g, rwkv_gn_b, w_out, ln1_g, ln1_b, mlp_w_up, mlp_w_down, ln2_g, ln2_b):
    bsz, seq, d = x.shape
    depth = w_in.shape[0]
    alpha = (2.0 * depth) ** 0.25
    tm = min(512, seq)
    tb = min(512, seq)
    tb_gl = min(1024, seq)
    tm_post = min(1024, seq)
    assert all(seq % t == 0 for t in (tm, tb, tb_gl, tm_post)) and tb % CHUNK == 0
    assert tb_gl % (A_GROUP_GL * CHUNK) == 0 and tm_post % 16 == 0
    assert w_in.shape[2] == N_HGRN_COLS + N_GLA_COLS + N_RWKV_COLS

    mod = _modulation(c, ada_w, ada_b).reshape(depth, bsz, 6, d)
    lbs = _lower_bounds(hgrn_lb_logits)

    gla_end = N_HGRN_COLS + N_GLA_COLS
    w_in_b = w_in.astype(BF16)
    w_alpha = jnp.pad(w_in_b[:, :, gla_end - GLA_GATE_RANK:gla_end], ((0, 0), (0, 0), (0, LANES - GLA_GATE_RANK)))
    w_rwkv = w_in_b[:, :, gla_end:]
    alpha_up_p = jnp.concatenate(
        [gla_alpha_up, jnp.zeros((depth, LANES - GLA_GATE_RANK, W_GLA), gla_alpha_up.dtype)], axis=1)
    zeros_r = jnp.zeros((depth, RWKV_DECAY_RANK, W_RWKV), rwkv_w_up.dtype)
    wa_blk = jnp.concatenate(
        [jnp.concatenate([rwkv_w_up, zeros_r], axis=2), jnp.concatenate([zeros_r, rwkv_a_up], axis=2)], axis=1)
    norm_g = jnp.concatenate([hgrn_norm_g, gla_norm_g], axis=1)
    w_out_b = w_out.astype(BF16)
    w_up_b = mlp_w_up.astype(BF16)
    w_dn_b = mlp_w_down.astype(BF16)
    vec = lambda a, l: a[l].reshape(1, -1)

    x2 = x.reshape(bsz * seq, d)
    for l in range(depth):
        gq, gk, gv, glg, ggate, rr, rlw, rk, rv, rkk, rka, rhg = _in_projection(
            x2, mod[l], w_in_b, w_alpha, w_rwkv, vec(lbs, l), alpha_up_p[l], vec(gla_alpha_b, l), vec(rwkv_mu, l),
            vec(rwkv_w0, l), vec(rwkv_a0, l), wa_blk[l], vec(rwkv_k_k, l), vec(rwkv_k_a, l), seq, tm, l)
        ogl = _gl_mixer(gq, gk, gv, glg, ggate, vec(norm_g, l), bsz, seq, tb_gl)
        orw = _rwkv_mixer(rr, rlw, rk, rv, rkk, rka, rhg, rwkv_g_up[l], vec(rwkv_r_k, l), vec(rwkv_gn_g, l),
                          vec(rwkv_gn_b, l), bsz, seq, tb)
        x2 = _post_mixer(x2, ogl, orw, mod[l], w_out_b, vec(ln1_g, l), vec(ln1_b, l), w_up_b, w_dn_b,
                         vec(ln2_g, l), vec(ln2_b, l), alpha, seq, tm_post, l)
    return x2.reshape(bsz, seq, d)
```

```python
import functools

import numpy as np
import jax
import jax.numpy as jnp
from jax import lax
from jax.experimental import pallas as pl
from jax.experimental.pallas import tpu as pltpu

F32 = jnp.float32
BF16 = jnp.bfloat16

HEAD_DIM = 64
H_HGRN, H_GLA, H_RWKV = 4, 6, 6
W_HGRN, W_GLA, W_RWKV = H_HGRN * HEAD_DIM, H_GLA * HEAD_DIM, H_RWKV * HEAD_DIM
GLA_GATE_RANK = 16
GLA_GATE_NORMALIZER = 16.0
RWKV_DECAY_RANK, RWKV_ICLR_RANK, RWKV_GATE_RANK = 64, 64, 128
RWKV_GN_EPS = 64e-5
N_HGRN_COLS = 4 * W_HGRN
N_GLA_COLS = 4 * W_GLA + GLA_GATE_RANK
N_RWKV_COLS = 3 * W_RWKV + RWKV_DECAY_RANK + RWKV_ICLR_RANK + RWKV_GATE_RANK
CHUNK = 64
LN_EPS = 1e-5
RMS_EPS = 1e-5
F_MIN = 1e-30

LANES = 128
PAIR = 2 * HEAD_DIM
SUB = 8
LEVEL_HALVES = (SUB, 2 * SUB, 4 * SUB)
assert 2 * LEVEL_HALVES[-1] == CHUNK
A_GROUP_GL = 16
OUT_SKEW = 4
LOG2E = 1.4426950408889634
VMEM_LIMIT = 56 * 1024 * 1024

W_GL = W_HGRN + W_GLA
N_GL_PAIRS = W_GL // PAIR
N_RW_PAIRS = W_RWKV // PAIR

NN = (((1,), (0,)), ((), ()))
NT = (((1,), (1,)), ((), ()))
TN = (((0,), (0,)), ((), ()))


def _mm(a, b, dims=NN):
    return lax.dot_general(a, b, dims, preferred_element_type=F32)


def _split2(x):
    hi = x.astype(BF16)
    lo = (x - hi.astype(F32)).astype(BF16)
    return hi, lo


def _bf16_parts(x, n):
    parts = []
    for i in range(n):
        p = x.astype(BF16)
        parts.append(p)
        if i + 1 < n:
            x = x - p.astype(F32)
    return parts


def _dot1(a, b, dims=NN):
    return _mm(a.astype(BF16), b.astype(BF16), dims)


def _dot3(a, b, dims=NN):
    ah, al = _split2(a)
    bh, bl = _split2(b)
    return _mm(ah, bh, dims) + (_mm(ah, bl, dims) + _mm(al, bh, dims))


def _dot_exact_lhs(a_bf16, b, n):
    return _mm(jnp.concatenate([a_bf16] * n, axis=1), jnp.concatenate(_bf16_parts(b, n), axis=0))


def _sigmoid(x):
    return 1.0 / (1.0 + jnp.exp(-x))


def _silu(x):
    return x * _sigmoid(x)


def _softplus(x):
    return jnp.maximum(x, 0.0) + jnp.log1p(jnp.exp(-jnp.abs(x)))


def _log_sigmoid(x):
    return -_softplus(-x)


def _layer_norm(y, g, b):
    mu = jnp.mean(y, axis=-1, keepdims=True)
    d = y - mu
    var = jnp.mean(d * d, axis=-1, keepdims=True)
    return d * lax.rsqrt(var + LN_EPS) * g + b


def _iota(shape, axis):
    return lax.broadcasted_iota(jnp.int32, shape, axis)


def _stack_heads(x):
    lane = _iota(x.shape, 1)
    return jnp.concatenate([jnp.where(lane < HEAD_DIM, x, 0.0), jnp.where(lane >= HEAD_DIM, x, 0.0)], axis=0)


def _pair_masks():
    t = _iota((CHUNK, PAIR), 0)
    s = _iota((CHUNK, PAIR), 1) % HEAD_DIM
    r2 = _iota((PAIR, PAIR), 0) // HEAD_DIM
    c2 = _iota((PAIR, PAIR), 1) // HEAD_DIM
    return dict(
        strict=s < t,
        incl=s <= t,
        eye=s == t,
        diag=(s // SUB == t // SUB) & (s <= t),
        level={hs: (s // (2 * hs) == t // (2 * hs)) & ((t // hs) % 2 == 1) & ((s // hs) % 2 == 0)
               for hs in LEVEL_HALVES},
        bd=r2 == c2,
    )


def _trace_skewed(stages, units, skew):
    for step in range(len(units) + skew * (len(stages) - 1)):
        for s, stage in enumerate(stages):
            u = step - skew * s
            if 0 <= u < len(units):
                stage(units[u])


def _tril_bf16():
    return jnp.where(_iota((CHUNK, CHUNK), 1) <= _iota((CHUNK, CHUNK), 0), 1.0, 0.0).astype(BF16)


def _seg_mean_matrix(width):
    r = _iota((width, width), 0) // HEAD_DIM
    c = _iota((width, width), 1) // HEAD_DIM
    return jnp.where(r == c, 1.0, 0.0).astype(BF16)


def _head_sums(xs):
    tiles = [x[:, j:j + LANES].astype(BF16) for x in xs for j in range(0, x.shape[1], LANES)]
    sums = []
    for i in range(0, len(tiles) - 1, 2):
        s = _mm(jnp.concatenate(tiles[i:i + 2], axis=1), _seg_mean_matrix(2 * LANES))
        sums += [s[:, 0:LANES], s[:, LANES:2 * LANES]]
    if len(tiles) % 2:
        sums.append(_mm(tiles[-1], _seg_mean_matrix(LANES)))
    out, i = [], 0
    for x in xs:
        n = x.shape[1] // LANES
        out.append(jnp.concatenate(sums[i:i + n], axis=1))
        i += n
    return out


def _mod_kernel(c_ref, w_ref, b_ref, o_ref):
    c = _silu(c_ref[...])
    o_ref[0] = _dot3(c, w_ref[0]) + b_ref[0]


def _modulation(c, ada_w, ada_b):
    depth, d, n = ada_w.shape
    bsz = c.shape[0]
    tn = 1536
    return pl.pallas_call(
        _mod_kernel,
        grid=(depth, n // tn),
        in_specs=[
            pl.BlockSpec((bsz, d), lambda l, j: (0, 0)),
            pl.BlockSpec((1, d, tn), lambda l, j: (l, 0, j)),
            pl.BlockSpec((1, 1, tn), lambda l, j: (l, 0, j)),
        ],
        out_specs=pl.BlockSpec((1, bsz, tn), lambda l, j: (l, 0, j)),
        out_shape=jax.ShapeDtypeStruct((depth, bsz, n), F32),
        compiler_params=pltpu.CompilerParams(
            dimension_semantics=("arbitrary", "arbitrary"), vmem_limit_bytes=VMEM_LIMIT),
        name="adaln_modulation",
    )(c, ada_w, ada_b.reshape(depth, 1, n))


def _lower_bound_kernel(x_ref, o_ref):
    depth = x_ref.shape[0]
    rows = [x_ref[l:l + 1, :] for l in range(depth)]
    m = functools.reduce(jnp.maximum, rows)
    e = [jnp.exp(r - m) for r in rows]
    tot = functools.reduce(lambda a, b: a + b, e)
    p = [ei / tot for ei in e]
    acc = jnp.zeros_like(p[0])
    for l in range(depth):
        acc = acc + p[l]
        o_ref[l:l + 1, :] = acc - p[0]


def _lower_bounds(logits):
    return pl.pallas_call(
        _lower_bound_kernel,
        out_shape=jax.ShapeDtypeStruct(logits.shape, F32),
        name="hgrn_lower_bounds",
    )(logits.astype(F32))


def _inproj_kernel(steps_per_batch, x_ref, mod_ref, w_ref, wal_ref, wr_ref, lb_ref, au_ref, ab_ref, mu_ref,
                   w0_ref, a0_ref, wa_ref, kk_ref, ka_ref,
                   gq_ref, gk_ref, gv_ref, gb_ref, gg_ref,
                   rr_ref, rlw_ref, rk_ref, rv_ref, rkk_ref, rka_ref, rhg_ref, carry_s):
    tm = x_ref.shape[0]
    w3 = W_RWKV

    @pl.when(pl.program_id(0) % steps_per_batch == 0)
    def _():
        carry_s[...] = jnp.zeros_like(carry_s)

    m = mod_ref[0]
    h = (x_ref[...] * (1.0 + m[1:2, :]) + m[0:1, :]).astype(BF16)

    def proj(c0, width):
        return _mm(h, w_ref[:, c0:c0 + width])

    first_row = _iota((tm, LANES), 0) == 0

    def shifted(c0, width):
        z = _mm(h, wr_ref[:, c0:c0 + width])
        z_prev = pltpu.roll(z, 1, axis=0)
        z_prev = jnp.where(jnp.concatenate([first_row] * (width // LANES), axis=1), carry_s[:, c0:c0 + width], z_prev)
        carry_s[:, c0:c0 + width] = z[tm - 1:tm, :]
        return z + (z_prev - z) * mu_ref[:, c0:c0 + width]

    g0 = N_HGRN_COLS
    n_rp = w3 // PAIR
    tile = lambda j: slice(j * PAIR, (j + 1) * PAIR)
    z_v = shifted(2 * w3, w3 + LANES)
    z_rk = shifted(0, 2 * w3)

    h_wa = z_v[:, w3:w3 + LANES]
    h_wa = jnp.where(_iota(h_wa.shape, 1) < RWKV_DECAY_RANK, jnp.tanh(h_wa), h_wa)
    wa = _dot1(h_wa, wa_ref[...])
    rv_ref[...] = z_v[:, 0:w3]
    rr_ref[...] = z_rk[:, 0:w3]

    later = [lambda: proj(W_HGRN, W_HGRN), lambda: proj(0, W_HGRN),
             lambda: shifted(3 * w3 + LANES, RWKV_GATE_RANK)]
    issued, a_tiles = [], []
    for j in range(n_rp):
        issued.append(later[j]())
        w_log = -_softplus(-(w0_ref[:, tile(j)] + wa[:, tile(j)])) - 0.5
        rlw_ref[:, tile(j)] = -jnp.exp(w_log)
        a_tiles.append(_sigmoid(a0_ref[:, tile(j)] + wa[:, w3 + j * PAIR:w3 + (j + 1) * PAIR]))
    zf, zq, z_hg = issued

    z_al = _mm(h, wal_ref[...])
    k = z_rk[:, w3:2 * w3]
    kk = k * kk_ref[...]
    kk = kk / jnp.maximum(jnp.sqrt(_head_sums([kk * kk])[0]), 1e-12)
    rkk_ref[...] = kk

    z_qk = proj(g0, 2 * W_GLA)
    for j in range(n_rp):
        rka_ref[:, tile(j)] = kk[:, tile(j)] * a_tiles[j]
        rk_ref[:, tile(j)] = k[:, tile(j)] * (1.0 + (a_tiles[j] - 1.0) * ka_ref[:, tile(j)])
    lb = lb_ref[...]
    f = lb + (1.0 - lb) * _sigmoid(zf)
    gb_ref[:, 0:W_HGRN] = jnp.log(jnp.maximum(f, F_MIN))
    k_h = (1.0 - lb) * _sigmoid(-zf)
    q_h = _silu(zq) * HEAD_DIM ** -0.5
    for p in range(W_HGRN // PAIR):
        gk_ref[p] = k_h[:, tile(p)]
        gq_ref[p] = q_h[:, tile(p)]

    z_vr = proj(g0 + 2 * W_GLA, 2 * W_GLA)
    rhg_ref[...] = _sigmoid(z_hg).astype(BF16)
    logit = _dot1(z_al, au_ref[...]) + ab_ref[...]
    gb_ref[:, W_HGRN:W_GL] = _log_sigmoid(logit) / GLA_GATE_NORMALIZER
    for p in range(W_GLA // PAIR):
        gq_ref[W_HGRN // PAIR + p] = z_qk[:, tile(p)] * HEAD_DIM ** -0.5
        gk_ref[W_HGRN // PAIR + p] = z_qk[:, W_GLA + p * PAIR:W_GLA + (p + 1) * PAIR]

    zg = proj(3 * W_HGRN, W_HGRN)
    gv_ref[:, W_HGRN:W_GL] = z_vr[:, 0:W_GLA].astype(BF16)
    gg_ref[:, W_HGRN:W_GL] = _silu(z_vr[:, W_GLA:2 * W_GLA]).astype(BF16)
    zi = proj(2 * W_HGRN, W_HGRN)
    gg_ref[:, 0:W_HGRN] = _silu(zg).astype(BF16)
    gv_ref[:, 0:W_HGRN] = zi.astype(BF16)


def _in_projection(x2, mod_l, w_main, w_alpha, w_rwkv, lb_l, alpha_up_p, alpha_b, mu, w0, a0, wa_blk, k_k, k_a,
                   seq, tm, layer):
    m_rows, d = x2.shape
    resident = lambda n: pl.BlockSpec((None, d, n), lambda i: (layer, 0, 0), pipeline_mode=pl.Buffered(1))
    steps_per_batch = seq // tm
    row = lambda i: (i, 0)
    const = lambda i: (0, 0)
    vec = lambda n: pl.BlockSpec((1, n), const)
    out = lambda n, dt: (pl.BlockSpec((tm, n), row), jax.ShapeDtypeStruct((m_rows, n), dt))
    pair_slabs = (pl.BlockSpec((N_GL_PAIRS, tm, PAIR), lambda i: (0, i, 0)),
                  jax.ShapeDtypeStruct((N_GL_PAIRS, m_rows, PAIR), F32))
    outs = [pair_slabs, pair_slabs, out(W_GL, BF16), out(W_GL, F32), out(W_GL, BF16),
            out(W_RWKV, F32), out(W_RWKV, F32), out(W_RWKV, F32), out(W_RWKV, F32), out(W_RWKV, F32),
            out(W_RWKV, F32), out(RWKV_GATE_RANK, BF16)]
    return pl.pallas_call(
        functools.partial(_inproj_kernel, steps_per_batch),
        grid=(m_rows // tm,),
        in_specs=[
            pl.BlockSpec((tm, d), row),
            pl.BlockSpec((1, 6, d), lambda i: (i // steps_per_batch, 0, 0)),
            resident(N_HGRN_COLS + 4 * W_GLA), resident(LANES), resident(N_RWKV_COLS),
            vec(W_HGRN),
            pl.BlockSpec((LANES, W_GLA), const),
            vec(W_GLA),
            vec(N_RWKV_COLS), vec(W_RWKV), vec(W_RWKV),
            pl.BlockSpec((LANES, 2 * W_RWKV), const),
            vec(W_RWKV), vec(W_RWKV),
        ],
        out_specs=[o[0] for o in outs],
        out_shape=[o[1] for o in outs],
        scratch_shapes=[pltpu.VMEM((1, N_RWKV_COLS), F32)],
        compiler_params=pltpu.CompilerParams(
            dimension_semantics=("arbitrary",), vmem_limit_bytes=VMEM_LIMIT),
        name="in_projection",
    )(x2, mod_l, w_main, w_alpha, w_rwkv, lb_l, alpha_up_p, alpha_b, mu, w0, a0, wa_blk, k_k, k_a)


def _gl_kernel(q_s, k_s, v_s, lg_ref, gate_ref, ng_ref, ind_ref, out_ref,
               st_ref, b_s, ad_s, e_s, o_s, sc_s, eend_s):
    tb = q_s.shape[1]
    n_chunks = tb // CHUNK
    n_groups = tb // SUB

    @pl.when(pl.program_id(1) == 0)
    def _():
        st_ref[...] = jnp.zeros_like(st_ref)

    tril = _tril_bf16()
    for c in range(n_chunks):
        rows = slice(c * CHUNK, (c + 1) * CHUNK)
        b = _dot_exact_lhs(tril, lg_ref[rows, :], 2)
        for p in range(N_GL_PAIRS):
            b_s[p, rows, :] = b[:, p * PAIR:(p + 1) * PAIR]

    for t in range(SUB - 1):
        e_s[t * n_groups:(t + 1) * n_groups, (t + 1) * PAIR:] = jnp.zeros((n_groups, (SUB - 1 - t) * PAIR), BF16)
    for p in range(N_GL_PAIRS):
        by_pos = lambda ref, t: ref[p, pl.ds(t, n_groups, stride=SUB), :]
        qs = [by_pos(q_s, t) for t in range(SUB)]
        ks = [by_pos(k_s, t) for t in range(SUB)]
        bs = [by_pos(b_s, t) * LOG2E for t in range(SUB)]
        for t in range(SUB):
            for sg in range(t + 1):
                e = qs[t] * ks[sg]
                if sg < t:
                    e = e * jnp.exp2(jnp.minimum(bs[t] - bs[sg], 0.0))
                e_s[t * n_groups:(t + 1) * n_groups, sg * PAIR:(sg + 1) * PAIR] = e.astype(BF16)
        ad = [jnp.zeros((n_groups, PAIR), F32)] * SUB
        for sg0 in range(0, SUB, 2):
            part = _mm(e_s[sg0 * n_groups:, sg0 * PAIR:(sg0 + 2) * PAIR], ind_ref[sg0 * PAIR:(sg0 + 2) * PAIR, :])
            for t in range(sg0, SUB):
                ad[t] = ad[t] + part[(t - sg0) * n_groups:(t - sg0 + 1) * n_groups, :]
        for t in range(SUB):
            ad_s[p, pl.ds(t, n_groups, stride=SUB), :] = ad[t]

    masks = _pair_masks()
    t_idx = _iota((CHUNK, PAIR), 0)
    second_half = {hs: (t_idx // hs) % 2 == 1 for hs in LEVEL_HALVES}
    level_sign = {hs: jnp.where(second_half[hs], LOG2E, -LOG2E) for hs in LEVEL_HALVES}

    def group_units(i):
        units = []
        for cc in range(A_GROUP_GL):
            c = i * A_GROUP_GL + cc
            rows = pl.ds(pl.multiple_of(c * CHUNK, CHUNK), CHUNK)
            for p in range(N_GL_PAIRS):
                units.append(dict(p=p, rows=rows, lanes=slice(p * PAIR, (p + 1) * PAIR),
                                  e_rows=pl.ds(pl.multiple_of(c * 8, 8), 8),
                                  st_rows=pl.ds(pl.multiple_of((c * N_GL_PAIRS + p) * PAIR, PAIR), PAIR)))
        return units

    def increments(i, carry):
        units = group_units(i)

        def decay_keys(un):
            b = b_s[un["p"], un["rows"], :]
            b_end = b[CHUNK - 1:CHUNK, :]
            un["k_end"] = k_s[un["p"], un["rows"], :] * jnp.exp(b_end - b)
            eend_s[un["e_rows"], un["lanes"]] = jnp.broadcast_to(jnp.exp(b_end), (8, PAIR))

        def outer(un):
            upd = _dot1(v_s[un["rows"], un["lanes"]], un["k_end"], TN)
            sc_s[un["st_rows"], :] = jnp.where(masks["bd"], upd, 0.0)

        _trace_skewed((decay_keys, outer), units, OUT_SKEW)
        return carry

    lax.fori_loop(0, n_chunks // A_GROUP_GL, increments, 0)

    sts = [st_ref[p] for p in range(N_GL_PAIRS)]
    for c in range(n_chunks):
        for p in range(N_GL_PAIRS):
            st_rows = slice((c * N_GL_PAIRS + p) * PAIR, (c * N_GL_PAIRS + p + 1) * PAIR)
            upd = sc_s[st_rows, :]
            sc_s[st_rows, :] = sts[p]
            sts[p] = sts[p] * eend_s[c * 8:c * 8 + 1, p * PAIR:(p + 1) * PAIR] + upd
    for p in range(N_GL_PAIRS):
        st_ref[p] = sts[p]

    def outputs(i, carry):
        units = group_units(i)

        def decays(un):
            q = q_s[un["p"], un["rows"], :]
            k = k_s[un["p"], un["rows"], :]
            b = b_s[un["p"], un["rows"], :]
            un["lv"] = []
            for hs in LEVEL_HALVES:
                e_mid = jnp.concatenate(
                    [jnp.broadcast_to(b[m:m + 1, :], (2 * hs, PAIR)) for m in range(hs - 1, CHUNK, 2 * hs)], axis=0)
                dec = jnp.exp2(jnp.minimum((b - e_mid) * level_sign[hs], 0.0))
                un["lv"].append(jnp.where(second_half[hs], q, k) * dec)
            un["q_in"] = q * jnp.exp(b)

        def scores(un):
            a = jnp.zeros((CHUNK, PAIR), F32)
            for hs, lv in zip(LEVEL_HALVES, un["lv"]):
                a = jnp.where(masks["level"][hs], _dot1(lv, jnp.transpose(_stack_heads(lv))), a)
            un["a"] = jnp.where(masks["diag"], ad_s[un["p"], un["rows"], :], a)

        def combine(un):
            o_s[un["rows"], un["lanes"]] = (_dot1(un["a"], _stack_heads(v_s[un["rows"], un["lanes"]]))
                                            + _dot1(un["q_in"], sc_s[un["st_rows"], :], NT))

        _trace_skewed((decays, scores, combine), units, OUT_SKEW)
        return carry

    lax.fori_loop(0, n_chunks // A_GROUP_GL, outputs, 0)

    o = o_s[...]
    ms = _head_sums([o * o])[0] * (1.0 / HEAD_DIM)
    out_ref[...] = (o * lax.rsqrt(ms + RMS_EPS) * ng_ref[...] * gate_ref[...]).astype(out_ref.dtype)


def _gl_indicator():
    r = np.arange(SUB * PAIR)
    c = np.arange(PAIR)
    sg = r // PAIR
    h = (r % PAIR) // HEAD_DIM
    ind = (h[:, None] == (c // HEAD_DIM)[None, :]) & (sg[:, None] == (c % SUB)[None, :])
    return jnp.asarray(ind, dtype=BF16)


def _gl_mixer(q, k, v, lg, gate, norm_g, bsz, seq, tb):
    steps = seq // tb
    row = lambda b, i: (b * steps + i, 0)
    const = lambda b, i: (0, 0)
    slab = pl.BlockSpec((tb, W_GL), row)
    pair_slabs = pl.BlockSpec((N_GL_PAIRS, tb, PAIR), lambda b, i: (0, b * steps + i, 0))
    scratch_f32 = pltpu.VMEM((tb, W_GL), F32)
    scratch_pairs = pltpu.VMEM((N_GL_PAIRS, tb, PAIR), F32)
    return pl.pallas_call(
        _gl_kernel,
        grid=(bsz, steps),
        in_specs=[
            pair_slabs, pair_slabs, slab, slab, slab,
            pl.BlockSpec((1, W_GL), const),
            pl.BlockSpec((SUB * PAIR, PAIR), const),
        ],
        out_specs=pl.BlockSpec((tb, W_GL), row),
        out_shape=jax.ShapeDtypeStruct((bsz * seq, W_GL), BF16),
        scratch_shapes=[
            pltpu.VMEM((N_GL_PAIRS, PAIR, PAIR), F32),
            scratch_pairs, scratch_pairs,
            pltpu.VMEM((tb, SUB * PAIR), BF16),
            scratch_f32,
            pltpu.VMEM((tb // CHUNK * N_GL_PAIRS * PAIR, PAIR), F32),
            pltpu.VMEM((tb // CHUNK * 8, W_GL), F32),
        ],
        compiler_params=pltpu.CompilerParams(
            dimension_semantics=("arbitrary", "arbitrary"), vmem_limit_bytes=VMEM_LIMIT),
        name="hgrn_gla_mixer",
    )(q, k, v, lg, gate, norm_g, _gl_indicator())


def _rwkv_kernel(r_s, lw_s, k_s, v_s, kk_s, ka_s, hg_s, gup_ref, rk_ref, gng_ref, gnb_ref,
                 out_ref, st_ref, y_s, rt_s, nb_s, nk_s, w_s, u0_s, bkt_s, ecol_s):
    bsz, tb = r_s.shape[0], r_s.shape[1]
    n_chunks = tb // CHUNK

    @pl.when(pl.program_id(0) == 0)
    def _():
        st_ref[...] = jnp.zeros_like(st_ref)

    masks = _pair_masks()
    tril = _tril_bf16()
    eye = jnp.where(masks["eye"], 1.0, 0.0)

    def phase_a(b):
        units = []
        for c in range(n_chunks):
            rows = slice(c * CHUNK, (c + 1) * CHUNK)
            g_all = _dot_exact_lhs(tril, lw_s[b, rows, :], 2)
            for p in range(N_RW_PAIRS):
                lanes = slice(p * PAIR, (p + 1) * PAIR)
                units.append(dict(c=c, p=p, rows=rows, lanes=lanes, g=g_all[:, lanes]))
        def scale(un):
            rows, lanes, g = un["rows"], un["lanes"], un["g"]
            kc = k_s[b, rows, lanes]
            kkc = kk_s[b, rows, lanes]
            ka = ka_s[b, rows, lanes]
            g_end = g[CHUNK - 1:CHUNK, :]
            e_neg = jnp.exp(-g)
            e_end = jnp.exp(g_end - g)
            un["at"] = -kkc * jnp.exp(g - lw_s[b, rows, lanes])
            rt = r_s[b, rows, lanes] * jnp.exp(g)
            un["rt"] = rt
            un["bt"] = ka * e_neg
            un["kt"] = kc * e_neg
            rt_s[b, rows, lanes] = rt
            slot = un["c"] * N_RW_PAIRS + un["p"]
            bkt_s[b, slot] = jnp.transpose(jnp.concatenate([ka * e_end, kc * e_end], axis=0))
            ecol_s[b, slot] = jnp.transpose(jnp.broadcast_to(jnp.exp(g_end), (PAIR, PAIR)))

        def gram(un):
            g2 = _dot1(jnp.concatenate([un["at"], un["rt"]], axis=0),
                       jnp.concatenate([_stack_heads(un["bt"]), _stack_heads(un["kt"])], axis=0), NT)
            un["l"] = jnp.where(masks["strict"], g2[0:CHUNK, 0:PAIR], 0.0)
            un["mk"] = jnp.where(masks["strict"], g2[0:CHUNK, PAIR:2 * PAIR], 0.0)
            nb_s[b, un["rows"], un["lanes"]] = jnp.where(masks["incl"], g2[CHUNK:2 * CHUNK, 0:PAIR], 0.0)
            nk_s[b, un["rows"], un["lanes"]] = jnp.where(masks["incl"], g2[CHUNK:2 * CHUNK, PAIR:2 * PAIR], 0.0)

        def inv_first(un):
            un["t"] = eye + un["l"]
            un["lp"] = _dot1(un["l"], _stack_heads(un["l"]))
            un["mv"] = _dot1(un["mk"], _stack_heads(v_s[b, un["rows"], un["lanes"]]))

        def inv_step(un):
            both = _dot1(jnp.concatenate([un["lp"], un["t"]], axis=0), _stack_heads(un["lp"]))
            un["lp"] = both[0:CHUNK, :]
            un["t"] = un["t"] + both[CHUNK:2 * CHUNK, :]

        def inv_last(un):
            un["t"] = un["t"] + _dot1(un["t"], _stack_heads(un["lp"]))

        def apply_inv(un):
            wu = _dot1(un["t"], jnp.concatenate([_stack_heads(un["at"]), _stack_heads(un["mv"])], axis=1))
            w_s[b, un["rows"], un["lanes"]] = wu[:, 0:PAIR]
            u0_s[b, un["rows"], un["lanes"]] = wu[:, PAIR:2 * PAIR]

        return units, (scale, gram, inv_first, inv_step, inv_step, inv_step, inv_step, inv_last, apply_inv)

    per_seq = [phase_a(b) for b in range(bsz)]
    for s in range(len(per_seq[0][1])):
        for units, stages in per_seq:
            for un in units:
                stages[s](un)

    def phase_b(c, carry):
        rows = pl.ds(pl.multiple_of(c * CHUNK, CHUNK), CHUNK)
        seqs = [(b, p, slice(p * PAIR, (p + 1) * PAIR)) for b in range(bsz) for p in range(N_RW_PAIRS)]
        sts = [st_ref[b * N_RW_PAIRS + p] for b, p, _ in seqs]
        vs = [v_s[b, rows, lanes] for b, _, lanes in seqs]
        wrs = [_dot1(jnp.concatenate([w_s[b, rows, lanes], rt_s[b, rows, lanes]], axis=0), st)
               for (b, _, lanes), st in zip(seqs, sts)]
        us = [wr[0:CHUNK, :] + u0_s[b, rows, lanes] for (b, _, lanes), wr in zip(seqs, wrs)]
        upds = [_dot1(bkt_s[b, c * N_RW_PAIRS + p], jnp.concatenate([u, vc], axis=0))
                for (b, p, _), u, vc in zip(seqs, us, vs)]
        for (b, p, _), st, upd in zip(seqs, sts, upds):
            st_ref[b * N_RW_PAIRS + p] = st * ecol_s[b, c * N_RW_PAIRS + p] + jnp.where(masks["bd"], upd, 0.0)
        for (b, _, lanes), wr, u, vc in zip(seqs, wrs, us, vs):
            y_s[b, rows, lanes] = (
                wr[CHUNK:2 * CHUNK, :]
                + _dot1(jnp.concatenate([nb_s[b, rows, lanes], nk_s[b, rows, lanes]], axis=1),
                        jnp.concatenate([_stack_heads(u), _stack_heads(vc)], axis=0)))
        return carry

    lax.fori_loop(0, n_chunks, phase_b, 0)

    for b in range(bsz):
        y = y_s[b]
        s_hi, s_lo = _head_sums(_bf16_parts(y, 2))
        d = y - (s_hi + s_lo) * (1.0 / HEAD_DIM)
        s_var, s_bonus = _head_sums([d * d, r_s[b] * k_s[b] * rk_ref[...]])
        yn = d * lax.rsqrt(s_var * (1.0 / HEAD_DIM) + RWKV_GN_EPS) * gng_ref[...] + gnb_ref[...]
        gate = _dot1(hg_s[b], gup_ref[...])
        out_ref[b] = ((yn + s_bonus * v_s[b]) * gate).astype(out_ref.dtype)


def _rwkv_mixer(r, lw, k, v, kk, ka, hg, g_up, r_k, gn_g, gn_b, bsz, seq, tb):
    blk = lambda i: (0, i, 0)
    const = lambda i: (0, 0)
    vec = pl.BlockSpec((1, W_RWKV), const)
    tok = pl.BlockSpec((bsz, tb, W_RWKV), blk)
    slab = pltpu.VMEM((bsz, tb, W_RWKV), F32)
    unit_tiles = pltpu.VMEM((bsz, tb // CHUNK * N_RW_PAIRS, PAIR, PAIR), F32)
    seq_major = lambda a: a.reshape(bsz, seq, a.shape[-1])
    out = pl.pallas_call(
        _rwkv_kernel,
        grid=(seq // tb,),
        in_specs=[
            tok, tok, tok, tok, tok, tok,
            pl.BlockSpec((bsz, tb, RWKV_GATE_RANK), blk),
            pl.BlockSpec((RWKV_GATE_RANK, W_RWKV), const),
            vec, vec, vec,
        ],
        out_specs=pl.BlockSpec((bsz, tb, W_RWKV), blk),
        out_shape=jax.ShapeDtypeStruct((bsz, seq, W_RWKV), BF16),
        scratch_shapes=[
            pltpu.VMEM((bsz * N_RW_PAIRS, PAIR, PAIR), F32),
            slab,
            slab, slab, slab, slab, slab,
            unit_tiles, unit_tiles,
        ],
        compiler_params=pltpu.CompilerParams(
            dimension_semantics=("arbitrary",), vmem_limit_bytes=VMEM_LIMIT),
        name="rwkv7_mixer",
    )(*[seq_major(a) for a in (r, lw, k, v, kk, ka, hg)], g_up, r_k, gn_g, gn_b)
    return out.reshape(bsz * seq, W_RWKV)


def _post_kernel(alpha, ff_chunk, x_ref, ogl_ref, orw_ref, mod_ref, wout_ref, ln1g_ref, ln1b_ref,
                 wup_ref, wdn_ref, ln2g_ref, ln2b_ref, out_ref):
    m = mod_ref[0]
    gate1, shift2, scale2, gate2 = m[2:3, :], m[3:4, :], m[4:5, :], m[5:6, :]
    d_ff = wup_ref.shape[1]
    n_ff = d_ff // ff_chunk
    half = x_ref.shape[0] // 2
    groups = [slice(0, half), slice(half, 2 * half)]

    def out_proj(rows):
        return (_mm(ogl_ref[rows, :], wout_ref[0:W_GL, :])
                + _mm(orw_ref[rows, :], wout_ref[W_GL:W_GL + W_RWKV, :]))

    def norm1(rows, o):
        x1 = _layer_norm(alpha * x_ref[rows, :] + (1.0 + gate1) * o, ln1g_ref[...], ln1b_ref[...])
        return x1, (x1 * (1.0 + scale2) + shift2).astype(BF16)

    def mlp(h, acc, js):
        for j in js:
            cols = slice(j * ff_chunk, (j + 1) * ff_chunk)
            u = jnp.maximum(_mm(h, wup_ref[:, cols]), 0.0)
            acc = acc + _mm((u * u).astype(BF16), wdn_ref[cols, :])
        return acc

    def norm2(rows, x1, acc):
        out_ref[rows, :] = _layer_norm(alpha * x1 + (1.0 + gate2) * acc, ln2g_ref[...], ln2b_ref[...])

    o_a, o_b = out_proj(groups[0]), out_proj(groups[1])
    x1_a, h_a = norm1(groups[0], o_a)
    acc_a = mlp(h_a, jnp.zeros(x1_a.shape, F32), range(0, 2))
    x1_b, h_b = norm1(groups[1], o_b)
    acc_a = mlp(h_a, acc_a, range(2, n_ff))
    acc_b = mlp(h_b, jnp.zeros(x1_b.shape, F32), range(0, 2))
    norm2(groups[0], x1_a, acc_a)
    acc_b = mlp(h_b, acc_b, range(2, n_ff))
    norm2(groups[1], x1_b, acc_b)


def _post_mixer(x2, ogl, orw, mod_l, w_out, ln1_g, ln1_b, w_up, w_dn, ln2_g, ln2_b, alpha, seq, tm, layer):
    m_rows, d = x2.shape
    d_ff = w_up.shape[2]
    steps_per_batch = seq // tm
    row = lambda i: (i, 0)
    const = lambda i: (0, 0)
    vec = pl.BlockSpec((1, d), const)
    resident = lambda shape: pl.BlockSpec((None,) + shape, lambda i: (layer, 0, 0), pipeline_mode=pl.Buffered(1))
    return pl.pallas_call(
        functools.partial(_post_kernel, alpha, 512),
        grid=(m_rows // tm,),
        in_specs=[
            pl.BlockSpec((tm, d), row),
            pl.BlockSpec((tm, W_GL), row),
            pl.BlockSpec((tm, W_RWKV), row),
            pl.BlockSpec((1, 6, d), lambda i: (i // steps_per_batch, 0, 0)),
            resident((W_GL + W_RWKV, d)),
            vec, vec,
            resident((d, d_ff)),
            resident((d_ff, d)),
            vec, vec,
        ],
        out_specs=pl.BlockSpec((tm, d), row),
        out_shape=jax.ShapeDtypeStruct((m_rows, d), F32),
        compiler_params=pltpu.CompilerParams(
            dimension_semantics=("arbitrary",), vmem_limit_bytes=VMEM_LIMIT),
        name="outproj_mlp",
    )(x2, ogl, orw, mod_l, w_out, ln1_g, ln1_b, w_up, w_dn, ln2_g, ln2_b)


def kernel(x, c, hgrn_lb_logits, ada_w, ada_b, w_in, hgrn_norm_g, gla_alpha_up, gla_alpha_b, gla_norm_g,
           rwkv_mu, rwkv_w0, rwkv_w_up, rwkv_a0, rwkv_a_up, rwkv_g_up, rwkv_k_k, rwkv_k_a, rwkv_r_k,
           rwkv_gn_g, rwkv_gn_b, w_out, ln1_g, ln1_b, mlp_w_up, mlp_w_down, ln2_g, ln2_b):
    bsz, seq, d = x.shape
    depth = w_in.shape[0]
    alpha = (2.0 * depth) ** 0.25
    tm = min(512, seq)
    tb = min(512, seq)
    tb_gl = min(1024, seq)
    tm_post = min(1024, seq)
    assert all(seq % t == 0 for t in (tm, tb, tb_gl, tm_post)) and tb % CHUNK == 0
    assert tb_gl % (A_GROUP_GL * CHUNK) == 0 and tm_post % 16 == 0
    assert w_in.shape[2] == N_HGRN_COLS + N_GLA_COLS + N_RWKV_COLS

    mod = _modulation(c, ada_w, ada_b).reshape(depth, bsz, 6, d)
    lbs = _lower_bounds(hgrn_lb_logits)

    gla_end = N_HGRN_COLS + N_GLA_COLS
    w_in_b = w_in.astype(BF16)
    w_alpha = jnp.pad(w_in_b[:, :, gla_end - GLA_GATE_RANK:gla_end], ((0, 0), (0, 0), (0, LANES - GLA_GATE_RANK)))
    w_rwkv = w_in_b[:, :, gla_end:]
    alpha_up_p = jnp.concatenate(
        [gla_alpha_up, jnp.zeros((depth, LANES - GLA_GATE_RANK, W_GLA), gla_alpha_up.dtype)], axis=1)
    zeros_r = jnp.zeros((depth, RWKV_DECAY_RANK, W_RWKV), rwkv_w_up.dtype)
    wa_blk = jnp.concatenate(
        [jnp.concatenate([rwkv_w_up, zeros_r], axis=2), jnp.concatenate([zeros_r, rwkv_a_up], axis=2)], axis=1)
    norm_g = jnp.concatenate([hgrn_norm_g, gla_norm_g], axis=1)
    w_out_b = w_out.astype(BF16)
    w_up_b = mlp_w_up.astype(BF16)
    w_dn_b = mlp_w_down.astype(BF16)
    vec = lambda a, l: a[l].reshape(1, -1)

    x2 = x.reshape(bsz * seq, d)
    for l in range(depth):
        gq, gk, gv, glg, ggate, rr, rlw, rk, rv, rkk, rka, rhg = _in_projection(
            x2, mod[l], w_in_b, w_alpha, w_rwkv, vec(lbs, l), alpha_up_p[l], vec(gla_alpha_b, l), vec(rwkv_mu, l),
            vec(rwkv_w0, l), vec(rwkv_a0, l), wa_blk[l], vec(rwkv_k_k, l), vec(rwkv_k_a, l), seq, tm, l)
        ogl = _gl_mixer(gq, gk, gv, glg, ggate, vec(norm_g, l), bsz, seq, tb_gl)
        orw = _rwkv_mixer(rr, rlw, rk, rv, rkk, rka, rhg, rwkv_g_up[l], vec(rwkv_r_k, l), vec(rwkv_gn_g, l),
                          vec(rwkv_gn_b, l), bsz, seq, tb)
        x2 = _post_mixer(x2, ogl, orw, mod[l], w_out_b, vec(ln1_g, l), vec(ln1_b, l), w_up_b, w_dn_b,
                         vec(ln2_g, l), vec(ln2_b, l), alpha, seq, tm_post, l)
    return x2.reshape(bsz, seq, d)
```
